```python
import math
import jax, jax.numpy as jnp
from jax import lax
import numpy as np

D_MODEL = 1024
BATCH = 4
SEQ = 8192
DEPTH = 4

MEM_LEN = 256
QBLK = 128
ROPE_THETA = 10000.0
EPS = 1e-6
N_EVEN = (DEPTH + 1) // 2
N_ODD = DEPTH // 2
DIFF_HEADS = 4
DIFF_QK = 64
DIFF_V = 2 * DIFF_QK
FOX_HEADS = 8
FOX_DIM = 64
NSA_HEADS = 8
NSA_GROUPS = 2
NSA_HPG = NSA_HEADS // NSA_GROUPS
NSA_DIM = 64
CMP_LEN = 32
CMP_STRIDE = 16
CMP_HIDDEN = 128
SLC_LEN = 64
SLC_TOPK = 16
WIN = 512
FORCE_SCORE = 1e4
MLA_HEADS = 8
MLA_NOPE = 64
MLA_ROPE = 32
MLA_V = 64
MLA_Q_RANK = 384
MLA_KV_RANK = 256
XA_HEADS = 4
XA_DIM = 128
D_FF = -(-8 * D_MODEL // (3 * 256)) * 256

EV_SIZES = (DIFF_HEADS * 2 * DIFF_QK, DIFF_HEADS * 2 * DIFF_QK, DIFF_HEADS * DIFF_V,
            FOX_HEADS * FOX_DIM, FOX_HEADS * FOX_DIM, FOX_HEADS * FOX_DIM, FOX_HEADS)
OD_SIZES = (NSA_HEADS * NSA_DIM,) + (NSA_GROUPS * NSA_DIM,) * 6 + (NSA_HEADS * 3, MLA_Q_RANK, MLA_KV_RANK, MLA_ROPE)
EV_OUT = DIFF_HEADS * DIFF_V + FOX_HEADS * FOX_DIM
OD_OUT = NSA_HEADS * NSA_DIM + MLA_HEADS * MLA_V

kernel_name = "hybrid_diff_fox_nsa_mla_trunk"


def _rms(x, g):
    xf = x.astype(jnp.float32)
    y = xf * lax.rsqrt(jnp.mean(xf * xf, axis=-1, keepdims=True) + EPS)
    return (y * g.astype(jnp.float32)).astype(x.dtype)


def _rope(x, pos):
    d = x.shape[-1]
    half = d // 2
    inv = 1.0 / (ROPE_THETA ** (jnp.arange(half, dtype=jnp.float32) / half))
    ang = pos.astype(jnp.float32)[:, None] * inv[None, :]
    cos = jnp.cos(ang)[:, None, :]
    sin = jnp.sin(ang)[:, None, :]
    xf = x.astype(jnp.float32)
    x1, x2 = xf[..., :half], xf[..., half:]
    return jnp.concatenate([x1 * cos - x2 * sin, x2 * cos + x1 * sin], axis=-1).astype(x.dtype)


def _split(h, sizes):
    return jnp.split(h, [int(v) for v in np.cumsum(sizes)[:-1]], axis=-1)


def _causal_attn(q, k, v, scale, logf_cum=None):
    B, S, H, _ = q.shape
    dv = v.shape[-1]
    nq = S // QBLK
    kpos = jnp.arange(S)
    ck = None if logf_cum is None else jnp.transpose(logf_cum, (0, 2, 1))

    def block(i):
        start = i * QBLK
        qi = lax.dynamic_slice_in_dim(q, start, QBLK, axis=1)
        s = jnp.einsum('bqhd,bkhd->bhqk', qi, k).astype(jnp.float32) * scale
        if ck is not None:
            cq = lax.dynamic_slice_in_dim(ck, start, QBLK, axis=2)
            s = s + cq[..., None] - ck[:, :, None, :]
        tq = start + jnp.arange(QBLK)
        s = jnp.where(kpos[None, :] <= tq[:, None], s, -jnp.inf)
        p = jax.nn.softmax(s, axis=-1).astype(v.dtype)
        return jnp.einsum('bhqk,bkhd->bqhd', p, v)

    out = lax.map(block, jnp.arange(nq))
    return jnp.moveaxis(out, 0, 1).reshape(B, S, H, dv)


def _even_mixer(xn, w_in, b_f, lam, subln, w_out, layer_idx, pos):
    B, S, _ = xn.shape
    aq, ak, av, fq, fk, fv, fz = _split(xn @ w_in, EV_SIZES)
    aq = _rope(aq.reshape(B, S, 2 * DIFF_HEADS, DIFF_QK), pos).reshape(B, S, DIFF_HEADS, 2, DIFF_QK)
    ak = _rope(ak.reshape(B, S, 2 * DIFF_HEADS, DIFF_QK), pos).reshape(B, S, DIFF_HEADS, 2, DIFF_QK)
    av = av.reshape(B, S, DIFF_HEADS, DIFF_V)
    lam_init = 0.8 - 0.6 * math.exp(-0.3 * layer_idx)
    lf = lam.astype(jnp.float32)
    lam_full = jnp.exp(jnp.sum(lf[0] * lf[1])) - jnp.exp(jnp.sum(lf[2] * lf[3])) + lam_init
    sc = DIFF_QK ** -0.5
    o1 = _causal_attn(aq[:, :, :, 0], ak[:, :, :, 0], av, sc)
    o2 = _causal_attn(aq[:, :, :, 1], ak[:, :, :, 1], av, sc)
    oa = (o1.astype(jnp.float32) - lam_full * o2.astype(jnp.float32)).astype(xn.dtype)
    oa = _rms(oa, subln) * (1.0 - lam_init)
    logf = jax.nn.log_sigmoid((fz + b_f).astype(jnp.float32))
    cum = jnp.cumsum(logf, axis=1)
    of = _causal_attn(fq.reshape(B, S, FOX_HEADS, FOX_DIM), fk.reshape(B, S, FOX_HEADS, FOX_DIM),
                      fv.reshape(B, S, FOX_HEADS, FOX_DIM), FOX_DIM ** -0.5, cum)
    o = jnp.concatenate([oa.reshape(B, S, -1), of.reshape(B, S, -1)], axis=-1)
    return o @ w_out


def _nsa_compress(kt, pos_emb, w1, w2):
    B, S, G, d = kt.shape
    nc = (S - CMP_LEN) // CMP_STRIDE + 1
    idx = np.arange(nc)[:, None] * CMP_STRIDE + np.arange(CMP_LEN)[None, :]
    blk = kt[:, idx] + pos_emb[None, None, :, None, :]
    blk = jnp.transpose(blk, (0, 1, 3, 2, 4)).reshape(B, nc, G, CMP_LEN * d)
    return jax.nn.silu(blk @ w1) @ w2


def _nsa(q, kc_t, vc_t, ks, vs, kw, vw, gates, cmp_pos, cmp_w1, cmp_w2):
    B, S, G, Hg, d = q.shape
    nq = S // QBLK
    kc = _nsa_compress(kc_t, cmp_pos[0], cmp_w1[0], cmp_w2[0])
    vc = _nsa_compress(vc_t, cmp_pos[1], cmp_w1[1], cmp_w2[1])
    nc = kc.shape[1]
    cmp_end = jnp.arange(nc) * CMP_STRIDE + CMP_LEN - 1
    nblk = S // SLC_LEN
    n_sel = min(SLC_TOPK, nblk)
    cs = np.arange(nc) * CMP_STRIDE
    bs = np.arange(nblk) * SLC_LEN
    ov = jnp.asarray(((cs[:, None] < bs[None, :] + SLC_LEN) & (cs[:, None] + CMP_LEN > bs[None, :])).astype(np.float32))
    ks_blk = jnp.transpose(ks, (0, 2, 1, 3)).reshape(B, G, nblk, SLC_LEN, d)
    vs_blk = jnp.transpose(vs, (0, 2, 1, 3)).reshape(B, G, nblk, SLC_LEN, d)
    kw_pad = jnp.pad(kw, ((0, 0), (WIN, 0), (0, 0), (0, 0)))
    vw_pad = jnp.pad(vw, ((0, 0), (WIN, 0), (0, 0), (0, 0)))
    bidx = jnp.arange(B)[:, None, None, None]
    gidx = jnp.arange(G)[None, :, None, None]
    blk_ids = jnp.arange(nblk)
    scale = d ** -0.5

    def block(i):
        start = i * QBLK
        qi = lax.dynamic_slice_in_dim(q, start, QBLK, axis=1)
        tq = start + jnp.arange(QBLK)
        s = jnp.einsum('bqghd,bcgd->bghqc', qi, kc).astype(jnp.float32) * scale
        s = jnp.where(cmp_end[None, :] <= tq[:, None], s, -jnp.inf)
        m = jnp.max(s, axis=-1, keepdims=True)
        e = jnp.exp(s - jnp.where(jnp.isfinite(m), m, 0.0))
        p = e / jnp.maximum(jnp.sum(e, axis=-1, keepdims=True), 1e-30)
        o_cmp = jnp.einsum('bghqc,bcgd->bqghd', p.astype(vc.dtype), vc)
        imp = jnp.einsum('bghqc,cn->bgqn', p, ov)
        cur = tq // SLC_LEN
        bid = blk_ids[None, :]
        forced = (bid == 0) | (bid == cur[:, None]) | (bid == cur[:, None] - 1)
        imp = jnp.where(forced, FORCE_SCORE, imp)
        imp = jnp.where(bid > cur[:, None], -jnp.inf, imp)
        _, sel = lax.top_k(imp, n_sel)
        kg = ks_blk[bidx, gidx, sel]
        vg = vs_blk[bidx, gidx, sel]
        tpos = sel[..., None] * SLC_LEN + jnp.arange(SLC_LEN)
        s = jnp.einsum('bqghd,bgqnld->bghqnl', qi, kg).astype(jnp.float32) * scale
        s = jnp.where((tpos <= tq[:, None, None])[:, :, None], s, -jnp.inf)
        p = jax.nn.softmax(s.reshape(B, G, Hg, QBLK, n_sel * SLC_LEN), axis=-1)
        o_slc = jnp.einsum('bghqm,bgqmd->bqghd', p.astype(vg.dtype), vg.reshape(B, G, QBLK, n_sel * SLC_LEN, d))
        kwi = lax.dynamic_slice_in_dim(kw_pad, start, WIN + QBLK, axis=1)
        vwi = lax.dynamic_slice_in_dim(vw_pad, start, WIN + QBLK, axis=1)
        kp = start - WIN + jnp.arange(WIN + QBLK)
        s = jnp.einsum('bqghd,bkgd->bghqk', qi, kwi).astype(jnp.float32) * scale
        dlt = tq[:, None] - kp[None, :]
        s = jnp.where((dlt >= 0) & (dlt < WIN) & (kp[None, :] >= 0), s, -jnp.inf)
        p = jax.nn.softmax(s, axis=-1)
        o_win = jnp.einsum('bghqk,bkgd->bqghd', p.astype(vwi.dtype), vwi)
        gi = lax.dynamic_slice_in_dim(gates, start, QBLK, axis=1)
        return gi[..., 0:1] * o_cmp + gi[..., 1:2] * o_slc + gi[..., 2:3] * o_win

    out = lax.map(block, jnp.arange(nq))
    return jnp.moveaxis(out, 0, 1).reshape(B, S, G * Hg * d)


def _odd_mixer(xn, w_in, cmp_pos, cmp_w1, cmp_w2, q_norm, kv_norm, w_uq, w_ukv, w_out, pos):
    B, S, _ = xn.shape
    nq_, kc, vc, ks, vs, kw, vw, gz, cq, ckv, kr = _split(xn @ w_in, OD_SIZES)
    kvs = (B, S, NSA_GROUPS, NSA_DIM)
    q = _rope(nq_.reshape(B, S, NSA_HEADS, NSA_DIM), pos).reshape(B, S, NSA_GROUPS, NSA_HPG, NSA_DIM)
    kc = _rope(kc.reshape(kvs), pos)
    ks = _rope(ks.reshape(kvs), pos)
    kw = _rope(kw.reshape(kvs), pos)
    gates = jax.nn.sigmoid(gz).reshape(B, S, NSA_GROUPS, NSA_HPG, 3)
    o_nsa = _nsa(q, kc, vc.reshape(kvs), ks, vs.reshape(kvs), kw, vw.reshape(kvs), gates, cmp_pos, cmp_w1, cmp_w2)
    qf = (_rms(cq, q_norm) @ w_uq).reshape(B, S, MLA_HEADS, MLA_NOPE + MLA_ROPE)
    q_nope, q_pe = qf[..., :MLA_NOPE], _rope(qf[..., MLA_NOPE:], pos)
    kvf = (_rms(ckv, kv_norm) @ w_ukv).reshape(B, S, MLA_HEADS, MLA_NOPE + MLA_V)
    k_nope, v = kvf[..., :MLA_NOPE], kvf[..., MLA_NOPE:]
    k_pe = _rope(kr.reshape(B, S, 1, MLA_ROPE), pos)
    qm = jnp.concatenate([q_nope, q_pe], axis=-1)
    km = jnp.concatenate([k_nope, jnp.broadcast_to(k_pe, (B, S, MLA_HEADS, MLA_ROPE))], axis=-1)
    o_mla = _causal_attn(qm, km, v, (MLA_NOPE + MLA_ROPE) ** -0.5)
    o = jnp.concatenate([o_nsa, o_mla.reshape(B, S, -1)], axis=-1)
    return o @ w_out


def _mem_attn(xn, mem_n, wq, wkv, wo):
    B, S, _ = xn.shape
    M = mem_n.shape[1]
    q = (xn @ wq).reshape(B, S, XA_HEADS, XA_DIM)
    k, v = jnp.split(mem_n @ wkv, 2, axis=-1)
    k = k.reshape(B, M, XA_HEADS, XA_DIM)
    v = v.reshape(B, M, XA_HEADS, XA_DIM)
    s = jnp.einsum('bshd,bmhd->bhsm', q, k).astype(jnp.float32) * (XA_DIM ** -0.5)
    p = jax.nn.softmax(s, axis=-1).astype(v.dtype)
    o = jnp.einsum('bhsm,bmhd->bshd', p, v).reshape(B, S, XA_HEADS * XA_DIM)
    return o @ wo


def _swiglu(xn, w13, w2):
    g, u = jnp.split(xn @ w13, 2, axis=-1)
    return (jax.nn.silu(g) * u) @ w2


def setup_inputs(seed: int = 0) -> dict:
    key = jax.random.key(seed)
    k = jax.random.split(key, 32)
    f32 = jnp.float32

    def w(kk, shape, fan_in):
        return jax.random.normal(kk, shape, f32) * (fan_in ** -0.5)

    def gain(kk, shape):
        return 1.0 + 0.02 * jax.random.normal(kk, shape, f32)

    D = D_MODEL
    return {
        "x": jax.random.normal(k[0], (BATCH, SEQ, D), f32),
        "mem": jax.random.normal(k[1], (BATCH, MEM_LEN, D), f32),
        "mem_norm": gain(k[2], (D,)),
        "norm_mix": gain(k[3], (DEPTH, D)),
        "norm_mem": gain(k[4], (DEPTH, D)),
        "norm_ffn": gain(k[5], (DEPTH, D)),
        "ev_w_in": w(k[6], (N_EVEN, D, sum(EV_SIZES)), D),
        "ev_b_f": 0.1 * jax.random.normal(k[7], (N_EVEN, FOX_HEADS), f32),
        "ev_lam": 0.1 * jax.random.normal(k[8], (N_EVEN, 4, DIFF_QK), f32),
        "ev_subln": gain(k[9], (N_EVEN, DIFF_V)),
        "ev_w_out": w(k[10], (N_EVEN, EV_OUT, D), EV_OUT),
        "od_w_in": w(k[11], (N_ODD, D, sum(OD_SIZES)), D),
        "nsa_cmp_pos": 0.02 * jax.random.normal(k[12], (N_ODD, 2, CMP_LEN, NSA_DIM), f32),
        "nsa_cmp_w1": w(k[13], (N_ODD, 2, CMP_LEN * NSA_DIM, CMP_HIDDEN), CMP_LEN * NSA_DIM),
        "nsa_cmp_w2": w(k[14], (N_ODD, 2, CMP_HIDDEN, NSA_DIM), CMP_HIDDEN),
        "mla_q_norm": gain(k[15], (N_ODD, MLA_Q_RANK)),
        "mla_kv_norm": gain(k[16], (N_ODD, MLA_KV_RANK)),
        "mla_w_uq": w(k[17], (N_ODD, MLA_Q_RANK, MLA_HEADS * (MLA_NOPE + MLA_ROPE)), MLA_Q_RANK),
        "mla_w_ukv": w(k[18], (N_ODD, MLA_KV_RANK, MLA_HEADS * (MLA_NOPE + MLA_V)), MLA_KV_RANK),
        "od_w_out": w(k[19], (N_ODD, OD_OUT, D), OD_OUT),
        "xa_wq": w(k[20], (DEPTH, D, XA_HEADS * XA_DIM), D),
        "xa_wkv": w(k[21], (DEPTH, D, 2 * XA_HEADS * XA_DIM), D),
        "xa_wo": w(k[22], (DEPTH, XA_HEADS * XA_DIM, D), XA_HEADS * XA_DIM),
        "ffn_w13": w(k[23], (DEPTH, D, 2 * D_FF), D),
        "ffn_w2": w(k[24], (DEPTH, D_FF, D), D_FF),
        "final_norm": gain(k[25], (D,)),
    }


def reference(x, mem, mem_norm, norm_mix, norm_mem, norm_ffn,
              ev_w_in, ev_b_f, ev_lam, ev_subln, ev_w_out,
              od_w_in, nsa_cmp_pos, nsa_cmp_w1, nsa_cmp_w2, mla_q_norm, mla_kv_norm, mla_w_uq, mla_w_ukv, od_w_out,
              xa_wq, xa_wkv, xa_wo, ffn_w13, ffn_w2, final_norm):
    S = x.shape[1]
    pos = jnp.arange(S)
    mem_n = _rms(mem, mem_norm)
    for li in range(DEPTH):
        j = li // 2
        h = _rms(x, norm_mix[li])
        if li % 2 == 0:
            x = x + _even_mixer(h, ev_w_in[j], ev_b_f[j], ev_lam[j], ev_subln[j], ev_w_out[j], li, pos)
        else:
            x = x + _odd_mixer(h, od_w_in[j], nsa_cmp_pos[j], nsa_cmp_w1[j], nsa_cmp_w2[j],
                               mla_q_norm[j], mla_kv_norm[j], mla_w_uq[j], mla_w_ukv[j], od_w_out[j], pos)
        x = x + _mem_attn(_rms(x, norm_mem[li]), mem_n, xa_wq[li], xa_wkv[li], xa_wo[li])
        x = x + _swiglu(_rms(x, norm_ffn[li]), ffn_w13[li], ffn_w2[li])
    return _rms(x, final_norm)
```

```python
import functools
import math
from typing import NamedTuple, Optional

import numpy as np
import jax
import jax.numpy as jnp
from jax import lax
from jax.experimental import pallas as pl
from jax.experimental.pallas import tpu as pltpu

F32 = jnp.float32
BF16 = jnp.bfloat16

LANES = 128
HALF_LANES = LANES // 2
ROPE_THETA = 10000.0
EPS = 1e-6
NEG = -1e30

DIFF_HEADS = 4
DIFF_QK = 64
FOX_HEADS = 8
NSA_HEADS = 8
NSA_GROUPS = 2
NSA_DIM = 64
CMP_LEN = 32
CMP_STRIDE = 16
CMP_HIDDEN = 128
SLC_LEN = 64
SLC_TOPK = 16
WIN = 512
FORCE_SCORE = 1e4
MLA_HEADS = 8
MLA_NOPE = 64
MLA_ROPE = 32
MLA_V = 64
MLA_Q_RANK = 384
MLA_KV_RANK = 256
XA_HEADS = 4
XA_DIM = 128

VMEM_LIMIT = 56 * 1024 * 1024
ROW_TILE = 512
COL_CHUNK = 512
ATT_TILE = 256

_NT = (((1,), (1,)), ((), ()))


def _params(*sem):
    return pltpu.CompilerParams(dimension_semantics=sem, vmem_limit_bytes=VMEM_LIMIT)


def _sigmoid(x):
    return 1.0 / (1.0 + jnp.exp(-x))


def _split_bf16(x, terms):
    out = []
    r = x
    for _ in range(terms):
        h = r.astype(BF16)
        out.append(h)
        r = r - h.astype(F32)
    return out


def _rope_tables(S, half, lane_lo, lane_hi):
    pos = jnp.arange(S, dtype=F32)
    inv = 1.0 / (ROPE_THETA ** (jnp.arange(half, dtype=F32) / half))
    ang = pos[:, None] * inv[None, :]
    cos, sin = jnp.cos(ang), jnp.sin(ang)
    lane = np.arange(LANES)
    active = (lane >= lane_lo) & (lane < lane_hi)
    j = (lane - lane_lo) % (2 * half)
    lower = active & (j < half)
    upper = active & (j >= half)
    idx = j % half
    cos_t = jnp.where(active[None, :], cos[:, idx], 1.0)
    sin_a = jnp.where(lower[None, :], -sin[:, idx], 0.0)
    sin_b = jnp.where(upper[None, :], sin[:, idx], 0.0)
    return cos_t, sin_a, sin_b


class Seg(NamedTuple):
    start: int
    width: int
    dtype: object
    rope: Optional[str] = None
    scale: float = 1.0


def _linear(xs, ws, segs, *, gain=None, residual=None, ropes=None, seq=None, name="linear"):
    N = xs[0].shape[0]
    tm = min(ROW_TILE, N)
    assert N % tm == 0
    n_in = len(xs)
    has_gain = gain is not None
    has_res = residual is not None
    rope_keys = sorted({s.rope for s in segs if s.rope})
    halves = {k: ropes[k][1] for k in rope_keys}

    def kern(*refs):
        it = iter(refs)
        x_refs = [next(it) for _ in range(n_in)]
        w_refs = [next(it) for _ in range(n_in)]
        g_ref = next(it) if has_gain else None
        r_ref = next(it) if has_res else None
        tabs = {k: (next(it), next(it), next(it)) for k in rope_keys}
        o_refs = [next(it) for _ in segs]
        acts = []
        for j, xr in enumerate(x_refs):
            x = xr[...]
            if j == 0 and has_gain:
                xf = x.astype(F32)
                y = xf * lax.rsqrt(jnp.mean(xf * xf, axis=-1, keepdims=True) + EPS)
                acts.append((y * g_ref[...]).astype(BF16))
            else:
                acts.append(x.astype(BF16))
        for seg, o_ref in zip(segs, o_refs):
            for c0 in range(0, seg.width, COL_CHUNK):
                cw = min(COL_CHUNK, seg.width - c0)
                col = seg.start + c0
                acc = None
                for a, wr in zip(acts, w_refs):
                    d = jnp.dot(a, wr[:, col:col + cw], preferred_element_type=F32)
                    acc = d if acc is None else acc + d
                if has_res:
                    acc = acc + r_ref[:, col:col + cw]
                if seg.scale != 1.0:
                    acc = acc * seg.scale
                if seg.rope is None:
                    o_ref[:, c0:c0 + cw] = acc.astype(o_ref.dtype)
                else:
                    cos_r, sa_r, sb_r = tabs[seg.rope]
                    half = halves[seg.rope]
                    cos, sa, sb = cos_r[...], sa_r[...], sb_r[...]
                    for s0 in range(0, cw, LANES):
                        xs_ = acc[:, s0:s0 + LANES]
                        y = (xs_ * cos + pltpu.roll(xs_, LANES - half, 1) * sa
                             + pltpu.roll(xs_, half, 1) * sb)
                        o_ref[:, c0 + s0:c0 + s0 + LANES] = y.astype(o_ref.dtype)

    in_specs, args = [], []
    for x in xs:
        in_specs.append(pl.BlockSpec((tm, x.shape[1]), lambda i: (i, 0)))
        args.append(x)
    for w in ws:
        in_specs.append(pl.BlockSpec(w.shape, lambda i: (0, 0)))
        args.append(w)
    if has_gain:
        in_specs.append(pl.BlockSpec((1, gain.shape[-1]), lambda i: (0, 0)))
        args.append(gain.reshape(1, -1).astype(F32))
    if has_res:
        in_specs.append(pl.BlockSpec((tm, residual.shape[1]), lambda i: (i, 0)))
        args.append(residual)
    for k in rope_keys:
        assert seq % tm == 0
        nt = seq // tm
        for t in ropes[k][0]:
            in_specs.append(pl.BlockSpec((tm, LANES), lambda i, nt=nt: (i % nt, 0)))
            args.append(t)
    out_shape = [jax.ShapeDtypeStruct((N, s.width), s.dtype) for s in segs]
    out_specs = [pl.BlockSpec((tm, s.width), lambda i: (i, 0)) for s in segs]
    return pl.pallas_call(
        kern, grid=(N // tm,), in_specs=in_specs, out_specs=out_specs, out_shape=out_shape,
        compiler_params=_params("parallel"), name=name)(*args)


def _logf_cumsum(fz, b_f, B, S):
    ch = ATT_TILE

    def kern(z_ref, b_ref, o_ref):
        r = lax.broadcasted_iota(jnp.int32, (ch, ch), 0)
        c = lax.broadcasted_iota(jnp.int32, (ch, ch), 1)
        tri = jnp.where(c <= r, 1.0, 0.0).astype(BF16)

        def body(j, carry):
            r0 = pl.multiple_of(j * ch, ch)
            z = z_ref[pl.ds(r0, ch), :] + b_ref[...]
            logf = -(jnp.maximum(-z, 0.0) + jnp.log1p(jnp.exp(-jnp.abs(z))))
            cs = carry
            for part in _split_bf16(logf, 3):
                cs = cs + jnp.dot(tri, part, preferred_element_type=F32)
            o_ref[pl.ds(r0, ch), :] = cs
            return cs[ch - 1:ch, :]

        lax.fori_loop(0, S // ch, body, jnp.zeros((1, LANES), F32))

    return pl.pallas_call(
        kern, grid=(B,),
        in_specs=[pl.BlockSpec((S, LANES), lambda b: (b, 0)),
                  pl.BlockSpec((1, LANES), lambda b: (0, 0))],
        out_specs=pl.BlockSpec((S, LANES), lambda b: (b, 0)),
        out_shape=jax.ShapeDtypeStruct((B * S, LANES), F32),
        compiler_params=_params("parallel"), name="logf_cumsum")(fz, b_f)


class Head(NamedTuple):
    q_off: int
    q_half: Optional[str]
    k_off: int


def _flash(q, k, v, *, B, S, P, q_w, k_w, heads, out_slabs, kind, out_dtype,
           cum=None, cum_t=None, sel=None, lam=None, subln=None, lam_init=None, name="flash"):
    T = ATT_TILE
    assert S % T == 0
    nq = S // T
    nh = len(heads)
    n_out = len(out_slabs)
    if kind == "win":
        assert WIN == 2 * T
    blk_per_tile = T // SLC_LEN

    def kern(*refs):
        it = iter(refs)
        q_ref, k_ref, v_ref = next(it), next(it), next(it)
        cq_ref = ck_ref = sel_ref = lam_ref = sub_ref = None
        if kind == "fox":
            cq_ref, ck_ref = next(it), next(it)
        if kind == "slc":
            sel_ref = next(it)
        if kind == "diff":
            lam_ref, sub_ref = next(it), next(it)
        o_ref, m_ref, l_ref, acc_ref = next(it), next(it), next(it), next(it)
        p_id = pl.program_id(1)
        i = pl.program_id(2)

        lane = lax.broadcasted_iota(jnp.int32, (T, LANES), 1)
        lo = lane < HALF_LANES
        q_all = q_ref[...]
        qs = []
        for h in heads:
            qh = q_all[:, h.q_off:h.q_off + LANES]
            if h.q_half == "lo":
                qh = jnp.where(lo, qh, jnp.zeros_like(qh))
            elif h.q_half == "hi":
                qh = jnp.where(lo, jnp.zeros_like(qh), qh)
            qs.append(qh)
        cqs = None
        if kind == "fox":
            cq_all = cq_ref[...]
            cqs = [jnp.sum(jnp.where(lane == 2 * p_id + j, cq_all, 0.0), axis=1, keepdims=True)
                   for j in range(nh)]
        selm = None
        if kind == "slc":
            selm = ((sel_ref[...].astype(F32) - 1.0) * (-NEG)).astype(BF16)

        m_ref[...] = jnp.full(m_ref.shape, NEG, F32)
        l_ref[...] = jnp.zeros(l_ref.shape, F32)
        acc_ref[...] = jnp.zeros(acc_ref.shape, F32)

        def step(kt, mode):
            k0 = pl.multiple_of(kt * T, T)
            vt = v_ref[pl.ds(k0, T), :]
            bias = None
            if kind == "slc":
                blk = lax.broadcasted_iota(jnp.int32, (LANES, T), 0)
                col = lax.broadcasted_iota(jnp.int32, (LANES, T), 1)
                expand = jnp.where(blk == kt * blk_per_tile + col // SLC_LEN, 1.0, 0.0).astype(BF16)
                bias = jnp.dot(selm, expand, preferred_element_type=F32)
            msk = None
            if mode is not None:
                rq = lax.broadcasted_iota(jnp.int32, (T, T), 0) + i * T
                ck = lax.broadcasted_iota(jnp.int32, (T, T), 1) + kt * T
                msk = (ck <= rq) if mode == "causal" else (rq - ck < WIN)
            for hi, h in enumerate(heads):
                kk = k_ref[pl.ds(k0, T), h.k_off:h.k_off + LANES]
                s = lax.dot_general(qs[hi], kk, _NT, preferred_element_type=F32)
                if kind == "fox":
                    s = s + (cqs[hi] - ck_ref[hi:hi + 1, pl.ds(k0, T)])
                if bias is not None:
                    s = s + bias
                if msk is not None:
                    s = jnp.where(msk, s, NEG)
                m_prev = m_ref[hi]
                m_new = jnp.maximum(m_prev, jnp.max(s, axis=1, keepdims=True))
                alpha = jnp.exp(m_prev - m_new)
                p = jnp.exp(s - m_new[:, :1])
                l_ref[hi] = alpha * l_ref[hi] + jnp.sum(p, axis=1, keepdims=True)
                acc_ref[hi] = alpha * acc_ref[hi] + jnp.dot(
                    p.astype(BF16), vt, preferred_element_type=F32)
                m_ref[hi] = m_new

        if kind == "win":
            @pl.when(i >= 2)
            def _():
                step(i - 2, "window")

            @pl.when(i >= 1)
            def _():
                step(i - 1, None)
        else:
            def body(kt, c):
                step(kt, None)
                return c
            lax.fori_loop(0, i, body, 0)
        step(i, "causal")

        outs = [acc_ref[hi] / l_ref[hi] for hi in range(nh)]
        for j, (ha, hb) in enumerate(out_slabs):
            if kind == "diff":
                lm = lam_ref[...]
                la = jnp.sum(lm[0:1] * lm[1:2], axis=1, keepdims=True)
                lb = jnp.sum(lm[2:3] * lm[3:4], axis=1, keepdims=True)
                lam_full = jnp.exp(la) - jnp.exp(lb) + lam_init
                oa = outs[ha] - lam_full * outs[hb]
                y = oa * lax.rsqrt(jnp.mean(oa * oa, axis=-1, keepdims=True) + EPS)
                y = (y * sub_ref[...]) * (1.0 - lam_init)
            else:
                y = jnp.where(lo, outs[ha], outs[hb])
            o_ref[:, j * LANES:(j + 1) * LANES] = y.astype(o_ref.dtype)

    in_specs = [
        pl.BlockSpec((T, q_w), lambda b, p, i: (b * nq + i, p)),
        pl.BlockSpec((S, k_w), lambda b, p, i: (b, p)),
        pl.BlockSpec((S, LANES), lambda b, p, i: (b, p)),
    ]
    args = [q, k, v]
    if kind == "fox":
        in_specs.append(pl.BlockSpec((T, LANES), lambda b, p, i: (b * nq + i, 0)))
        in_specs.append(pl.BlockSpec((None, None, nh, S), lambda b, p, i: (b, p, 0, 0)))
        args += [cum, cum_t]
    if kind == "slc":
        in_specs.append(pl.BlockSpec((T, LANES), lambda b, p, i: (b * nq + i, p)))
        args.append(sel)
    if kind == "diff":
        in_specs.append(pl.BlockSpec(lam.shape, lambda b, p, i: (0, 0)))
        in_specs.append(pl.BlockSpec((1, LANES), lambda b, p, i: (0, 0)))
        args += [lam, subln.reshape(1, LANES)]
    return pl.pallas_call(
        kern, grid=(B, P, nq), in_specs=in_specs,
        out_specs=pl.BlockSpec((T, n_out * LANES), lambda b, p, i: (b * nq + i, p)),
        out_shape=jax.ShapeDtypeStruct((B * S, P * n_out * LANES), out_dtype),
        scratch_shapes=[pltpu.VMEM((nh, T, LANES), F32)] * 3,
        compiler_params=_params("parallel", "parallel", "arbitrary"), name=name)(*args)


_PAIR_HEADS = (Head(0, "lo", 0), Head(0, "hi", 0))
_QUAD_HEADS = (Head(0, "lo", 0), Head(0, "hi", 0), Head(LANES, "lo", 0), Head(LANES, "hi", 0))
_WIDE_HEADS = (Head(0, None, 0), Head(LANES, None, LANES))


def _nsa_compress(rows, pos, w1, w2d):
    _, BG, R, K = rows.shape

    def kern(r_ref, pe_ref, w1_ref, w2_ref, o_ref):
        r = r_ref[...]
        pe = pe_ref[...]
        xa = (r + pe[:, :K]).astype(BF16)
        xb = (r + pe[:, K:]).astype(BF16)
        a = jnp.dot(xa, w1_ref[:K, :], preferred_element_type=F32)
        b = jnp.dot(xb, w1_ref[K:, :], preferred_element_type=F32)
        h = a + pltpu.roll(b, R - 1, 0)
        hs = h * _sigmoid(h)
        o_ref[...] = jnp.dot(hs.astype(BF16), w2_ref[...], preferred_element_type=F32)

    return pl.pallas_call(
        kern, grid=(2, BG),
        in_specs=[pl.BlockSpec((None, None, R, K), lambda t, g: (t, g, 0, 0)),
                  pl.BlockSpec((None, 1, 2 * K), lambda t, g: (t, 0, 0)),
                  pl.BlockSpec((None, 2 * K, CMP_HIDDEN), lambda t, g: (t, 0, 0)),
                  pl.BlockSpec((None, CMP_HIDDEN, LANES), lambda t, g: (t, 0, 0))],
        out_specs=pl.BlockSpec((None, None, R, LANES), lambda t, g: (t, g, 0, 0)),
        out_shape=jax.ShapeDtypeStruct((2, BG, R, LANES), F32),
        compiler_params=_params("parallel", "parallel"), name="nsa_compress")(rows, pos, w1, w2d)


def _nsa_cmp_select(q, cmp_kv, ov, *, B, S):
    T = ATT_TILE
    nq = S // T
    R = cmp_kv.shape[2]
    G = NSA_GROUPS
    n_sel = min(SLC_TOPK, S // SLC_LEN)

    def kern(q_ref, kc_ref, vc_ref, ov_ref, o_ref, sel_ref):
        i = pl.program_id(2)
        lane = lax.broadcasted_iota(jnp.int32, (T, LANES), 1)
        lo = lane < HALF_LANES
        tq = lax.broadcasted_iota(jnp.int32, (T, R), 0) + i * T
        cend = lax.broadcasted_iota(jnp.int32, (T, R), 1) * CMP_STRIDE + (CMP_LEN - 1)
        vis = cend <= tq
        kc = kc_ref[...].astype(BF16)
        vc = vc_ref[...].astype(BF16)
        q_all = q_ref[...]
        psum = jnp.zeros((T, R), F32)
        o_heads = []
        for hh in range(4):
            qh = q_all[:, (hh // 2) * LANES:(hh // 2 + 1) * LANES]
            qh = jnp.where(lo, qh, jnp.zeros_like(qh)) if hh % 2 == 0 else jnp.where(lo, jnp.zeros_like(qh), qh)
            s = lax.dot_general(qh, kc, _NT, preferred_element_type=F32)
            s = jnp.where(vis, s, -jnp.inf)
            m = jnp.max(s, axis=1, keepdims=True)
            e = jnp.exp(s - jnp.where(m > -jnp.inf, m, 0.0))
            p = e / jnp.maximum(jnp.sum(e, axis=1, keepdims=True), 1e-30)
            o_heads.append(jnp.dot(p.astype(BF16), vc, preferred_element_type=F32))
            psum = psum + p
        for a in range(2):
            o_ref[:, a * LANES:(a + 1) * LANES] = jnp.where(lo, o_heads[2 * a], o_heads[2 * a + 1])
        imp = jnp.zeros((T, LANES), F32)
        for part in _split_bf16(psum, 2):
            imp = imp + jnp.dot(part, ov_ref[...], preferred_element_type=F32)
        tq1 = lax.broadcasted_iota(jnp.int32, (T, LANES), 0) + i * T
        cur = tq1 // SLC_LEN
        forced = (lane == 0) | (lane == cur) | (lane == cur - 1)
        work = jnp.where(forced, FORCE_SCORE, imp)
        work = jnp.where(lane > cur, NEG, work)
        lane_f = lane.astype(F32)
        sel = jnp.zeros((T, LANES), F32)
        for _ in range(n_sel):
            mx = jnp.max(work, axis=1, keepdims=True)
            first = jnp.min(jnp.where(work == mx, lane_f, float(LANES)), axis=1, keepdims=True)
            hit = lane_f == first
            sel = jnp.where(hit, 1.0, sel)
            work = jnp.where(hit, -jnp.inf, work)
        sel_ref[...] = sel.astype(sel_ref.dtype)

    return pl.pallas_call(
        kern, grid=(B, G, nq),
        in_specs=[pl.BlockSpec((T, 2 * LANES), lambda b, g, i: (b * nq + i, g)),
                  pl.BlockSpec((None, None, R, LANES), lambda b, g, i: (0, b * G + g, 0, 0)),
                  pl.BlockSpec((None, None, R, LANES), lambda b, g, i: (1, b * G + g, 0, 0)),
                  pl.BlockSpec((R, LANES), lambda b, g, i: (0, 0))],
        out_specs=[pl.BlockSpec((T, 2 * LANES), lambda b, g, i: (b * nq + i, g)),
                   pl.BlockSpec((T, LANES), lambda b, g, i: (b * nq + i, g))],
        out_shape=[jax.ShapeDtypeStruct((B * S, G * 2 * LANES), F32),
                   jax.ShapeDtypeStruct((B * S, G * LANES), BF16)],
        compiler_params=_params("parallel", "parallel", "parallel"), name="nsa_cmp_select")(
            q, cmp_kv, cmp_kv, ov)


def _nsa_combine(o_cmp, o_slc, o_win, gz, expand):
    N, C = o_cmp.shape
    tm = min(ROW_TILE, N)

    def kern(c_ref, s_ref, w_ref, g_ref, e_ref, o_ref):
        gate = _sigmoid(g_ref[...])
        parts = _split_bf16(gate, 2)
        acc = jnp.zeros((tm, C), F32)
        for j, br in enumerate((c_ref, s_ref, w_ref)):
            gj = sum(jnp.dot(part, e_ref[j], preferred_element_type=F32) for part in parts)
            acc = acc + gj * br[...]
        o_ref[...] = acc.astype(o_ref.dtype)

    row = pl.BlockSpec((tm, C), lambda i: (i, 0))
    return pl.pallas_call(
        kern, grid=(N // tm,),
        in_specs=[row, row, row, pl.BlockSpec((tm, LANES), lambda i: (i, 0)),
                  pl.BlockSpec(expand.shape, lambda i: (0, 0, 0))],
        out_specs=row, out_shape=jax.ShapeDtypeStruct((N, C), BF16),
        compiler_params=_params("parallel"), name="nsa_combine")(o_cmp, o_slc, o_win, gz, expand)


def _mla_kv(ckv, gain, wk, wv, kr):
    N, K = ckv.shape
    tm = min(ROW_TILE, N)
    H = MLA_HEADS

    def kern(c_ref, g_ref, wk_ref, wv_ref, kr_ref, k_ref, v_ref):
        xf = c_ref[...]
        y = xf * lax.rsqrt(jnp.mean(xf * xf, axis=-1, keepdims=True) + EPS)
        a = (y * g_ref[...]).astype(BF16)
        kr_ = kr_ref[...]
        for h in range(H):
            kh = jnp.dot(a, wk_ref[:, h * LANES:(h + 1) * LANES], preferred_element_type=F32)
            k_ref[:, h * LANES:(h + 1) * LANES] = (kh + kr_).astype(k_ref.dtype)
        v_ref[...] = jnp.dot(a, wv_ref[...], preferred_element_type=F32).astype(v_ref.dtype)

    return pl.pallas_call(
        kern, grid=(N // tm,),
        in_specs=[pl.BlockSpec((tm, K), lambda i: (i, 0)),
                  pl.BlockSpec((1, K), lambda i: (0, 0)),
                  pl.BlockSpec(wk.shape, lambda i: (0, 0)),
                  pl.BlockSpec(wv.shape, lambda i: (0, 0)),
                  pl.BlockSpec((tm, LANES), lambda i: (i, 0))],
        out_specs=[pl.BlockSpec((tm, H * LANES), lambda i: (i, 0)),
                   pl.BlockSpec((tm, H * MLA_V), lambda i: (i, 0))],
        out_shape=[jax.ShapeDtypeStruct((N, H * LANES), BF16),
                   jax.ShapeDtypeStruct((N, H * MLA_V), BF16)],
        compiler_params=_params("parallel"), name="mla_kv")(
            ckv, gain.reshape(1, K).astype(F32), wk, wv, kr)


def _mem_attn(x, gain, wq, wo, mem_k, mem_v, *, S):
    N, D = x.shape
    tm = min(ROW_TILE, S)
    M = mem_k.shape[1]
    per_b = S // tm
    scale = XA_DIM ** -0.5

    def kern(x_ref, g_ref, wq_ref, wo_ref, k_ref, v_ref, o_ref):
        xf = x_ref[...]
        y = xf * lax.rsqrt(jnp.mean(xf * xf, axis=-1, keepdims=True) + EPS)
        a = (y * g_ref[...]).astype(BF16)
        q = (jnp.dot(a, wq_ref[...], preferred_element_type=F32) * scale).astype(BF16)
        acc = xf
        for h in range(XA_HEADS):
            sl = slice(h * XA_DIM, (h + 1) * XA_DIM)
            s = lax.dot_general(q[:, sl], k_ref[:, sl], _NT, preferred_element_type=F32)
            e = jnp.exp(s - jnp.max(s, axis=1, keepdims=True))
            p = e / jnp.sum(e, axis=1, keepdims=True)
            oh = jnp.dot(p.astype(BF16), v_ref[:, sl], preferred_element_type=F32)
            acc = acc + jnp.dot(oh.astype(BF16), wo_ref[sl, :], preferred_element_type=F32)
        o_ref[...] = acc

    return pl.pallas_call(
        kern, grid=(N // tm,),
        in_specs=[pl.BlockSpec((tm, D), lambda i: (i, 0)),
                  pl.BlockSpec((1, D), lambda i: (0, 0)),
                  pl.BlockSpec(wq.shape, lambda i: (0, 0)),
                  pl.BlockSpec(wo.shape, lambda i: (0, 0)),
                  pl.BlockSpec((None, M, XA_HEADS * XA_DIM), lambda i: (i // per_b, 0, 0)),
                  pl.BlockSpec((None, M, XA_HEADS * XA_DIM), lambda i: (i // per_b, 0, 0))],
        out_specs=pl.BlockSpec((tm, D), lambda i: (i, 0)),
        out_shape=jax.ShapeDtypeStruct((N, D), F32),
        compiler_params=_params("parallel"), name="mem_attn")(
            x, gain.reshape(1, D).astype(F32), wq, wo, mem_k, mem_v)


def _ffn(x, gain, w13, w2):
    N, D = x.shape
    FF = w2.shape[0]
    tm = min(ROW_TILE, N)
    chunk = 2 * LANES
    assert FF % chunk == 0

    def kern(x_ref, g_ref, w13_ref, w2_ref, o_ref):
        xf = x_ref[...]
        y = xf * lax.rsqrt(jnp.mean(xf * xf, axis=-1, keepdims=True) + EPS)
        a = (y * g_ref[...]).astype(BF16)
        acc = xf
        for c in range(0, FF, chunk):
            g = jnp.dot(a, w13_ref[:, c:c + chunk], preferred_element_type=F32)
            u = jnp.dot(a, w13_ref[:, FF + c:FF + c + chunk], preferred_element_type=F32)
            hdn = (g * _sigmoid(g) * u).astype(BF16)
            acc = acc + jnp.dot(hdn, w2_ref[c:c + chunk, :], preferred_element_type=F32)
        o_ref[...] = acc

    return pl.pallas_call(
        kern, grid=(N // tm,),
        in_specs=[pl.BlockSpec((tm, D), lambda i: (i, 0)),
                  pl.BlockSpec((1, D), lambda i: (0, 0)),
                  pl.BlockSpec(w13.shape, lambda i: (0, 0), pipeline_mode=pl.Buffered(1)),
                  pl.BlockSpec(w2.shape, lambda i: (0, 0), pipeline_mode=pl.Buffered(1))],
        out_specs=pl.BlockSpec((tm, D), lambda i: (i, 0)),
        out_shape=jax.ShapeDtypeStruct((N, D), F32),
        compiler_params=_params("parallel"), name="ffn")(
            x, gain.reshape(1, D).astype(F32), w13, w2)


def _final_norm(x, gain):
    N, D = x.shape
    tm = min(ROW_TILE, N)

    def kern(x_ref, g_ref, o_ref):
        xf = x_ref[...]
        y = xf * lax.rsqrt(jnp.mean(xf * xf, axis=-1, keepdims=True) + EPS)
        o_ref[...] = y * g_ref[...]

    return pl.pallas_call(
        kern, grid=(N // tm,),
        in_specs=[pl.BlockSpec((tm, D), lambda i: (i, 0)), pl.BlockSpec((1, D), lambda i: (0, 0))],
        out_specs=pl.BlockSpec((tm, D), lambda i: (i, 0)),
        out_shape=jax.ShapeDtypeStruct((N, D), F32),
        compiler_params=_params("parallel"), name="final_norm")(x, gain.reshape(1, D).astype(F32))


def _pad_cols(w, width):
    return jnp.pad(w, ((0, 0), (0, width - w.shape[1])))


def _even_mixer(x, gain, w_in, b_f, lam, subln, w_out, layer_idx, ropes, B, S):
    D = x.shape[1]
    blk = DIFF_HEADS * 2 * DIFF_QK
    w = _pad_cols(w_in, 6 * blk + LANES).astype(BF16)
    sc = DIFF_QK ** -0.5
    segs = [Seg(0, blk, BF16, "r64", sc), Seg(blk, blk, BF16, "r64"), Seg(2 * blk, blk, BF16),
            Seg(3 * blk, blk, BF16, None, sc), Seg(4 * blk, blk, BF16), Seg(5 * blk, blk, BF16),
            Seg(6 * blk, LANES, F32)]
    aq, ak, av, fq, fk, fv, fz = _linear([x], [w], segs, gain=gain, ropes=ropes, seq=S, name="even_in")
    lam_init = 0.8 - 0.6 * math.exp(-0.3 * layer_idx)
    oa = _flash(aq, ak, av, B=B, S=S, P=DIFF_HEADS, q_w=LANES, k_w=LANES, heads=_PAIR_HEADS,
                out_slabs=((0, 1),), kind="diff", out_dtype=BF16, lam=lam.astype(F32),
                subln=subln.astype(F32), lam_init=lam_init, name="diff_attn")
    cum = _logf_cumsum(fz, _pad_cols(b_f.reshape(1, -1), LANES).astype(F32), B, S)
    cum_t = cum.reshape(B, S, LANES)[:, :, :FOX_HEADS].transpose(0, 2, 1).reshape(B, FOX_HEADS // 2, 2, S)
    of = _flash(fq, fk, fv, B=B, S=S, P=FOX_HEADS // 2, q_w=LANES, k_w=LANES, heads=_PAIR_HEADS,
                out_slabs=((0, 1),), kind="fox", out_dtype=BF16, cum=cum, cum_t=cum_t, name="fox_attn")
    wo = w_out.astype(BF16)
    (y,) = _linear([oa, of], [wo[:blk], wo[blk:]], [Seg(0, D, F32)], residual=x, name="even_out")
    return y


def _odd_in_weight(w_in):
    d = NSA_DIM
    o = np.cumsum((0, NSA_HEADS * d) + (NSA_GROUPS * d,) * 6 + (NSA_HEADS * 3, MLA_Q_RANK, MLA_KV_RANK, MLA_ROPE))
    nq, kc, vc, ks, vs, kw, vw, gz, cq, ckv, kr = [w_in[:, o[j]:o[j + 1]] for j in range(11)]

    def dup(wg):
        return jnp.concatenate([wg[:, :d], wg[:, :d], wg[:, d:], wg[:, d:]], axis=1)

    zeros = lambda n: jnp.zeros((w_in.shape[0], n), w_in.dtype)
    kr_slab = jnp.concatenate([zeros(MLA_NOPE), kr, zeros(LANES - MLA_NOPE - MLA_ROPE)], axis=1)
    cols = [nq, kc, vc, dup(ks), dup(vs), dup(kw), dup(vw), _pad_cols(gz, LANES), cq, ckv, kr_slab]
    return jnp.concatenate(cols, axis=1).astype(BF16)


def _odd_mixer(x, gain, w_in, cmp_pos, cmp_w1, cmp_w2, q_norm, kv_norm, w_uq, w_ukv, w_out, ropes, B, S):
    D = x.shape[1]
    G, d = NSA_GROUPS, NSA_DIM
    w = _odd_in_weight(w_in)
    sc = d ** -0.5
    widths = [(NSA_HEADS * d, BF16, "r64", sc), (LANES, F32, "r64", 1.0), (LANES, F32, None, 1.0),
              (2 * LANES, BF16, "r64", 1.0), (2 * LANES, BF16, None, 1.0),
              (2 * LANES, BF16, "r64", 1.0), (2 * LANES, BF16, None, 1.0),
              (LANES, F32, None, 1.0), (MLA_Q_RANK, F32, None, 1.0), (MLA_KV_RANK, F32, None, 1.0),
              (LANES, F32, "mla", 1.0)]
    segs, start = [], 0
    for wd, dt, rp, s_ in widths:
        segs.append(Seg(start, wd, dt, rp, s_))
        start += wd
    q, kc, vc, ks, vs, kw, vw, gz, cq, ckv, kr = _linear(
        [x], [w], segs, gain=gain, ropes=ropes, seq=S, name="odd_in")

    R = S // CMP_STRIDE

    def to_rows(t):
        return t.reshape(B, R, CMP_STRIDE, G, d).transpose(0, 3, 1, 2, 4).reshape(B * G, R, CMP_STRIDE * d)

    rows = jnp.stack([to_rows(kc), to_rows(vc)])
    pos = cmp_pos.reshape(2, 1, CMP_LEN * d).astype(F32)
    w2d = jnp.concatenate([cmp_w2, cmp_w2], axis=-1).astype(BF16)
    cmp_kv = _nsa_compress(rows, pos, cmp_w1.astype(BF16), w2d)
    cs = np.arange(R) * CMP_STRIDE
    bs = np.arange(LANES) * SLC_LEN
    ov = ((cs[:, None] < bs[None, :] + SLC_LEN) & (cs[:, None] + CMP_LEN > bs[None, :])
          & (np.arange(R)[:, None] < R - 1) & (bs[None, :] < S))
    o_cmp, sel = _nsa_cmp_select(q, cmp_kv, jnp.asarray(ov.astype(np.float32), BF16), B=B, S=S)
    o_slc = _flash(q, ks, vs, B=B, S=S, P=G, q_w=2 * LANES, k_w=LANES, heads=_QUAD_HEADS,
                   out_slabs=((0, 1), (2, 3)), kind="slc", out_dtype=F32, sel=sel, name="nsa_slc")
    o_win = _flash(q, kw, vw, B=B, S=S, P=G, q_w=2 * LANES, k_w=LANES, heads=_QUAD_HEADS,
                   out_slabs=((0, 1), (2, 3)), kind="win", out_dtype=F32, name="nsa_win")
    ex = np.zeros((3, LANES, NSA_HEADS * d), np.float32)
    for hh in range(NSA_HEADS):
        for j in range(3):
            ex[j, hh * 3 + j, hh * d:(hh + 1) * d] = 1.0
    o_nsa = _nsa_combine(o_cmp, o_slc, o_win, gz, jnp.asarray(ex, BF16))

    H = MLA_HEADS
    qk = MLA_NOPE + MLA_ROPE
    wq_slab = jnp.pad(w_uq.reshape(MLA_Q_RANK, H, qk), ((0, 0), (0, 0), (0, LANES - qk)))
    wq_slab = wq_slab.reshape(MLA_Q_RANK, H * LANES).astype(BF16)
    (qm,) = _linear([cq], [wq_slab], [Seg(0, H * LANES, BF16, "mla", qk ** -0.5)],
                    gain=q_norm, ropes=ropes, seq=S, name="mla_q")
    wkv = w_ukv.reshape(MLA_KV_RANK, H, MLA_NOPE + MLA_V)
    wk = jnp.pad(wkv[:, :, :MLA_NOPE], ((0, 0), (0, 0), (0, LANES - MLA_NOPE)))
    wk = wk.reshape(MLA_KV_RANK, H * LANES).astype(BF16)
    wv = wkv[:, :, MLA_NOPE:].reshape(MLA_KV_RANK, H * MLA_V).astype(BF16)
    km, vm = _mla_kv(ckv, kv_norm, wk, wv, kr)
    o_mla = _flash(qm, km, vm, B=B, S=S, P=H // 2, q_w=2 * LANES, k_w=2 * LANES, heads=_WIDE_HEADS,
                   out_slabs=((0, 1),), kind="mla", out_dtype=BF16, name="mla_attn")
    wo = w_out.astype(BF16)
    half = NSA_HEADS * d
    (y,) = _linear([o_nsa, o_mla], [wo[:half], wo[half:]], [Seg(0, D, F32)], residual=x, name="odd_out")
    return y


def kernel(x, mem, mem_norm, norm_mix, norm_mem, norm_ffn, ev_w_in, ev_b_f, ev_lam, ev_subln, ev_w_out, od_w_in, nsa_cmp_pos, nsa_cmp_w1, nsa_cmp_w2, mla_q_norm, mla_kv_norm, mla_w_uq, mla_w_ukv, od_w_out, xa_wq, xa_wkv, xa_wo, ffn_w13, ffn_w2, final_norm):
    B, S, D = x.shape
    M = mem.shape[1]
    depth = norm_mix.shape[0]
    ropes = {
        "r64": (_rope_tables(S, NSA_DIM // 2, 0, LANES), NSA_DIM // 2),
        "mla": (_rope_tables(S, MLA_ROPE // 2, MLA_NOPE, MLA_NOPE + MLA_ROPE), MLA_ROPE // 2),
    }
    xa_w = XA_HEADS * XA_DIM
    h = x.reshape(B * S, D)
    mem2 = mem.reshape(B * M, D)
    for li in range(depth):
        j = li // 2
        if li % 2 == 0:
            h = _even_mixer(h, norm_mix[li], ev_w_in[j], ev_b_f[j], ev_lam[j], ev_subln[j], ev_w_out[j],
                            li, ropes, B, S)
        else:
            h = _odd_mixer(h, norm_mix[li], od_w_in[j], nsa_cmp_pos[j], nsa_cmp_w1[j], nsa_cmp_w2[j],
                           mla_q_norm[j], mla_kv_norm[j], mla_w_uq[j], mla_w_ukv[j], od_w_out[j],
                           ropes, B, S)
        mk, mv = _linear([mem2], [xa_wkv[li].astype(BF16)],
                         [Seg(0, xa_w, BF16), Seg(xa_w, xa_w, BF16)], gain=mem_norm, name="mem_kv")
        h = _mem_attn(h, norm_mem[li], xa_wq[li].astype(BF16), xa_wo[li].astype(BF16),
                      mk.reshape(B, M, xa_w), mv.reshape(B, M, xa_w), S=S)
        h = _ffn(h, norm_ffn[li], ffn_w13[li].astype(BF16), ffn_w2[li].astype(BF16))
    return _final_norm(h, final_norm).reshape(B, S, D)
```

```python
import math
from typing import NamedTuple, Optional

import numpy as np
import jax
import jax.numpy as jnp
from jax import lax
from jax.experimental import pallas as pl
from jax.experimental.pallas import tpu as pltpu

F32 = jnp.float32
BF16 = jnp.bfloat16

LANES = 128
HALF_LANES = LANES // 2
BF16_ROWS = 16
ROPE_THETA = 10000.0
EPS = 1e-6
NEG = -1e30
LOG2E = math.log2(math.e)
MASK_BIG = 2.0 ** 100

DIFF_HEADS = 4
DIFF_QK = 64
FOX_HEADS = 8
NSA_HEADS = 8
NSA_GROUPS = 2
NSA_DIM = 64
CMP_LEN = 32
CMP_STRIDE = 16
CMP_HIDDEN = 128
SLC_LEN = 64
SLC_TOPK = 16
WIN = 512
FORCE_SCORE = 1e4
MLA_HEADS = 8
MLA_NOPE = 64
MLA_ROPE = 32
MLA_V = 64
MLA_Q_RANK = 384
MLA_KV_RANK = 256
XA_HEADS = 4
XA_DIM = 128

VMEM_LIMIT = 56 * 1024 * 1024
ROW_TILE = 512
COL_CHUNK = 512
ATT_TILE = 256
KEY_TILE = 2 * ATT_TILE

_NT = (((1,), (1,)), ((), ()))


def _params(*sem):
    return pltpu.CompilerParams(dimension_semantics=sem, vmem_limit_bytes=VMEM_LIMIT)


def _sigmoid(x):
    return 1.0 / (1.0 + jnp.exp(-x))


def _split_bf16(x, terms):
    out = []
    r = x
    for _ in range(terms):
        h = r.astype(BF16)
        out.append(h)
        r = r - h.astype(F32)
    return out


def _rope_tables(S, half, lane_lo, lane_hi):
    pos = jnp.arange(S, dtype=F32)
    inv = 1.0 / (ROPE_THETA ** (jnp.arange(half, dtype=F32) / half))
    ang = pos[:, None] * inv[None, :]
    cos, sin = jnp.cos(ang), jnp.sin(ang)
    lane = np.arange(LANES)
    active = (lane >= lane_lo) & (lane < lane_hi)
    j = (lane - lane_lo) % (2 * half)
    lower = active & (j < half)
    upper = active & (j >= half)
    idx = j % half
    cos_t = jnp.where(active[None, :], cos[:, idx], 1.0)
    sin_a = jnp.where(lower[None, :], -sin[:, idx], 0.0)
    sin_b = jnp.where(upper[None, :], sin[:, idx], 0.0)
    return cos_t, sin_a, sin_b


class Seg(NamedTuple):
    start: int
    width: int
    dtype: object
    rope: Optional[str] = None
    scale: float = 1.0


def _linear(xs, ws, segs, *, gain=None, residual=None, ropes=None, seq=None, name="linear"):
    N = xs[0].shape[0]
    tm = min(ROW_TILE, N)
    assert N % tm == 0
    n_in = len(xs)
    has_gain = gain is not None
    has_res = residual is not None
    rope_keys = sorted({s.rope for s in segs if s.rope})
    halves = {k: ropes[k][1] for k in rope_keys}

    def kern(*refs):
        it = iter(refs)
        x_refs = [next(it) for _ in range(n_in)]
        w_refs = [next(it) for _ in range(n_in)]
        g_ref = next(it) if has_gain else None
        r_ref = next(it) if has_res else None
        tabs = {k: (next(it), next(it), next(it)) for k in rope_keys}
        o_refs = [next(it) for _ in segs]
        acts = []
        for j, xr in enumerate(x_refs):
            x = xr[...]
            if j == 0 and has_gain:
                xf = x.astype(F32)
                y = xf * lax.rsqrt(jnp.mean(xf * xf, axis=-1, keepdims=True) + EPS)
                acts.append((y * g_ref[...]).astype(BF16))
            else:
                acts.append(x.astype(BF16))
        for seg, o_ref in zip(segs, o_refs):
            for c0 in range(0, seg.width, COL_CHUNK):
                cw = min(COL_CHUNK, seg.width - c0)
                col = seg.start + c0
                acc = None
                for a, wr in zip(acts, w_refs):
                    d = jnp.dot(a, wr[:, col:col + cw], preferred_element_type=F32)
                    acc = d if acc is None else acc + d
                if has_res:
                    acc = acc + r_ref[:, col:col + cw]
                if seg.scale != 1.0:
                    acc = acc * seg.scale
                if seg.rope is None:
                    o_ref[:, c0:c0 + cw] = acc.astype(o_ref.dtype)
                else:
                    cos_r, sa_r, sb_r = tabs[seg.rope]
                    half = halves[seg.rope]
                    cos, sa, sb = cos_r[...], sa_r[...], sb_r[...]
                    for s0 in range(0, cw, LANES):
                        xs_ = acc[:, s0:s0 + LANES]
                        y = (xs_ * cos + pltpu.roll(xs_, LANES - half, 1) * sa
                             + pltpu.roll(xs_, half, 1) * sb)
                        o_ref[:, c0 + s0:c0 + s0 + LANES] = y.astype(o_ref.dtype)

    in_specs, args = [], []
    for x in xs:
        in_specs.append(pl.BlockSpec((tm, x.shape[1]), lambda i: (i, 0)))
        args.append(x)
    for w in ws:
        in_specs.append(pl.BlockSpec(w.shape, lambda i: (0, 0)))
        args.append(w)
    if has_gain:
        in_specs.append(pl.BlockSpec((1, gain.shape[-1]), lambda i: (0, 0)))
        args.append(gain.reshape(1, -1).astype(F32))
    if has_res:
        in_specs.append(pl.BlockSpec((tm, residual.shape[1]), lambda i: (i, 0)))
        args.append(residual)
    for k in rope_keys:
        assert seq % tm == 0
        nt = seq // tm
        for t in ropes[k][0]:
            in_specs.append(pl.BlockSpec((tm, LANES), lambda i, nt=nt: (i % nt, 0)))
            args.append(t)
    out_shape = [jax.ShapeDtypeStruct((N, s.width), s.dtype) for s in segs]
    out_specs = [pl.BlockSpec((tm, s.width), lambda i: (i, 0)) for s in segs]
    return pl.pallas_call(
        kern, grid=(N // tm,), in_specs=in_specs, out_specs=out_specs, out_shape=out_shape,
        compiler_params=_params("parallel"), name=name)(*args)


def _forget_key_terms(fz, b_f, B, S):
    ch = ATT_TILE
    place = np.zeros((3, LANES, LANES), np.float32)
    for h in range(FOX_HEADS):
        for j in range(3):
            place[j, h, 3 * h + j] = 1.0

    def kern(z_ref, b_ref, pl_ref, o_ref):
        r = lax.broadcasted_iota(jnp.int32, (ch, ch), 0)
        c = lax.broadcasted_iota(jnp.int32, (ch, ch), 1)
        tri = jnp.where(c <= r, 1.0, 0.0).astype(BF16)

        def body(j, carry):
            r0 = pl.multiple_of(j * ch, ch)
            z = z_ref[pl.ds(r0, ch), :] + b_ref[...]
            logf = -(jnp.maximum(-z, 0.0) + jnp.log1p(jnp.exp(-jnp.abs(z))))
            cs = carry
            for part in _split_bf16(logf, 3):
                cs = cs + jnp.dot(tri, part, preferred_element_type=F32)
            terms = jnp.zeros((ch, LANES), F32)
            for jj, part in enumerate(_split_bf16(cs * (-LOG2E), 3)):
                terms = terms + jnp.dot(part, pl_ref[jj], preferred_element_type=F32)
            o_ref[pl.ds(r0, ch), :] = terms.astype(o_ref.dtype)
            return cs[ch - 1:ch, :]

        lax.fori_loop(0, S // ch, body, jnp.zeros((1, LANES), F32))

    return pl.pallas_call(
        kern, grid=(B,),
        in_specs=[pl.BlockSpec((S, LANES), lambda b: (b, 0)),
                  pl.BlockSpec((1, LANES), lambda b: (0, 0)),
                  pl.BlockSpec(place.shape, lambda b: (0, 0, 0))],
        out_specs=pl.BlockSpec((S, LANES), lambda b: (b, 0)),
        out_shape=jax.ShapeDtypeStruct((B * S, LANES), BF16),
        compiler_params=_params("parallel"), name="forget_terms")(fz, b_f, jnp.asarray(place, BF16))


class Head(NamedTuple):
    q_off: int
    q_half: Optional[str]
    k_off: int
    v0: int
    v1: int


def _flash(q, k, vt, *, B, S, P, q_w, k_w, v_rows, heads, kind, out_dtype,
           ext=None, sel=None, lam=None, subln=None, lam_init=None, name="flash"):
    TQ, TK = ATT_TILE, KEY_TILE
    assert S % TK == 0
    nq = S // TQ
    nh = len(heads)
    rows = heads[0].v1 - heads[0].v0
    out_w = rows if kind == "diff" else nh * rows
    width = TK
    if kind == "win":
        width = WIN + TQ
        assert WIN % TQ == 0 and S >= width
    has_ext = kind in ("fox", "slc")

    def kern(*refs):
        it = iter(refs)
        q_ref, k_ref, v_ref = next(it), next(it), next(it)
        ext_ref = next(it) if has_ext else None
        sel_ref = next(it) if kind == "slc" else None
        lam_ref, sub_ref = (next(it), next(it)) if kind == "diff" else (None, None)
        o_ref, m_ref, acc_ref, sa_ref, sb_ref = next(it), next(it), next(it), next(it), next(it)
        p_id = pl.program_id(1)
        i = pl.program_id(2)

        lane = lax.broadcasted_iota(jnp.int32, (TQ, LANES), 1)
        lo = lane < HALF_LANES
        q_all = q_ref[...]
        qs = []
        for j, h in enumerate(heads):
            qh = q_all[:, h.q_off:h.q_off + LANES]
            if h.q_half == "lo":
                qh = jnp.where(lo, qh, jnp.zeros_like(qh))
            elif h.q_half == "hi":
                qh = jnp.where(lo, jnp.zeros_like(qh), qh)
            if kind == "fox":
                first = 3 * (nh * p_id + j)
                pick = jnp.where((lane >= first) & (lane < first + 3), 1.0, 0.0).astype(BF16)
                qh = jnp.concatenate([qh, pick], axis=1)
            if kind == "slc":
                drop = ((sel_ref[...].astype(F32) - 1.0) * MASK_BIG).astype(BF16)
                qh = jnp.concatenate([qh, drop], axis=1)
            qs.append(qh)

        m_ref[...] = jnp.full(m_ref.shape, NEG, F32)
        acc_ref[...] = jnp.zeros(acc_ref.shape, F32)

        def scores(k0, dst_ref):
            kx = k_ref[pl.ds(k0, width), :]
            ex = ext_ref[pl.ds(k0, width), :] if has_ext else None
            for hi, h in enumerate(heads):
                kk = kx[:, h.k_off:h.k_off + LANES]
                if ex is not None:
                    kk = jnp.concatenate([kk, ex], axis=1)
                dst_ref[hi] = lax.dot_general(kk, qs[hi], _NT, preferred_element_type=F32)

        def update(k0, src_ref, masked):
            ones = jnp.ones((BF16_ROWS, width), BF16)
            msk = None
            if masked:
                kpos = lax.broadcasted_iota(jnp.int32, (width, TQ), 0) + k0
                qpos = lax.broadcasted_iota(jnp.int32, (width, TQ), 1) + i * TQ
                msk = kpos <= qpos
                if kind == "win":
                    msk = msk & (qpos - kpos < WIN)
            for hi, h in enumerate(heads):
                st = src_ref[hi]
                if msk is not None:
                    st = jnp.where(msk, st, NEG)
                m_prev = m_ref[hi]
                m_new = jnp.maximum(m_prev, jnp.max(st, axis=0, keepdims=True))
                alpha = jnp.exp2(m_prev - m_new)
                pt = jnp.exp2(st - m_new).astype(BF16)
                lhs = jnp.concatenate([v_ref[h.v0:h.v1, pl.ds(k0, width)], ones], axis=0)
                acc_ref[hi] = alpha * acc_ref[hi] + jnp.dot(lhs, pt, preferred_element_type=F32)
                m_ref[hi] = m_new

        if kind == "win":
            k_first = pl.multiple_of(jnp.maximum(i * TQ - WIN, 0), TQ)
            scores(k_first, sa_ref)
            update(k_first, sa_ref, True)
        else:
            tile = lambda t: pl.multiple_of(t * TK, TK)
            n_full = i // (TK // TQ)
            n_pairs = n_full // 2
            scores(tile(0), sa_ref)

            def body(j, c):
                t = 2 * j
                scores(tile(t + 1), sb_ref)
                update(tile(t), sa_ref, False)
                scores(tile(t + 2), sa_ref)
                update(tile(t + 1), sb_ref, False)
                return c
            lax.fori_loop(0, n_pairs, body, 0)
            t0 = 2 * n_pairs

            @pl.when(t0 == n_full)
            def _():
                update(tile(t0), sa_ref, True)

            @pl.when(t0 != n_full)
            def _():
                scores(tile(t0 + 1), sb_ref)
                update(tile(t0), sa_ref, False)
                update(tile(t0 + 1), sb_ref, True)

        outs = []
        for hi in range(nh):
            acc = acc_ref[hi]
            outs.append(acc[:rows] / acc[rows:rows + 1])
        if kind == "diff":
            lm = lam_ref[...]
            la = jnp.sum(lm[0:1] * lm[1:2], axis=1, keepdims=True)
            lb = jnp.sum(lm[2:3] * lm[3:4], axis=1, keepdims=True)
            lam_full = jnp.exp(la) - jnp.exp(lb) + lam_init
            oa = jnp.transpose(outs[0] - lam_full * outs[1])
            y = oa * lax.rsqrt(jnp.mean(oa * oa, axis=-1, keepdims=True) + EPS)
            y = (y * sub_ref[...]) * (1.0 - lam_init)
        else:
            y = jnp.transpose(jnp.concatenate(outs, axis=0))
        o_ref[...] = y.astype(o_ref.dtype)

    in_specs = [
        pl.BlockSpec((TQ, q_w), lambda b, p, i: (b * nq + i, p)),
        pl.BlockSpec((S, k_w), lambda b, p, i: (b, p)),
        pl.BlockSpec((v_rows, S), lambda b, p, i: (p, b)),
    ]
    args = [q, k, vt]
    if kind == "fox":
        in_specs.append(pl.BlockSpec((S, LANES), lambda b, p, i: (b, 0)))
        args.append(ext)
    if kind == "slc":
        in_specs.append(pl.BlockSpec((S, LANES), lambda b, p, i: (0, 0)))
        in_specs.append(pl.BlockSpec((TQ, LANES), lambda b, p, i: (b * nq + i, p)))
        args += [ext, sel]
    if kind == "diff":
        in_specs.append(pl.BlockSpec(lam.shape, lambda b, p, i: (0, 0)))
        in_specs.append(pl.BlockSpec((1, LANES), lambda b, p, i: (0, 0)))
        args += [lam, subln.reshape(1, LANES)]
    return pl.pallas_call(
        kern, grid=(B, P, nq), in_specs=in_specs,
        out_specs=pl.BlockSpec((TQ, out_w), lambda b, p, i: (b * nq + i, p)),
        out_shape=jax.ShapeDtypeStruct((B * S, P * out_w), out_dtype),
        scratch_shapes=[pltpu.VMEM((nh, 1, TQ), F32),
                        pltpu.VMEM((nh, rows + BF16_ROWS, TQ), F32),
                        pltpu.VMEM((nh, width, TQ), F32),
                        pltpu.VMEM((nh, width, TQ), F32)],
        compiler_params=_params("parallel", "parallel", "arbitrary"), name=name)(*args)


_DIFF_HEADS = (Head(0, "lo", 0, 0, LANES), Head(0, "hi", 0, 0, LANES))
_PAIR_HEADS = (Head(0, "lo", 0, 0, HALF_LANES), Head(0, "hi", 0, HALF_LANES, LANES))
_QUAD_HEADS = (Head(0, "lo", 0, 0, HALF_LANES), Head(0, "hi", 0, 0, HALF_LANES),
               Head(LANES, "lo", 0, 0, HALF_LANES), Head(LANES, "hi", 0, 0, HALF_LANES))
_WIDE_HEADS = (Head(0, None, 0, 0, HALF_LANES), Head(LANES, None, LANES, HALF_LANES, LANES))


def _nsa_compress(rows, pos, w1, w2d):
    _, BG, R, K = rows.shape

    def kern(r_ref, pe_ref, w1_ref, w2_ref, o_ref):
        r = r_ref[...]
        pe = pe_ref[...]
        xa = (r + pe[:, :K]).astype(BF16)
        xb = (r + pe[:, K:]).astype(BF16)
        a = jnp.dot(xa, w1_ref[:K, :], preferred_element_type=F32)
        b = jnp.dot(xb, w1_ref[K:, :], preferred_element_type=F32)
        h = a + pltpu.roll(b, R - 1, 0)
        hs = h * _sigmoid(h)
        o_ref[...] = jnp.dot(hs.astype(BF16), w2_ref[...], preferred_element_type=F32)

    return pl.pallas_call(
        kern, grid=(2, BG),
        in_specs=[pl.BlockSpec((None, None, R, K), lambda t, g: (t, g, 0, 0)),
                  pl.BlockSpec((None, 1, 2 * K), lambda t, g: (t, 0, 0)),
                  pl.BlockSpec((None, 2 * K, CMP_HIDDEN), lambda t, g: (t, 0, 0)),
                  pl.BlockSpec((None, CMP_HIDDEN, LANES), lambda t, g: (t, 0, 0))],
        out_specs=pl.BlockSpec((None, None, R, LANES), lambda t, g: (t, g, 0, 0)),
        out_shape=jax.ShapeDtypeStruct((2, BG, R, LANES), F32),
        compiler_params=_params("parallel", "parallel"), name="nsa_compress")(rows, pos, w1, w2d)


def _nsa_cmp_select(q, cmp_kv, ov, *, B, S):
    T = ATT_TILE
    nq = S // T
    R = cmp_kv.shape[2]
    G = NSA_GROUPS
    n_sel = min(SLC_TOPK, S // SLC_LEN)

    def kern(q_ref, kc_ref, vc_ref, ov_ref, o_ref, sel_ref):
        i = pl.program_id(2)
        lane = lax.broadcasted_iota(jnp.int32, (T, LANES), 1)
        lo = lane < HALF_LANES
        tq = lax.broadcasted_iota(jnp.int32, (T, R), 0) + i * T
        cend = lax.broadcasted_iota(jnp.int32, (T, R), 1) * CMP_STRIDE + (CMP_LEN - 1)
        vis = cend <= tq
        kc = kc_ref[...].astype(BF16)
        vc = vc_ref[...].astype(BF16)
        q_all = q_ref[...]
        psum = jnp.zeros((T, R), F32)
        o_heads = []
        for hh in range(4):
            qh = q_all[:, (hh // 2) * LANES:(hh // 2 + 1) * LANES]
            qh = jnp.where(lo, qh, jnp.zeros_like(qh)) if hh % 2 == 0 else jnp.where(lo, jnp.zeros_like(qh), qh)
            s = lax.dot_general(qh, kc, _NT, preferred_element_type=F32)
            s = jnp.where(vis, s, -jnp.inf)
            m = jnp.max(s, axis=1, keepdims=True)
            e = jnp.exp2(s - jnp.where(m > -jnp.inf, m, 0.0))
            p = e / jnp.maximum(jnp.sum(e, axis=1, keepdims=True), 1e-30)
            o_heads.append(jnp.dot(p.astype(BF16), vc, preferred_element_type=F32))
            psum = psum + p
        for a in range(2):
            o_ref[:, a * LANES:(a + 1) * LANES] = jnp.where(lo, o_heads[2 * a], o_heads[2 * a + 1])
        imp = jnp.zeros((T, LANES), F32)
        for part in _split_bf16(psum, 2):
            imp = imp + jnp.dot(part, ov_ref[...], preferred_element_type=F32)
        tq1 = lax.broadcasted_iota(jnp.int32, (T, LANES), 0) + i * T
        cur = tq1 // SLC_LEN
        forced = (lane == 0) | (lane == cur) | (lane == cur - 1)
        work = jnp.where(forced, FORCE_SCORE, imp)
        work = jnp.where(lane > cur, NEG, work)
        lane_f = lane.astype(F32)
        sel = jnp.zeros((T, LANES), F32)
        for _ in range(n_sel):
            mx = jnp.max(work, axis=1, keepdims=True)
            first = jnp.min(jnp.where(work == mx, lane_f, float(LANES)), axis=1, keepdims=True)
            hit = lane_f == first
            sel = jnp.where(hit, 1.0, sel)
            work = jnp.where(hit, -jnp.inf, work)
        sel_ref[...] = sel.astype(sel_ref.dtype)

    return pl.pallas_call(
        kern, grid=(B, G, nq),
        in_specs=[pl.BlockSpec((T, 2 * LANES), lambda b, g, i: (b * nq + i, g)),
                  pl.BlockSpec((None, None, R, LANES), lambda b, g, i: (0, b * G + g, 0, 0)),
                  pl.BlockSpec((None, None, R, LANES), lambda b, g, i: (1, b * G + g, 0, 0)),
                  pl.BlockSpec((R, LANES), lambda b, g, i: (0, 0))],
        out_specs=[pl.BlockSpec((T, 2 * LANES), lambda b, g, i: (b * nq + i, g)),
                   pl.BlockSpec((T, LANES), lambda b, g, i: (b * nq + i, g))],
        out_shape=[jax.ShapeDtypeStruct((B * S, G * 2 * LANES), F32),
                   jax.ShapeDtypeStruct((B * S, G * LANES), BF16)],
        compiler_params=_params("parallel", "parallel", "parallel"), name="nsa_cmp_select")(
            q, cmp_kv, cmp_kv, ov)


def _nsa_combine(o_cmp, o_slc, o_win, gz, expand):
    N, C = o_cmp.shape
    tm = min(ROW_TILE, N)

    def kern(c_ref, s_ref, w_ref, g_ref, e_ref, o_ref):
        gate = _sigmoid(g_ref[...])
        parts = _split_bf16(gate, 2)
        acc = jnp.zeros((tm, C), F32)
        for j, br in enumerate((c_ref, s_ref, w_ref)):
            gj = sum(jnp.dot(part, e_ref[j], preferred_element_type=F32) for part in parts)
            acc = acc + gj * br[...]
        o_ref[...] = acc.astype(o_ref.dtype)

    row = pl.BlockSpec((tm, C), lambda i: (i, 0))
    return pl.pallas_call(
        kern, grid=(N // tm,),
        in_specs=[row, row, row, pl.BlockSpec((tm, LANES), lambda i: (i, 0)),
                  pl.BlockSpec(expand.shape, lambda i: (0, 0, 0))],
        out_specs=row, out_shape=jax.ShapeDtypeStruct((N, C), BF16),
        compiler_params=_params("parallel"), name="nsa_combine")(o_cmp, o_slc, o_win, gz, expand)


def _mla_kv(ckv, gain, wk, wv, kr):
    N, K = ckv.shape
    tm = min(ROW_TILE, N)
    H = MLA_HEADS

    def kern(c_ref, g_ref, wk_ref, wv_ref, kr_ref, k_ref, v_ref):
        xf = c_ref[...]
        y = xf * lax.rsqrt(jnp.mean(xf * xf, axis=-1, keepdims=True) + EPS)
        a = (y * g_ref[...]).astype(BF16)
        kr_ = kr_ref[...]
        for h in range(H):
            kh = jnp.dot(a, wk_ref[:, h * LANES:(h + 1) * LANES], preferred_element_type=F32)
            k_ref[:, h * LANES:(h + 1) * LANES] = (kh + kr_).astype(k_ref.dtype)
        v_ref[...] = jnp.dot(a, wv_ref[...], preferred_element_type=F32).astype(v_ref.dtype)

    return pl.pallas_call(
        kern, grid=(N // tm,),
        in_specs=[pl.BlockSpec((tm, K), lambda i: (i, 0)),
                  pl.BlockSpec((1, K), lambda i: (0, 0)),
                  pl.BlockSpec(wk.shape, lambda i: (0, 0)),
                  pl.BlockSpec(wv.shape, lambda i: (0, 0)),
                  pl.BlockSpec((tm, LANES), lambda i: (i, 0))],
        out_specs=[pl.BlockSpec((tm, H * LANES), lambda i: (i, 0)),
                   pl.BlockSpec((tm, H * MLA_V), lambda i: (i, 0))],
        out_shape=[jax.ShapeDtypeStruct((N, H * LANES), BF16),
                   jax.ShapeDtypeStruct((N, H * MLA_V), BF16)],
        compiler_params=_params("parallel"), name="mla_kv")(
            ckv, gain.reshape(1, K).astype(F32), wk, wv, kr)


def _mem_attn(x, gain, wq, wo, mem_k, mem_v, *, S):
    N, D = x.shape
    tm = min(ROW_TILE, S)
    M = mem_k.shape[1]
    per_b = S // tm
    scale = XA_DIM ** -0.5

    def kern(x_ref, g_ref, wq_ref, wo_ref, k_ref, v_ref, o_ref):
        xf = x_ref[...]
        y = xf * lax.rsqrt(jnp.mean(xf * xf, axis=-1, keepdims=True) + EPS)
        a = (y * g_ref[...]).astype(BF16)
        q = (jnp.dot(a, wq_ref[...], preferred_element_type=F32) * scale).astype(BF16)
        acc = xf
        for h in range(XA_HEADS):
            sl = slice(h * XA_DIM, (h + 1) * XA_DIM)
            s = lax.dot_general(q[:, sl], k_ref[:, sl], _NT, preferred_element_type=F32)
            e = jnp.exp(s - jnp.max(s, axis=1, keepdims=True))
            p = e / jnp.sum(e, axis=1, keepdims=True)
            oh = jnp.dot(p.astype(BF16), v_ref[:, sl], preferred_element_type=F32)
            acc = acc + jnp.dot(oh.astype(BF16), wo_ref[sl, :], preferred_element_type=F32)
        o_ref[...] = acc

    return pl.pallas_call(
        kern, grid=(N // tm,),
        in_specs=[pl.BlockSpec((tm, D), lambda i: (i, 0)),
                  pl.BlockSpec((1, D), lambda i: (0, 0)),
                  pl.BlockSpec(wq.shape, lambda i: (0, 0)),
                  pl.BlockSpec(wo.shape, lambda i: (0, 0)),
                  pl.BlockSpec((None, M, XA_HEADS * XA_DIM), lambda i: (i // per_b, 0, 0)),
                  pl.BlockSpec((None, M, XA_HEADS * XA_DIM), lambda i: (i // per_b, 0, 0))],
        out_specs=pl.BlockSpec((tm, D), lambda i: (i, 0)),
        out_shape=jax.ShapeDtypeStruct((N, D), F32),
        compiler_params=_params("parallel"), name="mem_attn")(
            x, gain.reshape(1, D).astype(F32), wq, wo, mem_k, mem_v)


def _ffn(x, gain, w13, w2):
    N, D = x.shape
    FF = w2.shape[0]
    tm = min(ROW_TILE, N)
    chunk = 2 * LANES
    assert FF % chunk == 0

    def kern(x_ref, g_ref, w13_ref, w2_ref, o_ref):
        xf = x_ref[...]
        y = xf * lax.rsqrt(jnp.mean(xf * xf, axis=-1, keepdims=True) + EPS)
        a = (y * g_ref[...]).astype(BF16)
        acc = xf
        for c in range(0, FF, chunk):
            g = jnp.dot(a, w13_ref[:, c:c + chunk], preferred_element_type=F32)
            u = jnp.dot(a, w13_ref[:, FF + c:FF + c + chunk], preferred_element_type=F32)
            hdn = (g * _sigmoid(g) * u).astype(BF16)
            acc = acc + jnp.dot(hdn, w2_ref[c:c + chunk, :], preferred_element_type=F32)
        o_ref[...] = acc

    return pl.pallas_call(
        kern, grid=(N // tm,),
        in_specs=[pl.BlockSpec((tm, D), lambda i: (i, 0)),
                  pl.BlockSpec((1, D), lambda i: (0, 0)),
                  pl.BlockSpec(w13.shape, lambda i: (0, 0), pipeline_mode=pl.Buffered(1)),
                  pl.BlockSpec(w2.shape, lambda i: (0, 0), pipeline_mode=pl.Buffered(1))],
        out_specs=pl.BlockSpec((tm, D), lambda i: (i, 0)),
        out_shape=jax.ShapeDtypeStruct((N, D), F32),
        compiler_params=_params("parallel"), name="ffn")(
            x, gain.reshape(1, D).astype(F32), w13, w2)


def _final_norm(x, gain):
    N, D = x.shape
    tm = min(ROW_TILE, N)

    def kern(x_ref, g_ref, o_ref):
        xf = x_ref[...]
        y = xf * lax.rsqrt(jnp.mean(xf * xf, axis=-1, keepdims=True) + EPS)
        o_ref[...] = y * g_ref[...]

    return pl.pallas_call(
        kern, grid=(N // tm,),
        in_specs=[pl.BlockSpec((tm, D), lambda i: (i, 0)), pl.BlockSpec((1, D), lambda i: (0, 0))],
        out_specs=pl.BlockSpec((tm, D), lambda i: (i, 0)),
        out_shape=jax.ShapeDtypeStruct((N, D), F32),
        compiler_params=_params("parallel"), name="final_norm")(x, gain.reshape(1, D).astype(F32))


def _pad_cols(w, width):
    return jnp.pad(w, ((0, 0), (0, width - w.shape[1])))


def _even_mixer(x, gain, w_in, b_f, lam, subln, w_out, layer_idx, ropes, B, S):
    D = x.shape[1]
    blk = DIFF_HEADS * 2 * DIFF_QK
    w = _pad_cols(w_in, 6 * blk + LANES).astype(BF16)
    sc = DIFF_QK ** -0.5 * LOG2E
    segs = [Seg(0, blk, BF16, "r64", sc), Seg(blk, blk, BF16, "r64"), Seg(2 * blk, blk, BF16),
            Seg(3 * blk, blk, BF16, None, sc), Seg(4 * blk, blk, BF16), Seg(5 * blk, blk, BF16),
            Seg(6 * blk, LANES, F32)]
    aq, ak, av, fq, fk, fv, fz = _linear([x], [w], segs, gain=gain, ropes=ropes, seq=S, name="even_in")
    lam_init = 0.8 - 0.6 * math.exp(-0.3 * layer_idx)
    oa = _flash(aq, ak, av.T, B=B, S=S, P=DIFF_HEADS, q_w=LANES, k_w=LANES, v_rows=LANES,
                heads=_DIFF_HEADS, kind="diff", out_dtype=BF16, lam=lam.astype(F32),
                subln=subln.astype(F32), lam_init=lam_init, name="diff_attn")
    terms = _forget_key_terms(fz, _pad_cols(b_f.reshape(1, -1), LANES).astype(F32), B, S)
    of = _flash(fq, fk, fv.T, B=B, S=S, P=FOX_HEADS // 2, q_w=LANES, k_w=LANES, v_rows=LANES,
                heads=_PAIR_HEADS, kind="fox", out_dtype=BF16, ext=terms, name="fox_attn")
    wo = w_out.astype(BF16)
    (y,) = _linear([oa, of], [wo[:blk], wo[blk:]], [Seg(0, D, F32)], residual=x, name="even_out")
    return y


def _odd_in_weight(w_in):
    d = NSA_DIM
    o = np.cumsum((0, NSA_HEADS * d) + (NSA_GROUPS * d,) * 6 + (NSA_HEADS * 3, MLA_Q_RANK, MLA_KV_RANK, MLA_ROPE))
    nq, kc, vc, ks, vs, kw, vw, gz, cq, ckv, kr = [w_in[:, o[j]:o[j + 1]] for j in range(11)]

    def dup(wg):
        return jnp.concatenate([wg[:, :d], wg[:, :d], wg[:, d:], wg[:, d:]], axis=1)

    zeros = lambda n: jnp.zeros((w_in.shape[0], n), w_in.dtype)
    kr_slab = jnp.concatenate([zeros(MLA_NOPE), kr, zeros(LANES - MLA_NOPE - MLA_ROPE)], axis=1)
    cols = [nq, kc, vc, dup(ks), vs, dup(kw), vw, _pad_cols(gz, LANES), cq, ckv, kr_slab]
    return jnp.concatenate(cols, axis=1).astype(BF16)


def _odd_mixer(x, gain, w_in, cmp_pos, cmp_w1, cmp_w2, q_norm, kv_norm, w_uq, w_ukv, w_out, ropes, B, S):
    D = x.shape[1]
    G, d = NSA_GROUPS, NSA_DIM
    w = _odd_in_weight(w_in)
    sc = d ** -0.5 * LOG2E
    widths = [(NSA_HEADS * d, BF16, "r64", sc), (LANES, F32, "r64", 1.0), (LANES, F32, None, 1.0),
              (2 * LANES, BF16, "r64", 1.0), (LANES, BF16, None, 1.0),
              (2 * LANES, BF16, "r64", 1.0), (LANES, BF16, None, 1.0),
              (LANES, F32, None, 1.0), (MLA_Q_RANK, F32, None, 1.0), (MLA_KV_RANK, F32, None, 1.0),
              (LANES, F32, "mla", 1.0)]
    segs, start = [], 0
    for wd, dt, rp, s_ in widths:
        segs.append(Seg(start, wd, dt, rp, s_))
        start += wd
    q, kc, vc, ks, vs, kw, vw, gz, cq, ckv, kr = _linear(
        [x], [w], segs, gain=gain, ropes=ropes, seq=S, name="odd_in")

    R = S // CMP_STRIDE

    def to_rows(t):
        return t.reshape(B, R, CMP_STRIDE, G, d).transpose(0, 3, 1, 2, 4).reshape(B * G, R, CMP_STRIDE * d)

    rows = jnp.stack([to_rows(kc), to_rows(vc)])
    pos = cmp_pos.reshape(2, 1, CMP_LEN * d).astype(F32)
    w2d = jnp.concatenate([cmp_w2, cmp_w2], axis=-1).astype(BF16)
    cmp_kv = _nsa_compress(rows, pos, cmp_w1.astype(BF16), w2d)
    cs = np.arange(R) * CMP_STRIDE
    bs = np.arange(LANES) * SLC_LEN
    ov = ((cs[:, None] < bs[None, :] + SLC_LEN) & (cs[:, None] + CMP_LEN > bs[None, :])
          & (np.arange(R)[:, None] < R - 1) & (bs[None, :] < S))
    o_cmp, sel = _nsa_cmp_select(q, cmp_kv, jnp.asarray(ov.astype(np.float32), BF16), B=B, S=S)
    onehot = (np.arange(S)[:, None] // SLC_LEN == np.arange(LANES)[None, :]).astype(np.float32)
    o_slc = _flash(q, ks, vs.T, B=B, S=S, P=G, q_w=2 * LANES, k_w=LANES, v_rows=HALF_LANES,
                   heads=_QUAD_HEADS, kind="slc", out_dtype=F32, ext=jnp.asarray(onehot, BF16),
                   sel=sel, name="nsa_slc")
    o_win = _flash(q, kw, vw.T, B=B, S=S, P=G, q_w=2 * LANES, k_w=LANES, v_rows=HALF_LANES,
                   heads=_QUAD_HEADS, kind="win", out_dtype=F32, name="nsa_win")
    ex = np.zeros((3, LANES, NSA_HEADS * d), np.float32)
    for hh in range(NSA_HEADS):
        for j in range(3):
            ex[j, hh * 3 + j, hh * d:(hh + 1) * d] = 1.0
    o_nsa = _nsa_combine(o_cmp, o_slc, o_win, gz, jnp.asarray(ex, BF16))

    H = MLA_HEADS
    qk = MLA_NOPE + MLA_ROPE
    wq_slab = jnp.pad(w_uq.reshape(MLA_Q_RANK, H, qk), ((0, 0), (0, 0), (0, LANES - qk)))
    wq_slab = wq_slab.reshape(MLA_Q_RANK, H * LANES).astype(BF16)
    (qm,) = _linear([cq], [wq_slab], [Seg(0, H * LANES, BF16, "mla", qk ** -0.5 * LOG2E)],
                    gain=q_norm, ropes=ropes, seq=S, name="mla_q")
    wkv = w_ukv.reshape(MLA_KV_RANK, H, MLA_NOPE + MLA_V)
    wk = jnp.pad(wkv[:, :, :MLA_NOPE], ((0, 0), (0, 0), (0, LANES - MLA_NOPE)))
    wk = wk.reshape(MLA_KV_RANK, H * LANES).astype(BF16)
    wv = wkv[:, :, MLA_NOPE:].reshape(MLA_KV_RANK, H * MLA_V).astype(BF16)
    km, vm = _mla_kv(ckv, kv_norm, wk, wv, kr)
    o_mla = _flash(qm, km, vm.T, B=B, S=S, P=H // 2, q_w=2 * LANES, k_w=2 * LANES, v_rows=LANES,
                   heads=_WIDE_HEADS, kind="mla", out_dtype=BF16, name="mla_attn")
    wo = w_out.astype(BF16)
    half = NSA_HEADS * d
    (y,) = _linear([o_nsa, o_mla], [wo[:half], wo[half:]], [Seg(0, D, F32)], residual=x, name="odd_out")
    return y


def kernel(x, mem, mem_norm, norm_mix, norm_mem, norm_ffn, ev_w_in, ev_b_f, ev_lam, ev_subln, ev_w_out, od_w_in, nsa_cmp_pos, nsa_cmp_w1, nsa_cmp_w2, mla_q_norm, mla_kv_norm, mla_w_uq, mla_w_ukv, od_w_out, xa_wq, xa_wkv, xa_wo, ffn_w13, ffn_w2, final_norm):
    B, S, D = x.shape
    M = mem.shape[1]
    depth = norm_mix.shape[0]
    ropes = {
        "r64": (_rope_tables(S, NSA_DIM // 2, 0, LANES), NSA_DIM // 2),
        "mla": (_rope_tables(S, MLA_ROPE // 2, MLA_NOPE, MLA_NOPE + MLA_ROPE), MLA_ROPE // 2),
    }
    xa_w = XA_HEADS * XA_DIM
    h = x.reshape(B * S, D)
    mem2 = mem.reshape(B * M, D)
    for li in range(depth):
        j = li // 2
        if li % 2 == 0:
            h = _even_mixer(h, norm_mix[li], ev_w_in[j], ev_b_f[j], ev_lam[j], ev_subln[j], ev_w_out[j],
                            li, ropes, B, S)
        else:
            h = _odd_mixer(h, norm_mix[li], od_w_in[j], nsa_cmp_pos[j], nsa_cmp_w1[j], nsa_cmp_w2[j],
                           mla_q_norm[j], mla_kv_norm[j], mla_w_uq[j], mla_w_ukv[j], od_w_out[j],
                           ropes, B, S)
        mk, mv = _linear([mem2], [xa_wkv[li].astype(BF16)],
                         [Seg(0, xa_w, BF16), Seg(xa_w, xa_w, BF16)], gain=mem_norm, name="mem_kv")
        h = _mem_attn(h, norm_mem[li], xa_wq[li].astype(BF16), xa_wo[li].astype(BF16),
                      mk.reshape(B, M, xa_w), mv.reshape(B, M, xa_w), S=S)
        h = _ffn(h, norm_ffn[li], ffn_w13[li].astype(BF16), ffn_w2[li].astype(BF16))
    return _final_norm(h, final_norm).reshape(B, S, D)
```

```python
import math
from typing import NamedTuple, Optional

import numpy as np
import jax
import jax.numpy as jnp
from jax import lax
from jax.experimental import pallas as pl
from jax.experimental.pallas import tpu as pltpu

F32 = jnp.float32
BF16 = jnp.bfloat16

LANES = 128
HALF_LANES = LANES // 2
BF16_ROWS = 16
ROPE_THETA = 10000.0
EPS = 1e-6
NEG = -1e30
LOG2E = math.log2(math.e)
MASK_BIG = 2.0 ** 100

DIFF_HEADS = 4
DIFF_QK = 64
FOX_HEADS = 8
NSA_HEADS = 8
NSA_GROUPS = 2
NSA_DIM = 64
CMP_LEN = 32
CMP_STRIDE = 16
CMP_HIDDEN = 128
SLC_LEN = 64
SLC_TOPK = 16
WIN = 512
FORCE_SCORE = 1e4
MLA_HEADS = 8
MLA_NOPE = 64
MLA_ROPE = 32
MLA_V = 64
MLA_Q_RANK = 384
MLA_KV_RANK = 256
XA_HEADS = 4
XA_DIM = 128

VMEM_LIMIT = 56 * 1024 * 1024
ROW_TILE = 512
COL_CHUNK = 512
ATT_TILE = 512
KEY_TILE = 512
SEL_TILE = 256
CUM_CHUNK = 256

_NT = (((1,), (1,)), ((), ()))


def _params(*sem):
    return pltpu.CompilerParams(dimension_semantics=sem, vmem_limit_bytes=VMEM_LIMIT)


def _sigmoid(x):
    return 1.0 / (1.0 + jnp.exp(-x))


def _split_bf16(x, terms):
    out = []
    r = x
    for _ in range(terms):
        h = r.astype(BF16)
        out.append(h)
        r = r - h.astype(F32)
    return out


def _rope_tables(S, half, lane_lo, lane_hi):
    pos = jnp.arange(S, dtype=F32)
    inv = 1.0 / (ROPE_THETA ** (jnp.arange(half, dtype=F32) / half))
    ang = pos[:, None] * inv[None, :]
    cos, sin = jnp.cos(ang), jnp.sin(ang)
    lane = np.arange(LANES)
    active = (lane >= lane_lo) & (lane < lane_hi)
    j = (lane - lane_lo) % (2 * half)
    lower = active & (j < half)
    upper = active & (j >= half)
    idx = j % half
    cos_t = jnp.where(active[None, :], cos[:, idx], 1.0)
    sin_a = jnp.where(lower[None, :], -sin[:, idx], 0.0)
    sin_b = jnp.where(upper[None, :], sin[:, idx], 0.0)
    return cos_t, sin_a, sin_b


class Seg(NamedTuple):
    start: int
    width: int
    dtype: object
    rope: Optional[str] = None
    scale: float = 1.0


def _linear(xs, ws, segs, *, gain=None, residual=None, ropes=None, seq=None, name="linear"):
    N = xs[0].shape[0]
    tm = min(ROW_TILE, N)
    assert N % tm == 0
    n_in = len(xs)
    has_gain = gain is not None
    has_res = residual is not None
    rope_keys = sorted({s.rope for s in segs if s.rope})
    halves = {k: ropes[k][1] for k in rope_keys}

    def kern(*refs):
        it = iter(refs)
        x_refs = [next(it) for _ in range(n_in)]
        w_refs = [next(it) for _ in range(n_in)]
        g_ref = next(it) if has_gain else None
        r_ref = next(it) if has_res else None
        tabs = {k: (next(it), next(it), next(it)) for k in rope_keys}
        o_refs = [next(it) for _ in segs]
        acts = []
        for j, xr in enumerate(x_refs):
            x = xr[...]
            if j == 0 and has_gain:
                xf = x.astype(F32)
                y = xf * lax.rsqrt(jnp.mean(xf * xf, axis=-1, keepdims=True) + EPS)
                acts.append((y * g_ref[...]).astype(BF16))
            else:
                acts.append(x.astype(BF16))
        for seg, o_ref in zip(segs, o_refs):
            for c0 in range(0, seg.width, COL_CHUNK):
                cw = min(COL_CHUNK, seg.width - c0)
                col = seg.start + c0
                acc = None
                for a, wr in zip(acts, w_refs):
                    d = jnp.dot(a, wr[:, col:col + cw], preferred_element_type=F32)
                    acc = d if acc is None else acc + d
                if has_res:
                    acc = acc + r_ref[:, col:col + cw]
                if seg.scale != 1.0:
                    acc = acc * seg.scale
                if seg.rope is None:
                    o_ref[:, c0:c0 + cw] = acc.astype(o_ref.dtype)
                else:
                    cos_r, sa_r, sb_r = tabs[seg.rope]
                    half = halves[seg.rope]
                    cos, sa, sb = cos_r[...], sa_r[...], sb_r[...]
                    for s0 in range(0, cw, LANES):
                        xs_ = acc[:, s0:s0 + LANES]
                        y = (xs_ * cos + pltpu.roll(xs_, LANES - half, 1) * sa
                             + pltpu.roll(xs_, half, 1) * sb)
                        o_ref[:, c0 + s0:c0 + s0 + LANES] = y.astype(o_ref.dtype)

    in_specs, args = [], []
    for x in xs:
        in_specs.append(pl.BlockSpec((tm, x.shape[1]), lambda i: (i, 0)))
        args.append(x)
    for w in ws:
        in_specs.append(pl.BlockSpec(w.shape, lambda i: (0, 0)))
        args.append(w)
    if has_gain:
        in_specs.append(pl.BlockSpec((1, gain.shape[-1]), lambda i: (0, 0)))
        args.append(gain.reshape(1, -1).astype(F32))
    if has_res:
        in_specs.append(pl.BlockSpec((tm, residual.shape[1]), lambda i: (i, 0)))
        args.append(residual)
    for k in rope_keys:
        assert seq % tm == 0
        nt = seq // tm
        for t in ropes[k][0]:
            in_specs.append(pl.BlockSpec((tm, LANES), lambda i, nt=nt: (i % nt, 0)))
            args.append(t)
    out_shape = [jax.ShapeDtypeStruct((N, s.width), s.dtype) for s in segs]
    out_specs = [pl.BlockSpec((tm, s.width), lambda i: (i, 0)) for s in segs]
    return pl.pallas_call(
        kern, grid=(N // tm,), in_specs=in_specs, out_specs=out_specs, out_shape=out_shape,
        compiler_params=_params("parallel"), name=name)(*args)


def _forget_key_terms(fz, b_f, B, S):
    ch = CUM_CHUNK
    place = np.zeros((3, LANES, LANES), np.float32)
    for h in range(FOX_HEADS):
        for j in range(3):
            place[j, h, 3 * h + j] = 1.0

    def kern(z_ref, b_ref, pl_ref, o_ref):
        r = lax.broadcasted_iota(jnp.int32, (ch, ch), 0)
        c = lax.broadcasted_iota(jnp.int32, (ch, ch), 1)
        tri = jnp.where(c <= r, 1.0, 0.0).astype(BF16)

        def body(j, carry):
            r0 = pl.multiple_of(j * ch, ch)
            z = z_ref[pl.ds(r0, ch), :] + b_ref[...]
            logf = -(jnp.maximum(-z, 0.0) + jnp.log1p(jnp.exp(-jnp.abs(z))))
            cs = carry
            for part in _split_bf16(logf, 3):
                cs = cs + jnp.dot(tri, part, preferred_element_type=F32)
            terms = jnp.zeros((ch, LANES), F32)
            for jj, part in enumerate(_split_bf16(cs * (-LOG2E), 3)):
                terms = terms + jnp.dot(part, pl_ref[jj], preferred_element_type=F32)
            o_ref[pl.ds(r0, ch), :] = terms.astype(o_ref.dtype)
            return cs[ch - 1:ch, :]

        lax.fori_loop(0, S // ch, body, jnp.zeros((1, LANES), F32))

    return pl.pallas_call(
        kern, grid=(B,),
        in_specs=[pl.BlockSpec((S, LANES), lambda b: (b, 0)),
                  pl.BlockSpec((1, LANES), lambda b: (0, 0)),
                  pl.BlockSpec(place.shape, lambda b: (0, 0, 0))],
        out_specs=pl.BlockSpec((S, LANES), lambda b: (b, 0)),
        out_shape=jax.ShapeDtypeStruct((B * S, LANES), BF16),
        compiler_params=_params("parallel"), name="forget_terms")(fz, b_f, jnp.asarray(place, BF16))


class Head(NamedTuple):
    q_off: int
    q_half: Optional[str]
    k_off: int
    v0: int
    v1: int


def _flash(q, k, vt, *, B, S, P, q_w, k_w, v_rows, heads, kind, out_dtype,
           ext=None, sel=None, lam=None, subln=None, lam_init=None, name="flash"):
    TQ, TK = ATT_TILE, KEY_TILE
    assert S % TK == 0
    nq = S // TQ
    nh = len(heads)
    rows = heads[0].v1 - heads[0].v0
    out_w = nh // 2 * rows if kind == "diff" else nh * rows
    width = TK
    if kind == "win":
        width = WIN + TQ
        assert WIN % TQ == 0 and S >= width
    has_ext = kind in ("fox", "slc")

    def kern(*refs):
        it = iter(refs)
        q_ref, k_ref, v_ref = next(it), next(it), next(it)
        ext_ref = next(it) if has_ext else None
        sel_ref = next(it) if kind == "slc" else None
        lam_ref, sub_ref = (next(it), next(it)) if kind == "diff" else (None, None)
        o_ref, m_ref, acc_ref, sa_ref, sb_ref = next(it), next(it), next(it), next(it), next(it)
        p_id = pl.program_id(1)
        i = pl.program_id(2)

        lane = lax.broadcasted_iota(jnp.int32, (TQ, LANES), 1)
        lo = lane < HALF_LANES
        q_all = q_ref[...]
        if kind == "slc":
            drop = ((sel_ref[...].astype(F32) - 1.0) * MASK_BIG).astype(BF16)
        qs = []
        for j, h in enumerate(heads):
            qh = q_all[:, h.q_off:h.q_off + LANES]
            if h.q_half == "lo":
                qh = jnp.where(lo, qh, jnp.zeros_like(qh))
            elif h.q_half == "hi":
                qh = jnp.where(lo, jnp.zeros_like(qh), qh)
            if kind == "fox":
                first = 3 * (nh * p_id + j)
                pick = jnp.where((lane >= first) & (lane < first + 3), 1.0, 0.0).astype(BF16)
                qh = jnp.concatenate([qh, pick], axis=1)
            if kind == "slc":
                qh = jnp.concatenate([qh, drop], axis=1)
            qs.append(qh)

        m_ref[...] = jnp.full(m_ref.shape, NEG, F32)
        acc_ref[...] = jnp.zeros(acc_ref.shape, F32)

        def scores(k0, dst_ref):
            kx = k_ref[pl.ds(k0, width), :]
            ex = ext_ref[pl.ds(k0, width), :] if has_ext else None
            for hi, h in enumerate(heads):
                kk = kx[:, h.k_off:h.k_off + LANES]
                if ex is not None:
                    kk = jnp.concatenate([kk, ex], axis=1)
                dst_ref[hi] = lax.dot_general(kk, qs[hi], _NT, preferred_element_type=F32)

        def update(k0, src_ref, masked):
            ones = jnp.ones((BF16_ROWS, width), BF16)
            msk = None
            if masked:
                kpos = lax.broadcasted_iota(jnp.int32, (width, TQ), 0) + k0
                qpos = lax.broadcasted_iota(jnp.int32, (width, TQ), 1) + i * TQ
                msk = kpos <= qpos
                if kind == "win":
                    msk = msk & (qpos - kpos < WIN)
            for hi, h in enumerate(heads):
                st = src_ref[hi]
                if msk is not None:
                    st = jnp.where(msk, st, NEG)
                m_prev = m_ref[hi]
                m_new = jnp.maximum(m_prev, jnp.max(st, axis=0, keepdims=True))
                alpha = jnp.exp2(m_prev - m_new)
                pt = jnp.exp2(st - m_new).astype(BF16)
                lhs = jnp.concatenate([v_ref[h.v0:h.v1, pl.ds(k0, width)], ones], axis=0)
                acc_ref[hi] = alpha * acc_ref[hi] + jnp.dot(lhs, pt, preferred_element_type=F32)
                m_ref[hi] = m_new

        if kind == "win":
            k_first = pl.multiple_of(jnp.maximum(i * TQ - WIN, 0), TQ)
            scores(k_first, sa_ref)
            update(k_first, sa_ref, True)
        else:
            tile = lambda t: pl.multiple_of(t * TK, TK)
            n_full = i // (TK // TQ)
            n_pairs = n_full // 2
            scores(tile(0), sa_ref)

            def body(j, c):
                t = 2 * j
                scores(tile(t + 1), sb_ref)
                update(tile(t), sa_ref, False)
                scores(tile(t + 2), sa_ref)
                update(tile(t + 1), sb_ref, False)
                return c
            lax.fori_loop(0, n_pairs, body, 0)
            t0 = 2 * n_pairs

            @pl.when(t0 == n_full)
            def _():
                update(tile(t0), sa_ref, True)

            @pl.when(t0 != n_full)
            def _():
                scores(tile(t0 + 1), sb_ref)
                update(tile(t0), sa_ref, False)
                update(tile(t0 + 1), sb_ref, True)

        outs = []
        for hi in range(nh):
            acc = acc_ref[hi]
            outs.append(acc[:rows] / acc[rows:rows + 1])
        if kind == "diff":
            lm = lam_ref[...]
            la = jnp.sum(lm[0:1] * lm[1:2], axis=1, keepdims=True)
            lb = jnp.sum(lm[2:3] * lm[3:4], axis=1, keepdims=True)
            lam_full = jnp.exp(la) - jnp.exp(lb) + lam_init
            for g in range(nh // 2):
                oa = jnp.transpose(outs[2 * g] - lam_full * outs[2 * g + 1])
                y = oa * lax.rsqrt(jnp.mean(oa * oa, axis=-1, keepdims=True) + EPS)
                y = (y * sub_ref[...]) * (1.0 - lam_init)
                o_ref[:, g * rows:(g + 1) * rows] = y.astype(o_ref.dtype)
        else:
            o_ref[...] = jnp.transpose(jnp.concatenate(outs, axis=0)).astype(o_ref.dtype)

    in_specs = [
        pl.BlockSpec((TQ, q_w), lambda b, p, i: (b * nq + i, p)),
        pl.BlockSpec((S, k_w), lambda b, p, i: (b, p)),
        pl.BlockSpec((v_rows, S), lambda b, p, i: (p, b)),
    ]
    args = [q, k, vt]
    if kind == "fox":
        in_specs.append(pl.BlockSpec((S, LANES), lambda b, p, i: (b, 0)))
        args.append(ext)
    if kind == "slc":
        in_specs.append(pl.BlockSpec((S, LANES), lambda b, p, i: (0, 0)))
        in_specs.append(pl.BlockSpec((TQ, LANES), lambda b, p, i: (b * nq + i, p)))
        args += [ext, sel]
    if kind == "diff":
        in_specs.append(pl.BlockSpec(lam.shape, lambda b, p, i: (0, 0)))
        in_specs.append(pl.BlockSpec((1, LANES), lambda b, p, i: (0, 0)))
        args += [lam, subln.reshape(1, LANES)]
    return pl.pallas_call(
        kern, grid=(B, P, nq), in_specs=in_specs,
        out_specs=pl.BlockSpec((TQ, out_w), lambda b, p, i: (b * nq + i, p)),
        out_shape=jax.ShapeDtypeStruct((B * S, P * out_w), out_dtype),
        scratch_shapes=[pltpu.VMEM((nh, 1, TQ), F32),
                        pltpu.VMEM((nh, rows + BF16_ROWS, TQ), F32),
                        pltpu.VMEM((nh, width, TQ), F32),
                        pltpu.VMEM((nh, 8 if kind == "win" else width, TQ), F32)],
        compiler_params=_params("parallel", "parallel", "arbitrary"), name=name)(*args)


_DIFF_HEADS = tuple(Head(s * LANES, half, s * LANES, s * LANES, (s + 1) * LANES)
                    for s in range(2) for half in ("lo", "hi"))
_PAIR_HEADS = tuple(Head(s * LANES, half, s * LANES, (2 * s + j) * HALF_LANES, (2 * s + j + 1) * HALF_LANES)
                    for s in range(2) for j, half in enumerate(("lo", "hi")))
_QUAD_HEADS = tuple(Head(s * LANES, half, 0, 0, HALF_LANES) for s in range(2) for half in ("lo", "hi"))
_WIDE_HEADS = tuple(Head(j * LANES, None, j * LANES, j * HALF_LANES, (j + 1) * HALF_LANES) for j in range(4))


def _nsa_compress(rows, pos, w1, w2d):
    _, BG, R, K = rows.shape

    def kern(r_ref, pe_ref, w1_ref, w2_ref, o_ref):
        r = r_ref[...]
        pe = pe_ref[...]
        xa = (r + pe[:, :K]).astype(BF16)
        xb = (r + pe[:, K:]).astype(BF16)
        a = jnp.dot(xa, w1_ref[:K, :], preferred_element_type=F32)
        b = jnp.dot(xb, w1_ref[K:, :], preferred_element_type=F32)
        h = a + pltpu.roll(b, R - 1, 0)
        hs = h * _sigmoid(h)
        o_ref[...] = jnp.dot(hs.astype(BF16), w2_ref[...], preferred_element_type=F32)

    return pl.pallas_call(
        kern, grid=(2, BG),
        in_specs=[pl.BlockSpec((None, None, R, K), lambda t, g: (t, g, 0, 0)),
                  pl.BlockSpec((None, 1, 2 * K), lambda t, g: (t, 0, 0)),
                  pl.BlockSpec((None, 2 * K, CMP_HIDDEN), lambda t, g: (t, 0, 0)),
                  pl.BlockSpec((None, CMP_HIDDEN, LANES), lambda t, g: (t, 0, 0))],
        out_specs=pl.BlockSpec((None, None, R, LANES), lambda t, g: (t, g, 0, 0)),
        out_shape=jax.ShapeDtypeStruct((2, BG, R, LANES), F32),
        compiler_params=_params("parallel", "parallel"), name="nsa_compress")(rows, pos, w1, w2d)


def _nsa_cmp_select(q, cmp_kv, ov, *, B, S):
    T = SEL_TILE
    nq = S // T
    R = cmp_kv.shape[2]
    G = NSA_GROUPS
    n_sel = min(SLC_TOPK, S // SLC_LEN)

    def kern(q_ref, kc_ref, vc_ref, ov_ref, o_ref, sel_ref):
        i = pl.program_id(2)
        lane = lax.broadcasted_iota(jnp.int32, (T, LANES), 1)
        lo = lane < HALF_LANES
        tq = lax.broadcasted_iota(jnp.int32, (T, R), 0) + i * T
        cend = lax.broadcasted_iota(jnp.int32, (T, R), 1) * CMP_STRIDE + (CMP_LEN - 1)
        vis = cend <= tq
        kc = kc_ref[...].astype(BF16)
        vc = vc_ref[...].astype(BF16)
        q_all = q_ref[...]
        psum = jnp.zeros((T, R), F32)
        o_heads = []
        for hh in range(4):
            qh = q_all[:, (hh // 2) * LANES:(hh // 2 + 1) * LANES]
            qh = jnp.where(lo, qh, jnp.zeros_like(qh)) if hh % 2 == 0 else jnp.where(lo, jnp.zeros_like(qh), qh)
            s = lax.dot_general(qh, kc, _NT, preferred_element_type=F32)
            s = jnp.where(vis, s, -jnp.inf)
            m = jnp.max(s, axis=1, keepdims=True)
            e = jnp.exp2(s - jnp.where(m > -jnp.inf, m, 0.0))
            p = e / jnp.maximum(jnp.sum(e, axis=1, keepdims=True), 1e-30)
            o_heads.append(jnp.dot(p.astype(BF16), vc, preferred_element_type=F32))
            psum = psum + p
        for a in range(2):
            o_ref[:, a * LANES:(a + 1) * LANES] = jnp.where(lo, o_heads[2 * a], o_heads[2 * a + 1])
        imp = jnp.zeros((T, LANES), F32)
        for part in _split_bf16(psum, 2):
            imp = imp + jnp.dot(part, ov_ref[...], preferred_element_type=F32)
        tq1 = lax.broadcasted_iota(jnp.int32, (T, LANES), 0) + i * T
        cur = tq1 // SLC_LEN
        forced = (lane == 0) | (lane == cur) | (lane == cur - 1)
        work = jnp.where(forced, FORCE_SCORE, imp)
        work = jnp.where(lane > cur, NEG, work)
        lane_f = lane.astype(F32)
        sel = jnp.zeros((T, LANES), F32)
        for _ in range(n_sel):
            mx = jnp.max(work, axis=1, keepdims=True)
            first = jnp.min(jnp.where(work == mx, lane_f, float(LANES)), axis=1, keepdims=True)
            hit = lane_f == first
            sel = jnp.where(hit, 1.0, sel)
            work = jnp.where(hit, -jnp.inf, work)
        sel_ref[...] = sel.astype(sel_ref.dtype)

    return pl.pallas_call(
        kern, grid=(B, G, nq),
        in_specs=[pl.BlockSpec((T, 2 * LANES), lambda b, g, i: (b * nq + i, g)),
                  pl.BlockSpec((None, None, R, LANES), lambda b, g, i: (0, b * G + g, 0, 0)),
                  pl.BlockSpec((None, None, R, LANES), lambda b, g, i: (1, b * G + g, 0, 0)),
                  pl.BlockSpec((R, LANES), lambda b, g, i: (0, 0))],
        out_specs=[pl.BlockSpec((T, 2 * LANES), lambda b, g, i: (b * nq + i, g)),
                   pl.BlockSpec((T, LANES), lambda b, g, i: (b * nq + i, g))],
        out_shape=[jax.ShapeDtypeStruct((B * S, G * 2 * LANES), F32),
                   jax.ShapeDtypeStruct((B * S, G * LANES), BF16)],
        compiler_params=_params("parallel", "parallel", "parallel"), name="nsa_cmp_select")(
            q, cmp_kv, cmp_kv, ov)


def _nsa_combine(o_cmp, o_slc, o_win, gz, expand):
    N, C = o_cmp.shape
    tm = min(ROW_TILE, N)

    def kern(c_ref, s_ref, w_ref, g_ref, e_ref, o_ref):
        gate = _sigmoid(g_ref[...])
        parts = _split_bf16(gate, 2)
        acc = jnp.zeros((tm, C), F32)
        for j, br in enumerate((c_ref, s_ref, w_ref)):
            gj = sum(jnp.dot(part, e_ref[j], preferred_element_type=F32) for part in parts)
            acc = acc + gj * br[...]
        o_ref[...] = acc.astype(o_ref.dtype)

    row = pl.BlockSpec((tm, C), lambda i: (i, 0))
    return pl.pallas_call(
        kern, grid=(N // tm,),
        in_specs=[row, row, row, pl.BlockSpec((tm, LANES), lambda i: (i, 0)),
                  pl.BlockSpec(expand.shape, lambda i: (0, 0, 0))],
        out_specs=row, out_shape=jax.ShapeDtypeStruct((N, C), BF16),
        compiler_params=_params("parallel"), name="nsa_combine")(o_cmp, o_slc, o_win, gz, expand)


def _mla_kv(ckv, gain, wk, wv, kr):
    N, K = ckv.shape
    tm = min(ROW_TILE, N)
    H = MLA_HEADS

    def kern(c_ref, g_ref, wk_ref, wv_ref, kr_ref, k_ref, v_ref):
        xf = c_ref[...]
        y = xf * lax.rsqrt(jnp.mean(xf * xf, axis=-1, keepdims=True) + EPS)
        a = (y * g_ref[...]).astype(BF16)
        kr_ = kr_ref[...]
        for h in range(H):
            kh = jnp.dot(a, wk_ref[:, h * LANES:(h + 1) * LANES], preferred_element_type=F32)
            k_ref[:, h * LANES:(h + 1) * LANES] = (kh + kr_).astype(k_ref.dtype)
        v_ref[...] = jnp.dot(a, wv_ref[...], preferred_element_type=F32).astype(v_ref.dtype)

    return pl.pallas_call(
        kern, grid=(N // tm,),
        in_specs=[pl.BlockSpec((tm, K), lambda i: (i, 0)),
                  pl.BlockSpec((1, K), lambda i: (0, 0)),
                  pl.BlockSpec(wk.shape, lambda i: (0, 0)),
                  pl.BlockSpec(wv.shape, lambda i: (0, 0)),
                  pl.BlockSpec((tm, LANES), lambda i: (i, 0))],
        out_specs=[pl.BlockSpec((tm, H * LANES), lambda i: (i, 0)),
                   pl.BlockSpec((tm, H * MLA_V), lambda i: (i, 0))],
        out_shape=[jax.ShapeDtypeStruct((N, H * LANES), BF16),
                   jax.ShapeDtypeStruct((N, H * MLA_V), BF16)],
        compiler_params=_params("parallel"), name="mla_kv")(
            ckv, gain.reshape(1, K).astype(F32), wk, wv, kr)


def _mem_attn(x, gain, wq, wo, mem_k, mem_v, *, S):
    N, D = x.shape
    tm = min(ROW_TILE, S)
    M = mem_k.shape[1]
    per_b = S // tm
    scale = XA_DIM ** -0.5

    def kern(x_ref, g_ref, wq_ref, wo_ref, k_ref, v_ref, o_ref):
        xf = x_ref[...]
        y = xf * lax.rsqrt(jnp.mean(xf * xf, axis=-1, keepdims=True) + EPS)
        a = (y * g_ref[...]).astype(BF16)
        q = (jnp.dot(a, wq_ref[...], preferred_element_type=F32) * scale).astype(BF16)
        acc = xf
        for h in range(XA_HEADS):
            sl = slice(h * XA_DIM, (h + 1) * XA_DIM)
            s = lax.dot_general(q[:, sl], k_ref[:, sl], _NT, preferred_element_type=F32)
            e = jnp.exp(s - jnp.max(s, axis=1, keepdims=True))
            p = e / jnp.sum(e, axis=1, keepdims=True)
            oh = jnp.dot(p.astype(BF16), v_ref[:, sl], preferred_element_type=F32)
            acc = acc + jnp.dot(oh.astype(BF16), wo_ref[sl, :], preferred_element_type=F32)
        o_ref[...] = acc

    return pl.pallas_call(
        kern, grid=(N // tm,),
        in_specs=[pl.BlockSpec((tm, D), lambda i: (i, 0)),
                  pl.BlockSpec((1, D), lambda i: (0, 0)),
                  pl.BlockSpec(wq.shape, lambda i: (0, 0)),
                  pl.BlockSpec(wo.shape, lambda i: (0, 0)),
                  pl.BlockSpec((None, M, XA_HEADS * XA_DIM), lambda i: (i // per_b, 0, 0)),
                  pl.BlockSpec((None, M, XA_HEADS * XA_DIM), lambda i: (i // per_b, 0, 0))],
        out_specs=pl.BlockSpec((tm, D), lambda i: (i, 0)),
        out_shape=jax.ShapeDtypeStruct((N, D), F32),
        compiler_params=_params("parallel"), name="mem_attn")(
            x, gain.reshape(1, D).astype(F32), wq, wo, mem_k, mem_v)


def _ffn(x, gain, w13, w2):
    N, D = x.shape
    FF = w2.shape[0]
    tm = min(ROW_TILE, N)
    chunk = 2 * LANES
    assert FF % chunk == 0

    def kern(x_ref, g_ref, w13_ref, w2_ref, o_ref):
        xf = x_ref[...]
        y = xf * lax.rsqrt(jnp.mean(xf * xf, axis=-1, keepdims=True) + EPS)
        a = (y * g_ref[...]).astype(BF16)
        acc = xf
        for c in range(0, FF, chunk):
            g = jnp.dot(a, w13_ref[:, c:c + chunk], preferred_element_type=F32)
            u = jnp.dot(a, w13_ref[:, FF + c:FF + c + chunk], preferred_element_type=F32)
            hdn = (g * _sigmoid(g) * u).astype(BF16)
            acc = acc + jnp.dot(hdn, w2_ref[c:c + chunk, :], preferred_element_type=F32)
        o_ref[...] = acc

    return pl.pallas_call(
        kern, grid=(N // tm,),
        in_specs=[pl.BlockSpec((tm, D), lambda i: (i, 0)),
                  pl.BlockSpec((1, D), lambda i: (0, 0)),
                  pl.BlockSpec(w13.shape, lambda i: (0, 0), pipeline_mode=pl.Buffered(1)),
                  pl.BlockSpec(w2.shape, lambda i: (0, 0), pipeline_mode=pl.Buffered(1))],
        out_specs=pl.BlockSpec((tm, D), lambda i: (i, 0)),
        out_shape=jax.ShapeDtypeStruct((N, D), F32),
        compiler_params=_params("parallel"), name="ffn")(
            x, gain.reshape(1, D).astype(F32), w13, w2)


def _final_norm(x, gain):
    N, D = x.shape
    tm = min(ROW_TILE, N)

    def kern(x_ref, g_ref, o_ref):
        xf = x_ref[...]
        y = xf * lax.rsqrt(jnp.mean(xf * xf, axis=-1, keepdims=True) + EPS)
        o_ref[...] = y * g_ref[...]

    return pl.pallas_call(
        kern, grid=(N // tm,),
        in_specs=[pl.BlockSpec((tm, D), lambda i: (i, 0)), pl.BlockSpec((1, D), lambda i: (0, 0))],
        out_specs=pl.BlockSpec((tm, D), lambda i: (i, 0)),
        out_shape=jax.ShapeDtypeStruct((N, D), F32),
        compiler_params=_params("parallel"), name="final_norm")(x, gain.reshape(1, D).astype(F32))


def _pad_cols(w, width):
    return jnp.pad(w, ((0, 0), (0, width - w.shape[1])))


def _even_mixer(x, gain, w_in, b_f, lam, subln, w_out, layer_idx, ropes, B, S):
    D = x.shape[1]
    blk = DIFF_HEADS * 2 * DIFF_QK
    w = _pad_cols(w_in, 6 * blk + LANES).astype(BF16)
    sc = DIFF_QK ** -0.5 * LOG2E
    segs = [Seg(0, blk, BF16, "r64", sc), Seg(blk, blk, BF16, "r64"), Seg(2 * blk, blk, BF16),
            Seg(3 * blk, blk, BF16, None, sc), Seg(4 * blk, blk, BF16), Seg(5 * blk, blk, BF16),
            Seg(6 * blk, LANES, F32)]
    aq, ak, av, fq, fk, fv, fz = _linear([x], [w], segs, gain=gain, ropes=ropes, seq=S, name="even_in")
    lam_init = 0.8 - 0.6 * math.exp(-0.3 * layer_idx)
    oa = _flash(aq, ak, av.T, B=B, S=S, P=DIFF_HEADS // 2, q_w=2 * LANES, k_w=2 * LANES, v_rows=2 * LANES,
                heads=_DIFF_HEADS, kind="diff", out_dtype=BF16, lam=lam.astype(F32),
                subln=subln.astype(F32), lam_init=lam_init, name="diff_attn")
    terms = _forget_key_terms(fz, _pad_cols(b_f.reshape(1, -1), LANES).astype(F32), B, S)
    of = _flash(fq, fk, fv.T, B=B, S=S, P=FOX_HEADS // 4, q_w=2 * LANES, k_w=2 * LANES, v_rows=2 * LANES,
                heads=_PAIR_HEADS, kind="fox", out_dtype=BF16, ext=terms, name="fox_attn")
    wo = w_out.astype(BF16)
    (y,) = _linear([oa, of], [wo[:blk], wo[blk:]], [Seg(0, D, F32)], residual=x, name="even_out")
    return y


def _odd_in_weight(w_in):
    d = NSA_DIM
    o = np.cumsum((0, NSA_HEADS * d) + (NSA_GROUPS * d,) * 6 + (NSA_HEADS * 3, MLA_Q_RANK, MLA_KV_RANK, MLA_ROPE))
    nq, kc, vc, ks, vs, kw, vw, gz, cq, ckv, kr = [w_in[:, o[j]:o[j + 1]] for j in range(11)]

    def dup(wg):
        return jnp.concatenate([wg[:, :d], wg[:, :d], wg[:, d:], wg[:, d:]], axis=1)

    zeros = lambda n: jnp.zeros((w_in.shape[0], n), w_in.dtype)
    kr_slab = jnp.concatenate([zeros(MLA_NOPE), kr, zeros(LANES - MLA_NOPE - MLA_ROPE)], axis=1)
    cols = [nq, kc, vc, dup(ks), vs, dup(kw), vw, _pad_cols(gz, LANES), cq, ckv, kr_slab]
    return jnp.concatenate(cols, axis=1).astype(BF16)


def _odd_mixer(x, gain, w_in, cmp_pos, cmp_w1, cmp_w2, q_norm, kv_norm, w_uq, w_ukv, w_out, ropes, B, S):
    D = x.shape[1]
    G, d = NSA_GROUPS, NSA_DIM
    w = _odd_in_weight(w_in)
    sc = d ** -0.5 * LOG2E
    widths = [(NSA_HEADS * d, BF16, "r64", sc), (LANES, F32, "r64", 1.0), (LANES, F32, None, 1.0),
              (2 * LANES, BF16, "r64", 1.0), (LANES, BF16, None, 1.0),
              (2 * LANES, BF16, "r64", 1.0), (LANES, BF16, None, 1.0),
              (LANES, F32, None, 1.0), (MLA_Q_RANK, F32, None, 1.0), (MLA_KV_RANK, F32, None, 1.0),
              (LANES, F32, "mla", 1.0)]
    segs, start = [], 0
    for wd, dt, rp, s_ in widths:
        segs.append(Seg(start, wd, dt, rp, s_))
        start += wd
    q, kc, vc, ks, vs, kw, vw, gz, cq, ckv, kr = _linear(
        [x], [w], segs, gain=gain, ropes=ropes, seq=S, name="odd_in")

    R = S // CMP_STRIDE

    def to_rows(t):
        return t.reshape(B, R, CMP_STRIDE, G, d).transpose(0, 3, 1, 2, 4).reshape(B * G, R, CMP_STRIDE * d)

    rows = jnp.stack([to_rows(kc), to_rows(vc)])
    pos = cmp_pos.reshape(2, 1, CMP_LEN * d).astype(F32)
    w2d = jnp.concatenate([cmp_w2, cmp_w2], axis=-1).astype(BF16)
    cmp_kv = _nsa_compress(rows, pos, cmp_w1.astype(BF16), w2d)
    cs = np.arange(R) * CMP_STRIDE
    bs = np.arange(LANES) * SLC_LEN
    ov = ((cs[:, None] < bs[None, :] + SLC_LEN) & (cs[:, None] + CMP_LEN > bs[None, :])
          & (np.arange(R)[:, None] < R - 1) & (bs[None, :] < S))
    o_cmp, sel = _nsa_cmp_select(q, cmp_kv, jnp.asarray(ov.astype(np.float32), BF16), B=B, S=S)
    onehot = (np.arange(S)[:, None] // SLC_LEN == np.arange(LANES)[None, :]).astype(np.float32)
    o_slc = _flash(q, ks, vs.T, B=B, S=S, P=G, q_w=2 * LANES, k_w=LANES, v_rows=HALF_LANES,
                   heads=_QUAD_HEADS, kind="slc", out_dtype=F32, ext=jnp.asarray(onehot, BF16),
                   sel=sel, name="nsa_slc")
    o_win = _flash(q, kw, vw.T, B=B, S=S, P=G, q_w=2 * LANES, k_w=LANES, v_rows=HALF_LANES,
                   heads=_QUAD_HEADS, kind="win", out_dtype=F32, name="nsa_win")
    ex = np.zeros((3, LANES, NSA_HEADS * d), np.float32)
    for hh in range(NSA_HEADS):
        for j in range(3):
            ex[j, hh * 3 + j, hh * d:(hh + 1) * d] = 1.0
    o_nsa = _nsa_combine(o_cmp, o_slc, o_win, gz, jnp.asarray(ex, BF16))

    H = MLA_HEADS
    qk = MLA_NOPE + MLA_ROPE
    wq_slab = jnp.pad(w_uq.reshape(MLA_Q_RANK, H, qk), ((0, 0), (0, 0), (0, LANES - qk)))
    wq_slab = wq_slab.reshape(MLA_Q_RANK, H * LANES).astype(BF16)
    (qm,) = _linear([cq], [wq_slab], [Seg(0, H * LANES, BF16, "mla", qk ** -0.5 * LOG2E)],
                    gain=q_norm, ropes=ropes, seq=S, name="mla_q")
    wkv = w_ukv.reshape(MLA_KV_RANK, H, MLA_NOPE + MLA_V)
    wk = jnp.pad(wkv[:, :, :MLA_NOPE], ((0, 0), (0, 0), (0, LANES - MLA_NOPE)))
    wk = wk.reshape(MLA_KV_RANK, H * LANES).astype(BF16)
    wv = wkv[:, :, MLA_NOPE:].reshape(MLA_KV_RANK, H * MLA_V).astype(BF16)
    km, vm = _mla_kv(ckv, kv_norm, wk, wv, kr)
    o_mla = _flash(qm, km, vm.T, B=B, S=S, P=H // 4, q_w=4 * LANES, k_w=4 * LANES, v_rows=2 * LANES,
                   heads=_WIDE_HEADS, kind="mla", out_dtype=BF16, name="mla_attn")
    wo = w_out.astype(BF16)
    half = NSA_HEADS * d
    (y,) = _linear([o_nsa, o_mla], [wo[:half], wo[half:]], [Seg(0, D, F32)], residual=x, name="odd_out")
    return y


def kernel(x, mem, mem_norm, norm_mix, norm_mem, norm_ffn, ev_w_in, ev_b_f, ev_lam, ev_subln, ev_w_out, od_w_in, nsa_cmp_pos, nsa_cmp_w1, nsa_cmp_w2, mla_q_norm, mla_kv_norm, mla_w_uq, mla_w_ukv, od_w_out, xa_wq, xa_wkv, xa_wo, ffn_w13, ffn_w2, final_norm):
    B, S, D = x.shape
    M = mem.shape[1]
    depth = norm_mix.shape[0]
    ropes = {
        "r64": (_rope_tables(S, NSA_DIM // 2, 0, LANES), NSA_DIM // 2),
        "mla": (_rope_tables(S, MLA_ROPE // 2, MLA_NOPE, MLA_NOPE + MLA_ROPE), MLA_ROPE // 2),
    }
    xa_w = XA_HEADS * XA_DIM
    h = x.reshape(B * S, D)
    mem2 = mem.reshape(B * M, D)
    for li in range(depth):
        j = li // 2
        if li % 2 == 0:
            h = _even_mixer(h, norm_mix[li], ev_w_in[j], ev_b_f[j], ev_lam[j], ev_subln[j], ev_w_out[j],
                            li, ropes, B, S)
        else:
            h = _odd_mixer(h, norm_mix[li], od_w_in[j], nsa_cmp_pos[j], nsa_cmp_w1[j], nsa_cmp_w2[j],
                           mla_q_norm[j], mla_kv_norm[j], mla_w_uq[j], mla_w_ukv[j], od_w_out[j],
                           ropes, B, S)
        mk, mv = _linear([mem2], [xa_wkv[li].astype(BF16)],
                         [Seg(0, xa_w, BF16), Seg(xa_w, xa_w, BF16)], gain=mem_norm, name="mem_kv")
        h = _mem_attn(h, norm_mem[li], xa_wq[li].astype(BF16), xa_wo[li].astype(BF16),
                      mk.reshape(B, M, xa_w), mv.reshape(B, M, xa_w), S=S)
        h = _ffn(h, norm_ffn[li], ffn_w13[li].astype(BF16), ffn_w2[li].astype(BF16))
    return _final_norm(h, final_norm).reshape(B, S, D)
```

```python
import math
from typing import NamedTuple, Optional

import numpy as np
import jax
import jax.numpy as jnp
from jax import lax
from jax.experimental import pallas as pl
from jax.experimental.pallas import tpu as pltpu

F32 = jnp.float32
BF16 = jnp.bfloat16

LANES = 128
HALF_LANES = LANES // 2
BF16_ROWS = 16
ROPE_THETA = 10000.0
EPS = 1e-6
NEG = -1e30
LOG2E = math.log2(math.e)
MASK_BIG = 2.0 ** 100

DIFF_HEADS = 4
DIFF_QK = 64
FOX_HEADS = 8
NSA_HEADS = 8
NSA_GROUPS = 2
NSA_DIM = 64
CMP_LEN = 32
CMP_STRIDE = 16
CMP_HIDDEN = 128
SLC_LEN = 64
SLC_TOPK = 16
WIN = 512
FORCE_SCORE = 1e4
MLA_HEADS = 8
MLA_NOPE = 64
MLA_ROPE = 32
MLA_V = 64
MLA_Q_RANK = 384
MLA_KV_RANK = 256
XA_HEADS = 4
XA_DIM = 128

VMEM_LIMIT = 56 * 1024 * 1024
ROW_TILE = 512
COL_CHUNK = 512
ATT_TILE = 512
KEY_TILE = 512
SEL_TILE = 512
CUM_CHUNK = 256

_NT = (((1,), (1,)), ((), ()))


def _params(*sem):
    return pltpu.CompilerParams(dimension_semantics=sem, vmem_limit_bytes=VMEM_LIMIT)


def _sigmoid(x):
    return 1.0 / (1.0 + jnp.exp(-x))


def _split_bf16(x, terms):
    out = []
    r = x
    for _ in range(terms):
        h = r.astype(BF16)
        out.append(h)
        r = r - h.astype(F32)
    return out


def _rope_tables(S, half, lane_lo, lane_hi):
    pos = jnp.arange(S, dtype=F32)
    inv = 1.0 / (ROPE_THETA ** (jnp.arange(half, dtype=F32) / half))
    ang = pos[:, None] * inv[None, :]
    cos, sin = jnp.cos(ang), jnp.sin(ang)
    lane = np.arange(LANES)
    active = (lane >= lane_lo) & (lane < lane_hi)
    j = (lane - lane_lo) % (2 * half)
    lower = active & (j < half)
    upper = active & (j >= half)
    idx = j % half
    cos_t = jnp.where(active[None, :], cos[:, idx], 1.0)
    sin_a = jnp.where(lower[None, :], -sin[:, idx], 0.0)
    sin_b = jnp.where(upper[None, :], sin[:, idx], 0.0)
    return cos_t, sin_a, sin_b


class Seg(NamedTuple):
    start: int
    width: int
    dtype: object
    rope: Optional[str] = None
    scale: float = 1.0


def _linear(xs, ws, segs, *, gain=None, residual=None, ropes=None, seq=None, name="linear"):
    N = xs[0].shape[0]
    tm = min(ROW_TILE, N)
    assert N % tm == 0
    n_in = len(xs)
    has_gain = gain is not None
    has_res = residual is not None
    rope_keys = sorted({s.rope for s in segs if s.rope})
    halves = {k: ropes[k][1] for k in rope_keys}

    def kern(*refs):
        it = iter(refs)
        x_refs = [next(it) for _ in range(n_in)]
        w_refs = [next(it) for _ in range(n_in)]
        g_ref = next(it) if has_gain else None
        r_ref = next(it) if has_res else None
        tabs = {k: (next(it), next(it), next(it)) for k in rope_keys}
        o_refs = [next(it) for _ in segs]
        acts = []
        for j, xr in enumerate(x_refs):
            x = xr[...]
            if j == 0 and has_gain:
                xf = x.astype(F32)
                y = xf * lax.rsqrt(jnp.mean(xf * xf, axis=-1, keepdims=True) + EPS)
                acts.append((y * g_ref[...]).astype(BF16))
            else:
                acts.append(x.astype(BF16))
        for seg, o_ref in zip(segs, o_refs):
            for c0 in range(0, seg.width, COL_CHUNK):
                cw = min(COL_CHUNK, seg.width - c0)
                col = seg.start + c0
                acc = None
                for a, wr in zip(acts, w_refs):
                    d = jnp.dot(a, wr[:, col:col + cw], preferred_element_type=F32)
                    acc = d if acc is None else acc + d
                if has_res:
                    acc = acc + r_ref[:, col:col + cw]
                if seg.scale != 1.0:
                    acc = acc * seg.scale
                if seg.rope is None:
                    o_ref[:, c0:c0 + cw] = acc.astype(o_ref.dtype)
                else:
                    cos_r, sa_r, sb_r = tabs[seg.rope]
                    half = halves[seg.rope]
                    cos, sa, sb = cos_r[...], sa_r[...], sb_r[...]
                    for s0 in range(0, cw, LANES):
                        xs_ = acc[:, s0:s0 + LANES]
                        y = (xs_ * cos + pltpu.roll(xs_, LANES - half, 1) * sa
                             + pltpu.roll(xs_, half, 1) * sb)
                        o_ref[:, c0 + s0:c0 + s0 + LANES] = y.astype(o_ref.dtype)

    in_specs, args = [], []
    for x in xs:
        in_specs.append(pl.BlockSpec((tm, x.shape[1]), lambda i: (i, 0)))
        args.append(x)
    for w in ws:
        in_specs.append(pl.BlockSpec(w.shape, lambda i: (0, 0)))
        args.append(w)
    if has_gain:
        in_specs.append(pl.BlockSpec((1, gain.shape[-1]), lambda i: (0, 0)))
        args.append(gain.reshape(1, -1).astype(F32))
    if has_res:
        in_specs.append(pl.BlockSpec((tm, residual.shape[1]), lambda i: (i, 0)))
        args.append(residual)
    for k in rope_keys:
        assert seq % tm == 0
        nt = seq // tm
        for t in ropes[k][0]:
            in_specs.append(pl.BlockSpec((tm, LANES), lambda i, nt=nt: (i % nt, 0)))
            args.append(t)
    out_shape = [jax.ShapeDtypeStruct((N, s.width), s.dtype) for s in segs]
    out_specs = [pl.BlockSpec((tm, s.width), lambda i: (i, 0)) for s in segs]
    return pl.pallas_call(
        kern, grid=(N // tm,), in_specs=in_specs, out_specs=out_specs, out_shape=out_shape,
        compiler_params=_params("parallel"), name=name)(*args)


def _forget_key_terms(fz, b_f, B, S):
    ch = CUM_CHUNK
    place = np.zeros((3, LANES, LANES), np.float32)
    for h in range(FOX_HEADS):
        for j in range(3):
            place[j, h, 3 * h + j] = 1.0

    def kern(z_ref, b_ref, pl_ref, o_ref):
        r = lax.broadcasted_iota(jnp.int32, (ch, ch), 0)
        c = lax.broadcasted_iota(jnp.int32, (ch, ch), 1)
        tri = jnp.where(c <= r, 1.0, 0.0).astype(BF16)

        def body(j, carry):
            r0 = pl.multiple_of(j * ch, ch)
            z = z_ref[pl.ds(r0, ch), :] + b_ref[...]
            logf = -(jnp.maximum(-z, 0.0) + jnp.log1p(jnp.exp(-jnp.abs(z))))
            cs = carry
            for part in _split_bf16(logf, 3):
                cs = cs + jnp.dot(tri, part, preferred_element_type=F32)
            terms = jnp.zeros((ch, LANES), F32)
            for jj, part in enumerate(_split_bf16(cs * (-LOG2E), 3)):
                terms = terms + jnp.dot(part, pl_ref[jj], preferred_element_type=F32)
            o_ref[pl.ds(r0, ch), :] = terms.astype(o_ref.dtype)
            return cs[ch - 1:ch, :]

        lax.fori_loop(0, S // ch, body, jnp.zeros((1, LANES), F32))

    return pl.pallas_call(
        kern, grid=(B,),
        in_specs=[pl.BlockSpec((S, LANES), lambda b: (b, 0)),
                  pl.BlockSpec((1, LANES), lambda b: (0, 0)),
                  pl.BlockSpec(place.shape, lambda b: (0, 0, 0))],
        out_specs=pl.BlockSpec((S, LANES), lambda b: (b, 0)),
        out_shape=jax.ShapeDtypeStruct((B * S, LANES), BF16),
        compiler_params=_params("parallel"), name="forget_terms")(fz, b_f, jnp.asarray(place, BF16))


class Head(NamedTuple):
    q_off: int
    q_half: Optional[str]
    k_off: int
    v0: int
    v1: int


def _flash(q, k, vt, *, B, S, P, q_w, k_w, v_rows, heads, kind, out_dtype,
           ext=None, sel=None, lam=None, subln=None, lam_init=None, name="flash"):
    TQ, TK = ATT_TILE, KEY_TILE
    assert S % TK == 0
    nq = S // TQ
    nh = len(heads)
    rows = heads[0].v1 - heads[0].v0
    out_w = nh // 2 * rows if kind == "diff" else nh * rows
    width = TK
    if kind == "win":
        width = WIN + TQ
        assert WIN % TQ == 0 and S >= width
    has_ext = kind in ("fox", "slc")

    def kern(*refs):
        it = iter(refs)
        q_ref, k_ref, v_ref = next(it), next(it), next(it)
        ext_ref = next(it) if has_ext else None
        sel_ref = next(it) if kind == "slc" else None
        lam_ref, sub_ref = (next(it), next(it)) if kind == "diff" else (None, None)
        o_ref, m_ref, acc_ref, sa_ref, sb_ref = next(it), next(it), next(it), next(it), next(it)
        p_id = pl.program_id(1)
        i = pl.program_id(2)

        lane = lax.broadcasted_iota(jnp.int32, (TQ, LANES), 1)
        lo = lane < HALF_LANES
        q_all = q_ref[...]
        if kind == "slc":
            drop = ((sel_ref[...].astype(F32) - 1.0) * MASK_BIG).astype(BF16)
        qs = []
        for j, h in enumerate(heads):
            qh = q_all[:, h.q_off:h.q_off + LANES]
            if h.q_half == "lo":
                qh = jnp.where(lo, qh, jnp.zeros_like(qh))
            elif h.q_half == "hi":
                qh = jnp.where(lo, jnp.zeros_like(qh), qh)
            if kind == "fox":
                first = 3 * (nh * p_id + j)
                pick = jnp.where((lane >= first) & (lane < first + 3), 1.0, 0.0).astype(BF16)
                qh = jnp.concatenate([qh, pick], axis=1)
            if kind == "slc":
                qh = jnp.concatenate([qh, drop], axis=1)
            qs.append(qh)

        m_ref[...] = jnp.full(m_ref.shape, NEG, F32)
        acc_ref[...] = jnp.zeros(acc_ref.shape, F32)

        def scores(k0, dst_ref):
            kx = k_ref[pl.ds(k0, width), :]
            ex = ext_ref[pl.ds(k0, width), :] if has_ext else None
            for hi, h in enumerate(heads):
                kk = kx[:, h.k_off:h.k_off + LANES]
                if ex is not None:
                    kk = jnp.concatenate([kk, ex], axis=1)
                dst_ref[hi] = lax.dot_general(kk, qs[hi], _NT, preferred_element_type=F32)

        def update(k0, src_ref, masked):
            ones = jnp.ones((BF16_ROWS, width), BF16)
            msk = None
            if masked:
                kpos = lax.broadcasted_iota(jnp.int32, (width, TQ), 0) + k0
                qpos = lax.broadcasted_iota(jnp.int32, (width, TQ), 1) + i * TQ
                msk = kpos <= qpos
                if kind == "win":
                    msk = msk & (qpos - kpos < WIN)
            for hi, h in enumerate(heads):
                st = src_ref[hi]
                if msk is not None:
                    st = jnp.where(msk, st, NEG)
                m_prev = m_ref[hi]
                m_new = jnp.maximum(m_prev, jnp.max(st, axis=0, keepdims=True))
                alpha = jnp.exp2(m_prev - m_new)
                pt = jnp.exp2(st - m_new).astype(BF16)
                lhs = jnp.concatenate([v_ref[h.v0:h.v1, pl.ds(k0, width)], ones], axis=0)
                acc_ref[hi] = alpha * acc_ref[hi] + jnp.dot(lhs, pt, preferred_element_type=F32)
                m_ref[hi] = m_new

        if kind == "win":
            k_first = pl.multiple_of(jnp.maximum(i * TQ - WIN, 0), TQ)
            scores(k_first, sa_ref)
            update(k_first, sa_ref, True)
        else:
            tile = lambda t: pl.multiple_of(t * TK, TK)
            n_full = i // (TK // TQ)
            n_pairs = n_full // 2
            scores(tile(0), sa_ref)

            def body(j, c):
                t = 2 * j
                scores(tile(t + 1), sb_ref)
                update(tile(t), sa_ref, False)
                scores(tile(t + 2), sa_ref)
                update(tile(t + 1), sb_ref, False)
                return c
            lax.fori_loop(0, n_pairs, body, 0)
            t0 = 2 * n_pairs

            @pl.when(t0 == n_full)
            def _():
                update(tile(t0), sa_ref, True)

            @pl.when(t0 != n_full)
            def _():
                scores(tile(t0 + 1), sb_ref)
                update(tile(t0), sa_ref, False)
                update(tile(t0 + 1), sb_ref, True)

        outs = []
        for hi in range(nh):
            acc = acc_ref[hi]
            outs.append(acc[:rows] / acc[rows:rows + 1])
        if kind == "diff":
            lm = lam_ref[...]
            la = jnp.sum(lm[0:1] * lm[1:2], axis=1, keepdims=True)
            lb = jnp.sum(lm[2:3] * lm[3:4], axis=1, keepdims=True)
            lam_full = jnp.exp(la) - jnp.exp(lb) + lam_init
            for g in range(nh // 2):
                oa = jnp.transpose(outs[2 * g] - lam_full * outs[2 * g + 1])
                y = oa * lax.rsqrt(jnp.mean(oa * oa, axis=-1, keepdims=True) + EPS)
                y = (y * sub_ref[...]) * (1.0 - lam_init)
                o_ref[:, g * rows:(g + 1) * rows] = y.astype(o_ref.dtype)
        else:
            o_ref[...] = jnp.transpose(jnp.concatenate(outs, axis=0)).astype(o_ref.dtype)

    in_specs = [
        pl.BlockSpec((TQ, q_w), lambda b, p, i: (b * nq + i, p)),
        pl.BlockSpec((S, k_w), lambda b, p, i: (b, p)),
        pl.BlockSpec((v_rows, S), lambda b, p, i: (p, b)),
    ]
    args = [q, k, vt]
    if kind == "fox":
        in_specs.append(pl.BlockSpec((S, LANES), lambda b, p, i: (b, 0)))
        args.append(ext)
    if kind == "slc":
        in_specs.append(pl.BlockSpec((S, LANES), lambda b, p, i: (0, 0)))
        in_specs.append(pl.BlockSpec((TQ, LANES), lambda b, p, i: (b * nq + i, p)))
        args += [ext, sel]
    if kind == "diff":
        in_specs.append(pl.BlockSpec(lam.shape, lambda b, p, i: (0, 0)))
        in_specs.append(pl.BlockSpec((1, LANES), lambda b, p, i: (0, 0)))
        args += [lam, subln.reshape(1, LANES)]
    return pl.pallas_call(
        kern, grid=(B, P, nq), in_specs=in_specs,
        out_specs=pl.BlockSpec((TQ, out_w), lambda b, p, i: (b * nq + i, p)),
        out_shape=jax.ShapeDtypeStruct((B * S, P * out_w), out_dtype),
        scratch_shapes=[pltpu.VMEM((nh, 1, TQ), F32),
                        pltpu.VMEM((nh, rows + BF16_ROWS, TQ), F32),
                        pltpu.VMEM((nh, width, TQ), F32),
                        pltpu.VMEM((nh, 8 if kind == "win" else width, TQ), F32)],
        compiler_params=_params("parallel", "parallel", "arbitrary"), name=name)(*args)


_DIFF_HEADS = tuple(Head(s * LANES, half, s * LANES, s * LANES, (s + 1) * LANES)
                    for s in range(2) for half in ("lo", "hi"))
_PAIR_HEADS = tuple(Head(s * LANES, half, s * LANES, (2 * s + j) * HALF_LANES, (2 * s + j + 1) * HALF_LANES)
                    for s in range(2) for j, half in enumerate(("lo", "hi")))
_QUAD_HEADS = tuple(Head(s * LANES, half, 0, 0, HALF_LANES) for s in range(2) for half in ("lo", "hi"))
_WIDE_HEADS = tuple(Head(j * LANES, None, j * LANES, j * HALF_LANES, (j + 1) * HALF_LANES) for j in range(4))


def _nsa_compress(xk, xv, pe, w1x, w2d):
    B, R, K = xk.shape
    G = NSA_GROUPS

    def kern(xk_ref, xv_ref, pe_ref, w1_ref, w2_ref, k_ref, vt_ref):
        for t, x_ref in enumerate((xk_ref, xv_ref)):
            x = x_ref[...]
            xa = (x + pe_ref[t, 0]).astype(BF16)
            xb = (x + pe_ref[t, 1]).astype(BF16)
            for g in range(G):
                a = jnp.dot(xa, w1_ref[t, 0, g], preferred_element_type=F32)
                b = jnp.dot(xb, w1_ref[t, 1, g], preferred_element_type=F32)
                h = a + pltpu.roll(b, R - 1, 0)
                hs = h * _sigmoid(h)
                o = jnp.dot(hs.astype(BF16), w2_ref[t], preferred_element_type=F32)
                if t == 0:
                    k_ref[g] = o
                else:
                    vt_ref[g] = jnp.transpose(o)

    return pl.pallas_call(
        kern, grid=(B,),
        in_specs=[pl.BlockSpec((None, R, K), lambda b: (b, 0, 0)),
                  pl.BlockSpec((None, R, K), lambda b: (b, 0, 0)),
                  pl.BlockSpec(pe.shape, lambda b: (0, 0, 0, 0)),
                  pl.BlockSpec(w1x.shape, lambda b: (0, 0, 0, 0, 0)),
                  pl.BlockSpec(w2d.shape, lambda b: (0, 0, 0))],
        out_specs=[pl.BlockSpec((None, G, R, LANES), lambda b: (b, 0, 0, 0)),
                   pl.BlockSpec((None, G, LANES, R), lambda b: (b, 0, 0, 0))],
        out_shape=[jax.ShapeDtypeStruct((B, G, R, LANES), F32),
                   jax.ShapeDtypeStruct((B, G, LANES, R), F32)],
        compiler_params=_params("parallel"), name="nsa_compress")(xk, xv, pe, w1x, w2d)


def _nsa_cmp_select(q, cmp_k, cmp_vt, ov_t, *, B, S):
    T = SEL_TILE
    nq = S // T
    R = cmp_k.shape[2]
    G = NSA_GROUPS
    d = NSA_DIM
    n_sel = min(SLC_TOPK, S // SLC_LEN)

    bucket = min(LANES, R)
    assert R % bucket == 0 and bucket % (SLC_LEN // CMP_STRIDE) == 0

    def kern(q_ref, kc_ref, vt_ref, ov_ref, o_ref, sel_ref):
        i = pl.program_id(2)
        lane = lax.broadcasted_iota(jnp.int32, (T, LANES), 1)
        lo = lane < HALF_LANES
        q_all = q_ref[...]
        qs = []
        for hh in range(4):
            qh = q_all[:, (hh // 2) * LANES:(hh // 2 + 1) * LANES]
            qs.append(jnp.where(lo, qh, jnp.zeros_like(qh)) if hh % 2 == 0
                      else jnp.where(lo, jnp.zeros_like(qh), qh))

        def variant(rv):
            nb = rv * CMP_STRIDE // SLC_LEN
            tq = lax.broadcasted_iota(jnp.int32, (rv, T), 1) + i * T
            cend = lax.broadcasted_iota(jnp.int32, (rv, T), 0) * CMP_STRIDE + (CMP_LEN - 1)
            vis = cend <= tq
            kc = kc_ref[:rv, :].astype(BF16)
            vt = vt_ref[:d, :rv].astype(BF16)
            psum = jnp.zeros((rv, T), F32)
            o_heads = []
            for hh in range(4):
                st = lax.dot_general(kc, qs[hh], _NT, preferred_element_type=F32)
                st = jnp.where(vis, st, -jnp.inf)
                m = jnp.max(st, axis=0, keepdims=True)
                e = jnp.exp2(st - jnp.where(m > -jnp.inf, m, 0.0))
                p = e / jnp.maximum(jnp.sum(e, axis=0, keepdims=True), 1e-30)
                o_heads.append(jnp.dot(vt, p.astype(BF16), preferred_element_type=F32))
                psum = psum + p
            o_ref[...] = jnp.transpose(jnp.concatenate(o_heads, axis=0))
            imp = jnp.zeros((nb, T), F32)
            for part in _split_bf16(psum, 2):
                imp = imp + jnp.dot(ov_ref[:nb, :rv], part, preferred_element_type=F32)
            blk = lax.broadcasted_iota(jnp.int32, (nb, T), 0)
            cur = (lax.broadcasted_iota(jnp.int32, (nb, T), 1) + i * T) // SLC_LEN
            forced = (blk == 0) | (blk == cur) | (blk == cur - 1)
            work = jnp.where(forced, FORCE_SCORE, imp)
            work = jnp.where(blk > cur, NEG, work)
            blk_f = blk.astype(F32)
            for _ in range(n_sel):
                mx = jnp.max(work, axis=0, keepdims=True)
                first = jnp.min(jnp.where(work == mx, blk_f, float(LANES)), axis=0, keepdims=True)
                work = jnp.where(blk_f == first, -jnp.inf, work)
            sel = jnp.where(work == -jnp.inf, 1.0, 0.0)
            if nb < LANES:
                sel = jnp.concatenate([sel, jnp.zeros((LANES - nb, T), F32)], axis=0)
            sel_ref[...] = jnp.transpose(sel).astype(sel_ref.dtype)

        need = (i + 1) * (T // CMP_STRIDE)
        which = (need - 1) // bucket
        for bk in range(R // bucket):
            pl.when(which == bk)(lambda bk=bk: variant((bk + 1) * bucket))

    return pl.pallas_call(
        kern, grid=(B, G, nq),
        in_specs=[pl.BlockSpec((T, 2 * LANES), lambda b, g, i: (b * nq + i, g)),
                  pl.BlockSpec((None, None, R, LANES), lambda b, g, i: (b, g, 0, 0)),
                  pl.BlockSpec((None, None, LANES, R), lambda b, g, i: (b, g, 0, 0)),
                  pl.BlockSpec((LANES, R), lambda b, g, i: (0, 0))],
        out_specs=[pl.BlockSpec((T, 2 * LANES), lambda b, g, i: (b * nq + i, g)),
                   pl.BlockSpec((T, LANES), lambda b, g, i: (b * nq + i, g))],
        out_shape=[jax.ShapeDtypeStruct((B * S, G * 2 * LANES), F32),
                   jax.ShapeDtypeStruct((B * S, G * LANES), BF16)],
        compiler_params=_params("parallel", "parallel", "parallel"), name="nsa_cmp_select")(
            q, cmp_k, cmp_vt, ov_t)


def _nsa_combine(o_cmp, o_slc, o_win, gz, expand):
    N, C = o_cmp.shape
    tm = min(ROW_TILE, N)

    def kern(c_ref, s_ref, w_ref, g_ref, e_ref, o_ref):
        gate = _sigmoid(g_ref[...])
        parts = _split_bf16(gate, 2)
        acc = jnp.zeros((tm, C), F32)
        for j, br in enumerate((c_ref, s_ref, w_ref)):
            gj = sum(jnp.dot(part, e_ref[j], preferred_element_type=F32) for part in parts)
            acc = acc + gj * br[...]
        o_ref[...] = acc.astype(o_ref.dtype)

    row = pl.BlockSpec((tm, C), lambda i: (i, 0))
    return pl.pallas_call(
        kern, grid=(N // tm,),
        in_specs=[row, row, row, pl.BlockSpec((tm, LANES), lambda i: (i, 0)),
                  pl.BlockSpec(expand.shape, lambda i: (0, 0, 0))],
        out_specs=row, out_shape=jax.ShapeDtypeStruct((N, C), BF16),
        compiler_params=_params("parallel"), name="nsa_combine")(o_cmp, o_slc, o_win, gz, expand)


def _mla_kv(ckv, gain, wk, wv, kr):
    N, K = ckv.shape
    tm = min(ROW_TILE, N)
    H = MLA_HEADS

    def kern(c_ref, g_ref, wk_ref, wv_ref, kr_ref, k_ref, v_ref):
        xf = c_ref[...]
        y = xf * lax.rsqrt(jnp.mean(xf * xf, axis=-1, keepdims=True) + EPS)
        a = (y * g_ref[...]).astype(BF16)
        kr_ = kr_ref[...]
        for h in range(H):
            kh = jnp.dot(a, wk_ref[:, h * LANES:(h + 1) * LANES], preferred_element_type=F32)
            k_ref[:, h * LANES:(h + 1) * LANES] = (kh + kr_).astype(k_ref.dtype)
        v_ref[...] = jnp.dot(a, wv_ref[...], preferred_element_type=F32).astype(v_ref.dtype)

    return pl.pallas_call(
        kern, grid=(N // tm,),
        in_specs=[pl.BlockSpec((tm, K), lambda i: (i, 0)),
                  pl.BlockSpec((1, K), lambda i: (0, 0)),
                  pl.BlockSpec(wk.shape, lambda i: (0, 0)),
                  pl.BlockSpec(wv.shape, lambda i: (0, 0)),
                  pl.BlockSpec((tm, LANES), lambda i: (i, 0))],
        out_specs=[pl.BlockSpec((tm, H * LANES), lambda i: (i, 0)),
                   pl.BlockSpec((tm, H * MLA_V), lambda i: (i, 0))],
        out_shape=[jax.ShapeDtypeStruct((N, H * LANES), BF16),
                   jax.ShapeDtypeStruct((N, H * MLA_V), BF16)],
        compiler_params=_params("parallel"), name="mla_kv")(
            ckv, gain.reshape(1, K).astype(F32), wk, wv, kr)


def _mem_attn(x, gain, wq, wo, mem_k, mem_v, *, S):
    N, D = x.shape
    tm = min(ROW_TILE, S)
    M = mem_k.shape[1]
    per_b = S // tm
    scale = XA_DIM ** -0.5

    def kern(x_ref, g_ref, wq_ref, wo_ref, k_ref, v_ref, o_ref):
        xf = x_ref[...]
        y = xf * lax.rsqrt(jnp.mean(xf * xf, axis=-1, keepdims=True) + EPS)
        a = (y * g_ref[...]).astype(BF16)
        q = (jnp.dot(a, wq_ref[...], preferred_element_type=F32) * scale).astype(BF16)
        heads = []
        for h in range(XA_HEADS):
            sl = slice(h * XA_DIM, (h + 1) * XA_DIM)
            s = lax.dot_general(q[:, sl], k_ref[:, sl], _NT, preferred_element_type=F32)
            e = jnp.exp(s - jnp.max(s, axis=1, keepdims=True))
            p = e / jnp.sum(e, axis=1, keepdims=True)
            heads.append(jnp.dot(p.astype(BF16), v_ref[:, sl], preferred_element_type=F32).astype(BF16))
        o = jnp.concatenate(heads, axis=1)
        o_ref[...] = xf + jnp.dot(o, wo_ref[...], preferred_element_type=F32)

    return pl.pallas_call(
        kern, grid=(N // tm,),
        in_specs=[pl.BlockSpec((tm, D), lambda i: (i, 0)),
                  pl.BlockSpec((1, D), lambda i: (0, 0)),
                  pl.BlockSpec(wq.shape, lambda i: (0, 0)),
                  pl.BlockSpec(wo.shape, lambda i: (0, 0)),
                  pl.BlockSpec((None, M, XA_HEADS * XA_DIM), lambda i: (i // per_b, 0, 0)),
                  pl.BlockSpec((None, M, XA_HEADS * XA_DIM), lambda i: (i // per_b, 0, 0))],
        out_specs=pl.BlockSpec((tm, D), lambda i: (i, 0)),
        out_shape=jax.ShapeDtypeStruct((N, D), F32),
        compiler_params=_params("parallel"), name="mem_attn")(
            x, gain.reshape(1, D).astype(F32), wq, wo, mem_k, mem_v)


def _ffn(x, gain, w13, w2, out_gain=None):
    N, D = x.shape
    FF = w2.shape[0]
    tm = min(ROW_TILE, N)
    chunk = 2 * LANES
    assert FF % chunk == 0

    final = out_gain is not None

    def kern(x_ref, g_ref, w13_ref, w2_ref, *rest):
        og_ref, o_ref = rest if final else (None, rest[0])
        xf = x_ref[...]
        y = xf * lax.rsqrt(jnp.mean(xf * xf, axis=-1, keepdims=True) + EPS)
        a = (y * g_ref[...]).astype(BF16)
        acc = xf
        for c in range(0, FF, chunk):
            g = jnp.dot(a, w13_ref[:, c:c + chunk], preferred_element_type=F32)
            u = jnp.dot(a, w13_ref[:, FF + c:FF + c + chunk], preferred_element_type=F32)
            hdn = (g * _sigmoid(g) * u).astype(BF16)
            acc = acc + jnp.dot(hdn, w2_ref[c:c + chunk, :], preferred_element_type=F32)
        if final:
            acc = acc * lax.rsqrt(jnp.mean(acc * acc, axis=-1, keepdims=True) + EPS) * og_ref[...]
        o_ref[...] = acc

    vec = pl.BlockSpec((1, D), lambda i: (0, 0))
    in_specs = [pl.BlockSpec((tm, D), lambda i: (i, 0)), vec,
                pl.BlockSpec(w13.shape, lambda i: (0, 0), pipeline_mode=pl.Buffered(1)),
                pl.BlockSpec(w2.shape, lambda i: (0, 0), pipeline_mode=pl.Buffered(1))]
    args = [x, gain.reshape(1, D).astype(F32), w13, w2]
    if final:
        in_specs.append(vec)
        args.append(out_gain.reshape(1, D).astype(F32))
    return pl.pallas_call(
        kern, grid=(N // tm,), in_specs=in_specs,
        out_specs=pl.BlockSpec((tm, D), lambda i: (i, 0)),
        out_shape=jax.ShapeDtypeStruct((N, D), F32),
        compiler_params=_params("parallel"), name="ffn")(*args)


def _pad_cols(w, width):
    return jnp.pad(w, ((0, 0), (0, width - w.shape[1])))


def _even_mixer(x, gain, w_in, b_f, lam, subln, w_out, layer_idx, ropes, B, S):
    D = x.shape[1]
    blk = DIFF_HEADS * 2 * DIFF_QK
    w = _pad_cols(w_in, 6 * blk + LANES).astype(BF16)
    sc = DIFF_QK ** -0.5 * LOG2E
    segs = [Seg(0, blk, BF16, "r64", sc), Seg(blk, blk, BF16, "r64"), Seg(2 * blk, blk, BF16),
            Seg(3 * blk, blk, BF16, None, sc), Seg(4 * blk, blk, BF16), Seg(5 * blk, blk, BF16),
            Seg(6 * blk, LANES, F32)]
    aq, ak, av, fq, fk, fv, fz = _linear([x], [w], segs, gain=gain, ropes=ropes, seq=S, name="even_in")
    lam_init = 0.8 - 0.6 * math.exp(-0.3 * layer_idx)
    oa = _flash(aq, ak, av.T, B=B, S=S, P=DIFF_HEADS // 2, q_w=2 * LANES, k_w=2 * LANES, v_rows=2 * LANES,
                heads=_DIFF_HEADS, kind="diff", out_dtype=BF16, lam=lam.astype(F32),
                subln=subln.astype(F32), lam_init=lam_init, name="diff_attn")
    terms = _forget_key_terms(fz, _pad_cols(b_f.reshape(1, -1), LANES).astype(F32), B, S)
    of = _flash(fq, fk, fv.T, B=B, S=S, P=FOX_HEADS // 4, q_w=2 * LANES, k_w=2 * LANES, v_rows=2 * LANES,
                heads=_PAIR_HEADS, kind="fox", out_dtype=BF16, ext=terms, name="fox_attn")
    wo = w_out.astype(BF16)
    (y,) = _linear([oa, of], [wo[:blk], wo[blk:]], [Seg(0, D, F32)], residual=x, name="even_out")
    return y


def _odd_in_weight(w_in):
    d = NSA_DIM
    o = np.cumsum((0, NSA_HEADS * d) + (NSA_GROUPS * d,) * 6 + (NSA_HEADS * 3, MLA_Q_RANK, MLA_KV_RANK, MLA_ROPE))
    nq, kc, vc, ks, vs, kw, vw, gz, cq, ckv, kr = [w_in[:, o[j]:o[j + 1]] for j in range(11)]

    def dup(wg):
        return jnp.concatenate([wg[:, :d], wg[:, :d], wg[:, d:], wg[:, d:]], axis=1)

    zeros = lambda n: jnp.zeros((w_in.shape[0], n), w_in.dtype)
    kr_slab = jnp.concatenate([zeros(MLA_NOPE), kr, zeros(LANES - MLA_NOPE - MLA_ROPE)], axis=1)
    cols = [nq, kc, vc, dup(ks), vs, dup(kw), vw, _pad_cols(gz, LANES), cq, ckv, kr_slab]
    return jnp.concatenate(cols, axis=1).astype(BF16)


def _odd_mixer(x, gain, w_in, cmp_pos, cmp_w1, cmp_w2, q_norm, kv_norm, w_uq, w_ukv, w_out, ropes, B, S):
    D = x.shape[1]
    G, d = NSA_GROUPS, NSA_DIM
    w = _odd_in_weight(w_in)
    sc = d ** -0.5 * LOG2E
    widths = [(NSA_HEADS * d, BF16, "r64", sc), (LANES, F32, "r64", 1.0), (LANES, F32, None, 1.0),
              (2 * LANES, BF16, "r64", 1.0), (LANES, BF16, None, 1.0),
              (2 * LANES, BF16, "r64", 1.0), (LANES, BF16, None, 1.0),
              (LANES, F32, None, 1.0), (MLA_Q_RANK, F32, None, 1.0), (MLA_KV_RANK, F32, None, 1.0),
              (LANES, F32, "mla", 1.0)]
    segs, start = [], 0
    for wd, dt, rp, s_ in widths:
        segs.append(Seg(start, wd, dt, rp, s_))
        start += wd
    q, kc, vc, ks, vs, kw, vw, gz, cq, ckv, kr = _linear(
        [x], [w], segs, gain=gain, ropes=ropes, seq=S, name="odd_in")

    R = S // CMP_STRIDE

    K = CMP_STRIDE * G * d
    half = CMP_LEN // 2
    pe = jnp.broadcast_to(cmp_pos.reshape(2, 2, half, 1, d), (2, 2, half, G, d)).reshape(2, 2, 1, K).astype(F32)
    eye = jnp.eye(G, dtype=cmp_w1.dtype)
    w1x = (cmp_w1.reshape(2, 2, 1, half, 1, d, CMP_HIDDEN) * eye[None, None, :, None, :, None, None])
    w1x = w1x.reshape(2, 2, G, K, CMP_HIDDEN).astype(BF16)
    w2d = jnp.concatenate([cmp_w2, cmp_w2], axis=-1).astype(BF16)
    cmp_k, cmp_vt = _nsa_compress(kc.reshape(B, R, K), vc.reshape(B, R, K), pe, w1x, w2d)
    cs = np.arange(R) * CMP_STRIDE
    bs = np.arange(LANES) * SLC_LEN
    ov = ((cs[:, None] < bs[None, :] + SLC_LEN) & (cs[:, None] + CMP_LEN > bs[None, :])
          & (np.arange(R)[:, None] < R - 1) & (bs[None, :] < S))
    o_cmp, sel = _nsa_cmp_select(q, cmp_k, cmp_vt, jnp.asarray(ov.T.astype(np.float32), BF16), B=B, S=S)
    onehot = (np.arange(S)[:, None] // SLC_LEN == np.arange(LANES)[None, :]).astype(np.float32)
    o_slc = _flash(q, ks, vs.T, B=B, S=S, P=G, q_w=2 * LANES, k_w=LANES, v_rows=HALF_LANES,
                   heads=_QUAD_HEADS, kind="slc", out_dtype=F32, ext=jnp.asarray(onehot, BF16),
                   sel=sel, name="nsa_slc")
    o_win = _flash(q, kw, vw.T, B=B, S=S, P=G, q_w=2 * LANES, k_w=LANES, v_rows=HALF_LANES,
                   heads=_QUAD_HEADS, kind="win", out_dtype=F32, name="nsa_win")
    ex = np.zeros((3, LANES, NSA_HEADS * d), np.float32)
    for hh in range(NSA_HEADS):
        for j in range(3):
            ex[j, hh * 3 + j, hh * d:(hh + 1) * d] = 1.0
    o_nsa = _nsa_combine(o_cmp, o_slc, o_win, gz, jnp.asarray(ex, BF16))

    H = MLA_HEADS
    qk = MLA_NOPE + MLA_ROPE
    wq_slab = jnp.pad(w_uq.reshape(MLA_Q_RANK, H, qk), ((0, 0), (0, 0), (0, LANES - qk)))
    wq_slab = wq_slab.reshape(MLA_Q_RANK, H * LANES).astype(BF16)
    (qm,) = _linear([cq], [wq_slab], [Seg(0, H * LANES, BF16, "mla", qk ** -0.5 * LOG2E)],
                    gain=q_norm, ropes=ropes, seq=S, name="mla_q")
    wkv = w_ukv.reshape(MLA_KV_RANK, H, MLA_NOPE + MLA_V)
    wk = jnp.pad(wkv[:, :, :MLA_NOPE], ((0, 0), (0, 0), (0, LANES - MLA_NOPE)))
    wk = wk.reshape(MLA_KV_RANK, H * LANES).astype(BF16)
    wv = wkv[:, :, MLA_NOPE:].reshape(MLA_KV_RANK, H * MLA_V).astype(BF16)
    km, vm = _mla_kv(ckv, kv_norm, wk, wv, kr)
    o_mla = _flash(qm, km, vm.T, B=B, S=S, P=H // 4, q_w=4 * LANES, k_w=4 * LANES, v_rows=2 * LANES,
                   heads=_WIDE_HEADS, kind="mla", out_dtype=BF16, name="mla_attn")
    wo = w_out.astype(BF16)
    half = NSA_HEADS * d
    (y,) = _linear([o_nsa, o_mla], [wo[:half], wo[half:]], [Seg(0, D, F32)], residual=x, name="odd_out")
    return y


def kernel(x, mem, mem_norm, norm_mix, norm_mem, norm_ffn, ev_w_in, ev_b_f, ev_lam, ev_subln, ev_w_out, od_w_in, nsa_cmp_pos, nsa_cmp_w1, nsa_cmp_w2, mla_q_norm, mla_kv_norm, mla_w_uq, mla_w_ukv, od_w_out, xa_wq, xa_wkv, xa_wo, ffn_w13, ffn_w2, final_norm):
    B, S, D = x.shape
    M = mem.shape[1]
    depth = norm_mix.shape[0]
    ropes = {
        "r64": (_rope_tables(S, NSA_DIM // 2, 0, LANES), NSA_DIM // 2),
        "mla": (_rope_tables(S, MLA_ROPE // 2, MLA_NOPE, MLA_NOPE + MLA_ROPE), MLA_ROPE // 2),
    }
    xa_w = XA_HEADS * XA_DIM
    h = x.reshape(B * S, D)
    mem2 = mem.reshape(B * M, D)
    for li in range(depth):
        j = li // 2
        if li % 2 == 0:
            h = _even_mixer(h, norm_mix[li], ev_w_in[j], ev_b_f[j], ev_lam[j], ev_subln[j], ev_w_out[j],
                            li, ropes, B, S)
        else:
            h = _odd_mixer(h, norm_mix[li], od_w_in[j], nsa_cmp_pos[j], nsa_cmp_w1[j], nsa_cmp_w2[j],
                           mla_q_norm[j], mla_kv_norm[j], mla_w_uq[j], mla_w_ukv[j], od_w_out[j],
                           ropes, B, S)
        mk, mv = _linear([mem2], [xa_wkv[li].astype(BF16)],
                         [Seg(0, xa_w, BF16), Seg(xa_w, xa_w, BF16)], gain=mem_norm, name="mem_kv")
        h = _mem_attn(h, norm_mem[li], xa_wq[li].astype(BF16), xa_wo[li].astype(BF16),
                      mk.reshape(B, M, xa_w), mv.reshape(B, M, xa_w), S=S)
        h = _ffn(h, norm_ffn[li], ffn_w13[li].astype(BF16), ffn_w2[li].astype(BF16),
                 out_gain=final_norm if li == depth - 1 else None)
    return h.reshape(B, S, D)
```

```python
import math
from typing import NamedTuple, Optional

import numpy as np
import jax
import jax.numpy as jnp
from jax import lax
from jax.experimental import pallas as pl
from jax.experimental.pallas import tpu as pltpu

F32 = jnp.float32
BF16 = jnp.bfloat16

LANES = 128
HALF_LANES = LANES // 2
BF16_ROWS = 16
ROPE_THETA = 10000.0
EPS = 1e-6
NEG = -1e30
LOG2E = math.log2(math.e)
MASK_BIG = 2.0 ** 100

DIFF_HEADS = 4
DIFF_QK = 64
FOX_HEADS = 8
NSA_HEADS = 8
NSA_GROUPS = 2
NSA_DIM = 64
CMP_LEN = 32
CMP_STRIDE = 16
CMP_HIDDEN = 128
SLC_LEN = 64
SLC_TOPK = 16
WIN = 512
FORCE_SCORE = 1e4
MLA_HEADS = 8
MLA_NOPE = 64
MLA_ROPE = 32
MLA_V = 64
MLA_Q_RANK = 384
MLA_KV_RANK = 256
XA_HEADS = 4
XA_DIM = 128

VMEM_LIMIT = 56 * 1024 * 1024
ROW_TILE = 512
COL_CHUNK = 512
ATT_TILE = 512
KEY_TILE = 512
ATT_BUFS = 3
SEL_TILE = 512
CUM_CHUNK = 256

_NT = (((1,), (1,)), ((), ()))


def _params(*sem):
    return pltpu.CompilerParams(dimension_semantics=sem, vmem_limit_bytes=VMEM_LIMIT)


def _sigmoid(x):
    return 1.0 / (1.0 + jnp.exp(-x))


def _split_bf16(x, terms):
    out = []
    r = x
    for _ in range(terms):
        h = r.astype(BF16)
        out.append(h)
        r = r - h.astype(F32)
    return out


def _rope_tables(S, half, lane_lo, lane_hi):
    pos = jnp.arange(S, dtype=F32)
    inv = 1.0 / (ROPE_THETA ** (jnp.arange(half, dtype=F32) / half))
    ang = pos[:, None] * inv[None, :]
    cos, sin = jnp.cos(ang), jnp.sin(ang)
    lane = np.arange(LANES)
    active = (lane >= lane_lo) & (lane < lane_hi)
    j = (lane - lane_lo) % (2 * half)
    lower = active & (j < half)
    upper = active & (j >= half)
    idx = j % half
    cos_t = jnp.where(active[None, :], cos[:, idx], 1.0)
    sin_a = jnp.where(lower[None, :], -sin[:, idx], 0.0)
    sin_b = jnp.where(upper[None, :], sin[:, idx], 0.0)
    return cos_t, sin_a, sin_b


class Seg(NamedTuple):
    start: int
    width: int
    dtype: object
    rope: Optional[str] = None
    scale: float = 1.0


def _linear(xs, ws, segs, *, gain=None, residual=None, ropes=None, seq=None, name="linear"):
    N = xs[0].shape[0]
    tm = min(ROW_TILE, N)
    assert N % tm == 0
    n_in = len(xs)
    has_gain = gain is not None
    has_res = residual is not None
    rope_keys = sorted({s.rope for s in segs if s.rope})
    halves = {k: ropes[k][1] for k in rope_keys}

    def kern(*refs):
        it = iter(refs)
        x_refs = [next(it) for _ in range(n_in)]
        w_refs = [next(it) for _ in range(n_in)]
        g_ref = next(it) if has_gain else None
        r_ref = next(it) if has_res else None
        tabs = {k: (next(it), next(it), next(it)) for k in rope_keys}
        o_refs = [next(it) for _ in segs]
        acts = []
        for j, xr in enumerate(x_refs):
            x = xr[...]
            if j == 0 and has_gain:
                xf = x.astype(F32)
                y = xf * lax.rsqrt(jnp.mean(xf * xf, axis=-1, keepdims=True) + EPS)
                acts.append((y * g_ref[...]).astype(BF16))
            else:
                acts.append(x.astype(BF16))
        for seg, o_ref in zip(segs, o_refs):
            for c0 in range(0, seg.width, COL_CHUNK):
                cw = min(COL_CHUNK, seg.width - c0)
                col = seg.start + c0
                acc = None
                for a, wr in zip(acts, w_refs):
                    d = jnp.dot(a, wr[:, col:col + cw], preferred_element_type=F32)
                    acc = d if acc is None else acc + d
                if has_res:
                    acc = acc + r_ref[:, col:col + cw]
                if seg.scale != 1.0:
                    acc = acc * seg.scale
                if seg.rope is None:
                    o_ref[:, c0:c0 + cw] = acc.astype(o_ref.dtype)
                else:
                    cos_r, sa_r, sb_r = tabs[seg.rope]
                    half = halves[seg.rope]
                    cos, sa, sb = cos_r[...], sa_r[...], sb_r[...]
                    for s0 in range(0, cw, LANES):
                        xs_ = acc[:, s0:s0 + LANES]
                        y = (xs_ * cos + pltpu.roll(xs_, LANES - half, 1) * sa
                             + pltpu.roll(xs_, half, 1) * sb)
                        o_ref[:, c0 + s0:c0 + s0 + LANES] = y.astype(o_ref.dtype)

    in_specs, args = [], []
    for x in xs:
        in_specs.append(pl.BlockSpec((tm, x.shape[1]), lambda i: (i, 0)))
        args.append(x)
    for w in ws:
        in_specs.append(pl.BlockSpec(w.shape, lambda i: (0, 0)))
        args.append(w)
    if has_gain:
        in_specs.append(pl.BlockSpec((1, gain.shape[-1]), lambda i: (0, 0)))
        args.append(gain.reshape(1, -1).astype(F32))
    if has_res:
        in_specs.append(pl.BlockSpec((tm, residual.shape[1]), lambda i: (i, 0)))
        args.append(residual)
    for k in rope_keys:
        assert seq % tm == 0
        nt = seq // tm
        for t in ropes[k][0]:
            in_specs.append(pl.BlockSpec((tm, LANES), lambda i, nt=nt: (i % nt, 0)))
            args.append(t)
    out_shape = [jax.ShapeDtypeStruct((N, s.width), s.dtype) for s in segs]
    out_specs = [pl.BlockSpec((tm, s.width), lambda i: (i, 0)) for s in segs]
    return pl.pallas_call(
        kern, grid=(N // tm,), in_specs=in_specs, out_specs=out_specs, out_shape=out_shape,
        compiler_params=_params("parallel"), name=name)(*args)


def _forget_key_terms(fz, b_f, B, S):
    ch = CUM_CHUNK
    place = np.zeros((3, LANES, LANES), np.float32)
    for h in range(FOX_HEADS):
        for j in range(3):
            place[j, h, 3 * h + j] = 1.0

    def kern(z_ref, b_ref, pl_ref, o_ref):
        r = lax.broadcasted_iota(jnp.int32, (ch, ch), 0)
        c = lax.broadcasted_iota(jnp.int32, (ch, ch), 1)
        tri = jnp.where(c <= r, 1.0, 0.0).astype(BF16)

        def body(j, carry):
            r0 = pl.multiple_of(j * ch, ch)
            z = z_ref[pl.ds(r0, ch), :] + b_ref[...]
            logf = -(jnp.maximum(-z, 0.0) + jnp.log1p(jnp.exp(-jnp.abs(z))))
            cs = carry
            for part in _split_bf16(logf, 3):
                cs = cs + jnp.dot(tri, part, preferred_element_type=F32)
            terms = jnp.zeros((ch, LANES), F32)
            for jj, part in enumerate(_split_bf16(cs * (-LOG2E), 3)):
                terms = terms + jnp.dot(part, pl_ref[jj], preferred_element_type=F32)
            o_ref[pl.ds(r0, ch), :] = terms.astype(o_ref.dtype)
            return cs[ch - 1:ch, :]

        lax.fori_loop(0, S // ch, body, jnp.zeros((1, LANES), F32))

    return pl.pallas_call(
        kern, grid=(B,),
        in_specs=[pl.BlockSpec((S, LANES), lambda b: (b, 0)),
                  pl.BlockSpec((1, LANES), lambda b: (0, 0)),
                  pl.BlockSpec(place.shape, lambda b: (0, 0, 0))],
        out_specs=pl.BlockSpec((S, LANES), lambda b: (b, 0)),
        out_shape=jax.ShapeDtypeStruct((B * S, LANES), BF16),
        compiler_params=_params("parallel"), name="forget_terms")(fz, b_f, jnp.asarray(place, BF16))


class Head(NamedTuple):
    q_off: int
    q_half: Optional[str]
    k_off: int
    v0: int
    v1: int


def _flash(q, k, vt, *, B, S, P, q_w, k_w, v_rows, heads, kind, out_dtype,
           ext=None, sel=None, lam=None, subln=None, lam_init=None, name="flash"):
    TQ, TK = ATT_TILE, KEY_TILE
    assert S % TK == 0
    nq = S // TQ
    nh = len(heads)
    rows = heads[0].v1 - heads[0].v0
    out_w = nh // 2 * rows if kind == "diff" else nh * rows
    width = TK
    n_bufs = ATT_BUFS
    if kind == "win":
        assert WIN == TK and TQ == TK and S >= 2 * TK
        n_bufs = 2
    has_ext = kind in ("fox", "slc")

    def kern(*refs):
        it = iter(refs)
        q_ref, k_ref, v_ref = next(it), next(it), next(it)
        ext_ref = next(it) if has_ext else None
        sel_ref = next(it) if kind == "slc" else None
        lam_ref, sub_ref = (next(it), next(it)) if kind == "diff" else (None, None)
        o_ref, m_ref, acc_ref = next(it), next(it), next(it)
        bufs = [next(it) for _ in range(n_bufs)]
        p_id = pl.program_id(1)
        i = pl.program_id(2)

        lane = lax.broadcasted_iota(jnp.int32, (TQ, LANES), 1)
        lo = lane < HALF_LANES
        q_all = q_ref[...]
        if kind == "slc":
            drop = ((sel_ref[...].astype(F32) - 1.0) * MASK_BIG).astype(BF16)
        qs = []
        for j, h in enumerate(heads):
            qh = q_all[:, h.q_off:h.q_off + LANES]
            if h.q_half == "lo":
                qh = jnp.where(lo, qh, jnp.zeros_like(qh))
            elif h.q_half == "hi":
                qh = jnp.where(lo, jnp.zeros_like(qh), qh)
            if kind == "fox":
                first = 3 * (nh * p_id + j)
                pick = jnp.where((lane >= first) & (lane < first + 3), 1.0, 0.0).astype(BF16)
                qh = jnp.concatenate([qh, pick], axis=1)
            if kind == "slc":
                qh = jnp.concatenate([qh, drop], axis=1)
            qs.append(qh)

        m_ref[...] = jnp.full(m_ref.shape, NEG, F32)
        acc_ref[...] = jnp.zeros(acc_ref.shape, F32)

        def scores(k0, dst_ref):
            kx = k_ref[pl.ds(k0, width), :]
            ex = ext_ref[pl.ds(k0, width), :] if has_ext else None
            for hi, h in enumerate(heads):
                kk = kx[:, h.k_off:h.k_off + LANES]
                if ex is not None:
                    kk = jnp.concatenate([kk, ex], axis=1)
                dst_ref[hi] = lax.dot_general(kk, qs[hi], _NT, preferred_element_type=F32)

        def update(k0, src_ref, masked):
            ones = jnp.ones((BF16_ROWS, width), BF16)
            msk = None
            if masked:
                kpos = lax.broadcasted_iota(jnp.int32, (width, TQ), 0) + k0
                qpos = lax.broadcasted_iota(jnp.int32, (width, TQ), 1) + i * TQ
                msk = kpos <= qpos
                if kind == "win":
                    msk = msk & (qpos - kpos < WIN)
            for hi, h in enumerate(heads):
                st = src_ref[hi]
                if msk is not None:
                    st = jnp.where(msk, st, NEG)
                m_prev = m_ref[hi]
                m_new = jnp.maximum(m_prev, jnp.max(st, axis=0, keepdims=True))
                alpha = jnp.exp2(m_prev - m_new)
                pt = jnp.exp2(st - m_new).astype(BF16)
                lhs = jnp.concatenate([v_ref[h.v0:h.v1, pl.ds(k0, width)], ones], axis=0)
                acc_ref[hi] = alpha * acc_ref[hi] + jnp.dot(lhs, pt, preferred_element_type=F32)
                m_ref[hi] = m_new

        tile = lambda t: pl.multiple_of(t * TK, TK)
        if kind == "win":
            t_first = jnp.maximum(i * TQ - WIN, 0) // TK
            scores(tile(t_first), bufs[0])
            scores(tile(t_first + 1), bufs[1])
            update(tile(t_first), bufs[0], True)
            update(tile(t_first + 1), bufs[1], True)
        else:
            U = len(bufs)
            n_full = i // (TK // TQ)
            n_rounds = n_full // U
            scores(tile(0), bufs[0])

            def body(j, c):
                for u in range(U):
                    scores(tile(U * j + u + 1), bufs[(u + 1) % U])
                    update(tile(U * j + u), bufs[u], False)
                return c
            lax.fori_loop(0, n_rounds, body, 0)
            t0 = U * n_rounds

            def tail(left):
                for u in range(left):
                    scores(tile(t0 + u + 1), bufs[u + 1])
                    update(tile(t0 + u), bufs[u], False)
                update(tile(t0 + left), bufs[left], True)

            for left in range(U):
                pl.when(n_full - t0 == left)(lambda left=left: tail(left))

        outs = []
        for hi in range(nh):
            acc = acc_ref[hi]
            outs.append(acc[:rows] / acc[rows:rows + 1])
        if kind == "diff":
            lm = lam_ref[...]
            la = jnp.sum(lm[0:1] * lm[1:2], axis=1, keepdims=True)
            lb = jnp.sum(lm[2:3] * lm[3:4], axis=1, keepdims=True)
            lam_full = jnp.exp(la) - jnp.exp(lb) + lam_init
            for g in range(nh // 2):
                oa = jnp.transpose(outs[2 * g] - lam_full * outs[2 * g + 1])
                y = oa * lax.rsqrt(jnp.mean(oa * oa, axis=-1, keepdims=True) + EPS)
                y = (y * sub_ref[...]) * (1.0 - lam_init)
                o_ref[:, g * rows:(g + 1) * rows] = y.astype(o_ref.dtype)
        else:
            o_ref[...] = jnp.transpose(jnp.concatenate(outs, axis=0)).astype(o_ref.dtype)

    in_specs = [
        pl.BlockSpec((TQ, q_w), lambda b, p, i: (b * nq + i, p)),
        pl.BlockSpec((S, k_w), lambda b, p, i: (b, p)),
        pl.BlockSpec((v_rows, S), lambda b, p, i: (p, b)),
    ]
    args = [q, k, vt]
    if kind == "fox":
        in_specs.append(pl.BlockSpec((S, LANES), lambda b, p, i: (b, 0)))
        args.append(ext)
    if kind == "slc":
        in_specs.append(pl.BlockSpec((S, LANES), lambda b, p, i: (0, 0)))
        in_specs.append(pl.BlockSpec((TQ, LANES), lambda b, p, i: (b * nq + i, p)))
        args += [ext, sel]
    if kind == "diff":
        in_specs.append(pl.BlockSpec(lam.shape, lambda b, p, i: (0, 0)))
        in_specs.append(pl.BlockSpec((1, LANES), lambda b, p, i: (0, 0)))
        args += [lam, subln.reshape(1, LANES)]
    return pl.pallas_call(
        kern, grid=(B, P, nq), in_specs=in_specs,
        out_specs=pl.BlockSpec((TQ, out_w), lambda b, p, i: (b * nq + i, p)),
        out_shape=jax.ShapeDtypeStruct((B * S, P * out_w), out_dtype),
        scratch_shapes=[pltpu.VMEM((nh, 1, TQ), F32),
                        pltpu.VMEM((nh, rows + BF16_ROWS, TQ), F32)]
        + [pltpu.VMEM((nh, width, TQ), F32)] * n_bufs,
        compiler_params=_params("parallel", "parallel", "arbitrary"), name=name)(*args)


_DIFF_HEADS = tuple(Head(s * LANES, half, s * LANES, s * LANES, (s + 1) * LANES)
                    for s in range(2) for half in ("lo", "hi"))
_PAIR_HEADS = tuple(Head(s * LANES, half, s * LANES, (2 * s + j) * HALF_LANES, (2 * s + j + 1) * HALF_LANES)
                    for s in range(2) for j, half in enumerate(("lo", "hi")))
_QUAD_HEADS = tuple(Head(s * LANES, half, 0, 0, HALF_LANES) for s in range(2) for half in ("lo", "hi"))
_WIDE_HEADS = tuple(Head(j * LANES, None, j * LANES, j * HALF_LANES, (j + 1) * HALF_LANES) for j in range(4))


def _nsa_compress(xk, xv, pe, w1x, w2d):
    B, R, K = xk.shape
    G = NSA_GROUPS

    def kern(xk_ref, xv_ref, pe_ref, w1_ref, w2_ref, k_ref, vt_ref):
        for t, x_ref in enumerate((xk_ref, xv_ref)):
            x = x_ref[...]
            xa = (x + pe_ref[t, 0]).astype(BF16)
            xb = (x + pe_ref[t, 1]).astype(BF16)
            for g in range(G):
                a = jnp.dot(xa, w1_ref[t, 0, g], preferred_element_type=F32)
                b = jnp.dot(xb, w1_ref[t, 1, g], preferred_element_type=F32)
                h = a + pltpu.roll(b, R - 1, 0)
                hs = h * _sigmoid(h)
                o = jnp.dot(hs.astype(BF16), w2_ref[t], preferred_element_type=F32)
                if t == 0:
                    k_ref[g] = o
                else:
                    vt_ref[g] = jnp.transpose(o)

    return pl.pallas_call(
        kern, grid=(B,),
        in_specs=[pl.BlockSpec((None, R, K), lambda b: (b, 0, 0)),
                  pl.BlockSpec((None, R, K), lambda b: (b, 0, 0)),
                  pl.BlockSpec(pe.shape, lambda b: (0, 0, 0, 0)),
                  pl.BlockSpec(w1x.shape, lambda b: (0, 0, 0, 0, 0)),
                  pl.BlockSpec(w2d.shape, lambda b: (0, 0, 0))],
        out_specs=[pl.BlockSpec((None, G, R, LANES), lambda b: (b, 0, 0, 0)),
                   pl.BlockSpec((None, G, LANES, R), lambda b: (b, 0, 0, 0))],
        out_shape=[jax.ShapeDtypeStruct((B, G, R, LANES), F32),
                   jax.ShapeDtypeStruct((B, G, LANES, R), F32)],
        compiler_params=_params("parallel"), name="nsa_compress")(xk, xv, pe, w1x, w2d)


def _nsa_cmp_select(q, cmp_k, cmp_vt, ov_t, *, B, S):
    T = SEL_TILE
    nq = S // T
    R = cmp_k.shape[2]
    G = NSA_GROUPS
    d = NSA_DIM
    n_sel = min(SLC_TOPK, S // SLC_LEN)

    bucket = min(LANES, R)
    assert R % bucket == 0 and bucket % (SLC_LEN // CMP_STRIDE) == 0

    def kern(q_ref, kc_ref, vt_ref, ov_ref, o_ref, sel_ref):
        i = pl.program_id(2)
        lane = lax.broadcasted_iota(jnp.int32, (T, LANES), 1)
        lo = lane < HALF_LANES
        q_all = q_ref[...]
        qs = []
        for hh in range(4):
            qh = q_all[:, (hh // 2) * LANES:(hh // 2 + 1) * LANES]
            qs.append(jnp.where(lo, qh, jnp.zeros_like(qh)) if hh % 2 == 0
                      else jnp.where(lo, jnp.zeros_like(qh), qh))

        def variant(rv):
            nb = rv * CMP_STRIDE // SLC_LEN
            tq = lax.broadcasted_iota(jnp.int32, (rv, T), 1) + i * T
            cend = lax.broadcasted_iota(jnp.int32, (rv, T), 0) * CMP_STRIDE + (CMP_LEN - 1)
            vis = cend <= tq
            kc = kc_ref[:rv, :].astype(BF16)
            vt = vt_ref[:d, :rv].astype(BF16)
            psum = jnp.zeros((rv, T), F32)
            o_heads = []
            for hh in range(4):
                st = lax.dot_general(kc, qs[hh], _NT, preferred_element_type=F32)
                st = jnp.where(vis, st, -jnp.inf)
                m = jnp.max(st, axis=0, keepdims=True)
                e = jnp.exp2(st - jnp.where(m > -jnp.inf, m, 0.0))
                p = e / jnp.maximum(jnp.sum(e, axis=0, keepdims=True), 1e-30)
                o_heads.append(jnp.dot(vt, p.astype(BF16), preferred_element_type=F32))
                psum = psum + p
            o_ref[...] = jnp.transpose(jnp.concatenate(o_heads, axis=0))
            imp = jnp.zeros((nb, T), F32)
            for part in _split_bf16(psum, 2):
                imp = imp + jnp.dot(ov_ref[:nb, :rv], part, preferred_element_type=F32)
            blk = lax.broadcasted_iota(jnp.int32, (nb, T), 0)
            cur = (lax.broadcasted_iota(jnp.int32, (nb, T), 1) + i * T) // SLC_LEN
            forced = (blk == 0) | (blk == cur) | (blk == cur - 1)
            work = jnp.where(forced, FORCE_SCORE, imp)
            work = jnp.where(blk > cur, NEG, work)
            blk_f = blk.astype(F32)
            for _ in range(n_sel):
                mx = jnp.max(work, axis=0, keepdims=True)
                first = jnp.min(jnp.where(work == mx, blk_f, float(LANES)), axis=0, keepdims=True)
                work = jnp.where(blk_f == first, -jnp.inf, work)
            sel = jnp.where(work == -jnp.inf, 1.0, 0.0)
            if nb < LANES:
                sel = jnp.concatenate([sel, jnp.zeros((LANES - nb, T), F32)], axis=0)
            sel_ref[...] = jnp.transpose(sel).astype(sel_ref.dtype)

        need = (i + 1) * (T // CMP_STRIDE)
        which = (need - 1) // bucket
        for bk in range(R // bucket):
            pl.when(which == bk)(lambda bk=bk: variant((bk + 1) * bucket))

    return pl.pallas_call(
        kern, grid=(B, G, nq),
        in_specs=[pl.BlockSpec((T, 2 * LANES), lambda b, g, i: (b * nq + i, g)),
                  pl.BlockSpec((None, None, R, LANES), lambda b, g, i: (b, g, 0, 0)),
                  pl.BlockSpec((None, None, LANES, R), lambda b, g, i: (b, g, 0, 0)),
                  pl.BlockSpec((LANES, R), lambda b, g, i: (0, 0))],
        out_specs=[pl.BlockSpec((T, 2 * LANES), lambda b, g, i: (b * nq + i, g)),
                   pl.BlockSpec((T, LANES), lambda b, g, i: (b * nq + i, g))],
        out_shape=[jax.ShapeDtypeStruct((B * S, G * 2 * LANES), F32),
                   jax.ShapeDtypeStruct((B * S, G * LANES), BF16)],
        compiler_params=_params("parallel", "parallel", "parallel"), name="nsa_cmp_select")(
            q, cmp_k, cmp_vt, ov_t)


def _nsa_combine(o_cmp, o_slc, o_win, gz, expand):
    N, C = o_cmp.shape
    tm = min(ROW_TILE, N)

    def kern(c_ref, s_ref, w_ref, g_ref, e_ref, o_ref):
        gate = _sigmoid(g_ref[...])
        parts = _split_bf16(gate, 2)
        acc = jnp.zeros((tm, C), F32)
        for j, br in enumerate((c_ref, s_ref, w_ref)):
            gj = sum(jnp.dot(part, e_ref[j], preferred_element_type=F32) for part in parts)
            acc = acc + gj * br[...]
        o_ref[...] = acc.astype(o_ref.dtype)

    row = pl.BlockSpec((tm, C), lambda i: (i, 0))
    return pl.pallas_call(
        kern, grid=(N // tm,),
        in_specs=[row, row, row, pl.BlockSpec((tm, LANES), lambda i: (i, 0)),
                  pl.BlockSpec(expand.shape, lambda i: (0, 0, 0))],
        out_specs=row, out_shape=jax.ShapeDtypeStruct((N, C), BF16),
        compiler_params=_params("parallel"), name="nsa_combine")(o_cmp, o_slc, o_win, gz, expand)


def _mla_kv(ckv, gain, wk, wv, kr):
    N, K = ckv.shape
    tm = min(ROW_TILE, N)
    H = MLA_HEADS

    def kern(c_ref, g_ref, wk_ref, wv_ref, kr_ref, k_ref, v_ref):
        xf = c_ref[...]
        y = xf * lax.rsqrt(jnp.mean(xf * xf, axis=-1, keepdims=True) + EPS)
        a = (y * g_ref[...]).astype(BF16)
        kr_ = kr_ref[...]
        for h in range(H):
            kh = jnp.dot(a, wk_ref[:, h * LANES:(h + 1) * LANES], preferred_element_type=F32)
            k_ref[:, h * LANES:(h + 1) * LANES] = (kh + kr_).astype(k_ref.dtype)
        v_ref[...] = jnp.dot(a, wv_ref[...], preferred_element_type=F32).astype(v_ref.dtype)

    return pl.pallas_call(
        kern, grid=(N // tm,),
        in_specs=[pl.BlockSpec((tm, K), lambda i: (i, 0)),
                  pl.BlockSpec((1, K), lambda i: (0, 0)),
                  pl.BlockSpec(wk.shape, lambda i: (0, 0)),
                  pl.BlockSpec(wv.shape, lambda i: (0, 0)),
                  pl.BlockSpec((tm, LANES), lambda i: (i, 0))],
        out_specs=[pl.BlockSpec((tm, H * LANES), lambda i: (i, 0)),
                   pl.BlockSpec((tm, H * MLA_V), lambda i: (i, 0))],
        out_shape=[jax.ShapeDtypeStruct((N, H * LANES), BF16),
                   jax.ShapeDtypeStruct((N, H * MLA_V), BF16)],
        compiler_params=_params("parallel"), name="mla_kv")(
            ckv, gain.reshape(1, K).astype(F32), wk, wv, kr)


def _mem_attn(x, gain, wq, wo, mem_k, mem_v, *, S):
    N, D = x.shape
    tm = min(ROW_TILE, S)
    M = mem_k.shape[1]
    per_b = S // tm
    scale = XA_DIM ** -0.5

    def kern(x_ref, g_ref, wq_ref, wo_ref, k_ref, v_ref, o_ref):
        xf = x_ref[...]
        y = xf * lax.rsqrt(jnp.mean(xf * xf, axis=-1, keepdims=True) + EPS)
        a = (y * g_ref[...]).astype(BF16)
        q = (jnp.dot(a, wq_ref[...], preferred_element_type=F32) * scale).astype(BF16)
        heads = []
        for h in range(XA_HEADS):
            sl = slice(h * XA_DIM, (h + 1) * XA_DIM)
            s = lax.dot_general(q[:, sl], k_ref[:, sl], _NT, preferred_element_type=F32)
            e = jnp.exp(s - jnp.max(s, axis=1, keepdims=True))
            p = e / jnp.sum(e, axis=1, keepdims=True)
            heads.append(jnp.dot(p.astype(BF16), v_ref[:, sl], preferred_element_type=F32).astype(BF16))
        o = jnp.concatenate(heads, axis=1)
        o_ref[...] = xf + jnp.dot(o, wo_ref[...], preferred_element_type=F32)

    return pl.pallas_call(
        kern, grid=(N // tm,),
        in_specs=[pl.BlockSpec((tm, D), lambda i: (i, 0)),
                  pl.BlockSpec((1, D), lambda i: (0, 0)),
                  pl.BlockSpec(wq.shape, lambda i: (0, 0)),
                  pl.BlockSpec(wo.shape, lambda i: (0, 0)),
                  pl.BlockSpec((None, M, XA_HEADS * XA_DIM), lambda i: (i // per_b, 0, 0)),
                  pl.BlockSpec((None, M, XA_HEADS * XA_DIM), lambda i: (i // per_b, 0, 0))],
        out_specs=pl.BlockSpec((tm, D), lambda i: (i, 0)),
        out_shape=jax.ShapeDtypeStruct((N, D), F32),
        compiler_params=_params("parallel"), name="mem_attn")(
            x, gain.reshape(1, D).astype(F32), wq, wo, mem_k, mem_v)


def _ffn(x, gain, w13, w2, out_gain=None):
    N, D = x.shape
    FF = w2.shape[0]
    tm = min(ROW_TILE, N)
    chunk = 2 * LANES
    assert FF % chunk == 0

    final = out_gain is not None

    def kern(x_ref, g_ref, w13_ref, w2_ref, *rest):
        og_ref, o_ref = rest if final else (None, rest[0])
        xf = x_ref[...]
        y = xf * lax.rsqrt(jnp.mean(xf * xf, axis=-1, keepdims=True) + EPS)
        a = (y * g_ref[...]).astype(BF16)
        acc = xf
        for c in range(0, FF, chunk):
            g = jnp.dot(a, w13_ref[:, c:c + chunk], preferred_element_type=F32)
            u = jnp.dot(a, w13_ref[:, FF + c:FF + c + chunk], preferred_element_type=F32)
            hdn = (g * _sigmoid(g) * u).astype(BF16)
            acc = acc + jnp.dot(hdn, w2_ref[c:c + chunk, :], preferred_element_type=F32)
        if final:
            acc = acc * lax.rsqrt(jnp.mean(acc * acc, axis=-1, keepdims=True) + EPS) * og_ref[...]
        o_ref[...] = acc

    vec = pl.BlockSpec((1, D), lambda i: (0, 0))
    in_specs = [pl.BlockSpec((tm, D), lambda i: (i, 0)), vec,
                pl.BlockSpec(w13.shape, lambda i: (0, 0), pipeline_mode=pl.Buffered(1)),
                pl.BlockSpec(w2.shape, lambda i: (0, 0), pipeline_mode=pl.Buffered(1))]
    args = [x, gain.reshape(1, D).astype(F32), w13, w2]
    if final:
        in_specs.append(vec)
        args.append(out_gain.reshape(1, D).astype(F32))
    return pl.pallas_call(
        kern, grid=(N // tm,), in_specs=in_specs,
        out_specs=pl.BlockSpec((tm, D), lambda i: (i, 0)),
        out_shape=jax.ShapeDtypeStruct((N, D), F32),
        compiler_params=_params("parallel"), name="ffn")(*args)


def _pad_cols(w, width):
    return jnp.pad(w, ((0, 0), (0, width - w.shape[1])))


def _even_mixer(x, gain, w_in, b_f, lam, subln, w_out, layer_idx, ropes, B, S):
    D = x.shape[1]
    blk = DIFF_HEADS * 2 * DIFF_QK
    w = _pad_cols(w_in, 6 * blk + LANES).astype(BF16)
    sc = DIFF_QK ** -0.5 * LOG2E
    segs = [Seg(0, blk, BF16, "r64", sc), Seg(blk, blk, BF16, "r64"), Seg(2 * blk, blk, BF16),
            Seg(3 * blk, blk, BF16, None, sc), Seg(4 * blk, blk, BF16), Seg(5 * blk, blk, BF16),
            Seg(6 * blk, LANES, F32)]
    aq, ak, av, fq, fk, fv, fz = _linear([x], [w], segs, gain=gain, ropes=ropes, seq=S, name="even_in")
    lam_init = 0.8 - 0.6 * math.exp(-0.3 * layer_idx)
    oa = _flash(aq, ak, av.T, B=B, S=S, P=DIFF_HEADS // 2, q_w=2 * LANES, k_w=2 * LANES, v_rows=2 * LANES,
                heads=_DIFF_HEADS, kind="diff", out_dtype=BF16, lam=lam.astype(F32),
                subln=subln.astype(F32), lam_init=lam_init, name="diff_attn")
    terms = _forget_key_terms(fz, _pad_cols(b_f.reshape(1, -1), LANES).astype(F32), B, S)
    of = _flash(fq, fk, fv.T, B=B, S=S, P=FOX_HEADS // 4, q_w=2 * LANES, k_w=2 * LANES, v_rows=2 * LANES,
                heads=_PAIR_HEADS, kind="fox", out_dtype=BF16, ext=terms, name="fox_attn")
    wo = w_out.astype(BF16)
    (y,) = _linear([oa, of], [wo[:blk], wo[blk:]], [Seg(0, D, F32)], residual=x, name="even_out")
    return y


def _odd_in_weight(w_in):
    d = NSA_DIM
    o = np.cumsum((0, NSA_HEADS * d) + (NSA_GROUPS * d,) * 6 + (NSA_HEADS * 3, MLA_Q_RANK, MLA_KV_RANK, MLA_ROPE))
    nq, kc, vc, ks, vs, kw, vw, gz, cq, ckv, kr = [w_in[:, o[j]:o[j + 1]] for j in range(11)]

    def dup(wg):
        return jnp.concatenate([wg[:, :d], wg[:, :d], wg[:, d:], wg[:, d:]], axis=1)

    zeros = lambda n: jnp.zeros((w_in.shape[0], n), w_in.dtype)
    kr_slab = jnp.concatenate([zeros(MLA_NOPE), kr, zeros(LANES - MLA_NOPE - MLA_ROPE)], axis=1)
    cols = [nq, kc, vc, dup(ks), vs, dup(kw), vw, _pad_cols(gz, LANES), cq, ckv, kr_slab]
    return jnp.concatenate(cols, axis=1).astype(BF16)


def _odd_mixer(x, gain, w_in, cmp_pos, cmp_w1, cmp_w2, q_norm, kv_norm, w_uq, w_ukv, w_out, ropes, B, S):
    D = x.shape[1]
    G, d = NSA_GROUPS, NSA_DIM
    w = _odd_in_weight(w_in)
    sc = d ** -0.5 * LOG2E
    widths = [(NSA_HEADS * d, BF16, "r64", sc), (LANES, F32, "r64", 1.0), (LANES, F32, None, 1.0),
              (2 * LANES, BF16, "r64", 1.0), (LANES, BF16, None, 1.0),
              (2 * LANES, BF16, "r64", 1.0), (LANES, BF16, None, 1.0),
              (LANES, F32, None, 1.0), (MLA_Q_RANK, F32, None, 1.0), (MLA_KV_RANK, F32, None, 1.0),
              (LANES, F32, "mla", 1.0)]
    segs, start = [], 0
    for wd, dt, rp, s_ in widths:
        segs.append(Seg(start, wd, dt, rp, s_))
        start += wd
    q, kc, vc, ks, vs, kw, vw, gz, cq, ckv, kr = _linear(
        [x], [w], segs, gain=gain, ropes=ropes, seq=S, name="odd_in")

    R = S // CMP_STRIDE

    K = CMP_STRIDE * G * d
    half = CMP_LEN // 2
    pe = jnp.broadcast_to(cmp_pos.reshape(2, 2, half, 1, d), (2, 2, half, G, d)).reshape(2, 2, 1, K).astype(F32)
    eye = jnp.eye(G, dtype=cmp_w1.dtype)
    w1x = (cmp_w1.reshape(2, 2, 1, half, 1, d, CMP_HIDDEN) * eye[None, None, :, None, :, None, None])
    w1x = w1x.reshape(2, 2, G, K, CMP_HIDDEN).astype(BF16)
    w2d = jnp.concatenate([cmp_w2, cmp_w2], axis=-1).astype(BF16)
    cmp_k, cmp_vt = _nsa_compress(kc.reshape(B, R, K), vc.reshape(B, R, K), pe, w1x, w2d)
    cs = np.arange(R) * CMP_STRIDE
    bs = np.arange(LANES) * SLC_LEN
    ov = ((cs[:, None] < bs[None, :] + SLC_LEN) & (cs[:, None] + CMP_LEN > bs[None, :])
          & (np.arange(R)[:, None] < R - 1) & (bs[None, :] < S))
    o_cmp, sel = _nsa_cmp_select(q, cmp_k, cmp_vt, jnp.asarray(ov.T.astype(np.float32), BF16), B=B, S=S)
    onehot = (np.arange(S)[:, None] // SLC_LEN == np.arange(LANES)[None, :]).astype(np.float32)
    o_slc = _flash(q, ks, vs.T, B=B, S=S, P=G, q_w=2 * LANES, k_w=LANES, v_rows=HALF_LANES,
                   heads=_QUAD_HEADS, kind="slc", out_dtype=F32, ext=jnp.asarray(onehot, BF16),
                   sel=sel, name="nsa_slc")
    o_win = _flash(q, kw, vw.T, B=B, S=S, P=G, q_w=2 * LANES, k_w=LANES, v_rows=HALF_LANES,
                   heads=_QUAD_HEADS, kind="win", out_dtype=F32, name="nsa_win")
    ex = np.zeros((3, LANES, NSA_HEADS * d), np.float32)
    for hh in range(NSA_HEADS):
        for j in range(3):
            ex[j, hh * 3 + j, hh * d:(hh + 1) * d] = 1.0
    o_nsa = _nsa_combine(o_cmp, o_slc, o_win, gz, jnp.asarray(ex, BF16))

    H = MLA_HEADS
    qk = MLA_NOPE + MLA_ROPE
    wq_slab = jnp.pad(w_uq.reshape(MLA_Q_RANK, H, qk), ((0, 0), (0, 0), (0, LANES - qk)))
    wq_slab = wq_slab.reshape(MLA_Q_RANK, H * LANES).astype(BF16)
    (qm,) = _linear([cq], [wq_slab], [Seg(0, H * LANES, BF16, "mla", qk ** -0.5 * LOG2E)],
                    gain=q_norm, ropes=ropes, seq=S, name="mla_q")
    wkv = w_ukv.reshape(MLA_KV_RANK, H, MLA_NOPE + MLA_V)
    wk = jnp.pad(wkv[:, :, :MLA_NOPE], ((0, 0), (0, 0), (0, LANES - MLA_NOPE)))
    wk = wk.reshape(MLA_KV_RANK, H * LANES).astype(BF16)
    wv = wkv[:, :, MLA_NOPE:].reshape(MLA_KV_RANK, H * MLA_V).astype(BF16)
    km, vm = _mla_kv(ckv, kv_norm, wk, wv, kr)
    o_mla = _flash(qm, km, vm.T, B=B, S=S, P=H // 4, q_w=4 * LANES, k_w=4 * LANES, v_rows=2 * LANES,
                   heads=_WIDE_HEADS, kind="mla", out_dtype=BF16, name="mla_attn")
    wo = w_out.astype(BF16)
    half = NSA_HEADS * d
    (y,) = _linear([o_nsa, o_mla], [wo[:half], wo[half:]], [Seg(0, D, F32)], residual=x, name="odd_out")
    return y


def kernel(x, mem, mem_norm, norm_mix, norm_mem, norm_ffn, ev_w_in, ev_b_f, ev_lam, ev_subln, ev_w_out, od_w_in, nsa_cmp_pos, nsa_cmp_w1, nsa_cmp_w2, mla_q_norm, mla_kv_norm, mla_w_uq, mla_w_ukv, od_w_out, xa_wq, xa_wkv, xa_wo, ffn_w13, ffn_w2, final_norm):
    B, S, D = x.shape
    M = mem.shape[1]
    depth = norm_mix.shape[0]
    ropes = {
        "r64": (_rope_tables(S, NSA_DIM // 2, 0, LANES), NSA_DIM // 2),
        "mla": (_rope_tables(S, MLA_ROPE // 2, MLA_NOPE, MLA_NOPE + MLA_ROPE), MLA_ROPE // 2),
    }
    xa_w = XA_HEADS * XA_DIM
    h = x.reshape(B * S, D)
    mem2 = mem.reshape(B * M, D)
    for li in range(depth):
        j = li // 2
        if li % 2 == 0:
            h = _even_mixer(h, norm_mix[li], ev_w_in[j], ev_b_f[j], ev_lam[j], ev_subln[j], ev_w_out[j],
                            li, ropes, B, S)
        else:
            h = _odd_mixer(h, norm_mix[li], od_w_in[j], nsa_cmp_pos[j], nsa_cmp_w1[j], nsa_cmp_w2[j],
                           mla_q_norm[j], mla_kv_norm[j], mla_w_uq[j], mla_w_ukv[j], od_w_out[j],
                           ropes, B, S)
        mk, mv = _linear([mem2], [xa_wkv[li].astype(BF16)],
                         [Seg(0, xa_w, BF16), Seg(xa_w, xa_w, BF16)], gain=mem_norm, name="mem_kv")
        h = _mem_attn(h, norm_mem[li], xa_wq[li].astype(BF16), xa_wo[li].astype(BF16),
                      mk.reshape(B, M, xa_w), mv.reshape(B, M, xa_w), S=S)
        h = _ffn(h, norm_ffn[li], ffn_w13[li].astype(BF16), ffn_w2[li].astype(BF16),
                 out_gain=final_norm if li == depth - 1 else None)
    return h.reshape(B, S, D)
```

```python
import math
from typing import NamedTuple, Optional

import numpy as np
import jax
import jax.numpy as jnp
from jax import lax
from jax.experimental import pallas as pl
from jax.experimental.pallas import tpu as pltpu

F32 = jnp.float32
BF16 = jnp.bfloat16

LANES = 128
HALF_LANES = LANES // 2
BF16_ROWS = 16
ROPE_THETA = 10000.0
EPS = 1e-6
NEG = -1e30
LOG2E = math.log2(math.e)
MASK_BIG = 2.0 ** 100

DIFF_HEADS = 4
DIFF_QK = 64
FOX_HEADS = 8
NSA_HEADS = 8
NSA_GROUPS = 2
NSA_DIM = 64
CMP_LEN = 32
CMP_STRIDE = 16
CMP_HIDDEN = 128
SLC_LEN = 64
SLC_TOPK = 16
WIN = 512
FORCE_SCORE = 1e4
MLA_HEADS = 8
MLA_NOPE = 64
MLA_ROPE = 32
MLA_V = 64
MLA_Q_RANK = 384
MLA_KV_RANK = 256
XA_HEADS = 4
XA_DIM = 128

VMEM_LIMIT = 56 * 1024 * 1024
ROW_TILE = 512
COL_CHUNK = 512
ATT_TILE = 512
KEY_TILE = 512
ATT_BUFS = 3
SEL_TILE = 512
CUM_CHUNK = 256

_NT = (((1,), (1,)), ((), ()))


def _params(*sem):
    return pltpu.CompilerParams(dimension_semantics=sem, vmem_limit_bytes=VMEM_LIMIT)


def _sigmoid(x):
    return 1.0 / (1.0 + jnp.exp(-x))


def _split_bf16(x, terms):
    out = []
    r = x
    for _ in range(terms):
        h = r.astype(BF16)
        out.append(h)
        r = r - h.astype(F32)
    return out


def _rope_tables(S, half, lane_lo, lane_hi):
    pos = jnp.arange(S, dtype=F32)
    inv = 1.0 / (ROPE_THETA ** (jnp.arange(half, dtype=F32) / half))
    ang = pos[:, None] * inv[None, :]
    cos, sin = jnp.cos(ang), jnp.sin(ang)
    lane = np.arange(LANES)
    active = (lane >= lane_lo) & (lane < lane_hi)
    j = (lane - lane_lo) % (2 * half)
    lower = active & (j < half)
    upper = active & (j >= half)
    idx = j % half
    cos_t = jnp.where(active[None, :], cos[:, idx], 1.0)
    sin_a = jnp.where(lower[None, :], -sin[:, idx], 0.0)
    sin_b = jnp.where(upper[None, :], sin[:, idx], 0.0)
    return cos_t, sin_a, sin_b


class Seg(NamedTuple):
    start: int
    width: int
    dtype: object
    rope: Optional[str] = None
    scale: float = 1.0


def _linear(xs, ws, segs, *, gain=None, residual=None, ropes=None, seq=None, name="linear"):
    N = xs[0].shape[0]
    tm = min(ROW_TILE, N)
    assert N % tm == 0
    n_in = len(xs)
    has_gain = gain is not None
    has_res = residual is not None
    rope_keys = sorted({s.rope for s in segs if s.rope})
    halves = {k: ropes[k][1] for k in rope_keys}

    def kern(*refs):
        it = iter(refs)
        x_refs = [next(it) for _ in range(n_in)]
        w_refs = [next(it) for _ in range(n_in)]
        g_ref = next(it) if has_gain else None
        r_ref = next(it) if has_res else None
        tabs = {k: (next(it), next(it), next(it)) for k in rope_keys}
        o_refs = [next(it) for _ in segs]
        acts = []
        for j, xr in enumerate(x_refs):
            x = xr[...]
            if j == 0 and has_gain:
                xf = x.astype(F32)
                y = xf * lax.rsqrt(jnp.mean(xf * xf, axis=-1, keepdims=True) + EPS)
                acts.append((y * g_ref[...]).astype(BF16))
            else:
                acts.append(x.astype(BF16))
        for seg, o_ref in zip(segs, o_refs):
            for c0 in range(0, seg.width, COL_CHUNK):
                cw = min(COL_CHUNK, seg.width - c0)
                col = seg.start + c0
                acc = None
                for a, wr in zip(acts, w_refs):
                    d = jnp.dot(a, wr[:, col:col + cw], preferred_element_type=F32)
                    acc = d if acc is None else acc + d
                if has_res:
                    acc = acc + r_ref[:, col:col + cw]
                if seg.scale != 1.0:
                    acc = acc * seg.scale
                if seg.rope is None:
                    o_ref[:, c0:c0 + cw] = acc.astype(o_ref.dtype)
                else:
                    cos_r, sa_r, sb_r = tabs[seg.rope]
                    half = halves[seg.rope]
                    cos, sa, sb = cos_r[...], sa_r[...], sb_r[...]
                    for s0 in range(0, cw, LANES):
                        xs_ = acc[:, s0:s0 + LANES]
                        y = (xs_ * cos + pltpu.roll(xs_, LANES - half, 1) * sa
                             + pltpu.roll(xs_, half, 1) * sb)
                        o_ref[:, c0 + s0:c0 + s0 + LANES] = y.astype(o_ref.dtype)

    in_specs, args = [], []
    for x in xs:
        in_specs.append(pl.BlockSpec((tm, x.shape[1]), lambda i: (i, 0)))
        args.append(x)
    for w in ws:
        in_specs.append(pl.BlockSpec(w.shape, lambda i: (0, 0)))
        args.append(w)
    if has_gain:
        in_specs.append(pl.BlockSpec((1, gain.shape[-1]), lambda i: (0, 0)))
        args.append(gain.reshape(1, -1).astype(F32))
    if has_res:
        in_specs.append(pl.BlockSpec((tm, residual.shape[1]), lambda i: (i, 0)))
        args.append(residual)
    for k in rope_keys:
        assert seq % tm == 0
        nt = seq // tm
        for t in ropes[k][0]:
            in_specs.append(pl.BlockSpec((tm, LANES), lambda i, nt=nt: (i % nt, 0)))
            args.append(t)
    out_shape = [jax.ShapeDtypeStruct((N, s.width), s.dtype) for s in segs]
    out_specs = [pl.BlockSpec((tm, s.width), lambda i: (i, 0)) for s in segs]
    return pl.pallas_call(
        kern, grid=(N // tm,), in_specs=in_specs, out_specs=out_specs, out_shape=out_shape,
        compiler_params=_params("parallel"), name=name)(*args)


def _forget_key_terms(fz, b_f, B, S):
    ch = CUM_CHUNK
    place = np.zeros((3, LANES, LANES), np.float32)
    for h in range(FOX_HEADS):
        for j in range(3):
            place[j, h, 3 * h + j] = 1.0

    def kern(z_ref, b_ref, pl_ref, o_ref):
        r = lax.broadcasted_iota(jnp.int32, (ch, ch), 0)
        c = lax.broadcasted_iota(jnp.int32, (ch, ch), 1)
        tri = jnp.where(c <= r, 1.0, 0.0).astype(BF16)

        def body(j, carry):
            r0 = pl.multiple_of(j * ch, ch)
            z = z_ref[pl.ds(r0, ch), :] + b_ref[...]
            logf = -(jnp.maximum(-z, 0.0) + jnp.log1p(jnp.exp(-jnp.abs(z))))
            cs = carry
            for part in _split_bf16(logf, 3):
                cs = cs + jnp.dot(tri, part, preferred_element_type=F32)
            terms = jnp.zeros((ch, LANES), F32)
            for jj, part in enumerate(_split_bf16(cs * (-LOG2E), 3)):
                terms = terms + jnp.dot(part, pl_ref[jj], preferred_element_type=F32)
            o_ref[pl.ds(r0, ch), :] = terms.astype(o_ref.dtype)
            return cs[ch - 1:ch, :]

        lax.fori_loop(0, S // ch, body, jnp.zeros((1, LANES), F32))

    return pl.pallas_call(
        kern, grid=(B,),
        in_specs=[pl.BlockSpec((S, LANES), lambda b: (b, 0)),
                  pl.BlockSpec((1, LANES), lambda b: (0, 0)),
                  pl.BlockSpec(place.shape, lambda b: (0, 0, 0))],
        out_specs=pl.BlockSpec((S, LANES), lambda b: (b, 0)),
        out_shape=jax.ShapeDtypeStruct((B * S, LANES), BF16),
        compiler_params=_params("parallel"), name="forget_terms")(fz, b_f, jnp.asarray(place, BF16))


class Head(NamedTuple):
    q_off: int
    q_half: Optional[str]
    k_off: int
    v0: int
    v1: int


def _flash(q, k, vt, *, B, S, P, q_w, k_w, v_rows, heads, kind, out_dtype, tq=None, n_bufs=None,
           ext=None, sel=None, lam=None, subln=None, lam_init=None, name="flash"):
    TQ, TK = tq or ATT_TILE, KEY_TILE
    assert S % TQ == 0 and TQ % TK == 0
    diag = TQ // TK
    nq = S // TQ
    nh = len(heads)
    rows = heads[0].v1 - heads[0].v0
    out_w = nh // 2 * rows if kind == "diff" else nh * rows
    width = TK
    n_bufs = n_bufs or ATT_BUFS
    if kind == "win":
        assert WIN == TK and TQ == TK and S >= 2 * TK and n_bufs == 2
    has_ext = kind in ("fox", "slc")

    def kern(*refs):
        it = iter(refs)
        q_ref, k_ref, v_ref = next(it), next(it), next(it)
        ext_ref = next(it) if has_ext else None
        sel_ref = next(it) if kind == "slc" else None
        lam_ref, sub_ref = (next(it), next(it)) if kind == "diff" else (None, None)
        o_ref, m_ref, acc_ref, mx_ref = next(it), next(it), next(it), next(it)
        bufs = [next(it) for _ in range(n_bufs)]
        p_id = pl.program_id(1)
        i = pl.program_id(2)

        lane = lax.broadcasted_iota(jnp.int32, (TQ, LANES), 1)
        lo = lane < HALF_LANES
        q_all = q_ref[...]
        if kind == "slc":
            drop = ((sel_ref[...].astype(F32) - 1.0) * MASK_BIG).astype(BF16)
        qs = []
        for j, h in enumerate(heads):
            qh = q_all[:, h.q_off:h.q_off + LANES]
            if h.q_half == "lo":
                qh = jnp.where(lo, qh, jnp.zeros_like(qh))
            elif h.q_half == "hi":
                qh = jnp.where(lo, jnp.zeros_like(qh), qh)
            if kind == "fox":
                first = 3 * (nh * p_id + j)
                pick = jnp.where((lane >= first) & (lane < first + 3), 1.0, 0.0).astype(BF16)
                qh = jnp.concatenate([qh, pick], axis=1)
            if kind == "slc":
                qh = jnp.concatenate([qh, drop], axis=1)
            qs.append(qh)

        m_ref[...] = jnp.full(m_ref.shape, NEG, F32)
        acc_ref[...] = jnp.zeros(acc_ref.shape, F32)

        def scores(k0, u, masked):
            kx = k_ref[pl.ds(k0, width), :]
            ex = ext_ref[pl.ds(k0, width), :] if has_ext else None
            msk = None
            if masked:
                rel = (lax.broadcasted_iota(jnp.int32, (width, TQ), 0)
                       - lax.broadcasted_iota(jnp.int32, (width, TQ), 1))
                ahead = i * TQ - k0
                msk = rel <= ahead
                if kind == "win":
                    msk = msk & (rel > ahead - WIN)
            for hi, h in enumerate(heads):
                kk = kx[:, h.k_off:h.k_off + LANES]
                if ex is not None:
                    kk = jnp.concatenate([kk, ex], axis=1)
                st = lax.dot_general(kk, qs[hi], _NT, preferred_element_type=F32)
                if msk is not None:
                    st = jnp.where(msk, st, NEG)
                bufs[u][hi] = st
                mx_ref[u * nh + hi] = jnp.max(st, axis=0, keepdims=True)

        def update(k0, u):
            ones = jnp.ones((BF16_ROWS, width), BF16)
            for hi, h in enumerate(heads):
                m_prev = m_ref[hi]
                m_new = jnp.maximum(m_prev, mx_ref[u * nh + hi])
                alpha = jnp.exp2(m_prev - m_new)
                pt = jnp.exp2(bufs[u][hi] - m_new).astype(BF16)
                lhs = jnp.concatenate([v_ref[h.v0:h.v1, pl.ds(k0, width)], ones], axis=0)
                acc_ref[hi] = alpha * acc_ref[hi] + jnp.dot(lhs, pt, preferred_element_type=F32)
                m_ref[hi] = m_new

        tile = lambda t: pl.multiple_of(t * TK, TK)
        U = len(bufs)
        if kind == "win":
            t_first = jnp.maximum(i * TQ - WIN, 0) // TK
            scores(tile(t_first), 0, True)
            scores(tile(t_first + 1), 1, True)
            update(tile(t_first), 0)
            update(tile(t_first + 1), 1)
        else:
            n_full = i * diag
            n_rounds = jnp.maximum(n_full - 1, 0) // U
            pl.when(n_full == 0)(lambda: scores(tile(0), 0, True))
            pl.when(n_full > 0)(lambda: scores(tile(0), 0, False))

            def body(j, c):
                for u in range(U):
                    scores(tile(U * j + u + 1), (u + 1) % U, False)
                    update(tile(U * j + u), u)
                return c
            lax.fori_loop(0, n_rounds, body, 0)
            t0 = U * n_rounds

            def tail(left):
                last = left + diag - 1
                for u in range(last):
                    scores(tile(t0 + u + 1), (u + 1) % U, u + 1 >= left)
                    update(tile(t0 + u), u % U)
                update(tile(t0 + last), last % U)

            for left in range(U + 1):
                pl.when(n_full - t0 == left)(lambda left=left: tail(left))

        outs = []
        for hi in range(nh):
            acc = acc_ref[hi]
            outs.append(acc[:rows] / acc[rows:rows + 1])
        if kind == "diff":
            lm = lam_ref[...]
            la = jnp.sum(lm[0:1] * lm[1:2], axis=1, keepdims=True)
            lb = jnp.sum(lm[2:3] * lm[3:4], axis=1, keepdims=True)
            lam_full = jnp.exp(la) - jnp.exp(lb) + lam_init
            for g in range(nh // 2):
                oa = jnp.transpose(outs[2 * g] - lam_full * outs[2 * g + 1])
                y = oa * lax.rsqrt(jnp.mean(oa * oa, axis=-1, keepdims=True) + EPS)
                y = (y * sub_ref[...]) * (1.0 - lam_init)
                o_ref[:, g * rows:(g + 1) * rows] = y.astype(o_ref.dtype)
        else:
            o_ref[...] = jnp.transpose(jnp.concatenate(outs, axis=0)).astype(o_ref.dtype)

    in_specs = [
        pl.BlockSpec((TQ, q_w), lambda b, p, i: (b * nq + i, p)),
        pl.BlockSpec((S, k_w), lambda b, p, i: (b, p)),
        pl.BlockSpec((v_rows, S), lambda b, p, i: (p, b)),
    ]
    args = [q, k, vt]
    if kind == "fox":
        in_specs.append(pl.BlockSpec((S, LANES), lambda b, p, i: (b, 0)))
        args.append(ext)
    if kind == "slc":
        in_specs.append(pl.BlockSpec((S, LANES), lambda b, p, i: (0, 0)))
        in_specs.append(pl.BlockSpec((TQ, LANES), lambda b, p, i: (b * nq + i, p)))
        args += [ext, sel]
    if kind == "diff":
        in_specs.append(pl.BlockSpec(lam.shape, lambda b, p, i: (0, 0)))
        in_specs.append(pl.BlockSpec((1, LANES), lambda b, p, i: (0, 0)))
        args += [lam, subln.reshape(1, LANES)]
    return pl.pallas_call(
        kern, grid=(B, P, nq), in_specs=in_specs,
        out_specs=pl.BlockSpec((TQ, out_w), lambda b, p, i: (b * nq + i, p)),
        out_shape=jax.ShapeDtypeStruct((B * S, P * out_w), out_dtype),
        scratch_shapes=[pltpu.VMEM((nh, 1, TQ), F32),
                        pltpu.VMEM((nh, rows + BF16_ROWS, TQ), F32),
                        pltpu.VMEM((n_bufs * nh, 1, TQ), F32)]
        + [pltpu.VMEM((nh, width, TQ), F32)] * n_bufs,
        compiler_params=_params("parallel", "parallel", "arbitrary"), name=name)(*args)


_DIFF_HEADS = tuple(Head(s * LANES, half, s * LANES, s * LANES, (s + 1) * LANES)
                    for s in range(2) for half in ("lo", "hi"))
_PAIR_HEADS = tuple(Head(s * LANES, half, s * LANES, (2 * s + j) * HALF_LANES, (2 * s + j + 1) * HALF_LANES)
                    for s in range(2) for j, half in enumerate(("lo", "hi")))
_QUAD_HEADS = tuple(Head(s * LANES, half, 0, 0, HALF_LANES) for s in range(2) for half in ("lo", "hi"))
_WIDE_HEADS = tuple(Head(j * LANES, None, j * LANES, j * HALF_LANES, (j + 1) * HALF_LANES) for j in range(4))


def _nsa_compress(xk, xv, pe, w1x, w2d):
    B, R, K = xk.shape
    G = NSA_GROUPS

    def kern(xk_ref, xv_ref, pe_ref, w1_ref, w2_ref, k_ref, vt_ref):
        for t, x_ref in enumerate((xk_ref, xv_ref)):
            x = x_ref[...]
            xa = (x + pe_ref[t, 0]).astype(BF16)
            xb = (x + pe_ref[t, 1]).astype(BF16)
            for g in range(G):
                a = jnp.dot(xa, w1_ref[t, 0, g], preferred_element_type=F32)
                b = jnp.dot(xb, w1_ref[t, 1, g], preferred_element_type=F32)
                h = a + pltpu.roll(b, R - 1, 0)
                hs = h * _sigmoid(h)
                o = jnp.dot(hs.astype(BF16), w2_ref[t], preferred_element_type=F32)
                if t == 0:
                    k_ref[g] = o
                else:
                    vt_ref[g] = jnp.transpose(o)

    return pl.pallas_call(
        kern, grid=(B,),
        in_specs=[pl.BlockSpec((None, R, K), lambda b: (b, 0, 0)),
                  pl.BlockSpec((None, R, K), lambda b: (b, 0, 0)),
                  pl.BlockSpec(pe.shape, lambda b: (0, 0, 0, 0)),
                  pl.BlockSpec(w1x.shape, lambda b: (0, 0, 0, 0, 0)),
                  pl.BlockSpec(w2d.shape, lambda b: (0, 0, 0))],
        out_specs=[pl.BlockSpec((None, G, R, LANES), lambda b: (b, 0, 0, 0)),
                   pl.BlockSpec((None, G, LANES, R), lambda b: (b, 0, 0, 0))],
        out_shape=[jax.ShapeDtypeStruct((B, G, R, LANES), F32),
                   jax.ShapeDtypeStruct((B, G, LANES, R), F32)],
        compiler_params=_params("parallel"), name="nsa_compress")(xk, xv, pe, w1x, w2d)


def _nsa_cmp_select(q, cmp_k, cmp_vt, ov_t, *, B, S):
    T = SEL_TILE
    nq = S // T
    R = cmp_k.shape[2]
    G = NSA_GROUPS
    d = NSA_DIM
    n_sel = min(SLC_TOPK, S // SLC_LEN)

    bucket = min(LANES, R)
    assert R % bucket == 0 and bucket % (SLC_LEN // CMP_STRIDE) == 0

    def kern(q_ref, kc_ref, vt_ref, ov_ref, o_ref, sel_ref):
        i = pl.program_id(2)
        lane = lax.broadcasted_iota(jnp.int32, (T, LANES), 1)
        lo = lane < HALF_LANES
        q_all = q_ref[...]
        qs = []
        for hh in range(4):
            qh = q_all[:, (hh // 2) * LANES:(hh // 2 + 1) * LANES]
            qs.append(jnp.where(lo, qh, jnp.zeros_like(qh)) if hh % 2 == 0
                      else jnp.where(lo, jnp.zeros_like(qh), qh))

        def variant(rv):
            nb = rv * CMP_STRIDE // SLC_LEN
            tq = lax.broadcasted_iota(jnp.int32, (rv, T), 1) + i * T
            cend = lax.broadcasted_iota(jnp.int32, (rv, T), 0) * CMP_STRIDE + (CMP_LEN - 1)
            vis = cend <= tq
            kc = kc_ref[:rv, :].astype(BF16)
            vt = vt_ref[:d, :rv].astype(BF16)
            psum = jnp.zeros((rv, T), F32)
            o_heads = []
            for hh in range(4):
                st = lax.dot_general(kc, qs[hh], _NT, preferred_element_type=F32)
                st = jnp.where(vis, st, -jnp.inf)
                m = jnp.max(st, axis=0, keepdims=True)
                e = jnp.exp2(st - jnp.where(m > -jnp.inf, m, 0.0))
                p = e / jnp.maximum(jnp.sum(e, axis=0, keepdims=True), 1e-30)
                o_heads.append(jnp.dot(vt, p.astype(BF16), preferred_element_type=F32))
                psum = psum + p
            o_ref[...] = jnp.transpose(jnp.concatenate(o_heads, axis=0))
            imp = jnp.zeros((nb, T), F32)
            for part in _split_bf16(psum, 2):
                imp = imp + jnp.dot(ov_ref[:nb, :rv], part, preferred_element_type=F32)
            blk = lax.broadcasted_iota(jnp.int32, (nb, T), 0)
            cur = (lax.broadcasted_iota(jnp.int32, (nb, T), 1) + i * T) // SLC_LEN
            forced = (blk == 0) | (blk == cur) | (blk == cur - 1)
            work = jnp.where(forced, FORCE_SCORE, imp)
            work = jnp.where(blk > cur, NEG, work)
            blk_f = blk.astype(F32)
            for _ in range(n_sel):
                mx = jnp.max(work, axis=0, keepdims=True)
                first = jnp.min(jnp.where(work == mx, blk_f, float(LANES)), axis=0, keepdims=True)
                work = jnp.where(blk_f == first, -jnp.inf, work)
            sel = jnp.where(work == -jnp.inf, 1.0, 0.0)
            if nb < LANES:
                sel = jnp.concatenate([sel, jnp.zeros((LANES - nb, T), F32)], axis=0)
            sel_ref[...] = jnp.transpose(sel).astype(sel_ref.dtype)

        need = (i + 1) * (T // CMP_STRIDE)
        which = (need - 1) // bucket
        for bk in range(R // bucket):
            pl.when(which == bk)(lambda bk=bk: variant((bk + 1) * bucket))

    return pl.pallas_call(
        kern, grid=(B, G, nq),
        in_specs=[pl.BlockSpec((T, 2 * LANES), lambda b, g, i: (b * nq + i, g)),
                  pl.BlockSpec((None, None, R, LANES), lambda b, g, i: (b, g, 0, 0)),
                  pl.BlockSpec((None, None, LANES, R), lambda b, g, i: (b, g, 0, 0)),
                  pl.BlockSpec((LANES, R), lambda b, g, i: (0, 0))],
        out_specs=[pl.BlockSpec((T, 2 * LANES), lambda b, g, i: (b * nq + i, g)),
                   pl.BlockSpec((T, LANES), lambda b, g, i: (b * nq + i, g))],
        out_shape=[jax.ShapeDtypeStruct((B * S, G * 2 * LANES), F32),
                   jax.ShapeDtypeStruct((B * S, G * LANES), BF16)],
        compiler_params=_params("parallel", "parallel", "parallel"), name="nsa_cmp_select")(
            q, cmp_k, cmp_vt, ov_t)


def _nsa_combine(o_cmp, o_slc, o_win, gz, expand):
    N, C = o_cmp.shape
    tm = min(ROW_TILE, N)

    def kern(c_ref, s_ref, w_ref, g_ref, e_ref, o_ref):
        gate = _sigmoid(g_ref[...])
        parts = _split_bf16(gate, 2)
        acc = jnp.zeros((tm, C), F32)
        for j, br in enumerate((c_ref, s_ref, w_ref)):
            gj = sum(jnp.dot(part, e_ref[j], preferred_element_type=F32) for part in parts)
            acc = acc + gj * br[...]
        o_ref[...] = acc.astype(o_ref.dtype)

    row = pl.BlockSpec((tm, C), lambda i: (i, 0))
    return pl.pallas_call(
        kern, grid=(N // tm,),
        in_specs=[row, row, row, pl.BlockSpec((tm, LANES), lambda i: (i, 0)),
                  pl.BlockSpec(expand.shape, lambda i: (0, 0, 0))],
        out_specs=row, out_shape=jax.ShapeDtypeStruct((N, C), BF16),
        compiler_params=_params("parallel"), name="nsa_combine")(o_cmp, o_slc, o_win, gz, expand)


def _mla_kv(ckv, gain, wk, wv, kr):
    N, K = ckv.shape
    tm = min(ROW_TILE, N)
    H = MLA_HEADS

    def kern(c_ref, g_ref, wk_ref, wv_ref, kr_ref, k_ref, v_ref):
        xf = c_ref[...]
        y = xf * lax.rsqrt(jnp.mean(xf * xf, axis=-1, keepdims=True) + EPS)
        a = (y * g_ref[...]).astype(BF16)
        kr_ = kr_ref[...]
        for h in range(H):
            kh = jnp.dot(a, wk_ref[:, h * LANES:(h + 1) * LANES], preferred_element_type=F32)
            k_ref[:, h * LANES:(h + 1) * LANES] = (kh + kr_).astype(k_ref.dtype)
        v_ref[...] = jnp.dot(a, wv_ref[...], preferred_element_type=F32).astype(v_ref.dtype)

    return pl.pallas_call(
        kern, grid=(N // tm,),
        in_specs=[pl.BlockSpec((tm, K), lambda i: (i, 0)),
                  pl.BlockSpec((1, K), lambda i: (0, 0)),
                  pl.BlockSpec(wk.shape, lambda i: (0, 0)),
                  pl.BlockSpec(wv.shape, lambda i: (0, 0)),
                  pl.BlockSpec((tm, LANES), lambda i: (i, 0))],
        out_specs=[pl.BlockSpec((tm, H * LANES), lambda i: (i, 0)),
                   pl.BlockSpec((tm, H * MLA_V), lambda i: (i, 0))],
        out_shape=[jax.ShapeDtypeStruct((N, H * LANES), BF16),
                   jax.ShapeDtypeStruct((N, H * MLA_V), BF16)],
        compiler_params=_params("parallel"), name="mla_kv")(
            ckv, gain.reshape(1, K).astype(F32), wk, wv, kr)


def _mem_attn(x, gain, wq, wo, mem_k, mem_vt, *, S):
    N, D = x.shape
    tm = min(ROW_TILE, S)
    M = mem_k.shape[1]
    per_b = S // tm
    scale = XA_DIM ** -0.5 * LOG2E

    def kern(x_ref, g_ref, wq_ref, wo_ref, k_ref, vt_ref, o_ref):
        xf = x_ref[...]
        y = xf * lax.rsqrt(jnp.mean(xf * xf, axis=-1, keepdims=True) + EPS)
        a = (y * g_ref[...]).astype(BF16)
        q = (jnp.dot(a, wq_ref[...], preferred_element_type=F32) * scale).astype(BF16)
        ones = jnp.ones((BF16_ROWS, M), BF16)
        heads = []
        for h in range(XA_HEADS):
            sl = slice(h * XA_DIM, (h + 1) * XA_DIM)
            st = lax.dot_general(k_ref[:, sl], q[:, sl], _NT, preferred_element_type=F32)
            pt = jnp.exp2(st - jnp.max(st, axis=0, keepdims=True)).astype(BF16)
            lhs = jnp.concatenate([vt_ref[sl, :], ones], axis=0)
            r = jnp.dot(lhs, pt, preferred_element_type=F32)
            heads.append(r[:XA_DIM] / r[XA_DIM:XA_DIM + 1])
        o = jnp.transpose(jnp.concatenate(heads, axis=0)).astype(BF16)
        o_ref[...] = xf + jnp.dot(o, wo_ref[...], preferred_element_type=F32)

    return pl.pallas_call(
        kern, grid=(N // tm,),
        in_specs=[pl.BlockSpec((tm, D), lambda i: (i, 0)),
                  pl.BlockSpec((1, D), lambda i: (0, 0)),
                  pl.BlockSpec(wq.shape, lambda i: (0, 0)),
                  pl.BlockSpec(wo.shape, lambda i: (0, 0)),
                  pl.BlockSpec((None, M, XA_HEADS * XA_DIM), lambda i: (i // per_b, 0, 0)),
                  pl.BlockSpec((None, XA_HEADS * XA_DIM, M), lambda i: (i // per_b, 0, 0))],
        out_specs=pl.BlockSpec((tm, D), lambda i: (i, 0)),
        out_shape=jax.ShapeDtypeStruct((N, D), F32),
        compiler_params=_params("parallel"), name="mem_attn")(
            x, gain.reshape(1, D).astype(F32), wq, wo, mem_k, mem_vt)


def _ffn(x, gain, w13, w2, out_gain=None):
    N, D = x.shape
    FF = w2.shape[0]
    tm = min(ROW_TILE, N)
    chunk = 2 * LANES
    assert FF % chunk == 0

    final = out_gain is not None

    def kern(x_ref, g_ref, w13_ref, w2_ref, *rest):
        og_ref, o_ref = rest if final else (None, rest[0])
        xf = x_ref[...]
        y = xf * lax.rsqrt(jnp.mean(xf * xf, axis=-1, keepdims=True) + EPS)
        a = (y * g_ref[...]).astype(BF16)
        acc = xf
        for c in range(0, FF, chunk):
            g = jnp.dot(a, w13_ref[:, c:c + chunk], preferred_element_type=F32)
            u = jnp.dot(a, w13_ref[:, FF + c:FF + c + chunk], preferred_element_type=F32)
            hdn = (g * _sigmoid(g) * u).astype(BF16)
            acc = acc + jnp.dot(hdn, w2_ref[c:c + chunk, :], preferred_element_type=F32)
        if final:
            acc = acc * lax.rsqrt(jnp.mean(acc * acc, axis=-1, keepdims=True) + EPS) * og_ref[...]
        o_ref[...] = acc

    vec = pl.BlockSpec((1, D), lambda i: (0, 0))
    in_specs = [pl.BlockSpec((tm, D), lambda i: (i, 0)), vec,
                pl.BlockSpec(w13.shape, lambda i: (0, 0), pipeline_mode=pl.Buffered(1)),
                pl.BlockSpec(w2.shape, lambda i: (0, 0), pipeline_mode=pl.Buffered(1))]
    args = [x, gain.reshape(1, D).astype(F32), w13, w2]
    if final:
        in_specs.append(vec)
        args.append(out_gain.reshape(1, D).astype(F32))
    return pl.pallas_call(
        kern, grid=(N // tm,), in_specs=in_specs,
        out_specs=pl.BlockSpec((tm, D), lambda i: (i, 0)),
        out_shape=jax.ShapeDtypeStruct((N, D), F32),
        compiler_params=_params("parallel"), name="ffn")(*args)


def _pad_cols(w, width):
    return jnp.pad(w, ((0, 0), (0, width - w.shape[1])))


def _even_mixer(x, gain, w_in, b_f, lam, subln, w_out, layer_idx, ropes, B, S):
    D = x.shape[1]
    blk = DIFF_HEADS * 2 * DIFF_QK
    w = _pad_cols(w_in, 6 * blk + LANES).astype(BF16)
    sc = DIFF_QK ** -0.5 * LOG2E
    segs = [Seg(0, blk, BF16, "r64", sc), Seg(blk, blk, BF16, "r64"), Seg(2 * blk, blk, BF16),
            Seg(3 * blk, blk, BF16, None, sc), Seg(4 * blk, blk, BF16), Seg(5 * blk, blk, BF16),
            Seg(6 * blk, LANES, F32)]
    aq, ak, av, fq, fk, fv, fz = _linear([x], [w], segs, gain=gain, ropes=ropes, seq=S, name="even_in")
    lam_init = 0.8 - 0.6 * math.exp(-0.3 * layer_idx)
    oa = _flash(aq, ak, av.T, B=B, S=S, P=DIFF_HEADS // 2, q_w=2 * LANES, k_w=2 * LANES, v_rows=2 * LANES,
                heads=_DIFF_HEADS, kind="diff", out_dtype=BF16, lam=lam.astype(F32),
                subln=subln.astype(F32), lam_init=lam_init, name="diff_attn")
    terms = _forget_key_terms(fz, _pad_cols(b_f.reshape(1, -1), LANES).astype(F32), B, S)
    of = _flash(fq, fk, fv.T, B=B, S=S, P=FOX_HEADS // 4, q_w=2 * LANES, k_w=2 * LANES, v_rows=2 * LANES,
                heads=_PAIR_HEADS, kind="fox", out_dtype=BF16, ext=terms, name="fox_attn")
    wo = w_out.astype(BF16)
    (y,) = _linear([oa, of], [wo[:blk], wo[blk:]], [Seg(0, D, F32)], residual=x, name="even_out")
    return y


def _odd_in_weight(w_in):
    d = NSA_DIM
    o = np.cumsum((0, NSA_HEADS * d) + (NSA_GROUPS * d,) * 6 + (NSA_HEADS * 3, MLA_Q_RANK, MLA_KV_RANK, MLA_ROPE))
    nq, kc, vc, ks, vs, kw, vw, gz, cq, ckv, kr = [w_in[:, o[j]:o[j + 1]] for j in range(11)]

    def dup(wg):
        return jnp.concatenate([wg[:, :d], wg[:, :d], wg[:, d:], wg[:, d:]], axis=1)

    zeros = lambda n: jnp.zeros((w_in.shape[0], n), w_in.dtype)
    kr_slab = jnp.concatenate([zeros(MLA_NOPE), kr, zeros(LANES - MLA_NOPE - MLA_ROPE)], axis=1)
    cols = [nq, kc, vc, dup(ks), vs, dup(kw), vw, _pad_cols(gz, LANES), cq, ckv, kr_slab]
    return jnp.concatenate(cols, axis=1).astype(BF16)


def _odd_mixer(x, gain, w_in, cmp_pos, cmp_w1, cmp_w2, q_norm, kv_norm, w_uq, w_ukv, w_out, ropes, B, S):
    D = x.shape[1]
    G, d = NSA_GROUPS, NSA_DIM
    w = _odd_in_weight(w_in)
    sc = d ** -0.5 * LOG2E
    widths = [(NSA_HEADS * d, BF16, "r64", sc), (LANES, F32, "r64", 1.0), (LANES, F32, None, 1.0),
              (2 * LANES, BF16, "r64", 1.0), (LANES, BF16, None, 1.0),
              (2 * LANES, BF16, "r64", 1.0), (LANES, BF16, None, 1.0),
              (LANES, F32, None, 1.0), (MLA_Q_RANK, F32, None, 1.0), (MLA_KV_RANK, F32, None, 1.0),
              (LANES, F32, "mla", 1.0)]
    segs, start = [], 0
    for wd, dt, rp, s_ in widths:
        segs.append(Seg(start, wd, dt, rp, s_))
        start += wd
    q, kc, vc, ks, vs, kw, vw, gz, cq, ckv, kr = _linear(
        [x], [w], segs, gain=gain, ropes=ropes, seq=S, name="odd_in")

    R = S // CMP_STRIDE

    K = CMP_STRIDE * G * d
    half = CMP_LEN // 2
    pe = jnp.broadcast_to(cmp_pos.reshape(2, 2, half, 1, d), (2, 2, half, G, d)).reshape(2, 2, 1, K).astype(F32)
    eye = jnp.eye(G, dtype=cmp_w1.dtype)
    w1x = (cmp_w1.reshape(2, 2, 1, half, 1, d, CMP_HIDDEN) * eye[None, None, :, None, :, None, None])
    w1x = w1x.reshape(2, 2, G, K, CMP_HIDDEN).astype(BF16)
    w2d = jnp.concatenate([cmp_w2, cmp_w2], axis=-1).astype(BF16)
    cmp_k, cmp_vt = _nsa_compress(kc.reshape(B, R, K), vc.reshape(B, R, K), pe, w1x, w2d)
    cs = np.arange(R) * CMP_STRIDE
    bs = np.arange(LANES) * SLC_LEN
    ov = ((cs[:, None] < bs[None, :] + SLC_LEN) & (cs[:, None] + CMP_LEN > bs[None, :])
          & (np.arange(R)[:, None] < R - 1) & (bs[None, :] < S))
    o_cmp, sel = _nsa_cmp_select(q, cmp_k, cmp_vt, jnp.asarray(ov.T.astype(np.float32), BF16), B=B, S=S)
    onehot = (np.arange(S)[:, None] // SLC_LEN == np.arange(LANES)[None, :]).astype(np.float32)
    o_slc = _flash(q, ks, vs.T, B=B, S=S, P=G, q_w=2 * LANES, k_w=LANES, v_rows=HALF_LANES,
                   heads=_QUAD_HEADS, kind="slc", out_dtype=F32, ext=jnp.asarray(onehot, BF16),
                   sel=sel, name="nsa_slc")
    o_win = _flash(q, kw, vw.T, B=B, S=S, P=G, q_w=2 * LANES, k_w=LANES, v_rows=HALF_LANES,
                   heads=_QUAD_HEADS, kind="win", out_dtype=F32, tq=WIN, n_bufs=2, name="nsa_win")
    ex = np.zeros((3, LANES, NSA_HEADS * d), np.float32)
    for hh in range(NSA_HEADS):
        for j in range(3):
            ex[j, hh * 3 + j, hh * d:(hh + 1) * d] = 1.0
    o_nsa = _nsa_combine(o_cmp, o_slc, o_win, gz, jnp.asarray(ex, BF16))

    H = MLA_HEADS
    qk = MLA_NOPE + MLA_ROPE
    wq_slab = jnp.pad(w_uq.reshape(MLA_Q_RANK, H, qk), ((0, 0), (0, 0), (0, LANES - qk)))
    wq_slab = wq_slab.reshape(MLA_Q_RANK, H * LANES).astype(BF16)
    (qm,) = _linear([cq], [wq_slab], [Seg(0, H * LANES, BF16, "mla", qk ** -0.5 * LOG2E)],
                    gain=q_norm, ropes=ropes, seq=S, name="mla_q")
    wkv = w_ukv.reshape(MLA_KV_RANK, H, MLA_NOPE + MLA_V)
    wk = jnp.pad(wkv[:, :, :MLA_NOPE], ((0, 0), (0, 0), (0, LANES - MLA_NOPE)))
    wk = wk.reshape(MLA_KV_RANK, H * LANES).astype(BF16)
    wv = wkv[:, :, MLA_NOPE:].reshape(MLA_KV_RANK, H * MLA_V).astype(BF16)
    km, vm = _mla_kv(ckv, kv_norm, wk, wv, kr)
    o_mla = _flash(qm, km, vm.T, B=B, S=S, P=H // 4, q_w=4 * LANES, k_w=4 * LANES, v_rows=2 * LANES,
                   heads=_WIDE_HEADS, kind="mla", out_dtype=BF16, name="mla_attn")
    wo = w_out.astype(BF16)
    half = NSA_HEADS * d
    (y,) = _linear([o_nsa, o_mla], [wo[:half], wo[half:]], [Seg(0, D, F32)], residual=x, name="odd_out")
    return y


def kernel(x, mem, mem_norm, norm_mix, norm_mem, norm_ffn, ev_w_in, ev_b_f, ev_lam, ev_subln, ev_w_out, od_w_in, nsa_cmp_pos, nsa_cmp_w1, nsa_cmp_w2, mla_q_norm, mla_kv_norm, mla_w_uq, mla_w_ukv, od_w_out, xa_wq, xa_wkv, xa_wo, ffn_w13, ffn_w2, final_norm):
    B, S, D = x.shape
    M = mem.shape[1]
    depth = norm_mix.shape[0]
    ropes = {
        "r64": (_rope_tables(S, NSA_DIM // 2, 0, LANES), NSA_DIM // 2),
        "mla": (_rope_tables(S, MLA_ROPE // 2, MLA_NOPE, MLA_NOPE + MLA_ROPE), MLA_ROPE // 2),
    }
    xa_w = XA_HEADS * XA_DIM
    h = x.reshape(B * S, D)
    mem2 = mem.reshape(B * M, D)
    for li in range(depth):
        j = li // 2
        if li % 2 == 0:
            h = _even_mixer(h, norm_mix[li], ev_w_in[j], ev_b_f[j], ev_lam[j], ev_subln[j], ev_w_out[j],
                            li, ropes, B, S)
        else:
            h = _odd_mixer(h, norm_mix[li], od_w_in[j], nsa_cmp_pos[j], nsa_cmp_w1[j], nsa_cmp_w2[j],
                           mla_q_norm[j], mla_kv_norm[j], mla_w_uq[j], mla_w_ukv[j], od_w_out[j],
                           ropes, B, S)
        mk, mv = _linear([mem2], [xa_wkv[li].astype(BF16)],
                         [Seg(0, xa_w, BF16), Seg(xa_w, xa_w, BF16)], gain=mem_norm, name="mem_kv")
        h = _mem_attn(h, norm_mem[li], xa_wq[li].astype(BF16), xa_wo[li].astype(BF16),
                      mk.reshape(B, M, xa_w), mv.reshape(B, M, xa_w).transpose(0, 2, 1), S=S)
        h = _ffn(h, norm_ffn[li], ffn_w13[li].astype(BF16), ffn_w2[li].astype(BF16),
                 out_gain=final_norm if li == depth - 1 else None)
    return h.reshape(B, S, D)
```

```python
import math
from typing import NamedTuple, Optional

import numpy as np
import jax
import jax.numpy as jnp
from jax import lax
from jax.experimental import pallas as pl
from jax.experimental.pallas import tpu as pltpu

F32 = jnp.float32
BF16 = jnp.bfloat16

LANES = 128
HALF_LANES = LANES // 2
BF16_ROWS = 16
ROPE_THETA = 10000.0
EPS = 1e-6
NEG = -1e30
LOG2E = math.log2(math.e)
MASK_BIG = 2.0 ** 100

DIFF_HEADS = 4
DIFF_QK = 64
FOX_HEADS = 8
NSA_HEADS = 8
NSA_GROUPS = 2
NSA_DIM = 64
CMP_LEN = 32
CMP_STRIDE = 16
CMP_HIDDEN = 128
SLC_LEN = 64
SLC_TOPK = 16
WIN = 512
FORCE_SCORE = 1e4
MLA_HEADS = 8
MLA_NOPE = 64
MLA_ROPE = 32
MLA_V = 64
MLA_Q_RANK = 384
MLA_KV_RANK = 256
XA_HEADS = 4
XA_DIM = 128

VMEM_LIMIT = 56 * 1024 * 1024
ROW_TILE = 512
COL_CHUNK = 512
ATT_TILE = 512
KEY_TILE = 512
ATT_BUFS = 3
SEL_TILE = 512
CUM_CHUNK = 256

_NT = (((1,), (1,)), ((), ()))


def _params(*sem):
    return pltpu.CompilerParams(dimension_semantics=sem, vmem_limit_bytes=VMEM_LIMIT)


def _sigmoid(x):
    return 1.0 / (1.0 + jnp.exp(-x))


def _split_bf16(x, terms):
    out = []
    r = x
    for _ in range(terms):
        h = r.astype(BF16)
        out.append(h)
        r = r - h.astype(F32)
    return out


def _rope_tables(S, half, lane_lo, lane_hi):
    pos = jnp.arange(S, dtype=F32)
    inv = 1.0 / (ROPE_THETA ** (jnp.arange(half, dtype=F32) / half))
    ang = pos[:, None] * inv[None, :]
    cos, sin = jnp.cos(ang), jnp.sin(ang)
    lane = np.arange(LANES)
    active = (lane >= lane_lo) & (lane < lane_hi)
    j = (lane - lane_lo) % (2 * half)
    lower = active & (j < half)
    upper = active & (j >= half)
    idx = j % half
    cos_t = jnp.where(active[None, :], cos[:, idx], 1.0)
    sin_a = jnp.where(lower[None, :], -sin[:, idx], 0.0)
    sin_b = jnp.where(upper[None, :], sin[:, idx], 0.0)
    return cos_t, sin_a, sin_b


class Seg(NamedTuple):
    start: int
    width: int
    dtype: object
    rope: Optional[str] = None
    scale: float = 1.0


def _linear(xs, ws, segs, *, gain=None, residual=None, ropes=None, seq=None, name="linear"):
    N = xs[0].shape[0]
    tm = min(ROW_TILE, N)
    assert N % tm == 0
    n_in = len(xs)
    has_gain = gain is not None
    has_res = residual is not None
    rope_keys = sorted({s.rope for s in segs if s.rope})
    halves = {k: ropes[k][1] for k in rope_keys}

    def kern(*refs):
        it = iter(refs)
        x_refs = [next(it) for _ in range(n_in)]
        w_refs = [next(it) for _ in range(n_in)]
        g_ref = next(it) if has_gain else None
        r_ref = next(it) if has_res else None
        tabs = {k: (next(it), next(it), next(it)) for k in rope_keys}
        o_refs = [next(it) for _ in segs]
        acts = []
        for j, xr in enumerate(x_refs):
            x = xr[...]
            if j == 0 and has_gain:
                xf = x.astype(F32)
                y = xf * lax.rsqrt(jnp.mean(xf * xf, axis=-1, keepdims=True) + EPS)
                acts.append((y * g_ref[...]).astype(BF16))
            else:
                acts.append(x.astype(BF16))
        for seg, o_ref in zip(segs, o_refs):
            for c0 in range(0, seg.width, COL_CHUNK):
                cw = min(COL_CHUNK, seg.width - c0)
                col = seg.start + c0
                acc = None
                for a, wr in zip(acts, w_refs):
                    d = jnp.dot(a, wr[:, col:col + cw], preferred_element_type=F32)
                    acc = d if acc is None else acc + d
                if has_res:
                    acc = acc + r_ref[:, col:col + cw]
                if seg.scale != 1.0:
                    acc = acc * seg.scale
                if seg.rope is None:
                    o_ref[:, c0:c0 + cw] = acc.astype(o_ref.dtype)
                else:
                    cos_r, sa_r, sb_r = tabs[seg.rope]
                    half = halves[seg.rope]
                    cos, sa, sb = cos_r[...], sa_r[...], sb_r[...]
                    for s0 in range(0, cw, LANES):
                        xs_ = acc[:, s0:s0 + LANES]
                        y = (xs_ * cos + pltpu.roll(xs_, LANES - half, 1) * sa
                             + pltpu.roll(xs_, half, 1) * sb)
                        o_ref[:, c0 + s0:c0 + s0 + LANES] = y.astype(o_ref.dtype)

    in_specs, args = [], []
    for x in xs:
        in_specs.append(pl.BlockSpec((tm, x.shape[1]), lambda i: (i, 0)))
        args.append(x)
    for w in ws:
        in_specs.append(pl.BlockSpec(w.shape, lambda i: (0, 0)))
        args.append(w)
    if has_gain:
        in_specs.append(pl.BlockSpec((1, gain.shape[-1]), lambda i: (0, 0)))
        args.append(gain.reshape(1, -1).astype(F32))
    if has_res:
        in_specs.append(pl.BlockSpec((tm, residual.shape[1]), lambda i: (i, 0)))
        args.append(residual)
    for k in rope_keys:
        assert seq % tm == 0
        nt = seq // tm
        for t in ropes[k][0]:
            in_specs.append(pl.BlockSpec((tm, LANES), lambda i, nt=nt: (i % nt, 0)))
            args.append(t)
    out_shape = [jax.ShapeDtypeStruct((N, s.width), s.dtype) for s in segs]
    out_specs = [pl.BlockSpec((tm, s.width), lambda i: (i, 0)) for s in segs]
    return pl.pallas_call(
        kern, grid=(N // tm,), in_specs=in_specs, out_specs=out_specs, out_shape=out_shape,
        compiler_params=_params("parallel"), name=name)(*args)


def _forget_key_terms(fz, b_f, B, S):
    ch = CUM_CHUNK
    place = np.zeros((3, LANES, LANES), np.float32)
    for h in range(FOX_HEADS):
        for j in range(3):
            place[j, h, 3 * h + j] = 1.0

    def kern(z_ref, b_ref, pl_ref, o_ref):
        r = lax.broadcasted_iota(jnp.int32, (ch, ch), 0)
        c = lax.broadcasted_iota(jnp.int32, (ch, ch), 1)
        tri = jnp.where(c <= r, 1.0, 0.0).astype(BF16)

        def body(j, carries):
            out = []
            for b, carry in enumerate(carries):
                r0 = pl.multiple_of(b * S + j * ch, ch)
                z = z_ref[pl.ds(r0, ch), :] + b_ref[...]
                logf = -(jnp.maximum(-z, 0.0) + jnp.log1p(jnp.exp(-jnp.abs(z))))
                cs = carry
                for part in _split_bf16(logf, 3):
                    cs = cs + jnp.dot(tri, part, preferred_element_type=F32)
                terms = jnp.zeros((ch, LANES), F32)
                for jj, part in enumerate(_split_bf16(cs * (-LOG2E), 3)):
                    terms = terms + jnp.dot(part, pl_ref[jj], preferred_element_type=F32)
                o_ref[pl.ds(r0, ch), :] = terms.astype(o_ref.dtype)
                out.append(cs[ch - 1:ch, :])
            return tuple(out)

        lax.fori_loop(0, S // ch, body, tuple(jnp.zeros((1, LANES), F32) for _ in range(B)))

    return pl.pallas_call(
        kern, grid=(1,),
        in_specs=[pl.BlockSpec((B * S, LANES), lambda g: (0, 0), pipeline_mode=pl.Buffered(1)),
                  pl.BlockSpec((1, LANES), lambda g: (0, 0)),
                  pl.BlockSpec(place.shape, lambda g: (0, 0, 0))],
        out_specs=pl.BlockSpec((B * S, LANES), lambda g: (0, 0)),
        out_shape=jax.ShapeDtypeStruct((B * S, LANES), BF16),
        compiler_params=_params("arbitrary"), name="forget_terms")(fz, b_f, jnp.asarray(place, BF16))


class Head(NamedTuple):
    q_off: int
    q_half: Optional[str]
    k_off: int
    v0: int
    v1: int


def _flash(q, k, vt, *, B, S, P, q_w, k_w, v_rows, heads, kind, out_dtype, tq=None, tk=None, n_bufs=None,
           ext=None, sel=None, lam=None, subln=None, lam_init=None, name="flash"):
    TQ, TK = tq or ATT_TILE, tk or KEY_TILE
    assert S % TQ == 0 and TQ % TK == 0
    diag = TQ // TK
    nq = S // TQ
    nh = len(heads)
    rows = heads[0].v1 - heads[0].v0
    out_w = nh // 2 * rows if kind == "diff" else nh * rows
    width = TK
    n_bufs = n_bufs or ATT_BUFS
    if kind == "win":
        assert WIN == TK and TQ == TK and S >= 2 * TK and n_bufs == 2
    has_ext = kind in ("fox", "slc")

    def kern(*refs):
        it = iter(refs)
        q_ref, k_ref, v_ref = next(it), next(it), next(it)
        ext_ref = next(it) if has_ext else None
        sel_ref = next(it) if kind == "slc" else None
        lam_ref, sub_ref = (next(it), next(it)) if kind == "diff" else (None, None)
        o_ref, m_ref, acc_ref, mx_ref = next(it), next(it), next(it), next(it)
        bufs = [next(it) for _ in range(n_bufs)]
        p_id = pl.program_id(1)
        i = pl.program_id(2)

        lane = lax.broadcasted_iota(jnp.int32, (TQ, LANES), 1)
        lo = lane < HALF_LANES
        q_all = q_ref[...]
        if kind == "slc":
            drop = ((sel_ref[...].astype(F32) - 1.0) * MASK_BIG).astype(BF16)
        qs = []
        for j, h in enumerate(heads):
            qh = q_all[:, h.q_off:h.q_off + LANES]
            if h.q_half == "lo":
                qh = jnp.where(lo, qh, jnp.zeros_like(qh))
            elif h.q_half == "hi":
                qh = jnp.where(lo, jnp.zeros_like(qh), qh)
            if kind == "fox":
                first = 3 * (nh * p_id + j)
                pick = jnp.where((lane >= first) & (lane < first + 3), 1.0, 0.0).astype(BF16)
                qh = jnp.concatenate([qh, pick], axis=1)
            if kind == "slc":
                qh = jnp.concatenate([qh, drop], axis=1)
            qs.append(qh)

        m_ref[...] = jnp.full(m_ref.shape, NEG, F32)
        acc_ref[...] = jnp.zeros(acc_ref.shape, F32)

        def scores(k0, u, masked):
            kx = k_ref[pl.ds(k0, width), :]
            ex = ext_ref[pl.ds(k0, width), :] if has_ext else None
            msk = None
            if masked:
                rel = (lax.broadcasted_iota(jnp.int32, (width, TQ), 0)
                       - lax.broadcasted_iota(jnp.int32, (width, TQ), 1))
                ahead = i * TQ - k0
                msk = rel <= ahead
                if kind == "win":
                    msk = msk & (rel > ahead - WIN)
            for hi, h in enumerate(heads):
                kk = kx[:, h.k_off:h.k_off + LANES]
                if ex is not None:
                    kk = jnp.concatenate([kk, ex], axis=1)
                st = lax.dot_general(kk, qs[hi], _NT, preferred_element_type=F32)
                if msk is not None:
                    st = jnp.where(msk, st, NEG)
                bufs[u][hi] = st
                mx_ref[u * nh + hi] = jnp.max(st, axis=0, keepdims=True)

        def update(k0, u):
            ones = jnp.ones((BF16_ROWS, width), BF16)
            for hi, h in enumerate(heads):
                m_prev = m_ref[hi]
                m_new = jnp.maximum(m_prev, mx_ref[u * nh + hi])
                alpha = jnp.exp2(m_prev - m_new)
                pt = jnp.exp2(bufs[u][hi] - m_new).astype(BF16)
                lhs = jnp.concatenate([v_ref[h.v0:h.v1, pl.ds(k0, width)], ones], axis=0)
                acc_ref[hi] = alpha * acc_ref[hi] + jnp.dot(lhs, pt, preferred_element_type=F32)
                m_ref[hi] = m_new

        tile = lambda t: pl.multiple_of(t * TK, TK)
        U = len(bufs)
        if kind == "win":
            t_first = jnp.maximum(i * TQ - WIN, 0) // TK
            scores(tile(t_first), 0, True)
            scores(tile(t_first + 1), 1, True)
            update(tile(t_first), 0)
            update(tile(t_first + 1), 1)
        else:
            n_full = i * diag
            n_rounds = jnp.maximum(n_full - 1, 0) // U
            pl.when(n_full == 0)(lambda: scores(tile(0), 0, True))
            pl.when(n_full > 0)(lambda: scores(tile(0), 0, False))

            def body(j, c):
                for u in range(U):
                    scores(tile(U * j + u + 1), (u + 1) % U, False)
                    update(tile(U * j + u), u)
                return c
            lax.fori_loop(0, n_rounds, body, 0)
            t0 = U * n_rounds

            def tail(left):
                last = left + diag - 1
                for u in range(last):
                    scores(tile(t0 + u + 1), (u + 1) % U, u + 1 >= left)
                    update(tile(t0 + u), u % U)
                update(tile(t0 + last), last % U)

            for left in range(U + 1):
                pl.when(n_full - t0 == left)(lambda left=left: tail(left))

        outs = []
        for hi in range(nh):
            acc = acc_ref[hi]
            outs.append(acc[:rows] / acc[rows:rows + 1])
        if kind == "diff":
            lm = lam_ref[...]
            la = jnp.sum(lm[0:1] * lm[1:2], axis=1, keepdims=True)
            lb = jnp.sum(lm[2:3] * lm[3:4], axis=1, keepdims=True)
            lam_full = jnp.exp(la) - jnp.exp(lb) + lam_init
            for g in range(nh // 2):
                oa = jnp.transpose(outs[2 * g] - lam_full * outs[2 * g + 1])
                y = oa * lax.rsqrt(jnp.mean(oa * oa, axis=-1, keepdims=True) + EPS)
                y = (y * sub_ref[...]) * (1.0 - lam_init)
                o_ref[:, g * rows:(g + 1) * rows] = y.astype(o_ref.dtype)
        else:
            o_ref[...] = jnp.transpose(jnp.concatenate(outs, axis=0)).astype(o_ref.dtype)

    in_specs = [
        pl.BlockSpec((TQ, q_w), lambda b, p, i: (b * nq + i, p)),
        pl.BlockSpec((S, k_w), lambda b, p, i: (b, p)),
        pl.BlockSpec((v_rows, S), lambda b, p, i: (p, b)),
    ]
    args = [q, k, vt]
    if kind == "fox":
        in_specs.append(pl.BlockSpec((S, LANES), lambda b, p, i: (b, 0)))
        args.append(ext)
    if kind == "slc":
        in_specs.append(pl.BlockSpec((S, LANES), lambda b, p, i: (0, 0)))
        in_specs.append(pl.BlockSpec((TQ, LANES), lambda b, p, i: (b * nq + i, p)))
        args += [ext, sel]
    if kind == "diff":
        in_specs.append(pl.BlockSpec(lam.shape, lambda b, p, i: (0, 0)))
        in_specs.append(pl.BlockSpec((1, LANES), lambda b, p, i: (0, 0)))
        args += [lam, subln.reshape(1, LANES)]
    return pl.pallas_call(
        kern, grid=(B, P, nq), in_specs=in_specs,
        out_specs=pl.BlockSpec((TQ, out_w), lambda b, p, i: (b * nq + i, p)),
        out_shape=jax.ShapeDtypeStruct((B * S, P * out_w), out_dtype),
        scratch_shapes=[pltpu.VMEM((nh, 1, TQ), F32),
                        pltpu.VMEM((nh, rows + BF16_ROWS, TQ), F32),
                        pltpu.VMEM((n_bufs * nh, 1, TQ), F32)]
        + [pltpu.VMEM((nh, width, TQ), F32)] * n_bufs,
        compiler_params=_params("parallel", "parallel", "arbitrary"), name=name)(*args)


_DIFF_HEADS = tuple(Head(s * LANES, half, s * LANES, s * LANES, (s + 1) * LANES)
                    for s in range(2) for half in ("lo", "hi"))
_PAIR_HEADS = tuple(Head(s * LANES, half, s * LANES, (2 * s + j) * HALF_LANES, (2 * s + j + 1) * HALF_LANES)
                    for s in range(2) for j, half in enumerate(("lo", "hi")))
_QUAD_HEADS = tuple(Head(s * LANES, half, 0, 0, HALF_LANES) for s in range(2) for half in ("lo", "hi"))
_WIDE_HEADS = tuple(Head(j * LANES, None, j * LANES, j * HALF_LANES, (j + 1) * HALF_LANES) for j in range(4))


def _nsa_compress(xk, xv, pe, w1x, w2d):
    B, R, K = xk.shape
    G = NSA_GROUPS

    def kern(xk_ref, xv_ref, pe_ref, w1_ref, w2_ref, k_ref, vt_ref):
        for t, x_ref in enumerate((xk_ref, xv_ref)):
            x = x_ref[...]
            xa = (x + pe_ref[t, 0]).astype(BF16)
            xb = (x + pe_ref[t, 1]).astype(BF16)
            for g in range(G):
                a = jnp.dot(xa, w1_ref[t, 0, g], preferred_element_type=F32)
                b = jnp.dot(xb, w1_ref[t, 1, g], preferred_element_type=F32)
                h = a + pltpu.roll(b, R - 1, 0)
                hs = h * _sigmoid(h)
                o = jnp.dot(hs.astype(BF16), w2_ref[t], preferred_element_type=F32)
                if t == 0:
                    k_ref[g] = o
                else:
                    vt_ref[g] = jnp.transpose(o)

    return pl.pallas_call(
        kern, grid=(B,),
        in_specs=[pl.BlockSpec((None, R, K), lambda b: (b, 0, 0)),
                  pl.BlockSpec((None, R, K), lambda b: (b, 0, 0)),
                  pl.BlockSpec(pe.shape, lambda b: (0, 0, 0, 0)),
                  pl.BlockSpec(w1x.shape, lambda b: (0, 0, 0, 0, 0)),
                  pl.BlockSpec(w2d.shape, lambda b: (0, 0, 0))],
        out_specs=[pl.BlockSpec((None, G, R, LANES), lambda b: (b, 0, 0, 0)),
                   pl.BlockSpec((None, G, LANES, R), lambda b: (b, 0, 0, 0))],
        out_shape=[jax.ShapeDtypeStruct((B, G, R, LANES), F32),
                   jax.ShapeDtypeStruct((B, G, LANES, R), F32)],
        compiler_params=_params("parallel"), name="nsa_compress")(xk, xv, pe, w1x, w2d)


def _nsa_cmp_select(q, cmp_k, cmp_vt, ov_t, *, B, S):
    T = SEL_TILE
    nq = S // T
    R = cmp_k.shape[2]
    G = NSA_GROUPS
    d = NSA_DIM
    n_sel = min(SLC_TOPK, S // SLC_LEN)

    bucket = min(LANES, R)
    assert R % bucket == 0 and bucket % (SLC_LEN // CMP_STRIDE) == 0

    def kern(q_ref, kc_ref, vt_ref, ov_ref, o_ref, sel_ref):
        i = pl.program_id(2)
        lane = lax.broadcasted_iota(jnp.int32, (T, LANES), 1)
        lo = lane < HALF_LANES
        q_all = q_ref[...]
        qs = []
        for hh in range(4):
            qh = q_all[:, (hh // 2) * LANES:(hh // 2 + 1) * LANES]
            qs.append(jnp.where(lo, qh, jnp.zeros_like(qh)) if hh % 2 == 0
                      else jnp.where(lo, jnp.zeros_like(qh), qh))

        def variant(rv):
            nb = rv * CMP_STRIDE // SLC_LEN
            tq = lax.broadcasted_iota(jnp.int32, (rv, T), 1) + i * T
            cend = lax.broadcasted_iota(jnp.int32, (rv, T), 0) * CMP_STRIDE + (CMP_LEN - 1)
            vis = cend <= tq
            kc = kc_ref[:rv, :].astype(BF16)
            vt = vt_ref[:d, :rv].astype(BF16)
            psum = jnp.zeros((rv, T), F32)
            o_heads = []
            for hh in range(4):
                st = lax.dot_general(kc, qs[hh], _NT, preferred_element_type=F32)
                st = jnp.where(vis, st, -jnp.inf)
                m = jnp.max(st, axis=0, keepdims=True)
                e = jnp.exp2(st - jnp.where(m > -jnp.inf, m, 0.0))
                p = e / jnp.maximum(jnp.sum(e, axis=0, keepdims=True), 1e-30)
                o_heads.append(jnp.dot(vt, p.astype(BF16), preferred_element_type=F32))
                psum = psum + p
            o_ref[...] = jnp.transpose(jnp.concatenate(o_heads, axis=0))
            imp = jnp.zeros((nb, T), F32)
            for part in _split_bf16(psum, 2):
                imp = imp + jnp.dot(ov_ref[:nb, :rv], part, preferred_element_type=F32)
            blk = lax.broadcasted_iota(jnp.int32, (nb, T), 0)
            cur = (lax.broadcasted_iota(jnp.int32, (nb, T), 1) + i * T) // SLC_LEN
            forced = (blk == 0) | (blk == cur) | (blk == cur - 1)
            work = jnp.where(forced, FORCE_SCORE, imp)
            work = jnp.where(blk > cur, NEG, work)
            blk_f = blk.astype(F32)
            for _ in range(n_sel):
                mx = jnp.max(work, axis=0, keepdims=True)
                first = jnp.min(jnp.where(work == mx, blk_f, float(LANES)), axis=0, keepdims=True)
                work = jnp.where(blk_f == first, -jnp.inf, work)
            sel = jnp.where(work == -jnp.inf, 1.0, 0.0)
            if nb < LANES:
                sel = jnp.concatenate([sel, jnp.zeros((LANES - nb, T), F32)], axis=0)
            sel_ref[...] = jnp.transpose(sel).astype(sel_ref.dtype)

        need = (i + 1) * (T // CMP_STRIDE)
        which = (need - 1) // bucket
        for bk in range(R // bucket):
            pl.when(which == bk)(lambda bk=bk: variant((bk + 1) * bucket))

    return pl.pallas_call(
        kern, grid=(B, G, nq),
        in_specs=[pl.BlockSpec((T, 2 * LANES), lambda b, g, i: (b * nq + i, g)),
                  pl.BlockSpec((None, None, R, LANES), lambda b, g, i: (b, g, 0, 0)),
                  pl.BlockSpec((None, None, LANES, R), lambda b, g, i: (b, g, 0, 0)),
                  pl.BlockSpec((LANES, R), lambda b, g, i: (0, 0))],
        out_specs=[pl.BlockSpec((T, 2 * LANES), lambda b, g, i: (b * nq + i, g)),
                   pl.BlockSpec((T, LANES), lambda b, g, i: (b * nq + i, g))],
        out_shape=[jax.ShapeDtypeStruct((B * S, G * 2 * LANES), F32),
                   jax.ShapeDtypeStruct((B * S, G * LANES), BF16)],
        compiler_params=_params("parallel", "parallel", "parallel"), name="nsa_cmp_select")(
            q, cmp_k, cmp_vt, ov_t)


def _nsa_combine(o_cmp, o_slc, o_win, gz, expand):
    N, C = o_cmp.shape
    tm = min(ROW_TILE, N)

    def kern(c_ref, s_ref, w_ref, g_ref, e_ref, o_ref):
        gate = _sigmoid(g_ref[...])
        parts = _split_bf16(gate, 2)
        acc = jnp.zeros((tm, C), F32)
        for j, br in enumerate((c_ref, s_ref, w_ref)):
            gj = sum(jnp.dot(part, e_ref[j], preferred_element_type=F32) for part in parts)
            acc = acc + gj * br[...]
        o_ref[...] = acc.astype(o_ref.dtype)

    row = pl.BlockSpec((tm, C), lambda i: (i, 0))
    return pl.pallas_call(
        kern, grid=(N // tm,),
        in_specs=[row, row, row, pl.BlockSpec((tm, LANES), lambda i: (i, 0)),
                  pl.BlockSpec(expand.shape, lambda i: (0, 0, 0))],
        out_specs=row, out_shape=jax.ShapeDtypeStruct((N, C), BF16),
        compiler_params=_params("parallel"), name="nsa_combine")(o_cmp, o_slc, o_win, gz, expand)


def _mla_kv(ckv, gain, wk, wv, kr):
    N, K = ckv.shape
    tm = min(ROW_TILE, N)
    H = MLA_HEADS

    def kern(c_ref, g_ref, wk_ref, wv_ref, kr_ref, k_ref, v_ref):
        xf = c_ref[...]
        y = xf * lax.rsqrt(jnp.mean(xf * xf, axis=-1, keepdims=True) + EPS)
        a = (y * g_ref[...]).astype(BF16)
        kr_ = kr_ref[...]
        for h in range(H):
            kh = jnp.dot(a, wk_ref[:, h * LANES:(h + 1) * LANES], preferred_element_type=F32)
            k_ref[:, h * LANES:(h + 1) * LANES] = (kh + kr_).astype(k_ref.dtype)
        v_ref[...] = jnp.dot(a, wv_ref[...], preferred_element_type=F32).astype(v_ref.dtype)

    return pl.pallas_call(
        kern, grid=(N // tm,),
        in_specs=[pl.BlockSpec((tm, K), lambda i: (i, 0)),
                  pl.BlockSpec((1, K), lambda i: (0, 0)),
                  pl.BlockSpec(wk.shape, lambda i: (0, 0)),
                  pl.BlockSpec(wv.shape, lambda i: (0, 0)),
                  pl.BlockSpec((tm, LANES), lambda i: (i, 0))],
        out_specs=[pl.BlockSpec((tm, H * LANES), lambda i: (i, 0)),
                   pl.BlockSpec((tm, H * MLA_V), lambda i: (i, 0))],
        out_shape=[jax.ShapeDtypeStruct((N, H * LANES), BF16),
                   jax.ShapeDtypeStruct((N, H * MLA_V), BF16)],
        compiler_params=_params("parallel"), name="mla_kv")(
            ckv, gain.reshape(1, K).astype(F32), wk, wv, kr)


def _mem_attn(x, gain, wq, wo, mem_k, mem_v, *, S):
    N, D = x.shape
    tm = min(ROW_TILE, S)
    M = mem_k.shape[1]
    per_b = S // tm
    scale = XA_DIM ** -0.5

    def kern(x_ref, g_ref, wq_ref, wo_ref, k_ref, v_ref, o_ref):
        xf = x_ref[...]
        y = xf * lax.rsqrt(jnp.mean(xf * xf, axis=-1, keepdims=True) + EPS)
        a = (y * g_ref[...]).astype(BF16)
        q = (jnp.dot(a, wq_ref[...], preferred_element_type=F32) * scale).astype(BF16)
        heads = []
        for h in range(XA_HEADS):
            sl = slice(h * XA_DIM, (h + 1) * XA_DIM)
            s = lax.dot_general(q[:, sl], k_ref[:, sl], _NT, preferred_element_type=F32)
            e = jnp.exp(s - jnp.max(s, axis=1, keepdims=True))
            p = e / jnp.sum(e, axis=1, keepdims=True)
            heads.append(jnp.dot(p.astype(BF16), v_ref[:, sl], preferred_element_type=F32).astype(BF16))
        o = jnp.concatenate(heads, axis=1)
        o_ref[...] = xf + jnp.dot(o, wo_ref[...], preferred_element_type=F32)

    return pl.pallas_call(
        kern, grid=(N // tm,),
        in_specs=[pl.BlockSpec((tm, D), lambda i: (i, 0)),
                  pl.BlockSpec((1, D), lambda i: (0, 0)),
                  pl.BlockSpec(wq.shape, lambda i: (0, 0)),
                  pl.BlockSpec(wo.shape, lambda i: (0, 0)),
                  pl.BlockSpec((None, M, XA_HEADS * XA_DIM), lambda i: (i // per_b, 0, 0)),
                  pl.BlockSpec((None, M, XA_HEADS * XA_DIM), lambda i: (i // per_b, 0, 0))],
        out_specs=pl.BlockSpec((tm, D), lambda i: (i, 0)),
        out_shape=jax.ShapeDtypeStruct((N, D), F32),
        compiler_params=_params("parallel"), name="mem_attn")(
            x, gain.reshape(1, D).astype(F32), wq, wo, mem_k, mem_v)


def _ffn(x, gain, w13, w2, out_gain=None):
    N, D = x.shape
    FF = w2.shape[0]
    tm = min(ROW_TILE, N)
    chunk = 2 * LANES
    assert FF % chunk == 0

    final = out_gain is not None

    def kern(x_ref, g_ref, w13_ref, w2_ref, *rest):
        og_ref, o_ref = rest if final else (None, rest[0])
        xf = x_ref[...]
        y = xf * lax.rsqrt(jnp.mean(xf * xf, axis=-1, keepdims=True) + EPS)
        a = (y * g_ref[...]).astype(BF16)
        acc = xf
        for c in range(0, FF, chunk):
            g = jnp.dot(a, w13_ref[:, c:c + chunk], preferred_element_type=F32)
            u = jnp.dot(a, w13_ref[:, FF + c:FF + c + chunk], preferred_element_type=F32)
            hdn = (g * _sigmoid(g) * u).astype(BF16)
            acc = acc + jnp.dot(hdn, w2_ref[c:c + chunk, :], preferred_element_type=F32)
        if final:
            acc = acc * lax.rsqrt(jnp.mean(acc * acc, axis=-1, keepdims=True) + EPS) * og_ref[...]
        o_ref[...] = acc

    vec = pl.BlockSpec((1, D), lambda i: (0, 0))
    in_specs = [pl.BlockSpec((tm, D), lambda i: (i, 0)), vec,
                pl.BlockSpec(w13.shape, lambda i: (0, 0), pipeline_mode=pl.Buffered(1)),
                pl.BlockSpec(w2.shape, lambda i: (0, 0), pipeline_mode=pl.Buffered(1))]
    args = [x, gain.reshape(1, D).astype(F32), w13, w2]
    if final:
        in_specs.append(vec)
        args.append(out_gain.reshape(1, D).astype(F32))
    return pl.pallas_call(
        kern, grid=(N // tm,), in_specs=in_specs,
        out_specs=pl.BlockSpec((tm, D), lambda i: (i, 0)),
        out_shape=jax.ShapeDtypeStruct((N, D), F32),
        compiler_params=_params("parallel"), name="ffn")(*args)


def _pad_cols(w, width):
    return jnp.pad(w, ((0, 0), (0, width - w.shape[1])))


def _even_mixer(x, gain, w_in, b_f, lam, subln, w_out, layer_idx, ropes, B, S):
    D = x.shape[1]
    blk = DIFF_HEADS * 2 * DIFF_QK
    w = _pad_cols(w_in, 6 * blk + LANES).astype(BF16)
    sc = DIFF_QK ** -0.5 * LOG2E
    segs = [Seg(0, blk, BF16, "r64", sc), Seg(blk, blk, BF16, "r64"), Seg(2 * blk, blk, BF16),
            Seg(3 * blk, blk, BF16, None, sc), Seg(4 * blk, blk, BF16), Seg(5 * blk, blk, BF16),
            Seg(6 * blk, LANES, F32)]
    aq, ak, av, fq, fk, fv, fz = _linear([x], [w], segs, gain=gain, ropes=ropes, seq=S, name="even_in")
    lam_init = 0.8 - 0.6 * math.exp(-0.3 * layer_idx)
    oa = _flash(aq, ak, av.T, B=B, S=S, P=DIFF_HEADS // 2, q_w=2 * LANES, k_w=2 * LANES, v_rows=2 * LANES,
                heads=_DIFF_HEADS, kind="diff", out_dtype=BF16, lam=lam.astype(F32),
                subln=subln.astype(F32), lam_init=lam_init, name="diff_attn")
    terms = _forget_key_terms(fz, _pad_cols(b_f.reshape(1, -1), LANES).astype(F32), B, S)
    of = _flash(fq, fk, fv.T, B=B, S=S, P=FOX_HEADS // 4, q_w=2 * LANES, k_w=2 * LANES, v_rows=2 * LANES,
                heads=_PAIR_HEADS, kind="fox", out_dtype=BF16, ext=terms, name="fox_attn")
    wo = w_out.astype(BF16)
    (y,) = _linear([oa, of], [wo[:blk], wo[blk:]], [Seg(0, D, F32)], residual=x, name="even_out")
    return y


def _odd_in_weight(w_in):
    d = NSA_DIM
    o = np.cumsum((0, NSA_HEADS * d) + (NSA_GROUPS * d,) * 6 + (NSA_HEADS * 3, MLA_Q_RANK, MLA_KV_RANK, MLA_ROPE))
    nq, kc, vc, ks, vs, kw, vw, gz, cq, ckv, kr = [w_in[:, o[j]:o[j + 1]] for j in range(11)]

    def dup(wg):
        return jnp.concatenate([wg[:, :d], wg[:, :d], wg[:, d:], wg[:, d:]], axis=1)

    zeros = lambda n: jnp.zeros((w_in.shape[0], n), w_in.dtype)
    kr_slab = jnp.concatenate([zeros(MLA_NOPE), kr, zeros(LANES - MLA_NOPE - MLA_ROPE)], axis=1)
    cols = [nq, kc, vc, dup(ks), vs, dup(kw), vw, _pad_cols(gz, LANES), cq, ckv, kr_slab]
    return jnp.concatenate(cols, axis=1).astype(BF16)


def _odd_mixer(x, gain, w_in, cmp_pos, cmp_w1, cmp_w2, q_norm, kv_norm, w_uq, w_ukv, w_out, ropes, B, S):
    D = x.shape[1]
    G, d = NSA_GROUPS, NSA_DIM
    w = _odd_in_weight(w_in)
    sc = d ** -0.5 * LOG2E
    widths = [(NSA_HEADS * d, BF16, "r64", sc), (LANES, F32, "r64", 1.0), (LANES, F32, None, 1.0),
              (2 * LANES, BF16, "r64", 1.0), (LANES, BF16, None, 1.0),
              (2 * LANES, BF16, "r64", 1.0), (LANES, BF16, None, 1.0),
              (LANES, F32, None, 1.0), (MLA_Q_RANK, F32, None, 1.0), (MLA_KV_RANK, F32, None, 1.0),
              (LANES, F32, "mla", 1.0)]
    segs, start = [], 0
    for wd, dt, rp, s_ in widths:
        segs.append(Seg(start, wd, dt, rp, s_))
        start += wd
    q, kc, vc, ks, vs, kw, vw, gz, cq, ckv, kr = _linear(
        [x], [w], segs, gain=gain, ropes=ropes, seq=S, name="odd_in")

    R = S // CMP_STRIDE

    K = CMP_STRIDE * G * d
    half = CMP_LEN // 2
    pe = jnp.broadcast_to(cmp_pos.reshape(2, 2, half, 1, d), (2, 2, half, G, d)).reshape(2, 2, 1, K).astype(F32)
    eye = jnp.eye(G, dtype=cmp_w1.dtype)
    w1x = (cmp_w1.reshape(2, 2, 1, half, 1, d, CMP_HIDDEN) * eye[None, None, :, None, :, None, None])
    w1x = w1x.reshape(2, 2, G, K, CMP_HIDDEN).astype(BF16)
    w2d = jnp.concatenate([cmp_w2, cmp_w2], axis=-1).astype(BF16)
    cmp_k, cmp_vt = _nsa_compress(kc.reshape(B, R, K), vc.reshape(B, R, K), pe, w1x, w2d)
    cs = np.arange(R) * CMP_STRIDE
    bs = np.arange(LANES) * SLC_LEN
    ov = ((cs[:, None] < bs[None, :] + SLC_LEN) & (cs[:, None] + CMP_LEN > bs[None, :])
          & (np.arange(R)[:, None] < R - 1) & (bs[None, :] < S))
    o_cmp, sel = _nsa_cmp_select(q, cmp_k, cmp_vt, jnp.asarray(ov.T.astype(np.float32), BF16), B=B, S=S)
    onehot = (np.arange(S)[:, None] // SLC_LEN == np.arange(LANES)[None, :]).astype(np.float32)
    o_slc = _flash(q, ks, vs.T, B=B, S=S, P=G, q_w=2 * LANES, k_w=LANES, v_rows=HALF_LANES,
                   heads=_QUAD_HEADS, kind="slc", out_dtype=F32, ext=jnp.asarray(onehot, BF16),
                   sel=sel, name="nsa_slc")
    o_win = _flash(q, kw, vw.T, B=B, S=S, P=G, q_w=2 * LANES, k_w=LANES, v_rows=HALF_LANES,
                   heads=_QUAD_HEADS, kind="win", out_dtype=F32, tq=WIN, tk=WIN, n_bufs=2, name="nsa_win")
    ex = np.zeros((3, LANES, NSA_HEADS * d), np.float32)
    for hh in range(NSA_HEADS):
        for j in range(3):
            ex[j, hh * 3 + j, hh * d:(hh + 1) * d] = 1.0
    o_nsa = _nsa_combine(o_cmp, o_slc, o_win, gz, jnp.asarray(ex, BF16))

    H = MLA_HEADS
    qk = MLA_NOPE + MLA_ROPE
    wq_slab = jnp.pad(w_uq.reshape(MLA_Q_RANK, H, qk), ((0, 0), (0, 0), (0, LANES - qk)))
    wq_slab = wq_slab.reshape(MLA_Q_RANK, H * LANES).astype(BF16)
    (qm,) = _linear([cq], [wq_slab], [Seg(0, H * LANES, BF16, "mla", qk ** -0.5 * LOG2E)],
                    gain=q_norm, ropes=ropes, seq=S, name="mla_q")
    wkv = w_ukv.reshape(MLA_KV_RANK, H, MLA_NOPE + MLA_V)
    wk = jnp.pad(wkv[:, :, :MLA_NOPE], ((0, 0), (0, 0), (0, LANES - MLA_NOPE)))
    wk = wk.reshape(MLA_KV_RANK, H * LANES).astype(BF16)
    wv = wkv[:, :, MLA_NOPE:].reshape(MLA_KV_RANK, H * MLA_V).astype(BF16)
    km, vm = _mla_kv(ckv, kv_norm, wk, wv, kr)
    o_mla = _flash(qm, km, vm.T, B=B, S=S, P=H // 4, q_w=4 * LANES, k_w=4 * LANES, v_rows=2 * LANES,
                   heads=_WIDE_HEADS, kind="mla", out_dtype=BF16, name="mla_attn")
    wo = w_out.astype(BF16)
    half = NSA_HEADS * d
    (y,) = _linear([o_nsa, o_mla], [wo[:half], wo[half:]], [Seg(0, D, F32)], residual=x, name="odd_out")
    return y


def kernel(x, mem, mem_norm, norm_mix, norm_mem, norm_ffn, ev_w_in, ev_b_f, ev_lam, ev_subln, ev_w_out, od_w_in, nsa_cmp_pos, nsa_cmp_w1, nsa_cmp_w2, mla_q_norm, mla_kv_norm, mla_w_uq, mla_w_ukv, od_w_out, xa_wq, xa_wkv, xa_wo, ffn_w13, ffn_w2, final_norm):
    B, S, D = x.shape
    M = mem.shape[1]
    depth = norm_mix.shape[0]
    ropes = {
        "r64": (_rope_tables(S, NSA_DIM // 2, 0, LANES), NSA_DIM // 2),
        "mla": (_rope_tables(S, MLA_ROPE // 2, MLA_NOPE, MLA_NOPE + MLA_ROPE), MLA_ROPE // 2),
    }
    xa_w = XA_HEADS * XA_DIM
    h = x.reshape(B * S, D)
    mem2 = mem.reshape(B * M, D)
    for li in range(depth):
        j = li // 2
        if li % 2 == 0:
            h = _even_mixer(h, norm_mix[li], ev_w_in[j], ev_b_f[j], ev_lam[j], ev_subln[j], ev_w_out[j],
                            li, ropes, B, S)
        else:
            h = _odd_mixer(h, norm_mix[li], od_w_in[j], nsa_cmp_pos[j], nsa_cmp_w1[j], nsa_cmp_w2[j],
                           mla_q_norm[j], mla_kv_norm[j], mla_w_uq[j], mla_w_ukv[j], od_w_out[j],
                           ropes, B, S)
        mk, mv = _linear([mem2], [xa_wkv[li].astype(BF16)],
                         [Seg(0, xa_w, BF16), Seg(xa_w, xa_w, BF16)], gain=mem_norm, name="mem_kv")
        h = _mem_attn(h, norm_mem[li], xa_wq[li].astype(BF16), xa_wo[li].astype(BF16),
                      mk.reshape(B, M, xa_w), mv.reshape(B, M, xa_w), S=S)
        h = _ffn(h, norm_ffn[li], ffn_w13[li].astype(BF16), ffn_w2[li].astype(BF16),
                 out_gain=final_norm if li == depth - 1 else None)
    return h.reshape(B, S, D)
```

```python
import math
from typing import NamedTuple, Optional

import numpy as np
import jax
import jax.numpy as jnp
from jax import lax
from jax.experimental import pallas as pl
from jax.experimental.pallas import tpu as pltpu

F32 = jnp.float32
BF16 = jnp.bfloat16

LANES = 128
HALF_LANES = LANES // 2
BF16_ROWS = 16
ROPE_THETA = 10000.0
EPS = 1e-6
NEG = -1e30
LOG2E = math.log2(math.e)
MASK_BIG = 2.0 ** 100

DIFF_HEADS = 4
DIFF_QK = 64
FOX_HEADS = 8
NSA_HEADS = 8
NSA_GROUPS = 2
NSA_DIM = 64
CMP_LEN = 32
CMP_STRIDE = 16
CMP_HIDDEN = 128
SLC_LEN = 64
SLC_TOPK = 16
WIN = 512
FORCE_SCORE = 1e4
MLA_HEADS = 8
MLA_NOPE = 64
MLA_ROPE = 32
MLA_V = 64
MLA_Q_RANK = 384
MLA_KV_RANK = 256
XA_HEADS = 4
XA_DIM = 128

VMEM_LIMIT = 56 * 1024 * 1024
ROW_TILE = 512
COL_CHUNK = 512
ATT_TILE = 512
KEY_TILE = 512
ATT_BUFS = 3
SEL_TILE = 512
CUM_CHUNK = 256

_NT = (((1,), (1,)), ((), ()))


def _params(*sem):
    return pltpu.CompilerParams(dimension_semantics=sem, vmem_limit_bytes=VMEM_LIMIT)


def _sigmoid(x):
    return 1.0 / (1.0 + jnp.exp(-x))


def _split_bf16(x, terms):
    out = []
    r = x
    for _ in range(terms):
        h = r.astype(BF16)
        out.append(h)
        r = r - h.astype(F32)
    return out


def _rope_tables(S, half, lane_lo, lane_hi):
    pos = jnp.arange(S, dtype=F32)
    inv = 1.0 / (ROPE_THETA ** (jnp.arange(half, dtype=F32) / half))
    ang = pos[:, None] * inv[None, :]
    cos, sin = jnp.cos(ang), jnp.sin(ang)
    lane = np.arange(LANES)
    active = (lane >= lane_lo) & (lane < lane_hi)
    j = (lane - lane_lo) % (2 * half)
    lower = active & (j < half)
    upper = active & (j >= half)
    idx = j % half
    cos_t = jnp.where(active[None, :], cos[:, idx], 1.0)
    sin_a = jnp.where(lower[None, :], -sin[:, idx], 0.0)
    sin_b = jnp.where(upper[None, :], sin[:, idx], 0.0)
    return cos_t, sin_a, sin_b


class Seg(NamedTuple):
    start: int
    width: int
    dtype: object
    rope: Optional[str] = None
    scale: float = 1.0


def _linear(xs, ws, segs, *, gain=None, residual=None, ropes=None, seq=None, name="linear"):
    N = xs[0].shape[0]
    tm = min(ROW_TILE, N)
    assert N % tm == 0
    n_in = len(xs)
    has_gain = gain is not None
    has_res = residual is not None
    rope_keys = sorted({s.rope for s in segs if s.rope})
    halves = {k: ropes[k][1] for k in rope_keys}

    def kern(*refs):
        it = iter(refs)
        x_refs = [next(it) for _ in range(n_in)]
        w_refs = [next(it) for _ in range(n_in)]
        g_ref = next(it) if has_gain else None
        r_ref = next(it) if has_res else None
        tabs = {k: (next(it), next(it), next(it)) for k in rope_keys}
        o_refs = [next(it) for _ in segs]
        acts = []
        for j, xr in enumerate(x_refs):
            x = xr[...]
            if j == 0 and has_gain:
                xf = x.astype(F32)
                y = xf * lax.rsqrt(jnp.mean(xf * xf, axis=-1, keepdims=True) + EPS)
                acts.append((y * g_ref[...]).astype(BF16))
            else:
                acts.append(x.astype(BF16))
        for seg, o_ref in zip(segs, o_refs):
            for c0 in range(0, seg.width, COL_CHUNK):
                cw = min(COL_CHUNK, seg.width - c0)
                col = seg.start + c0
                acc = None
                for a, wr in zip(acts, w_refs):
                    d = jnp.dot(a, wr[:, col:col + cw], preferred_element_type=F32)
                    acc = d if acc is None else acc + d
                if has_res:
                    acc = acc + r_ref[:, col:col + cw]
                if seg.scale != 1.0:
                    acc = acc * seg.scale
                if seg.rope is None:
                    o_ref[:, c0:c0 + cw] = acc.astype(o_ref.dtype)
                else:
                    cos_r, sa_r, sb_r = tabs[seg.rope]
                    half = halves[seg.rope]
                    cos, sa, sb = cos_r[...], sa_r[...], sb_r[...]
                    for s0 in range(0, cw, LANES):
                        xs_ = acc[:, s0:s0 + LANES]
                        y = (xs_ * cos + pltpu.roll(xs_, LANES - half, 1) * sa
                             + pltpu.roll(xs_, half, 1) * sb)
                        o_ref[:, c0 + s0:c0 + s0 + LANES] = y.astype(o_ref.dtype)

    in_specs, args = [], []
    for x in xs:
        in_specs.append(pl.BlockSpec((tm, x.shape[1]), lambda i: (i, 0)))
        args.append(x)
    for w in ws:
        in_specs.append(pl.BlockSpec(w.shape, lambda i: (0, 0)))
        args.append(w)
    if has_gain:
        in_specs.append(pl.BlockSpec((1, gain.shape[-1]), lambda i: (0, 0)))
        args.append(gain.reshape(1, -1).astype(F32))
    if has_res:
        in_specs.append(pl.BlockSpec((tm, residual.shape[1]), lambda i: (i, 0)))
        args.append(residual)
    for k in rope_keys:
        assert seq % tm == 0
        nt = seq // tm
        for t in ropes[k][0]:
            in_specs.append(pl.BlockSpec((tm, LANES), lambda i, nt=nt: (i % nt, 0)))
            args.append(t)
    out_shape = [jax.ShapeDtypeStruct((N, s.width), s.dtype) for s in segs]
    out_specs = [pl.BlockSpec((tm, s.width), lambda i: (i, 0)) for s in segs]
    return pl.pallas_call(
        kern, grid=(N // tm,), in_specs=in_specs, out_specs=out_specs, out_shape=out_shape,
        compiler_params=_params("parallel"), name=name)(*args)


def _forget_key_terms(fz, b_f, B, S):
    ch = CUM_CHUNK
    place = np.zeros((3, LANES, LANES), np.float32)
    for h in range(FOX_HEADS):
        for j in range(3):
            place[j, h, 3 * h + j] = 1.0

    def kern(z_ref, b_ref, pl_ref, o_ref):
        r = lax.broadcasted_iota(jnp.int32, (ch, ch), 0)
        c = lax.broadcasted_iota(jnp.int32, (ch, ch), 1)
        tri = jnp.where(c <= r, 1.0, 0.0).astype(BF16)

        def body(j, carries):
            out = []
            for b, carry in enumerate(carries):
                r0 = pl.multiple_of(b * S + j * ch, ch)
                z = z_ref[pl.ds(r0, ch), :] + b_ref[...]
                logf = -(jnp.maximum(-z, 0.0) + jnp.log1p(jnp.exp(-jnp.abs(z))))
                cs = carry
                for part in _split_bf16(logf, 3):
                    cs = cs + jnp.dot(tri, part, preferred_element_type=F32)
                terms = jnp.zeros((ch, LANES), F32)
                for jj, part in enumerate(_split_bf16(cs * (-LOG2E), 3)):
                    terms = terms + jnp.dot(part, pl_ref[jj], preferred_element_type=F32)
                o_ref[pl.ds(r0, ch), :] = terms.astype(o_ref.dtype)
                out.append(cs[ch - 1:ch, :])
            return tuple(out)

        lax.fori_loop(0, S // ch, body, tuple(jnp.zeros((1, LANES), F32) for _ in range(B)))

    return pl.pallas_call(
        kern, grid=(1,),
        in_specs=[pl.BlockSpec((B * S, LANES), lambda g: (0, 0), pipeline_mode=pl.Buffered(1)),
                  pl.BlockSpec((1, LANES), lambda g: (0, 0)),
                  pl.BlockSpec(place.shape, lambda g: (0, 0, 0))],
        out_specs=pl.BlockSpec((B * S, LANES), lambda g: (0, 0)),
        out_shape=jax.ShapeDtypeStruct((B * S, LANES), BF16),
        compiler_params=_params("arbitrary"), name="forget_terms")(fz, b_f, jnp.asarray(place, BF16))


class Head(NamedTuple):
    q_off: int
    q_half: Optional[str]
    k_off: int
    v0: int
    v1: int


def _flash(q, k, vt, *, B, S, P, q_w, k_w, v_rows, heads, kind, out_dtype, tq=None, tk=None, n_bufs=None,
           ext=None, sel=None, lam=None, subln=None, lam_init=None, name="flash"):
    TQ, TK = tq or ATT_TILE, tk or KEY_TILE
    assert S % TQ == 0 and TQ % TK == 0
    diag = TQ // TK
    nq = S // TQ
    nh = len(heads)
    rows = heads[0].v1 - heads[0].v0
    out_w = nh // 2 * rows if kind == "diff" else nh * rows
    width = TK
    n_bufs = n_bufs or ATT_BUFS
    if kind == "win":
        assert WIN == TK and TQ == TK and S >= 2 * TK and n_bufs == 2
    has_ext = kind in ("fox", "slc")

    def kern(*refs):
        it = iter(refs)
        q_ref, k_ref, v_ref = next(it), next(it), next(it)
        ext_ref = next(it) if has_ext else None
        sel_ref = next(it) if kind == "slc" else None
        lam_ref, sub_ref = (next(it), next(it)) if kind == "diff" else (None, None)
        o_ref, m_ref, acc_ref, mx_ref = next(it), next(it), next(it), next(it)
        bufs = [next(it) for _ in range(n_bufs)]
        p_id = pl.program_id(1)
        i = pl.program_id(2)

        lane = lax.broadcasted_iota(jnp.int32, (TQ, LANES), 1)
        lo = lane < HALF_LANES
        q_all = q_ref[...]
        if kind == "slc":
            drop = ((sel_ref[...].astype(F32) - 1.0) * MASK_BIG).astype(BF16)
        qs = []
        for j, h in enumerate(heads):
            qh = q_all[:, h.q_off:h.q_off + LANES]
            if h.q_half == "lo":
                qh = jnp.where(lo, qh, jnp.zeros_like(qh))
            elif h.q_half == "hi":
                qh = jnp.where(lo, jnp.zeros_like(qh), qh)
            if kind == "fox":
                first = 3 * (nh * p_id + j)
                pick = jnp.where((lane >= first) & (lane < first + 3), 1.0, 0.0).astype(BF16)
                qh = jnp.concatenate([qh, pick], axis=1)
            if kind == "slc":
                qh = jnp.concatenate([qh, drop], axis=1)
            qs.append(qh)

        m_ref[...] = jnp.full(m_ref.shape, NEG, F32)
        acc_ref[...] = jnp.zeros(acc_ref.shape, F32)

        def scores(k0, u, ahead=None):
            kx = k_ref[pl.ds(k0, width), :]
            ex = ext_ref[pl.ds(k0, width), :] if has_ext else None
            msk = None
            if ahead is not None:
                rel = (lax.broadcasted_iota(jnp.int32, (width, TQ), 0)
                       - lax.broadcasted_iota(jnp.int32, (width, TQ), 1))
                msk = rel <= ahead
                if kind == "win" and ahead - WIN > -TQ:
                    msk = msk & (rel > ahead - WIN)
            for hi, h in enumerate(heads):
                kk = kx[:, h.k_off:h.k_off + LANES]
                if ex is not None:
                    kk = jnp.concatenate([kk, ex], axis=1)
                st = lax.dot_general(kk, qs[hi], _NT, preferred_element_type=F32)
                if msk is not None:
                    st = jnp.where(msk, st, NEG)
                bufs[u][hi] = st
                mx_ref[u * nh + hi] = jnp.max(st, axis=0, keepdims=True)

        def update(k0, u):
            ones = jnp.ones((BF16_ROWS, width), BF16)
            for hi, h in enumerate(heads):
                m_prev = m_ref[hi]
                m_new = jnp.maximum(m_prev, mx_ref[u * nh + hi])
                alpha = jnp.exp2(m_prev - m_new)
                pt = jnp.exp2(bufs[u][hi] - m_new).astype(BF16)
                lhs = jnp.concatenate([v_ref[h.v0:h.v1, pl.ds(k0, width)], ones], axis=0)
                acc_ref[hi] = alpha * acc_ref[hi] + jnp.dot(lhs, pt, preferred_element_type=F32)
                m_ref[hi] = m_new

        tile = lambda t: pl.multiple_of(t * TK, TK)
        U = len(bufs)
        if kind == "win":
            @pl.when(i == 0)
            def _():
                scores(tile(0), 0, 0)
                update(tile(0), 0)

            @pl.when(i > 0)
            def _():
                scores(tile(i - 1), 0, WIN)
                scores(tile(i), 1, 0)
                update(tile(i - 1), 0)
                update(tile(i), 1)
        else:
            n_full = i * diag
            n_rounds = jnp.maximum(n_full - 1, 0) // U
            pl.when(n_full == 0)(lambda: scores(tile(0), 0, 0))
            pl.when(n_full > 0)(lambda: scores(tile(0), 0))

            def body(j, c):
                for u in range(U):
                    scores(tile(U * j + u + 1), (u + 1) % U)
                    update(tile(U * j + u), u)
                return c
            lax.fori_loop(0, n_rounds, body, 0)
            t0 = U * n_rounds

            def tail(left):
                last = left + diag - 1
                for u in range(last):
                    d = u + 1 - left
                    scores(tile(t0 + u + 1), (u + 1) % U, -d * TK if d >= 0 else None)
                    update(tile(t0 + u), u % U)
                update(tile(t0 + last), last % U)

            for left in range(U + 1):
                pl.when(n_full - t0 == left)(lambda left=left: tail(left))

        outs = []
        for hi in range(nh):
            acc = acc_ref[hi]
            outs.append(acc[:rows] / acc[rows:rows + 1])
        if kind == "diff":
            lm = lam_ref[...]
            la = jnp.sum(lm[0:1] * lm[1:2], axis=1, keepdims=True)
            lb = jnp.sum(lm[2:3] * lm[3:4], axis=1, keepdims=True)
            lam_full = jnp.exp(la) - jnp.exp(lb) + lam_init
            for g in range(nh // 2):
                oa = jnp.transpose(outs[2 * g] - lam_full * outs[2 * g + 1])
                y = oa * lax.rsqrt(jnp.mean(oa * oa, axis=-1, keepdims=True) + EPS)
                y = (y * sub_ref[...]) * (1.0 - lam_init)
                o_ref[:, g * rows:(g + 1) * rows] = y.astype(o_ref.dtype)
        else:
            o_ref[...] = jnp.transpose(jnp.concatenate(outs, axis=0)).astype(o_ref.dtype)

    in_specs = [
        pl.BlockSpec((TQ, q_w), lambda b, p, i: (b * nq + i, p)),
        pl.BlockSpec((S, k_w), lambda b, p, i: (b, p)),
        pl.BlockSpec((v_rows, S), lambda b, p, i: (p, b)),
    ]
    args = [q, k, vt]
    if kind == "fox":
        in_specs.append(pl.BlockSpec((S, LANES), lambda b, p, i: (b, 0)))
        args.append(ext)
    if kind == "slc":
        in_specs.append(pl.BlockSpec((S, LANES), lambda b, p, i: (0, 0)))
        in_specs.append(pl.BlockSpec((TQ, LANES), lambda b, p, i: (b * nq + i, p)))
        args += [ext, sel]
    if kind == "diff":
        in_specs.append(pl.BlockSpec(lam.shape, lambda b, p, i: (0, 0)))
        in_specs.append(pl.BlockSpec((1, LANES), lambda b, p, i: (0, 0)))
        args += [lam, subln.reshape(1, LANES)]
    return pl.pallas_call(
        kern, grid=(B, P, nq), in_specs=in_specs,
        out_specs=pl.BlockSpec((TQ, out_w), lambda b, p, i: (b * nq + i, p)),
        out_shape=jax.ShapeDtypeStruct((B * S, P * out_w), out_dtype),
        scratch_shapes=[pltpu.VMEM((nh, 1, TQ), F32),
                        pltpu.VMEM((nh, rows + BF16_ROWS, TQ), F32),
                        pltpu.VMEM((n_bufs * nh, 1, TQ), F32)]
        + [pltpu.VMEM((nh, width, TQ), F32)] * n_bufs,
        compiler_params=_params("parallel", "parallel", "arbitrary"), name=name)(*args)


_DIFF_HEADS = tuple(Head(s * LANES, half, s * LANES, s * LANES, (s + 1) * LANES)
                    for s in range(2) for half in ("lo", "hi"))
_PAIR_HEADS = tuple(Head(s * LANES, half, s * LANES, (2 * s + j) * HALF_LANES, (2 * s + j + 1) * HALF_LANES)
                    for s in range(2) for j, half in enumerate(("lo", "hi")))
_QUAD_HEADS = tuple(Head(s * LANES, half, 0, 0, HALF_LANES) for s in range(2) for half in ("lo", "hi"))
_WIDE_HEADS = tuple(Head(j * LANES, None, j * LANES, j * HALF_LANES, (j + 1) * HALF_LANES) for j in range(4))


def _nsa_compress(xk, xv, pe, w1x, w2d):
    B, R, K = xk.shape
    G = NSA_GROUPS

    def kern(xk_ref, xv_ref, pe_ref, w1_ref, w2_ref, k_ref, vt_ref):
        for t, x_ref in enumerate((xk_ref, xv_ref)):
            x = x_ref[...]
            xa = (x + pe_ref[t, 0]).astype(BF16)
            xb = (x + pe_ref[t, 1]).astype(BF16)
            for g in range(G):
                a = jnp.dot(xa, w1_ref[t, 0, g], preferred_element_type=F32)
                b = jnp.dot(xb, w1_ref[t, 1, g], preferred_element_type=F32)
                h = a + pltpu.roll(b, R - 1, 0)
                hs = h * _sigmoid(h)
                o = jnp.dot(hs.astype(BF16), w2_ref[t], preferred_element_type=F32)
                if t == 0:
                    k_ref[g] = o
                else:
                    vt_ref[g] = jnp.transpose(o)

    return pl.pallas_call(
        kern, grid=(B,),
        in_specs=[pl.BlockSpec((None, R, K), lambda b: (b, 0, 0)),
                  pl.BlockSpec((None, R, K), lambda b: (b, 0, 0)),
                  pl.BlockSpec(pe.shape, lambda b: (0, 0, 0, 0)),
                  pl.BlockSpec(w1x.shape, lambda b: (0, 0, 0, 0, 0)),
                  pl.BlockSpec(w2d.shape, lambda b: (0, 0, 0))],
        out_specs=[pl.BlockSpec((None, G, R, LANES), lambda b: (b, 0, 0, 0)),
                   pl.BlockSpec((None, G, LANES, R), lambda b: (b, 0, 0, 0))],
        out_shape=[jax.ShapeDtypeStruct((B, G, R, LANES), F32),
                   jax.ShapeDtypeStruct((B, G, LANES, R), F32)],
        compiler_params=_params("parallel"), name="nsa_compress")(xk, xv, pe, w1x, w2d)


def _nsa_cmp_select(q, cmp_k, cmp_vt, ov_t, *, B, S):
    T = SEL_TILE
    nq = S // T
    R = cmp_k.shape[2]
    G = NSA_GROUPS
    d = NSA_DIM
    n_sel = min(SLC_TOPK, S // SLC_LEN)
    assert n_sel >= 3 and FORCE_SCORE > NSA_HEADS // G

    bucket = min(LANES, R)
    assert R % bucket == 0 and bucket % (SLC_LEN // CMP_STRIDE) == 0

    def kern(q_ref, kc_ref, vt_ref, ov_ref, o_ref, sel_ref):
        i = pl.program_id(2)
        lane = lax.broadcasted_iota(jnp.int32, (T, LANES), 1)
        lo = lane < HALF_LANES
        q_all = q_ref[...]
        qs = []
        for hh in range(4):
            qh = q_all[:, (hh // 2) * LANES:(hh // 2 + 1) * LANES]
            qs.append(jnp.where(lo, qh, jnp.zeros_like(qh)) if hh % 2 == 0
                      else jnp.where(lo, jnp.zeros_like(qh), qh))

        def variant(rv):
            nb = rv * CMP_STRIDE // SLC_LEN
            tq = lax.broadcasted_iota(jnp.int32, (rv, T), 1) + i * T
            cend = lax.broadcasted_iota(jnp.int32, (rv, T), 0) * CMP_STRIDE + (CMP_LEN - 1)
            vis = cend <= tq
            kc = kc_ref[:rv, :].astype(BF16)
            vt = vt_ref[:d, :rv].astype(BF16)
            psum = jnp.zeros((rv, T), F32)
            o_heads = []
            for hh in range(4):
                st = lax.dot_general(kc, qs[hh], _NT, preferred_element_type=F32)
                st = jnp.where(vis, st, -jnp.inf)
                m = jnp.max(st, axis=0, keepdims=True)
                e = jnp.exp2(st - jnp.where(m > -jnp.inf, m, 0.0))
                p = e / jnp.maximum(jnp.sum(e, axis=0, keepdims=True), 1e-30)
                o_heads.append(jnp.dot(vt, p.astype(BF16), preferred_element_type=F32))
                psum = psum + p
            o_ref[...] = jnp.transpose(jnp.concatenate(o_heads, axis=0)).astype(o_ref.dtype)
            imp = jnp.zeros((nb, T), F32)
            for part in _split_bf16(psum, 2):
                imp = imp + jnp.dot(ov_ref[:nb, :rv], part, preferred_element_type=F32)
            blk = lax.broadcasted_iota(jnp.int32, (nb, T), 0)
            cur = (lax.broadcasted_iota(jnp.int32, (nb, T), 1) + i * T) // SLC_LEN
            forced = (blk == 0) | (blk == cur) | (blk == cur - 1)
            work = jnp.where(forced, -jnp.inf, imp)
            work = jnp.where(blk > cur, NEG, work)
            blk_f = blk.astype(F32)
            for _ in range(n_sel - 3):
                mx = jnp.max(work, axis=0, keepdims=True)
                first = jnp.min(jnp.where(work == mx, blk_f, float(LANES)), axis=0, keepdims=True)
                work = jnp.where(blk_f == first, -jnp.inf, work)
            sel = jnp.where(work == -jnp.inf, 1.0, 0.0)
            if nb < LANES:
                sel = jnp.concatenate([sel, jnp.zeros((LANES - nb, T), F32)], axis=0)
            sel_ref[...] = jnp.transpose(sel).astype(sel_ref.dtype)

        need = (i + 1) * (T // CMP_STRIDE)
        which = (need - 1) // bucket
        for bk in range(R // bucket):
            pl.when(which == bk)(lambda bk=bk: variant((bk + 1) * bucket))

    return pl.pallas_call(
        kern, grid=(B, G, nq),
        in_specs=[pl.BlockSpec((T, 2 * LANES), lambda b, g, i: (b * nq + i, g)),
                  pl.BlockSpec((None, None, R, LANES), lambda b, g, i: (b, g, 0, 0)),
                  pl.BlockSpec((None, None, LANES, R), lambda b, g, i: (b, g, 0, 0)),
                  pl.BlockSpec((LANES, R), lambda b, g, i: (0, 0))],
        out_specs=[pl.BlockSpec((T, 2 * LANES), lambda b, g, i: (b * nq + i, g)),
                   pl.BlockSpec((T, LANES), lambda b, g, i: (b * nq + i, g))],
        out_shape=[jax.ShapeDtypeStruct((B * S, G * 2 * LANES), BF16),
                   jax.ShapeDtypeStruct((B * S, G * LANES), BF16)],
        compiler_params=_params("parallel", "parallel", "parallel"), name="nsa_cmp_select")(
            q, cmp_k, cmp_vt, ov_t)


def _nsa_combine(o_cmp, o_slc, o_win, gz, expand):
    N, C = o_cmp.shape
    tm = min(ROW_TILE, N)

    def kern(c_ref, s_ref, w_ref, g_ref, e_ref, o_ref):
        gate = _sigmoid(g_ref[...])
        parts = _split_bf16(gate, 2)
        acc = jnp.zeros((tm, C), F32)
        for j, br in enumerate((c_ref, s_ref, w_ref)):
            gj = sum(jnp.dot(part, e_ref[j], preferred_element_type=F32) for part in parts)
            acc = acc + gj * br[...].astype(F32)
        o_ref[...] = acc.astype(o_ref.dtype)

    row = pl.BlockSpec((tm, C), lambda i: (i, 0))
    return pl.pallas_call(
        kern, grid=(N // tm,),
        in_specs=[row, row, row, pl.BlockSpec((tm, LANES), lambda i: (i, 0)),
                  pl.BlockSpec(expand.shape, lambda i: (0, 0, 0))],
        out_specs=row, out_shape=jax.ShapeDtypeStruct((N, C), BF16),
        compiler_params=_params("parallel"), name="nsa_combine")(o_cmp, o_slc, o_win, gz, expand)


def _mla_kv(ckv, gain, wk, wv, kr):
    N, K = ckv.shape
    tm = min(ROW_TILE, N)
    H = MLA_HEADS

    def kern(c_ref, g_ref, wk_ref, wv_ref, kr_ref, k_ref, v_ref):
        xf = c_ref[...]
        y = xf * lax.rsqrt(jnp.mean(xf * xf, axis=-1, keepdims=True) + EPS)
        a = (y * g_ref[...]).astype(BF16)
        kr_ = kr_ref[...]
        for h in range(H):
            kh = jnp.dot(a, wk_ref[:, h * LANES:(h + 1) * LANES], preferred_element_type=F32)
            k_ref[:, h * LANES:(h + 1) * LANES] = (kh + kr_).astype(k_ref.dtype)
        v_ref[...] = jnp.dot(a, wv_ref[...], preferred_element_type=F32).astype(v_ref.dtype)

    return pl.pallas_call(
        kern, grid=(N // tm,),
        in_specs=[pl.BlockSpec((tm, K), lambda i: (i, 0)),
                  pl.BlockSpec((1, K), lambda i: (0, 0)),
                  pl.BlockSpec(wk.shape, lambda i: (0, 0)),
                  pl.BlockSpec(wv.shape, lambda i: (0, 0)),
                  pl.BlockSpec((tm, LANES), lambda i: (i, 0))],
        out_specs=[pl.BlockSpec((tm, H * LANES), lambda i: (i, 0)),
                   pl.BlockSpec((tm, H * MLA_V), lambda i: (i, 0))],
        out_shape=[jax.ShapeDtypeStruct((N, H * LANES), BF16),
                   jax.ShapeDtypeStruct((N, H * MLA_V), BF16)],
        compiler_params=_params("parallel"), name="mla_kv")(
            ckv, gain.reshape(1, K).astype(F32), wk, wv, kr)


def _mem_attn(x, gain, wq, wo, mem_k, mem_v, *, S):
    N, D = x.shape
    tm = min(ROW_TILE, S)
    M = mem_k.shape[1]
    per_b = S // tm
    scale = XA_DIM ** -0.5

    def kern(x_ref, g_ref, wq_ref, wo_ref, k_ref, v_ref, o_ref):
        xf = x_ref[...]
        y = xf * lax.rsqrt(jnp.mean(xf * xf, axis=-1, keepdims=True) + EPS)
        a = (y * g_ref[...]).astype(BF16)
        q = (jnp.dot(a, wq_ref[...], preferred_element_type=F32) * scale).astype(BF16)
        heads = []
        for h in range(XA_HEADS):
            sl = slice(h * XA_DIM, (h + 1) * XA_DIM)
            s = lax.dot_general(q[:, sl], k_ref[:, sl], _NT, preferred_element_type=F32)
            e = jnp.exp(s - jnp.max(s, axis=1, keepdims=True))
            p = e / jnp.sum(e, axis=1, keepdims=True)
            heads.append(jnp.dot(p.astype(BF16), v_ref[:, sl], preferred_element_type=F32).astype(BF16))
        o = jnp.concatenate(heads, axis=1)
        o_ref[...] = xf + jnp.dot(o, wo_ref[...], preferred_element_type=F32)

    return pl.pallas_call(
        kern, grid=(N // tm,),
        in_specs=[pl.BlockSpec((tm, D), lambda i: (i, 0)),
                  pl.BlockSpec((1, D), lambda i: (0, 0)),
                  pl.BlockSpec(wq.shape, lambda i: (0, 0)),
                  pl.BlockSpec(wo.shape, lambda i: (0, 0)),
                  pl.BlockSpec((None, M, XA_HEADS * XA_DIM), lambda i: (i // per_b, 0, 0)),
                  pl.BlockSpec((None, M, XA_HEADS * XA_DIM), lambda i: (i // per_b, 0, 0))],
        out_specs=pl.BlockSpec((tm, D), lambda i: (i, 0)),
        out_shape=jax.ShapeDtypeStruct((N, D), F32),
        compiler_params=_params("parallel"), name="mem_attn")(
            x, gain.reshape(1, D).astype(F32), wq, wo, mem_k, mem_v)


def _ffn(x, gain, w13, w2, out_gain=None):
    N, D = x.shape
    FF = w2.shape[0]
    tm = min(ROW_TILE, N)
    chunk = 2 * LANES
    assert FF % chunk == 0

    final = out_gain is not None

    def kern(x_ref, g_ref, w13_ref, w2_ref, *rest):
        og_ref, o_ref = rest if final else (None, rest[0])
        xf = x_ref[...]
        y = xf * lax.rsqrt(jnp.mean(xf * xf, axis=-1, keepdims=True) + EPS)
        a = (y * g_ref[...]).astype(BF16)
        acc = xf
        for c in range(0, FF, chunk):
            g = jnp.dot(a, w13_ref[:, c:c + chunk], preferred_element_type=F32)
            u = jnp.dot(a, w13_ref[:, FF + c:FF + c + chunk], preferred_element_type=F32)
            hdn = (g * _sigmoid(g) * u).astype(BF16)
            acc = acc + jnp.dot(hdn, w2_ref[c:c + chunk, :], preferred_element_type=F32)
        if final:
            acc = acc * lax.rsqrt(jnp.mean(acc * acc, axis=-1, keepdims=True) + EPS) * og_ref[...]
        o_ref[...] = acc

    vec = pl.BlockSpec((1, D), lambda i: (0, 0))
    in_specs = [pl.BlockSpec((tm, D), lambda i: (i, 0)), vec,
                pl.BlockSpec(w13.shape, lambda i: (0, 0), pipeline_mode=pl.Buffered(1)),
                pl.BlockSpec(w2.shape, lambda i: (0, 0), pipeline_mode=pl.Buffered(1))]
    args = [x, gain.reshape(1, D).astype(F32), w13, w2]
    if final:
        in_specs.append(vec)
        args.append(out_gain.reshape(1, D).astype(F32))
    return pl.pallas_call(
        kern, grid=(N // tm,), in_specs=in_specs,
        out_specs=pl.BlockSpec((tm, D), lambda i: (i, 0)),
        out_shape=jax.ShapeDtypeStruct((N, D), F32),
        compiler_params=_params("parallel"), name="ffn")(*args)


def _pad_cols(w, width):
    return jnp.pad(w, ((0, 0), (0, width - w.shape[1])))


def _even_mixer(x, gain, w_in, b_f, lam, subln, w_out, layer_idx, ropes, B, S):
    D = x.shape[1]
    blk = DIFF_HEADS * 2 * DIFF_QK
    w = _pad_cols(w_in, 6 * blk + LANES).astype(BF16)
    sc = DIFF_QK ** -0.5 * LOG2E
    segs = [Seg(0, blk, BF16, "r64", sc), Seg(blk, blk, BF16, "r64"), Seg(2 * blk, blk, BF16),
            Seg(3 * blk, blk, BF16, None, sc), Seg(4 * blk, blk, BF16), Seg(5 * blk, blk, BF16),
            Seg(6 * blk, LANES, F32)]
    aq, ak, av, fq, fk, fv, fz = _linear([x], [w], segs, gain=gain, ropes=ropes, seq=S, name="even_in")
    lam_init = 0.8 - 0.6 * math.exp(-0.3 * layer_idx)
    oa = _flash(aq, ak, av.T, B=B, S=S, P=DIFF_HEADS // 2, q_w=2 * LANES, k_w=2 * LANES, v_rows=2 * LANES,
                heads=_DIFF_HEADS, kind="diff", out_dtype=BF16, lam=lam.astype(F32),
                subln=subln.astype(F32), lam_init=lam_init, name="diff_attn")
    terms = _forget_key_terms(fz, _pad_cols(b_f.reshape(1, -1), LANES).astype(F32), B, S)
    of = _flash(fq, fk, fv.T, B=B, S=S, P=FOX_HEADS // 4, q_w=2 * LANES, k_w=2 * LANES, v_rows=2 * LANES,
                heads=_PAIR_HEADS, kind="fox", out_dtype=BF16, ext=terms, name="fox_attn")
    wo = w_out.astype(BF16)
    (y,) = _linear([oa, of], [wo[:blk], wo[blk:]], [Seg(0, D, F32)], residual=x, name="even_out")
    return y


def _odd_in_weight(w_in):
    d = NSA_DIM
    o = np.cumsum((0, NSA_HEADS * d) + (NSA_GROUPS * d,) * 6 + (NSA_HEADS * 3, MLA_Q_RANK, MLA_KV_RANK, MLA_ROPE))
    nq, kc, vc, ks, vs, kw, vw, gz, cq, ckv, kr = [w_in[:, o[j]:o[j + 1]] for j in range(11)]

    def dup(wg):
        return jnp.concatenate([wg[:, :d], wg[:, :d], wg[:, d:], wg[:, d:]], axis=1)

    zeros = lambda n: jnp.zeros((w_in.shape[0], n), w_in.dtype)
    kr_slab = jnp.concatenate([zeros(MLA_NOPE), kr, zeros(LANES - MLA_NOPE - MLA_ROPE)], axis=1)
    cols = [nq, kc, vc, dup(ks), vs, dup(kw), vw, _pad_cols(gz, LANES), cq, ckv, kr_slab]
    return jnp.concatenate(cols, axis=1).astype(BF16)


def _odd_mixer(x, gain, w_in, cmp_pos, cmp_w1, cmp_w2, q_norm, kv_norm, w_uq, w_ukv, w_out, ropes, B, S):
    D = x.shape[1]
    G, d = NSA_GROUPS, NSA_DIM
    w = _odd_in_weight(w_in)
    sc = d ** -0.5 * LOG2E
    widths = [(NSA_HEADS * d, BF16, "r64", sc), (LANES, F32, "r64", 1.0), (LANES, F32, None, 1.0),
              (2 * LANES, BF16, "r64", 1.0), (LANES, BF16, None, 1.0),
              (2 * LANES, BF16, "r64", 1.0), (LANES, BF16, None, 1.0),
              (LANES, F32, None, 1.0), (MLA_Q_RANK, F32, None, 1.0), (MLA_KV_RANK, F32, None, 1.0),
              (LANES, F32, "mla", 1.0)]
    segs, start = [], 0
    for wd, dt, rp, s_ in widths:
        segs.append(Seg(start, wd, dt, rp, s_))
        start += wd
    q, kc, vc, ks, vs, kw, vw, gz, cq, ckv, kr = _linear(
        [x], [w], segs, gain=gain, ropes=ropes, seq=S, name="odd_in")

    R = S // CMP_STRIDE

    K = CMP_STRIDE * G * d
    half = CMP_LEN // 2
    pe = jnp.broadcast_to(cmp_pos.reshape(2, 2, half, 1, d), (2, 2, half, G, d)).reshape(2, 2, 1, K).astype(F32)
    eye = jnp.eye(G, dtype=cmp_w1.dtype)
    w1x = (cmp_w1.reshape(2, 2, 1, half, 1, d, CMP_HIDDEN) * eye[None, None, :, None, :, None, None])
    w1x = w1x.reshape(2, 2, G, K, CMP_HIDDEN).astype(BF16)
    w2d = jnp.concatenate([cmp_w2, cmp_w2], axis=-1).astype(BF16)
    cmp_k, cmp_vt = _nsa_compress(kc.reshape(B, R, K), vc.reshape(B, R, K), pe, w1x, w2d)
    cs = np.arange(R) * CMP_STRIDE
    bs = np.arange(LANES) * SLC_LEN
    ov = ((cs[:, None] < bs[None, :] + SLC_LEN) & (cs[:, None] + CMP_LEN > bs[None, :])
          & (np.arange(R)[:, None] < R - 1) & (bs[None, :] < S))
    o_cmp, sel = _nsa_cmp_select(q, cmp_k, cmp_vt, jnp.asarray(ov.T.astype(np.float32), BF16), B=B, S=S)
    onehot = (np.arange(S)[:, None] // SLC_LEN == np.arange(LANES)[None, :]).astype(np.float32)
    o_slc = _flash(q, ks, vs.T, B=B, S=S, P=G, q_w=2 * LANES, k_w=LANES, v_rows=HALF_LANES,
                   heads=_QUAD_HEADS, kind="slc", out_dtype=BF16, ext=jnp.asarray(onehot, BF16),
                   sel=sel, name="nsa_slc")
    o_win = _flash(q, kw, vw.T, B=B, S=S, P=G, q_w=2 * LANES, k_w=LANES, v_rows=HALF_LANES,
                   heads=_QUAD_HEADS, kind="win", out_dtype=BF16, tq=WIN, tk=WIN, n_bufs=2, name="nsa_win")
    ex = np.zeros((3, LANES, NSA_HEADS * d), np.float32)
    for hh in range(NSA_HEADS):
        for j in range(3):
            ex[j, hh * 3 + j, hh * d:(hh + 1) * d] = 1.0
    o_nsa = _nsa_combine(o_cmp, o_slc, o_win, gz, jnp.asarray(ex, BF16))

    H = MLA_HEADS
    qk = MLA_NOPE + MLA_ROPE
    wq_slab = jnp.pad(w_uq.reshape(MLA_Q_RANK, H, qk), ((0, 0), (0, 0), (0, LANES - qk)))
    wq_slab = wq_slab.reshape(MLA_Q_RANK, H * LANES).astype(BF16)
    (qm,) = _linear([cq], [wq_slab], [Seg(0, H * LANES, BF16, "mla", qk ** -0.5 * LOG2E)],
                    gain=q_norm, ropes=ropes, seq=S, name="mla_q")
    wkv = w_ukv.reshape(MLA_KV_RANK, H, MLA_NOPE + MLA_V)
    wk = jnp.pad(wkv[:, :, :MLA_NOPE], ((0, 0), (0, 0), (0, LANES - MLA_NOPE)))
    wk = wk.reshape(MLA_KV_RANK, H * LANES).astype(BF16)
    wv = wkv[:, :, MLA_NOPE:].reshape(MLA_KV_RANK, H * MLA_V).astype(BF16)
    km, vm = _mla_kv(ckv, kv_norm, wk, wv, kr)
    o_mla = _flash(qm, km, vm.T, B=B, S=S, P=H // 4, q_w=4 * LANES, k_w=4 * LANES, v_rows=2 * LANES,
                   heads=_WIDE_HEADS, kind="mla", out_dtype=BF16, name="mla_attn")
    wo = w_out.astype(BF16)
    half = NSA_HEADS * d
    (y,) = _linear([o_nsa, o_mla], [wo[:half], wo[half:]], [Seg(0, D, F32)], residual=x, name="odd_out")
    return y


def kernel(x, mem, mem_norm, norm_mix, norm_mem, norm_ffn, ev_w_in, ev_b_f, ev_lam, ev_subln, ev_w_out, od_w_in, nsa_cmp_pos, nsa_cmp_w1, nsa_cmp_w2, mla_q_norm, mla_kv_norm, mla_w_uq, mla_w_ukv, od_w_out, xa_wq, xa_wkv, xa_wo, ffn_w13, ffn_w2, final_norm):
    B, S, D = x.shape
    M = mem.shape[1]
    depth = norm_mix.shape[0]
    ropes = {
        "r64": (_rope_tables(S, NSA_DIM // 2, 0, LANES), NSA_DIM // 2),
        "mla": (_rope_tables(S, MLA_ROPE // 2, MLA_NOPE, MLA_NOPE + MLA_ROPE), MLA_ROPE // 2),
    }
    xa_w = XA_HEADS * XA_DIM
    h = x.reshape(B * S, D)
    mem2 = mem.reshape(B * M, D)
    for li in range(depth):
        j = li // 2
        if li % 2 == 0:
            h = _even_mixer(h, norm_mix[li], ev_w_in[j], ev_b_f[j], ev_lam[j], ev_subln[j], ev_w_out[j],
                            li, ropes, B, S)
        else:
            h = _odd_mixer(h, norm_mix[li], od_w_in[j], nsa_cmp_pos[j], nsa_cmp_w1[j], nsa_cmp_w2[j],
                           mla_q_norm[j], mla_kv_norm[j], mla_w_uq[j], mla_w_ukv[j], od_w_out[j],
                           ropes, B, S)
        mk, mv = _linear([mem2], [xa_wkv[li].astype(BF16)],
                         [Seg(0, xa_w, BF16), Seg(xa_w, xa_w, BF16)], gain=mem_norm, name="mem_kv")
        h = _mem_attn(h, norm_mem[li], xa_wq[li].astype(BF16), xa_wo[li].astype(BF16),
                      mk.reshape(B, M, xa_w), mv.reshape(B, M, xa_w), S=S)
        h = _ffn(h, norm_ffn[li], ffn_w13[li].astype(BF16), ffn_w2[li].astype(BF16),
                 out_gain=final_norm if li == depth - 1 else None)
    return h.reshape(B, S, D)
```

```python
import math
from typing import NamedTuple, Optional

import numpy as np
import jax
import jax.numpy as jnp
from jax import lax
from jax.experimental import pallas as pl
from jax.experimental.pallas import tpu as pltpu

F32 = jnp.float32
BF16 = jnp.bfloat16

LANES = 128
HALF_LANES = LANES // 2
BF16_ROWS = 16
ROPE_THETA = 10000.0
EPS = 1e-6
NEG = -1e30
LOG2E = math.log2(math.e)
MASK_BIG = 2.0 ** 100

DIFF_HEADS = 4
DIFF_QK = 64
FOX_HEADS = 8
NSA_HEADS = 8
NSA_GROUPS = 2
NSA_DIM = 64
CMP_LEN = 32
CMP_STRIDE = 16
CMP_HIDDEN = 128
SLC_LEN = 64
SLC_TOPK = 16
WIN = 512
FORCE_SCORE = 1e4
MLA_HEADS = 8
MLA_NOPE = 64
MLA_ROPE = 32
MLA_V = 64
MLA_Q_RANK = 384
MLA_KV_RANK = 256
XA_HEADS = 4
XA_DIM = 128

VMEM_LIMIT = 56 * 1024 * 1024
ROW_TILE = 512
COL_CHUNK = 512
ATT_TILE = 512
KEY_TILE = 512
ATT_BUFS = 3
MASK_BLOCK = 256
SEL_TILE = 512
CUM_CHUNK = 256

_NT = (((1,), (1,)), ((), ()))


def _params(*sem):
    return pltpu.CompilerParams(dimension_semantics=sem, vmem_limit_bytes=VMEM_LIMIT)


def _sigmoid(x):
    return 1.0 / (1.0 + jnp.exp(-x))


def _split_bf16(x, terms):
    out = []
    r = x
    for _ in range(terms):
        h = r.astype(BF16)
        out.append(h)
        r = r - h.astype(F32)
    return out


def _rope_tables(S, half, lane_lo, lane_hi):
    pos = jnp.arange(S, dtype=F32)
    inv = 1.0 / (ROPE_THETA ** (jnp.arange(half, dtype=F32) / half))
    ang = pos[:, None] * inv[None, :]
    cos, sin = jnp.cos(ang), jnp.sin(ang)
    lane = np.arange(LANES)
    active = (lane >= lane_lo) & (lane < lane_hi)
    j = (lane - lane_lo) % (2 * half)
    lower = active & (j < half)
    upper = active & (j >= half)
    idx = j % half
    cos_t = jnp.where(active[None, :], cos[:, idx], 1.0)
    sin_a = jnp.where(lower[None, :], -sin[:, idx], 0.0)
    sin_b = jnp.where(upper[None, :], sin[:, idx], 0.0)
    return cos_t, sin_a, sin_b


class Seg(NamedTuple):
    start: int
    width: int
    dtype: object
    rope: Optional[str] = None
    scale: float = 1.0


def _linear(xs, ws, segs, *, gain=None, residual=None, ropes=None, seq=None, name="linear"):
    N = xs[0].shape[0]
    tm = min(ROW_TILE, N)
    assert N % tm == 0
    n_in = len(xs)
    has_gain = gain is not None
    has_res = residual is not None
    rope_keys = sorted({s.rope for s in segs if s.rope})
    halves = {k: ropes[k][1] for k in rope_keys}

    def kern(*refs):
        it = iter(refs)
        x_refs = [next(it) for _ in range(n_in)]
        w_refs = [next(it) for _ in range(n_in)]
        g_ref = next(it) if has_gain else None
        r_ref = next(it) if has_res else None
        tabs = {k: (next(it), next(it), next(it)) for k in rope_keys}
        o_refs = [next(it) for _ in segs]
        acts = []
        for j, xr in enumerate(x_refs):
            x = xr[...]
            if j == 0 and has_gain:
                xf = x.astype(F32)
                y = xf * lax.rsqrt(jnp.mean(xf * xf, axis=-1, keepdims=True) + EPS)
                acts.append((y * g_ref[...]).astype(BF16))
            else:
                acts.append(x.astype(BF16))
        for seg, o_ref in zip(segs, o_refs):
            for c0 in range(0, seg.width, COL_CHUNK):
                cw = min(COL_CHUNK, seg.width - c0)
                col = seg.start + c0
                acc = None
                for a, wr in zip(acts, w_refs):
                    d = jnp.dot(a, wr[:, col:col + cw], preferred_element_type=F32)
                    acc = d if acc is None else acc + d
                if has_res:
                    acc = acc + r_ref[:, col:col + cw]
                if seg.scale != 1.0:
                    acc = acc * seg.scale
                if seg.rope is None:
                    o_ref[:, c0:c0 + cw] = acc.astype(o_ref.dtype)
                else:
                    cos_r, sa_r, sb_r = tabs[seg.rope]
                    half = halves[seg.rope]
                    cos, sa, sb = cos_r[...], sa_r[...], sb_r[...]
                    for s0 in range(0, cw, LANES):
                        xs_ = acc[:, s0:s0 + LANES]
                        y = (xs_ * cos + pltpu.roll(xs_, LANES - half, 1) * sa
                             + pltpu.roll(xs_, half, 1) * sb)
                        o_ref[:, c0 + s0:c0 + s0 + LANES] = y.astype(o_ref.dtype)

    in_specs, args = [], []
    for x in xs:
        in_specs.append(pl.BlockSpec((tm, x.shape[1]), lambda i: (i, 0)))
        args.append(x)
    for w in ws:
        in_specs.append(pl.BlockSpec(w.shape, lambda i: (0, 0)))
        args.append(w)
    if has_gain:
        in_specs.append(pl.BlockSpec((1, gain.shape[-1]), lambda i: (0, 0)))
        args.append(gain.reshape(1, -1).astype(F32))
    if has_res:
        in_specs.append(pl.BlockSpec((tm, residual.shape[1]), lambda i: (i, 0)))
        args.append(residual)
    for k in rope_keys:
        assert seq % tm == 0
        nt = seq // tm
        for t in ropes[k][0]:
            in_specs.append(pl.BlockSpec((tm, LANES), lambda i, nt=nt: (i % nt, 0)))
            args.append(t)
    out_shape = [jax.ShapeDtypeStruct((N, s.width), s.dtype) for s in segs]
    out_specs = [pl.BlockSpec((tm, s.width), lambda i: (i, 0)) for s in segs]
    return pl.pallas_call(
        kern, grid=(N // tm,), in_specs=in_specs, out_specs=out_specs, out_shape=out_shape,
        compiler_params=_params("parallel"), name=name)(*args)


def _forget_key_terms(fz, b_f, B, S):
    ch = CUM_CHUNK
    place = np.zeros((3, LANES, LANES), np.float32)
    for h in range(FOX_HEADS):
        for j in range(3):
            place[j, h, 3 * h + j] = 1.0

    def kern(z_ref, b_ref, pl_ref, o_ref):
        r = lax.broadcasted_iota(jnp.int32, (ch, ch), 0)
        c = lax.broadcasted_iota(jnp.int32, (ch, ch), 1)
        tri = jnp.where(c <= r, 1.0, 0.0).astype(BF16)

        def body(j, carries):
            out = []
            for b, carry in enumerate(carries):
                r0 = pl.multiple_of(b * S + j * ch, ch)
                z = z_ref[pl.ds(r0, ch), :] + b_ref[...]
                logf = -(jnp.maximum(-z, 0.0) + jnp.log1p(jnp.exp(-jnp.abs(z))))
                cs = carry
                for part in _split_bf16(logf, 3):
                    cs = cs + jnp.dot(tri, part, preferred_element_type=F32)
                terms = jnp.zeros((ch, LANES), F32)
                for jj, part in enumerate(_split_bf16(cs * (-LOG2E), 3)):
                    terms = terms + jnp.dot(part, pl_ref[jj], preferred_element_type=F32)
                o_ref[pl.ds(r0, ch), :] = terms.astype(o_ref.dtype)
                out.append(cs[ch - 1:ch, :])
            return tuple(out)

        lax.fori_loop(0, S // ch, body, tuple(jnp.zeros((1, LANES), F32) for _ in range(B)))

    return pl.pallas_call(
        kern, grid=(1,),
        in_specs=[pl.BlockSpec((B * S, LANES), lambda g: (0, 0), pipeline_mode=pl.Buffered(1)),
                  pl.BlockSpec((1, LANES), lambda g: (0, 0)),
                  pl.BlockSpec(place.shape, lambda g: (0, 0, 0))],
        out_specs=pl.BlockSpec((B * S, LANES), lambda g: (0, 0)),
        out_shape=jax.ShapeDtypeStruct((B * S, LANES), BF16),
        compiler_params=_params("arbitrary"), name="forget_terms")(fz, b_f, jnp.asarray(place, BF16))


class Head(NamedTuple):
    q_off: int
    q_half: Optional[str]
    k_off: int
    v0: int
    v1: int


def _flash(q, k, vt, *, B, S, P, q_w, k_w, v_rows, heads, kind, out_dtype, tq=None, tk=None, n_bufs=None,
           ext=None, sel=None, lam=None, subln=None, lam_init=None, name="flash"):
    TQ, TK = tq or ATT_TILE, tk or KEY_TILE
    assert S % TQ == 0 and TQ % TK == 0
    diag = TQ // TK
    nq = S // TQ
    nh = len(heads)
    rows = heads[0].v1 - heads[0].v0
    out_w = nh // 2 * rows if kind == "diff" else nh * rows
    width = TK
    n_bufs = n_bufs or ATT_BUFS
    if kind == "win":
        assert WIN == TK and TQ == TK and S >= 2 * TK and n_bufs == 2
    has_ext = kind in ("fox", "slc")

    def kern(*refs):
        it = iter(refs)
        q_ref, k_ref, v_ref = next(it), next(it), next(it)
        ext_ref = next(it) if has_ext else None
        sel_ref = next(it) if kind == "slc" else None
        lam_ref, sub_ref = (next(it), next(it)) if kind == "diff" else (None, None)
        o_ref, m_ref, acc_ref, mx_ref = next(it), next(it), next(it), next(it)
        bufs = [next(it) for _ in range(n_bufs)]
        p_id = pl.program_id(1)
        i = pl.program_id(2)

        lane = lax.broadcasted_iota(jnp.int32, (TQ, LANES), 1)
        lo = lane < HALF_LANES
        q_all = q_ref[...]
        if kind == "slc":
            drop = ((sel_ref[...].astype(F32) - 1.0) * MASK_BIG).astype(BF16)
        qs = []
        for j, h in enumerate(heads):
            qh = q_all[:, h.q_off:h.q_off + LANES]
            if h.q_half == "lo":
                qh = jnp.where(lo, qh, jnp.zeros_like(qh))
            elif h.q_half == "hi":
                qh = jnp.where(lo, jnp.zeros_like(qh), qh)
            if kind == "fox":
                first = 3 * (nh * p_id + j)
                pick = jnp.where((lane >= first) & (lane < first + 3), 1.0, 0.0).astype(BF16)
                qh = jnp.concatenate([qh, pick], axis=1)
            if kind == "slc":
                qh = jnp.concatenate([qh, drop], axis=1)
            qs.append(qh)

        m_ref[...] = jnp.full(m_ref.shape, NEG, F32)
        acc_ref[...] = jnp.zeros(acc_ref.shape, F32)

        def block_kind(ahead, kb, qb):
            lo_rel = (kb - qb) * MASK_BLOCK - (MASK_BLOCK - 1)
            hi_rel = (kb - qb) * MASK_BLOCK + (MASK_BLOCK - 1)
            floor = ahead - WIN if kind == "win" else lo_rel - 1
            if lo_rel > ahead or hi_rel <= floor:
                return "dead"
            if hi_rel <= ahead and lo_rel > floor:
                return "full"
            return "part"

        def block_mask(ahead, kb, qb):
            rel = (lax.broadcasted_iota(jnp.int32, (MASK_BLOCK, MASK_BLOCK), 0)
                   - lax.broadcasted_iota(jnp.int32, (MASK_BLOCK, MASK_BLOCK), 1)
                   + (kb - qb) * MASK_BLOCK)
            msk = rel <= ahead
            if kind == "win":
                msk = msk & (rel > ahead - WIN)
            return msk

        kbs, qbs = range(width // MASK_BLOCK), range(TQ // MASK_BLOCK)
        blk = lambda b: slice(b * MASK_BLOCK, (b + 1) * MASK_BLOCK)

        def scores(k0, u, ahead=None):
            kx = k_ref[pl.ds(k0, width), :]
            ex = ext_ref[pl.ds(k0, width), :] if has_ext else None
            for hi, h in enumerate(heads):
                kk = kx[:, h.k_off:h.k_off + LANES]
                if ex is not None:
                    kk = jnp.concatenate([kk, ex], axis=1)
                if ahead is None:
                    st = lax.dot_general(kk, qs[hi], _NT, preferred_element_type=F32)
                    bufs[u][hi] = st
                    mx_ref[u * nh + hi] = jnp.max(st, axis=0, keepdims=True)
                    continue
                col_max = [None] * len(qbs)
                for kb in kbs:
                    live = [qb for qb in qbs if block_kind(ahead, kb, qb) != "dead"]
                    if not live:
                        continue
                    q_rows = slice(live[0] * MASK_BLOCK, (live[-1] + 1) * MASK_BLOCK)
                    st = lax.dot_general(kk[blk(kb)], qs[hi][q_rows], _NT, preferred_element_type=F32)
                    for qb in range(live[0], live[-1] + 1):
                        sb = st[:, blk(qb - live[0])]
                        if block_kind(ahead, kb, qb) == "part":
                            sb = jnp.where(block_mask(ahead, kb, qb), sb, NEG)
                        bufs[u][hi, blk(kb), blk(qb)] = sb
                        cm = jnp.max(sb, axis=0, keepdims=True)
                        col_max[qb] = cm if col_max[qb] is None else jnp.maximum(col_max[qb], cm)
                mx_ref[u * nh + hi] = jnp.concatenate(col_max, axis=1)

        def update(k0, u, ahead=None):
            for hi, h in enumerate(heads):
                m_prev = m_ref[hi]
                m_new = jnp.maximum(m_prev, mx_ref[u * nh + hi])
                alpha = jnp.exp2(m_prev - m_new)
                if ahead is None:
                    ones = jnp.ones((BF16_ROWS, width), BF16)
                    pt = jnp.exp2(bufs[u][hi] - m_new).astype(BF16)
                    lhs = jnp.concatenate([v_ref[h.v0:h.v1, pl.ds(k0, width)], ones], axis=0)
                    new = jnp.dot(lhs, pt, preferred_element_type=F32)
                else:
                    ones = jnp.ones((BF16_ROWS, MASK_BLOCK), BF16)
                    cols = []
                    for qb in qbs:
                        col = None
                        for kb in kbs:
                            if block_kind(ahead, kb, qb) == "dead":
                                continue
                            pt = jnp.exp2(bufs[u][hi, blk(kb), blk(qb)] - m_new[:, blk(qb)]).astype(BF16)
                            kstart = pl.multiple_of(k0 + kb * MASK_BLOCK, MASK_BLOCK)
                            lhs = jnp.concatenate(
                                [v_ref[h.v0:h.v1, pl.ds(kstart, MASK_BLOCK)], ones], axis=0)
                            part = jnp.dot(lhs, pt, preferred_element_type=F32)
                            col = part if col is None else col + part
                        cols.append(col)
                    new = jnp.concatenate(cols, axis=1)
                acc_ref[hi] = alpha * acc_ref[hi] + new
                m_ref[hi] = m_new

        tile = lambda t: pl.multiple_of(t * TK, TK)
        U = len(bufs)
        if kind == "win":
            @pl.when(i == 0)
            def _():
                scores(tile(0), 0, 0)
                update(tile(0), 0, 0)

            @pl.when(i > 0)
            def _():
                scores(tile(i - 1), 0, WIN)
                scores(tile(i), 1, 0)
                update(tile(i - 1), 0, WIN)
                update(tile(i), 1, 0)
        else:
            n_full = i * diag
            n_rounds = jnp.maximum(n_full - 1, 0) // U
            pl.when(n_full == 0)(lambda: scores(tile(0), 0, 0))
            pl.when(n_full > 0)(lambda: scores(tile(0), 0))

            def body(j, c):
                for u in range(U):
                    scores(tile(U * j + u + 1), (u + 1) % U)
                    update(tile(U * j + u), u)
                return c
            lax.fori_loop(0, n_rounds, body, 0)
            t0 = U * n_rounds

            def tail(left):
                ahead = lambda u: -(u - left) * TK if u >= left else None
                last = left + diag - 1
                for u in range(last):
                    scores(tile(t0 + u + 1), (u + 1) % U, ahead(u + 1))
                    update(tile(t0 + u), u % U, ahead(u))
                update(tile(t0 + last), last % U, ahead(last))

            for left in range(U + 1):
                pl.when(n_full - t0 == left)(lambda left=left: tail(left))

        outs = []
        for hi in range(nh):
            acc = acc_ref[hi]
            outs.append(acc[:rows] / acc[rows:rows + 1])
        if kind == "diff":
            lm = lam_ref[...]
            la = jnp.sum(lm[0:1] * lm[1:2], axis=1, keepdims=True)
            lb = jnp.sum(lm[2:3] * lm[3:4], axis=1, keepdims=True)
            lam_full = jnp.exp(la) - jnp.exp(lb) + lam_init
            for g in range(nh // 2):
                oa = jnp.transpose(outs[2 * g] - lam_full * outs[2 * g + 1])
                y = oa * lax.rsqrt(jnp.mean(oa * oa, axis=-1, keepdims=True) + EPS)
                y = (y * sub_ref[...]) * (1.0 - lam_init)
                o_ref[:, g * rows:(g + 1) * rows] = y.astype(o_ref.dtype)
        else:
            o_ref[...] = jnp.transpose(jnp.concatenate(outs, axis=0)).astype(o_ref.dtype)

    in_specs = [
        pl.BlockSpec((TQ, q_w), lambda b, p, i: (b * nq + i, p)),
        pl.BlockSpec((S, k_w), lambda b, p, i: (b, p)),
        pl.BlockSpec((v_rows, S), lambda b, p, i: (p, b)),
    ]
    args = [q, k, vt]
    if kind == "fox":
        in_specs.append(pl.BlockSpec((S, LANES), lambda b, p, i: (b, 0)))
        args.append(ext)
    if kind == "slc":
        in_specs.append(pl.BlockSpec((S, LANES), lambda b, p, i: (0, 0)))
        in_specs.append(pl.BlockSpec((TQ, LANES), lambda b, p, i: (b * nq + i, p)))
        args += [ext, sel]
    if kind == "diff":
        in_specs.append(pl.BlockSpec(lam.shape, lambda b, p, i: (0, 0)))
        in_specs.append(pl.BlockSpec((1, LANES), lambda b, p, i: (0, 0)))
        args += [lam, subln.reshape(1, LANES)]
    return pl.pallas_call(
        kern, grid=(B, P, nq), in_specs=in_specs,
        out_specs=pl.BlockSpec((TQ, out_w), lambda b, p, i: (b * nq + i, p)),
        out_shape=jax.ShapeDtypeStruct((B * S, P * out_w), out_dtype),
        scratch_shapes=[pltpu.VMEM((nh, 1, TQ), F32),
                        pltpu.VMEM((nh, rows + BF16_ROWS, TQ), F32),
                        pltpu.VMEM((n_bufs * nh, 1, TQ), F32)]
        + [pltpu.VMEM((nh, width, TQ), F32)] * n_bufs,
        compiler_params=_params("parallel", "parallel", "arbitrary"), name=name)(*args)


_DIFF_HEADS = tuple(Head(s * LANES, half, s * LANES, s * LANES, (s + 1) * LANES)
                    for s in range(2) for half in ("lo", "hi"))
_PAIR_HEADS = tuple(Head(s * LANES, half, s * LANES, (2 * s + j) * HALF_LANES, (2 * s + j + 1) * HALF_LANES)
                    for s in range(2) for j, half in enumerate(("lo", "hi")))
_QUAD_HEADS = tuple(Head(s * LANES, half, 0, 0, HALF_LANES) for s in range(2) for half in ("lo", "hi"))
_WIDE_HEADS = tuple(Head(j * LANES, None, j * LANES, j * HALF_LANES, (j + 1) * HALF_LANES) for j in range(4))


def _nsa_compress(xk, xv, pe, w1x, w2d):
    B, R, K = xk.shape
    G = NSA_GROUPS

    def kern(xk_ref, xv_ref, pe_ref, w1_ref, w2_ref, k_ref, vt_ref):
        for t, x_ref in enumerate((xk_ref, xv_ref)):
            x = x_ref[...]
            xa = (x + pe_ref[t, 0]).astype(BF16)
            xb = (x + pe_ref[t, 1]).astype(BF16)
            for g in range(G):
                a = jnp.dot(xa, w1_ref[t, 0, g], preferred_element_type=F32)
                b = jnp.dot(xb, w1_ref[t, 1, g], preferred_element_type=F32)
                h = a + pltpu.roll(b, R - 1, 0)
                hs = h * _sigmoid(h)
                o = jnp.dot(hs.astype(BF16), w2_ref[t], preferred_element_type=F32)
                if t == 0:
                    k_ref[g] = o
                else:
                    vt_ref[g] = jnp.transpose(o)

    return pl.pallas_call(
        kern, grid=(B,),
        in_specs=[pl.BlockSpec((None, R, K), lambda b: (b, 0, 0)),
                  pl.BlockSpec((None, R, K), lambda b: (b, 0, 0)),
                  pl.BlockSpec(pe.shape, lambda b: (0, 0, 0, 0)),
                  pl.BlockSpec(w1x.shape, lambda b: (0, 0, 0, 0, 0)),
                  pl.BlockSpec(w2d.shape, lambda b: (0, 0, 0))],
        out_specs=[pl.BlockSpec((None, G, R, LANES), lambda b: (b, 0, 0, 0)),
                   pl.BlockSpec((None, G, LANES, R), lambda b: (b, 0, 0, 0))],
        out_shape=[jax.ShapeDtypeStruct((B, G, R, LANES), F32),
                   jax.ShapeDtypeStruct((B, G, LANES, R), F32)],
        compiler_params=_params("parallel"), name="nsa_compress")(xk, xv, pe, w1x, w2d)


def _nsa_cmp_select(q, cmp_k, cmp_vt, ov_t, *, B, S):
    T = SEL_TILE
    nq = S // T
    R = cmp_k.shape[2]
    G = NSA_GROUPS
    d = NSA_DIM
    n_sel = min(SLC_TOPK, S // SLC_LEN)
    assert n_sel >= 3 and FORCE_SCORE > NSA_HEADS // G

    bucket = min(LANES, R)
    assert R % bucket == 0 and bucket % (SLC_LEN // CMP_STRIDE) == 0

    def kern(q_ref, kc_ref, vt_ref, ov_ref, o_ref, sel_ref):
        i = pl.program_id(2)
        lane = lax.broadcasted_iota(jnp.int32, (T, LANES), 1)
        lo = lane < HALF_LANES
        q_all = q_ref[...]
        qs = []
        for hh in range(4):
            qh = q_all[:, (hh // 2) * LANES:(hh // 2 + 1) * LANES]
            qs.append(jnp.where(lo, qh, jnp.zeros_like(qh)) if hh % 2 == 0
                      else jnp.where(lo, jnp.zeros_like(qh), qh))

        def variant(rv):
            nb = rv * CMP_STRIDE // SLC_LEN
            tq = lax.broadcasted_iota(jnp.int32, (rv, T), 1) + i * T
            cend = lax.broadcasted_iota(jnp.int32, (rv, T), 0) * CMP_STRIDE + (CMP_LEN - 1)
            vis = cend <= tq
            kc = kc_ref[:rv, :].astype(BF16)
            vt = vt_ref[:d, :rv].astype(BF16)
            psum = jnp.zeros((rv, T), F32)
            o_heads = []
            for hh in range(4):
                st = lax.dot_general(kc, qs[hh], _NT, preferred_element_type=F32)
                st = jnp.where(vis, st, -jnp.inf)
                m = jnp.max(st, axis=0, keepdims=True)
                e = jnp.exp2(st - jnp.where(m > -jnp.inf, m, 0.0))
                p = e / jnp.maximum(jnp.sum(e, axis=0, keepdims=True), 1e-30)
                o_heads.append(jnp.dot(vt, p.astype(BF16), preferred_element_type=F32))
                psum = psum + p
            o_ref[...] = jnp.transpose(jnp.concatenate(o_heads, axis=0)).astype(o_ref.dtype)
            imp = jnp.zeros((nb, T), F32)
            for part in _split_bf16(psum, 2):
                imp = imp + jnp.dot(ov_ref[:nb, :rv], part, preferred_element_type=F32)
            blk = lax.broadcasted_iota(jnp.int32, (nb, T), 0)
            cur = (lax.broadcasted_iota(jnp.int32, (nb, T), 1) + i * T) // SLC_LEN
            forced = (blk == 0) | (blk == cur) | (blk == cur - 1)
            work = jnp.where(forced, -jnp.inf, imp)
            work = jnp.where(blk > cur, NEG, work)
            blk_f = blk.astype(F32)
            for _ in range(n_sel - 3):
                mx = jnp.max(work, axis=0, keepdims=True)
                first = jnp.min(jnp.where(work == mx, blk_f, float(LANES)), axis=0, keepdims=True)
                work = jnp.where(blk_f == first, -jnp.inf, work)
            sel = jnp.where(work == -jnp.inf, 1.0, 0.0)
            if nb < LANES:
                sel = jnp.concatenate([sel, jnp.zeros((LANES - nb, T), F32)], axis=0)
            sel_ref[...] = jnp.transpose(sel).astype(sel_ref.dtype)

        need = (i + 1) * (T // CMP_STRIDE)
        which = (need - 1) // bucket
        for bk in range(R // bucket):
            pl.when(which == bk)(lambda bk=bk: variant((bk + 1) * bucket))

    return pl.pallas_call(
        kern, grid=(B, G, nq),
        in_specs=[pl.BlockSpec((T, 2 * LANES), lambda b, g, i: (b * nq + i, g)),
                  pl.BlockSpec((None, None, R, LANES), lambda b, g, i: (b, g, 0, 0)),
                  pl.BlockSpec((None, None, LANES, R), lambda b, g, i: (b, g, 0, 0)),
                  pl.BlockSpec((LANES, R), lambda b, g, i: (0, 0))],
        out_specs=[pl.BlockSpec((T, 2 * LANES), lambda b, g, i: (b * nq + i, g)),
                   pl.BlockSpec((T, LANES), lambda b, g, i: (b * nq + i, g))],
        out_shape=[jax.ShapeDtypeStruct((B * S, G * 2 * LANES), BF16),
                   jax.ShapeDtypeStruct((B * S, G * LANES), BF16)],
        compiler_params=_params("parallel", "parallel", "parallel"), name="nsa_cmp_select")(
            q, cmp_k, cmp_vt, ov_t)


def _nsa_combine(o_cmp, o_slc, o_win, gz, expand):
    N, C = o_cmp.shape
    tm = min(ROW_TILE, N)

    def kern(c_ref, s_ref, w_ref, g_ref, e_ref, o_ref):
        gate = _sigmoid(g_ref[...])
        parts = _split_bf16(gate, 2)
        acc = jnp.zeros((tm, C), F32)
        for j, br in enumerate((c_ref, s_ref, w_ref)):
            gj = sum(jnp.dot(part, e_ref[j], preferred_element_type=F32) for part in parts)
            acc = acc + gj * br[...].astype(F32)
        o_ref[...] = acc.astype(o_ref.dtype)

    row = pl.BlockSpec((tm, C), lambda i: (i, 0))
    return pl.pallas_call(
        kern, grid=(N // tm,),
        in_specs=[row, row, row, pl.BlockSpec((tm, LANES), lambda i: (i, 0)),
                  pl.BlockSpec(expand.shape, lambda i: (0, 0, 0))],
        out_specs=row, out_shape=jax.ShapeDtypeStruct((N, C), BF16),
        compiler_params=_params("parallel"), name="nsa_combine")(o_cmp, o_slc, o_win, gz, expand)


def _mla_kv(ckv, gain, wk, wv, kr):
    N, K = ckv.shape
    tm = min(ROW_TILE, N)
    H = MLA_HEADS

    def kern(c_ref, g_ref, wk_ref, wv_ref, kr_ref, k_ref, v_ref):
        xf = c_ref[...]
        y = xf * lax.rsqrt(jnp.mean(xf * xf, axis=-1, keepdims=True) + EPS)
        a = (y * g_ref[...]).astype(BF16)
        kr_ = kr_ref[...]
        for h in range(H):
            kh = jnp.dot(a, wk_ref[:, h * LANES:(h + 1) * LANES], preferred_element_type=F32)
            k_ref[:, h * LANES:(h + 1) * LANES] = (kh + kr_).astype(k_ref.dtype)
        v_ref[...] = jnp.dot(a, wv_ref[...], preferred_element_type=F32).astype(v_ref.dtype)

    return pl.pallas_call(
        kern, grid=(N // tm,),
        in_specs=[pl.BlockSpec((tm, K), lambda i: (i, 0)),
                  pl.BlockSpec((1, K), lambda i: (0, 0)),
                  pl.BlockSpec(wk.shape, lambda i: (0, 0)),
                  pl.BlockSpec(wv.shape, lambda i: (0, 0)),
                  pl.BlockSpec((tm, LANES), lambda i: (i, 0))],
        out_specs=[pl.BlockSpec((tm, H * LANES), lambda i: (i, 0)),
                   pl.BlockSpec((tm, H * MLA_V), lambda i: (i, 0))],
        out_shape=[jax.ShapeDtypeStruct((N, H * LANES), BF16),
                   jax.ShapeDtypeStruct((N, H * MLA_V), BF16)],
        compiler_params=_params("parallel"), name="mla_kv")(
            ckv, gain.reshape(1, K).astype(F32), wk, wv, kr)


def _mem_attn(x, gain, wq, wo, mem_k, mem_v, *, S):
    N, D = x.shape
    tm = min(ROW_TILE, S)
    M = mem_k.shape[1]
    per_b = S // tm
    scale = XA_DIM ** -0.5

    def kern(x_ref, g_ref, wq_ref, wo_ref, k_ref, v_ref, o_ref):
        xf = x_ref[...]
        y = xf * lax.rsqrt(jnp.mean(xf * xf, axis=-1, keepdims=True) + EPS)
        a = (y * g_ref[...]).astype(BF16)
        q = (jnp.dot(a, wq_ref[...], preferred_element_type=F32) * scale).astype(BF16)
        heads = []
        for h in range(XA_HEADS):
            sl = slice(h * XA_DIM, (h + 1) * XA_DIM)
            s = lax.dot_general(q[:, sl], k_ref[:, sl], _NT, preferred_element_type=F32)
            e = jnp.exp(s - jnp.max(s, axis=1, keepdims=True))
            p = e / jnp.sum(e, axis=1, keepdims=True)
            heads.append(jnp.dot(p.astype(BF16), v_ref[:, sl], preferred_element_type=F32).astype(BF16))
        o = jnp.concatenate(heads, axis=1)
        o_ref[...] = xf + jnp.dot(o, wo_ref[...], preferred_element_type=F32)

    return pl.pallas_call(
        kern, grid=(N // tm,),
        in_specs=[pl.BlockSpec((tm, D), lambda i: (i, 0)),
                  pl.BlockSpec((1, D), lambda i: (0, 0)),
                  pl.BlockSpec(wq.shape, lambda i: (0, 0)),
                  pl.BlockSpec(wo.shape, lambda i: (0, 0)),
                  pl.BlockSpec((None, M, XA_HEADS * XA_DIM), lambda i: (i // per_b, 0, 0)),
                  pl.BlockSpec((None, M, XA_HEADS * XA_DIM), lambda i: (i // per_b, 0, 0))],
        out_specs=pl.BlockSpec((tm, D), lambda i: (i, 0)),
        out_shape=jax.ShapeDtypeStruct((N, D), F32),
        compiler_params=_params("parallel"), name="mem_attn")(
            x, gain.reshape(1, D).astype(F32), wq, wo, mem_k, mem_v)


def _ffn(x, gain, w13, w2, out_gain=None):
    N, D = x.shape
    FF = w2.shape[0]
    tm = min(ROW_TILE, N)
    chunk = 2 * LANES
    assert FF % chunk == 0

    final = out_gain is not None

    def kern(x_ref, g_ref, w13_ref, w2_ref, *rest):
        og_ref, o_ref = rest if final else (None, rest[0])
        xf = x_ref[...]
        y = xf * lax.rsqrt(jnp.mean(xf * xf, axis=-1, keepdims=True) + EPS)
        a = (y * g_ref[...]).astype(BF16)
        acc = xf
        for c in range(0, FF, chunk):
            g = jnp.dot(a, w13_ref[:, c:c + chunk], preferred_element_type=F32)
            u = jnp.dot(a, w13_ref[:, FF + c:FF + c + chunk], preferred_element_type=F32)
            hdn = (g * _sigmoid(g) * u).astype(BF16)
            acc = acc + jnp.dot(hdn, w2_ref[c:c + chunk, :], preferred_element_type=F32)
        if final:
            acc = acc * lax.rsqrt(jnp.mean(acc * acc, axis=-1, keepdims=True) + EPS) * og_ref[...]
        o_ref[...] = acc

    vec = pl.BlockSpec((1, D), lambda i: (0, 0))
    in_specs = [pl.BlockSpec((tm, D), lambda i: (i, 0)), vec,
                pl.BlockSpec(w13.shape, lambda i: (0, 0), pipeline_mode=pl.Buffered(1)),
                pl.BlockSpec(w2.shape, lambda i: (0, 0), pipeline_mode=pl.Buffered(1))]
    args = [x, gain.reshape(1, D).astype(F32), w13, w2]
    if final:
        in_specs.append(vec)
        args.append(out_gain.reshape(1, D).astype(F32))
    return pl.pallas_call(
        kern, grid=(N // tm,), in_specs=in_specs,
        out_specs=pl.BlockSpec((tm, D), lambda i: (i, 0)),
        out_shape=jax.ShapeDtypeStruct((N, D), F32),
        compiler_params=_params("parallel"), name="ffn")(*args)


def _pad_cols(w, width):
    return jnp.pad(w, ((0, 0), (0, width - w.shape[1])))


def _even_mixer(x, gain, w_in, b_f, lam, subln, w_out, layer_idx, ropes, B, S):
    D = x.shape[1]
    blk = DIFF_HEADS * 2 * DIFF_QK
    w = _pad_cols(w_in, 6 * blk + LANES).astype(BF16)
    sc = DIFF_QK ** -0.5 * LOG2E
    segs = [Seg(0, blk, BF16, "r64", sc), Seg(blk, blk, BF16, "r64"), Seg(2 * blk, blk, BF16),
            Seg(3 * blk, blk, BF16, None, sc), Seg(4 * blk, blk, BF16), Seg(5 * blk, blk, BF16),
            Seg(6 * blk, LANES, F32)]
    aq, ak, av, fq, fk, fv, fz = _linear([x], [w], segs, gain=gain, ropes=ropes, seq=S, name="even_in")
    lam_init = 0.8 - 0.6 * math.exp(-0.3 * layer_idx)
    oa = _flash(aq, ak, av.T, B=B, S=S, P=DIFF_HEADS // 2, q_w=2 * LANES, k_w=2 * LANES, v_rows=2 * LANES,
                heads=_DIFF_HEADS, kind="diff", out_dtype=BF16, lam=lam.astype(F32),
                subln=subln.astype(F32), lam_init=lam_init, name="diff_attn")
    terms = _forget_key_terms(fz, _pad_cols(b_f.reshape(1, -1), LANES).astype(F32), B, S)
    of = _flash(fq, fk, fv.T, B=B, S=S, P=FOX_HEADS // 4, q_w=2 * LANES, k_w=2 * LANES, v_rows=2 * LANES,
                heads=_PAIR_HEADS, kind="fox", out_dtype=BF16, ext=terms, name="fox_attn")
    wo = w_out.astype(BF16)
    (y,) = _linear([oa, of], [wo[:blk], wo[blk:]], [Seg(0, D, F32)], residual=x, name="even_out")
    return y


def _odd_in_weight(w_in):
    d = NSA_DIM
    o = np.cumsum((0, NSA_HEADS * d) + (NSA_GROUPS * d,) * 6 + (NSA_HEADS * 3, MLA_Q_RANK, MLA_KV_RANK, MLA_ROPE))
    nq, kc, vc, ks, vs, kw, vw, gz, cq, ckv, kr = [w_in[:, o[j]:o[j + 1]] for j in range(11)]

    def dup(wg):
        return jnp.concatenate([wg[:, :d], wg[:, :d], wg[:, d:], wg[:, d:]], axis=1)

    zeros = lambda n: jnp.zeros((w_in.shape[0], n), w_in.dtype)
    kr_slab = jnp.concatenate([zeros(MLA_NOPE), kr, zeros(LANES - MLA_NOPE - MLA_ROPE)], axis=1)
    cols = [nq, kc, vc, dup(ks), vs, dup(kw), vw, _pad_cols(gz, LANES), cq, ckv, kr_slab]
    return jnp.concatenate(cols, axis=1).astype(BF16)


def _odd_mixer(x, gain, w_in, cmp_pos, cmp_w1, cmp_w2, q_norm, kv_norm, w_uq, w_ukv, w_out, ropes, B, S):
    D = x.shape[1]
    G, d = NSA_GROUPS, NSA_DIM
    w = _odd_in_weight(w_in)
    sc = d ** -0.5 * LOG2E
    widths = [(NSA_HEADS * d, BF16, "r64", sc), (LANES, F32, "r64", 1.0), (LANES, F32, None, 1.0),
              (2 * LANES, BF16, "r64", 1.0), (LANES, BF16, None, 1.0),
              (2 * LANES, BF16, "r64", 1.0), (LANES, BF16, None, 1.0),
              (LANES, F32, None, 1.0), (MLA_Q_RANK, F32, None, 1.0), (MLA_KV_RANK, F32, None, 1.0),
              (LANES, F32, "mla", 1.0)]
    segs, start = [], 0
    for wd, dt, rp, s_ in widths:
        segs.append(Seg(start, wd, dt, rp, s_))
        start += wd
    q, kc, vc, ks, vs, kw, vw, gz, cq, ckv, kr = _linear(
        [x], [w], segs, gain=gain, ropes=ropes, seq=S, name="odd_in")

    R = S // CMP_STRIDE

    K = CMP_STRIDE * G * d
    half = CMP_LEN // 2
    pe = jnp.broadcast_to(cmp_pos.reshape(2, 2, half, 1, d), (2, 2, half, G, d)).reshape(2, 2, 1, K).astype(F32)
    eye = jnp.eye(G, dtype=cmp_w1.dtype)
    w1x = (cmp_w1.reshape(2, 2, 1, half, 1, d, CMP_HIDDEN) * eye[None, None, :, None, :, None, None])
    w1x = w1x.reshape(2, 2, G, K, CMP_HIDDEN).astype(BF16)
    w2d = jnp.concatenate([cmp_w2, cmp_w2], axis=-1).astype(BF16)
    cmp_k, cmp_vt = _nsa_compress(kc.reshape(B, R, K), vc.reshape(B, R, K), pe, w1x, w2d)
    cs = np.arange(R) * CMP_STRIDE
    bs = np.arange(LANES) * SLC_LEN
    ov = ((cs[:, None] < bs[None, :] + SLC_LEN) & (cs[:, None] + CMP_LEN > bs[None, :])
          & (np.arange(R)[:, None] < R - 1) & (bs[None, :] < S))
    o_cmp, sel = _nsa_cmp_select(q, cmp_k, cmp_vt, jnp.asarray(ov.T.astype(np.float32), BF16), B=B, S=S)
    onehot = (np.arange(S)[:, None] // SLC_LEN == np.arange(LANES)[None, :]).astype(np.float32)
    o_slc = _flash(q, ks, vs.T, B=B, S=S, P=G, q_w=2 * LANES, k_w=LANES, v_rows=HALF_LANES,
                   heads=_QUAD_HEADS, kind="slc", out_dtype=BF16, ext=jnp.asarray(onehot, BF16),
                   sel=sel, name="nsa_slc")
    o_win = _flash(q, kw, vw.T, B=B, S=S, P=G, q_w=2 * LANES, k_w=LANES, v_rows=HALF_LANES,
                   heads=_QUAD_HEADS, kind="win", out_dtype=BF16, tq=WIN, tk=WIN, n_bufs=2, name="nsa_win")
    ex = np.zeros((3, LANES, NSA_HEADS * d), np.float32)
    for hh in range(NSA_HEADS):
        for j in range(3):
            ex[j, hh * 3 + j, hh * d:(hh + 1) * d] = 1.0
    o_nsa = _nsa_combine(o_cmp, o_slc, o_win, gz, jnp.asarray(ex, BF16))

    H = MLA_HEADS
    qk = MLA_NOPE + MLA_ROPE
    wq_slab = jnp.pad(w_uq.reshape(MLA_Q_RANK, H, qk), ((0, 0), (0, 0), (0, LANES - qk)))
    wq_slab = wq_slab.reshape(MLA_Q_RANK, H * LANES).astype(BF16)
    (qm,) = _linear([cq], [wq_slab], [Seg(0, H * LANES, BF16, "mla", qk ** -0.5 * LOG2E)],
                    gain=q_norm, ropes=ropes, seq=S, name="mla_q")
    wkv = w_ukv.reshape(MLA_KV_RANK, H, MLA_NOPE + MLA_V)
    wk = jnp.pad(wkv[:, :, :MLA_NOPE], ((0, 0), (0, 0), (0, LANES - MLA_NOPE)))
    wk = wk.reshape(MLA_KV_RANK, H * LANES).astype(BF16)
    wv = wkv[:, :, MLA_NOPE:].reshape(MLA_KV_RANK, H * MLA_V).astype(BF16)
    km, vm = _mla_kv(ckv, kv_norm, wk, wv, kr)
    o_mla = _flash(qm, km, vm.T, B=B, S=S, P=H // 4, q_w=4 * LANES, k_w=4 * LANES, v_rows=2 * LANES,
                   heads=_WIDE_HEADS, kind="mla", out_dtype=BF16, name="mla_attn")
    wo = w_out.astype(BF16)
    half = NSA_HEADS * d
    (y,) = _linear([o_nsa, o_mla], [wo[:half], wo[half:]], [Seg(0, D, F32)], residual=x, name="odd_out")
    return y


def kernel(x, mem, mem_norm, norm_mix, norm_mem, norm_ffn, ev_w_in, ev_b_f, ev_lam, ev_subln, ev_w_out, od_w_in, nsa_cmp_pos, nsa_cmp_w1, nsa_cmp_w2, mla_q_norm, mla_kv_norm, mla_w_uq, mla_w_ukv, od_w_out, xa_wq, xa_wkv, xa_wo, ffn_w13, ffn_w2, final_norm):
    B, S, D = x.shape
    M = mem.shape[1]
    depth = norm_mix.shape[0]
    ropes = {
        "r64": (_rope_tables(S, NSA_DIM // 2, 0, LANES), NSA_DIM // 2),
        "mla": (_rope_tables(S, MLA_ROPE // 2, MLA_NOPE, MLA_NOPE + MLA_ROPE), MLA_ROPE // 2),
    }
    xa_w = XA_HEADS * XA_DIM
    h = x.reshape(B * S, D)
    mem2 = mem.reshape(B * M, D)
    for li in range(depth):
        j = li // 2
        if li % 2 == 0:
            h = _even_mixer(h, norm_mix[li], ev_w_in[j], ev_b_f[j], ev_lam[j], ev_subln[j], ev_w_out[j],
                            li, ropes, B, S)
        else:
            h = _odd_mixer(h, norm_mix[li], od_w_in[j], nsa_cmp_pos[j], nsa_cmp_w1[j], nsa_cmp_w2[j],
                           mla_q_norm[j], mla_kv_norm[j], mla_w_uq[j], mla_w_ukv[j], od_w_out[j],
                           ropes, B, S)
        mk, mv = _linear([mem2], [xa_wkv[li].astype(BF16)],
                         [Seg(0, xa_w, BF16), Seg(xa_w, xa_w, BF16)], gain=mem_norm, name="mem_kv")
        h = _mem_attn(h, norm_mem[li], xa_wq[li].astype(BF16), xa_wo[li].astype(BF16),
                      mk.reshape(B, M, xa_w), mv.reshape(B, M, xa_w), S=S)
        h = _ffn(h, norm_ffn[li], ffn_w13[li].astype(BF16), ffn_w2[li].astype(BF16),
                 out_gain=final_norm if li == depth - 1 else None)
    return h.reshape(B, S, D)
```

```python
import math
from typing import NamedTuple, Optional

import numpy as np
import jax
import jax.numpy as jnp
from jax import lax
from jax.experimental import pallas as pl
from jax.experimental.pallas import tpu as pltpu

F32 = jnp.float32
BF16 = jnp.bfloat16

LANES = 128
HALF_LANES = LANES // 2
BF16_ROWS = 16
ROPE_THETA = 10000.0
EPS = 1e-6
NEG = -1e30
LOG2E = math.log2(math.e)
MASK_BIG = 2.0 ** 100

DIFF_HEADS = 4
DIFF_QK = 64
FOX_HEADS = 8
NSA_HEADS = 8
NSA_GROUPS = 2
NSA_DIM = 64
CMP_LEN = 32
CMP_STRIDE = 16
CMP_HIDDEN = 128
SLC_LEN = 64
SLC_TOPK = 16
WIN = 512
FORCE_SCORE = 1e4
MLA_HEADS = 8
MLA_NOPE = 64
MLA_ROPE = 32
MLA_V = 64
MLA_Q_RANK = 384
MLA_KV_RANK = 256
XA_HEADS = 4
XA_DIM = 128

VMEM_LIMIT = 56 * 1024 * 1024
ROW_TILE = 512
COL_CHUNK = 512
ATT_TILE = 512
KEY_TILE = 512
ATT_BUFS = 3
MASK_BLOCK = 256
SEL_TILE = 512
CUM_CHUNK = 256

_NT = (((1,), (1,)), ((), ()))


def _params(*sem):
    return pltpu.CompilerParams(dimension_semantics=sem, vmem_limit_bytes=VMEM_LIMIT)


def _sigmoid(x):
    return 1.0 / (1.0 + jnp.exp(-x))


def _split_bf16(x, terms):
    out = []
    r = x
    for _ in range(terms):
        h = r.astype(BF16)
        out.append(h)
        r = r - h.astype(F32)
    return out


def _rope_tables(S, half, lane_lo, lane_hi):
    pos = jnp.arange(S, dtype=F32)
    inv = 1.0 / (ROPE_THETA ** (jnp.arange(half, dtype=F32) / half))
    ang = pos[:, None] * inv[None, :]
    cos, sin = jnp.cos(ang), jnp.sin(ang)
    lane = np.arange(LANES)
    active = (lane >= lane_lo) & (lane < lane_hi)
    j = (lane - lane_lo) % (2 * half)
    lower = active & (j < half)
    upper = active & (j >= half)
    idx = j % half
    cos_t = jnp.where(active[None, :], cos[:, idx], 1.0)
    sin_a = jnp.where(lower[None, :], -sin[:, idx], 0.0)
    sin_b = jnp.where(upper[None, :], sin[:, idx], 0.0)
    return cos_t, sin_a, sin_b


class Seg(NamedTuple):
    start: int
    width: int
    dtype: object
    rope: Optional[str] = None
    scale: float = 1.0


def _linear(xs, ws, segs, *, gain=None, residual=None, ropes=None, seq=None, name="linear"):
    N = xs[0].shape[0]
    tm = min(ROW_TILE, N)
    assert N % tm == 0
    n_in = len(xs)
    has_gain = gain is not None
    has_res = residual is not None
    rope_keys = sorted({s.rope for s in segs if s.rope})
    halves = {k: ropes[k][1] for k in rope_keys}

    def kern(*refs):
        it = iter(refs)
        x_refs = [next(it) for _ in range(n_in)]
        w_refs = [next(it) for _ in range(n_in)]
        g_ref = next(it) if has_gain else None
        r_ref = next(it) if has_res else None
        tabs = {k: (next(it), next(it), next(it)) for k in rope_keys}
        o_refs = [next(it) for _ in segs]
        acts = []
        for j, xr in enumerate(x_refs):
            x = xr[...]
            if j == 0 and has_gain:
                xf = x.astype(F32)
                y = xf * lax.rsqrt(jnp.mean(xf * xf, axis=-1, keepdims=True) + EPS)
                acts.append((y * g_ref[...]).astype(BF16))
            else:
                acts.append(x.astype(BF16))
        for seg, o_ref in zip(segs, o_refs):
            for c0 in range(0, seg.width, COL_CHUNK):
                cw = min(COL_CHUNK, seg.width - c0)
                col = seg.start + c0
                acc = None
                for a, wr in zip(acts, w_refs):
                    d = jnp.dot(a, wr[:, col:col + cw], preferred_element_type=F32)
                    acc = d if acc is None else acc + d
                if has_res:
                    acc = acc + r_ref[:, col:col + cw]
                if seg.scale != 1.0:
                    acc = acc * seg.scale
                if seg.rope is None:
                    o_ref[:, c0:c0 + cw] = acc.astype(o_ref.dtype)
                else:
                    cos_r, sa_r, sb_r = tabs[seg.rope]
                    half = halves[seg.rope]
                    cos, sa, sb = cos_r[...], sa_r[...], sb_r[...]
                    for s0 in range(0, cw, LANES):
                        xs_ = acc[:, s0:s0 + LANES]
                        y = (xs_ * cos + pltpu.roll(xs_, LANES - half, 1) * sa
                             + pltpu.roll(xs_, half, 1) * sb)
                        o_ref[:, c0 + s0:c0 + s0 + LANES] = y.astype(o_ref.dtype)

    in_specs, args = [], []
    for x in xs:
        in_specs.append(pl.BlockSpec((tm, x.shape[1]), lambda i: (i, 0)))
        args.append(x)
    for w in ws:
        in_specs.append(pl.BlockSpec(w.shape, lambda i: (0, 0)))
        args.append(w)
    if has_gain:
        in_specs.append(pl.BlockSpec((1, gain.shape[-1]), lambda i: (0, 0)))
        args.append(gain.reshape(1, -1).astype(F32))
    if has_res:
        in_specs.append(pl.BlockSpec((tm, residual.shape[1]), lambda i: (i, 0)))
        args.append(residual)
    for k in rope_keys:
        assert seq % tm == 0
        nt = seq // tm
        for t in ropes[k][0]:
            in_specs.append(pl.BlockSpec((tm, LANES), lambda i, nt=nt: (i % nt, 0)))
            args.append(t)
    out_shape = [jax.ShapeDtypeStruct((N, s.width), s.dtype) for s in segs]
    out_specs = [pl.BlockSpec((tm, s.width), lambda i: (i, 0)) for s in segs]
    return pl.pallas_call(
        kern, grid=(N // tm,), in_specs=in_specs, out_specs=out_specs, out_shape=out_shape,
        compiler_params=_params("parallel"), name=name)(*args)


def _forget_key_terms(fz, b_f, B, S):
    ch = CUM_CHUNK
    place = np.zeros((3, LANES, LANES), np.float32)
    for h in range(FOX_HEADS):
        for j in range(3):
            place[j, h, 3 * h + j] = 1.0

    def kern(z_ref, b_ref, pl_ref, o_ref):
        r = lax.broadcasted_iota(jnp.int32, (ch, ch), 0)
        c = lax.broadcasted_iota(jnp.int32, (ch, ch), 1)
        tri = jnp.where(c <= r, 1.0, 0.0).astype(BF16)

        def body(j, carries):
            out = []
            for b, carry in enumerate(carries):
                r0 = pl.multiple_of(b * S + j * ch, ch)
                z = z_ref[pl.ds(r0, ch), :] + b_ref[...]
                logf = -(jnp.maximum(-z, 0.0) + jnp.log1p(jnp.exp(-jnp.abs(z))))
                cs = carry
                for part in _split_bf16(logf, 3):
                    cs = cs + jnp.dot(tri, part, preferred_element_type=F32)
                terms = jnp.zeros((ch, LANES), F32)
                for jj, part in enumerate(_split_bf16(cs * (-LOG2E), 3)):
                    terms = terms + jnp.dot(part, pl_ref[jj], preferred_element_type=F32)
                o_ref[pl.ds(r0, ch), :] = terms.astype(o_ref.dtype)
                out.append(cs[ch - 1:ch, :])
            return tuple(out)

        lax.fori_loop(0, S // ch, body, tuple(jnp.zeros((1, LANES), F32) for _ in range(B)))

    return pl.pallas_call(
        kern, grid=(1,),
        in_specs=[pl.BlockSpec((B * S, LANES), lambda g: (0, 0), pipeline_mode=pl.Buffered(1)),
                  pl.BlockSpec((1, LANES), lambda g: (0, 0)),
                  pl.BlockSpec(place.shape, lambda g: (0, 0, 0))],
        out_specs=pl.BlockSpec((B * S, LANES), lambda g: (0, 0)),
        out_shape=jax.ShapeDtypeStruct((B * S, LANES), BF16),
        compiler_params=_params("arbitrary"), name="forget_terms")(fz, b_f, jnp.asarray(place, BF16))


class Head(NamedTuple):
    q_off: int
    q_half: Optional[str]
    k_off: int
    v0: int
    v1: int


def _flash(q, k, vt, *, B, S, P, q_w, k_w, v_rows, heads, kind, out_dtype, tq=None, tk=None, n_bufs=None,
           ext=None, sel=None, lam=None, subln=None, lam_init=None, name="flash"):
    TQ, TK = tq or ATT_TILE, tk or KEY_TILE
    assert S % TQ == 0 and TQ % TK == 0
    diag = TQ // TK
    nq = S // TQ
    nh = len(heads)
    rows = heads[0].v1 - heads[0].v0
    out_w = nh // 2 * rows if kind == "diff" else nh * rows
    width = TK
    U = n_bufs or ATT_BUFS
    if kind == "win":
        assert WIN == TK and TQ == TK and S >= 2 * TK and U == 2
        per_step, n_bufs = 1, U
    else:
        per_step, n_bufs = 2, U + 1
    assert nq % per_step == 0
    n_steps = nq // per_step
    has_ext = kind in ("fox", "slc")

    def kern(*refs):
        it = iter(refs)
        q_ref, k_ref, v_ref = next(it), next(it), next(it)
        ext_ref = next(it) if has_ext else None
        sel_ref = next(it) if kind == "slc" else None
        lam_ref, sub_ref = (next(it), next(it)) if kind == "diff" else (None, None)
        o_ref, m_ref, acc_ref, mx_ref = next(it), next(it), next(it), next(it)
        bufs = [next(it) for _ in range(n_bufs)]
        p_id = pl.program_id(1)
        step = pl.program_id(2)

        lane = lax.broadcasted_iota(jnp.int32, (TQ, LANES), 1)
        lo = lane < HALF_LANES

        def load_queries(r):
            tq_rows = slice(r * TQ, (r + 1) * TQ)
            q_all = q_ref[tq_rows, :]
            if kind == "slc":
                drop = ((sel_ref[tq_rows, :].astype(F32) - 1.0) * MASK_BIG).astype(BF16)
            qs = []
            for j, h in enumerate(heads):
                qh = q_all[:, h.q_off:h.q_off + LANES]
                if h.q_half == "lo":
                    qh = jnp.where(lo, qh, jnp.zeros_like(qh))
                elif h.q_half == "hi":
                    qh = jnp.where(lo, jnp.zeros_like(qh), qh)
                if kind == "fox":
                    first = 3 * (nh * p_id + j)
                    pick = jnp.where((lane >= first) & (lane < first + 3), 1.0, 0.0).astype(BF16)
                    qh = jnp.concatenate([qh, pick], axis=1)
                if kind == "slc":
                    qh = jnp.concatenate([qh, drop], axis=1)
                qs.append(qh)
            return qs

        def reset():
            m_ref[...] = jnp.full(m_ref.shape, NEG, F32)
            acc_ref[...] = jnp.zeros(acc_ref.shape, F32)

        def block_kind(ahead, kb, qb):
            lo_rel = (kb - qb) * MASK_BLOCK - (MASK_BLOCK - 1)
            hi_rel = (kb - qb) * MASK_BLOCK + (MASK_BLOCK - 1)
            floor = ahead - WIN if kind == "win" else lo_rel - 1
            if lo_rel > ahead or hi_rel <= floor:
                return "dead"
            if hi_rel <= ahead and lo_rel > floor:
                return "full"
            return "part"

        def block_mask(ahead, kb, qb):
            rel = (lax.broadcasted_iota(jnp.int32, (MASK_BLOCK, MASK_BLOCK), 0)
                   - lax.broadcasted_iota(jnp.int32, (MASK_BLOCK, MASK_BLOCK), 1)
                   + (kb - qb) * MASK_BLOCK)
            msk = rel <= ahead
            if kind == "win":
                msk = msk & (rel > ahead - WIN)
            return msk

        kbs, qbs = range(width // MASK_BLOCK), range(TQ // MASK_BLOCK)
        blk = lambda b: slice(b * MASK_BLOCK, (b + 1) * MASK_BLOCK)

        def scores(qs, k0, u, ahead=None):
            kx = k_ref[pl.ds(k0, width), :]
            ex = ext_ref[pl.ds(k0, width), :] if has_ext else None
            for hi, h in enumerate(heads):
                kk = kx[:, h.k_off:h.k_off + LANES]
                if ex is not None:
                    kk = jnp.concatenate([kk, ex], axis=1)
                if ahead is None:
                    st = lax.dot_general(kk, qs[hi], _NT, preferred_element_type=F32)
                    bufs[u][hi] = st
                    mx_ref[u * nh + hi] = jnp.max(st, axis=0, keepdims=True)
                    continue
                col_max = [None] * len(qbs)
                for kb in kbs:
                    live = [qb for qb in qbs if block_kind(ahead, kb, qb) != "dead"]
                    if not live:
                        continue
                    q_rows = slice(live[0] * MASK_BLOCK, (live[-1] + 1) * MASK_BLOCK)
                    st = lax.dot_general(kk[blk(kb)], qs[hi][q_rows], _NT, preferred_element_type=F32)
                    for qb in range(live[0], live[-1] + 1):
                        sb = st[:, blk(qb - live[0])]
                        if block_kind(ahead, kb, qb) == "part":
                            sb = jnp.where(block_mask(ahead, kb, qb), sb, NEG)
                        bufs[u][hi, blk(kb), blk(qb)] = sb
                        cm = jnp.max(sb, axis=0, keepdims=True)
                        col_max[qb] = cm if col_max[qb] is None else jnp.maximum(col_max[qb], cm)
                mx_ref[u * nh + hi] = jnp.concatenate(col_max, axis=1)

        def update(k0, u, ahead=None):
            for hi, h in enumerate(heads):
                m_prev = m_ref[hi]
                m_new = jnp.maximum(m_prev, mx_ref[u * nh + hi])
                alpha = jnp.exp2(m_prev - m_new)
                if ahead is None:
                    ones = jnp.ones((BF16_ROWS, width), BF16)
                    pt = jnp.exp2(bufs[u][hi] - m_new).astype(BF16)
                    lhs = jnp.concatenate([v_ref[h.v0:h.v1, pl.ds(k0, width)], ones], axis=0)
                    new = jnp.dot(lhs, pt, preferred_element_type=F32)
                else:
                    ones = jnp.ones((BF16_ROWS, MASK_BLOCK), BF16)
                    cols = []
                    for qb in qbs:
                        col = None
                        for kb in kbs:
                            if block_kind(ahead, kb, qb) == "dead":
                                continue
                            pt = jnp.exp2(bufs[u][hi, blk(kb), blk(qb)] - m_new[:, blk(qb)]).astype(BF16)
                            kstart = pl.multiple_of(k0 + kb * MASK_BLOCK, MASK_BLOCK)
                            lhs = jnp.concatenate(
                                [v_ref[h.v0:h.v1, pl.ds(kstart, MASK_BLOCK)], ones], axis=0)
                            part = jnp.dot(lhs, pt, preferred_element_type=F32)
                            col = part if col is None else col + part
                        cols.append(col)
                    new = jnp.concatenate(cols, axis=1)
                acc_ref[hi] = alpha * acc_ref[hi] + new
                m_ref[hi] = m_new

        def finish(r):
            out_rows = slice(r * TQ, (r + 1) * TQ)
            outs = []
            for hi in range(nh):
                acc = acc_ref[hi]
                outs.append(acc[:rows] / acc[rows:rows + 1])
            if kind == "diff":
                lm = lam_ref[...]
                la = jnp.sum(lm[0:1] * lm[1:2], axis=1, keepdims=True)
                lb = jnp.sum(lm[2:3] * lm[3:4], axis=1, keepdims=True)
                lam_full = jnp.exp(la) - jnp.exp(lb) + lam_init
                for g in range(nh // 2):
                    oa = jnp.transpose(outs[2 * g] - lam_full * outs[2 * g + 1])
                    y = oa * lax.rsqrt(jnp.mean(oa * oa, axis=-1, keepdims=True) + EPS)
                    y = (y * sub_ref[...]) * (1.0 - lam_init)
                    o_ref[out_rows, g * rows:(g + 1) * rows] = y.astype(o_ref.dtype)
            else:
                o_ref[out_rows, :] = jnp.transpose(jnp.concatenate(outs, axis=0)).astype(o_ref.dtype)

        tile = lambda t: pl.multiple_of(t * TK, TK)

        def stream(qs, n_full, first, after_first_scores=None, before_last_update=None):
            tl = lambda t: tile(first + t)
            n_rounds = jnp.maximum(n_full - 1, 0) // U

            def prologue(ahead):
                scores(qs, tl(0), 0, ahead)
                if after_first_scores is not None:
                    after_first_scores()
            pl.when(n_full == 0)(lambda: prologue(0))
            pl.when(n_full > 0)(lambda: prologue(None))

            def body(j, c):
                for u in range(U):
                    scores(qs, tl(U * j + u + 1), (u + 1) % U)
                    update(tl(U * j + u), u)
                return c
            lax.fori_loop(0, n_rounds, body, 0)
            t0 = U * n_rounds

            def tail(left):
                ahead = lambda u: -(u - left) * TK if u >= left else None
                last = left + diag - 1
                for u in range(last):
                    scores(qs, tl(t0 + u + 1), (u + 1) % U, ahead(u + 1))
                    update(tl(t0 + u), u % U, ahead(u))
                if before_last_update is not None:
                    before_last_update()
                update(tl(t0 + last), last % U, ahead(last))

            for left in range(U + 1):
                pl.when(n_full - t0 == left)(lambda left=left: tail(left))

        if kind == "win":
            qs = load_queries(0)
            reset()

            @pl.when(step == 0)
            def _():
                scores(qs, tile(0), 0, 0)
                update(tile(0), 0, 0)

            @pl.when(step > 0)
            def _():
                scores(qs, tile(step - 1), 0, WIN)
                scores(qs, tile(step), 1, 0)
                update(tile(step - 1), 0, WIN)
                update(tile(step), 1, 0)
            finish(0)
        else:
            assert per_step == 2 and diag == 1
            qs_a, qs_b = load_queries(0), load_queries(1)
            i_a = per_step * step
            reset()
            stream(qs_a, i_a, 0, before_last_update=lambda: scores(qs_b, tile(0), U))
            finish(0)
            reset()
            stream(qs_b, i_a, 1, after_first_scores=lambda: update(tile(0), U))
            finish(1)

    in_specs = [
        pl.BlockSpec((per_step * TQ, q_w), lambda b, p, i: (b * n_steps + i, p)),
        pl.BlockSpec((S, k_w), lambda b, p, i: (b, p)),
        pl.BlockSpec((v_rows, S), lambda b, p, i: (p, b)),
    ]
    args = [q, k, vt]
    if kind == "fox":
        in_specs.append(pl.BlockSpec((S, LANES), lambda b, p, i: (b, 0)))
        args.append(ext)
    if kind == "slc":
        in_specs.append(pl.BlockSpec((S, LANES), lambda b, p, i: (0, 0)))
        in_specs.append(pl.BlockSpec((per_step * TQ, LANES), lambda b, p, i: (b * n_steps + i, p)))
        args += [ext, sel]
    if kind == "diff":
        in_specs.append(pl.BlockSpec(lam.shape, lambda b, p, i: (0, 0)))
        in_specs.append(pl.BlockSpec((1, LANES), lambda b, p, i: (0, 0)))
        args += [lam, subln.reshape(1, LANES)]
    return pl.pallas_call(
        kern, grid=(B, P, n_steps), in_specs=in_specs,
        out_specs=pl.BlockSpec((per_step * TQ, out_w), lambda b, p, i: (b * n_steps + i, p)),
        out_shape=jax.ShapeDtypeStruct((B * S, P * out_w), out_dtype),
        scratch_shapes=[pltpu.VMEM((nh, 1, TQ), F32),
                        pltpu.VMEM((nh, rows + BF16_ROWS, TQ), F32),
                        pltpu.VMEM((n_bufs * nh, 1, TQ), F32)]
        + [pltpu.VMEM((nh, width, TQ), F32)] * n_bufs,
        compiler_params=_params("parallel", "parallel", "arbitrary"), name=name)(*args)


_DIFF_HEADS = tuple(Head(s * LANES, half, s * LANES, s * LANES, (s + 1) * LANES)
                    for s in range(2) for half in ("lo", "hi"))
_PAIR_HEADS = tuple(Head(s * LANES, half, s * LANES, (2 * s + j) * HALF_LANES, (2 * s + j + 1) * HALF_LANES)
                    for s in range(2) for j, half in enumerate(("lo", "hi")))
_QUAD_HEADS = tuple(Head(s * LANES, half, 0, 0, HALF_LANES) for s in range(2) for half in ("lo", "hi"))
_WIDE_HEADS = tuple(Head(j * LANES, None, j * LANES, j * HALF_LANES, (j + 1) * HALF_LANES) for j in range(4))


def _nsa_compress(xk, xv, pe, w1x, w2d):
    B, R, K = xk.shape
    G = NSA_GROUPS

    def kern(xk_ref, xv_ref, pe_ref, w1_ref, w2_ref, k_ref, vt_ref):
        for t, x_ref in enumerate((xk_ref, xv_ref)):
            x = x_ref[...]
            xa = (x + pe_ref[t, 0]).astype(BF16)
            xb = (x + pe_ref[t, 1]).astype(BF16)
            for g in range(G):
                a = jnp.dot(xa, w1_ref[t, 0, g], preferred_element_type=F32)
                b = jnp.dot(xb, w1_ref[t, 1, g], preferred_element_type=F32)
                h = a + pltpu.roll(b, R - 1, 0)
                hs = h * _sigmoid(h)
                o = jnp.dot(hs.astype(BF16), w2_ref[t], preferred_element_type=F32)
                if t == 0:
                    k_ref[g] = o
                else:
                    vt_ref[g] = jnp.transpose(o)

    return pl.pallas_call(
        kern, grid=(B,),
        in_specs=[pl.BlockSpec((None, R, K), lambda b: (b, 0, 0)),
                  pl.BlockSpec((None, R, K), lambda b: (b, 0, 0)),
                  pl.BlockSpec(pe.shape, lambda b: (0, 0, 0, 0)),
                  pl.BlockSpec(w1x.shape, lambda b: (0, 0, 0, 0, 0)),
                  pl.BlockSpec(w2d.shape, lambda b: (0, 0, 0))],
        out_specs=[pl.BlockSpec((None, G, R, LANES), lambda b: (b, 0, 0, 0)),
                   pl.BlockSpec((None, G, LANES, R), lambda b: (b, 0, 0, 0))],
        out_shape=[jax.ShapeDtypeStruct((B, G, R, LANES), F32),
                   jax.ShapeDtypeStruct((B, G, LANES, R), F32)],
        compiler_params=_params("parallel"), name="nsa_compress")(xk, xv, pe, w1x, w2d)


def _nsa_cmp_select(q, cmp_k, cmp_vt, ov_t, *, B, S):
    T = SEL_TILE
    nq = S // T
    R = cmp_k.shape[2]
    G = NSA_GROUPS
    d = NSA_DIM
    n_sel = min(SLC_TOPK, S // SLC_LEN)
    assert n_sel >= 3 and FORCE_SCORE > NSA_HEADS // G

    bucket = min(LANES, R)
    assert R % bucket == 0 and bucket % (SLC_LEN // CMP_STRIDE) == 0

    def kern(q_ref, kc_ref, vt_ref, ov_ref, o_ref, sel_ref):
        i = pl.program_id(2)
        lane = lax.broadcasted_iota(jnp.int32, (T, LANES), 1)
        lo = lane < HALF_LANES
        q_all = q_ref[...]
        qs = []
        for hh in range(4):
            qh = q_all[:, (hh // 2) * LANES:(hh // 2 + 1) * LANES]
            qs.append(jnp.where(lo, qh, jnp.zeros_like(qh)) if hh % 2 == 0
                      else jnp.where(lo, jnp.zeros_like(qh), qh))

        def variant(rv):
            nb = rv * CMP_STRIDE // SLC_LEN
            tq = lax.broadcasted_iota(jnp.int32, (rv, T), 1) + i * T
            cend = lax.broadcasted_iota(jnp.int32, (rv, T), 0) * CMP_STRIDE + (CMP_LEN - 1)
            vis = cend <= tq
            kc = kc_ref[:rv, :].astype(BF16)
            vt = vt_ref[:d, :rv].astype(BF16)
            psum = jnp.zeros((rv, T), F32)
            o_heads = []
            for hh in range(4):
                st = lax.dot_general(kc, qs[hh], _NT, preferred_element_type=F32)
                st = jnp.where(vis, st, -jnp.inf)
                m = jnp.max(st, axis=0, keepdims=True)
                e = jnp.exp2(st - jnp.where(m > -jnp.inf, m, 0.0))
                p = e / jnp.maximum(jnp.sum(e, axis=0, keepdims=True), 1e-30)
                o_heads.append(jnp.dot(vt, p.astype(BF16), preferred_element_type=F32))
                psum = psum + p
            o_ref[...] = jnp.transpose(jnp.concatenate(o_heads, axis=0)).astype(o_ref.dtype)
            imp = jnp.zeros((nb, T), F32)
            for part in _split_bf16(psum, 2):
                imp = imp + jnp.dot(ov_ref[:nb, :rv], part, preferred_element_type=F32)
            blk = lax.broadcasted_iota(jnp.int32, (nb, T), 0)
            cur = (lax.broadcasted_iota(jnp.int32, (nb, T), 1) + i * T) // SLC_LEN
            forced = (blk == 0) | (blk == cur) | (blk == cur - 1)
            work = jnp.where(forced, -jnp.inf, imp)
            work = jnp.where(blk > cur, NEG, work)
            blk_f = blk.astype(F32)
            for _ in range(n_sel - 3):
                mx = jnp.max(work, axis=0, keepdims=True)
                first = jnp.min(jnp.where(work == mx, blk_f, float(LANES)), axis=0, keepdims=True)
                work = jnp.where(blk_f == first, -jnp.inf, work)
            sel = jnp.where(work == -jnp.inf, 1.0, 0.0)
            if nb < LANES:
                sel = jnp.concatenate([sel, jnp.zeros((LANES - nb, T), F32)], axis=0)
            sel_ref[...] = jnp.transpose(sel).astype(sel_ref.dtype)

        need = (i + 1) * (T // CMP_STRIDE)
        which = (need - 1) // bucket
        for bk in range(R // bucket):
            pl.when(which == bk)(lambda bk=bk: variant((bk + 1) * bucket))

    return pl.pallas_call(
        kern, grid=(B, G, nq),
        in_specs=[pl.BlockSpec((T, 2 * LANES), lambda b, g, i: (b * nq + i, g)),
                  pl.BlockSpec((None, None, R, LANES), lambda b, g, i: (b, g, 0, 0)),
                  pl.BlockSpec((None, None, LANES, R), lambda b, g, i: (b, g, 0, 0)),
                  pl.BlockSpec((LANES, R), lambda b, g, i: (0, 0))],
        out_specs=[pl.BlockSpec((T, 2 * LANES), lambda b, g, i: (b * nq + i, g)),
                   pl.BlockSpec((T, LANES), lambda b, g, i: (b * nq + i, g))],
        out_shape=[jax.ShapeDtypeStruct((B * S, G * 2 * LANES), BF16),
                   jax.ShapeDtypeStruct((B * S, G * LANES), BF16)],
        compiler_params=_params("parallel", "parallel", "parallel"), name="nsa_cmp_select")(
            q, cmp_k, cmp_vt, ov_t)


def _nsa_combine(o_cmp, o_slc, o_win, gz, expand):
    N, C = o_cmp.shape
    tm = min(ROW_TILE, N)

    def kern(c_ref, s_ref, w_ref, g_ref, e_ref, o_ref):
        gate = _sigmoid(g_ref[...])
        parts = _split_bf16(gate, 2)
        acc = jnp.zeros((tm, C), F32)
        for j, br in enumerate((c_ref, s_ref, w_ref)):
            gj = sum(jnp.dot(part, e_ref[j], preferred_element_type=F32) for part in parts)
            acc = acc + gj * br[...].astype(F32)
        o_ref[...] = acc.astype(o_ref.dtype)

    row = pl.BlockSpec((tm, C), lambda i: (i, 0))
    return pl.pallas_call(
        kern, grid=(N // tm,),
        in_specs=[row, row, row, pl.BlockSpec((tm, LANES), lambda i: (i, 0)),
                  pl.BlockSpec(expand.shape, lambda i: (0, 0, 0))],
        out_specs=row, out_shape=jax.ShapeDtypeStruct((N, C), BF16),
        compiler_params=_params("parallel"), name="nsa_combine")(o_cmp, o_slc, o_win, gz, expand)


def _mla_kv(ckv, gain, wk, wv, kr):
    N, K = ckv.shape
    tm = min(ROW_TILE, N)
    H = MLA_HEADS

    def kern(c_ref, g_ref, wk_ref, wv_ref, kr_ref, k_ref, v_ref):
        xf = c_ref[...]
        y = xf * lax.rsqrt(jnp.mean(xf * xf, axis=-1, keepdims=True) + EPS)
        a = (y * g_ref[...]).astype(BF16)
        kr_ = kr_ref[...]
        for h in range(H):
            kh = jnp.dot(a, wk_ref[:, h * LANES:(h + 1) * LANES], preferred_element_type=F32)
            k_ref[:, h * LANES:(h + 1) * LANES] = (kh + kr_).astype(k_ref.dtype)
        v_ref[...] = jnp.dot(a, wv_ref[...], preferred_element_type=F32).astype(v_ref.dtype)

    return pl.pallas_call(
        kern, grid=(N // tm,),
        in_specs=[pl.BlockSpec((tm, K), lambda i: (i, 0)),
                  pl.BlockSpec((1, K), lambda i: (0, 0)),
                  pl.BlockSpec(wk.shape, lambda i: (0, 0)),
                  pl.BlockSpec(wv.shape, lambda i: (0, 0)),
                  pl.BlockSpec((tm, LANES), lambda i: (i, 0))],
        out_specs=[pl.BlockSpec((tm, H * LANES), lambda i: (i, 0)),
                   pl.BlockSpec((tm, H * MLA_V), lambda i: (i, 0))],
        out_shape=[jax.ShapeDtypeStruct((N, H * LANES), BF16),
                   jax.ShapeDtypeStruct((N, H * MLA_V), BF16)],
        compiler_params=_params("parallel"), name="mla_kv")(
            ckv, gain.reshape(1, K).astype(F32), wk, wv, kr)


def _mem_attn(x, gain, wq, wo, mem_k, mem_v, *, S):
    N, D = x.shape
    tm = min(ROW_TILE, S)
    M = mem_k.shape[1]
    per_b = S // tm
    scale = XA_DIM ** -0.5

    def kern(x_ref, g_ref, wq_ref, wo_ref, k_ref, v_ref, o_ref):
        xf = x_ref[...]
        y = xf * lax.rsqrt(jnp.mean(xf * xf, axis=-1, keepdims=True) + EPS)
        a = (y * g_ref[...]).astype(BF16)
        q = (jnp.dot(a, wq_ref[...], preferred_element_type=F32) * scale).astype(BF16)
        heads = []
        for h in range(XA_HEADS):
            sl = slice(h * XA_DIM, (h + 1) * XA_DIM)
            s = lax.dot_general(q[:, sl], k_ref[:, sl], _NT, preferred_element_type=F32)
            e = jnp.exp(s - jnp.max(s, axis=1, keepdims=True))
            p = e / jnp.sum(e, axis=1, keepdims=True)
            heads.append(jnp.dot(p.astype(BF16), v_ref[:, sl], preferred_element_type=F32).astype(BF16))
        o = jnp.concatenate(heads, axis=1)
        o_ref[...] = xf + jnp.dot(o, wo_ref[...], preferred_element_type=F32)

    return pl.pallas_call(
        kern, grid=(N // tm,),
        in_specs=[pl.BlockSpec((tm, D), lambda i: (i, 0)),
                  pl.BlockSpec((1, D), lambda i: (0, 0)),
                  pl.BlockSpec(wq.shape, lambda i: (0, 0)),
                  pl.BlockSpec(wo.shape, lambda i: (0, 0)),
                  pl.BlockSpec((None, M, XA_HEADS * XA_DIM), lambda i: (i // per_b, 0, 0)),
                  pl.BlockSpec((None, M, XA_HEADS * XA_DIM), lambda i: (i // per_b, 0, 0))],
        out_specs=pl.BlockSpec((tm, D), lambda i: (i, 0)),
        out_shape=jax.ShapeDtypeStruct((N, D), F32),
        compiler_params=_params("parallel"), name="mem_attn")(
            x, gain.reshape(1, D).astype(F32), wq, wo, mem_k, mem_v)


def _ffn(x, gain, w13, w2, out_gain=None):
    N, D = x.shape
    FF = w2.shape[0]
    tm = min(ROW_TILE, N)
    chunk = 2 * LANES
    assert FF % chunk == 0

    final = out_gain is not None

    def kern(x_ref, g_ref, w13_ref, w2_ref, *rest):
        og_ref, o_ref = rest if final else (None, rest[0])
        xf = x_ref[...]
        y = xf * lax.rsqrt(jnp.mean(xf * xf, axis=-1, keepdims=True) + EPS)
        a = (y * g_ref[...]).astype(BF16)
        acc = xf
        for c in range(0, FF, chunk):
            g = jnp.dot(a, w13_ref[:, c:c + chunk], preferred_element_type=F32)
            u = jnp.dot(a, w13_ref[:, FF + c:FF + c + chunk], preferred_element_type=F32)
            hdn = (g * _sigmoid(g) * u).astype(BF16)
            acc = acc + jnp.dot(hdn, w2_ref[c:c + chunk, :], preferred_element_type=F32)
        if final:
            acc = acc * lax.rsqrt(jnp.mean(acc * acc, axis=-1, keepdims=True) + EPS) * og_ref[...]
        o_ref[...] = acc

    vec = pl.BlockSpec((1, D), lambda i: (0, 0))
    in_specs = [pl.BlockSpec((tm, D), lambda i: (i, 0)), vec,
                pl.BlockSpec(w13.shape, lambda i: (0, 0), pipeline_mode=pl.Buffered(1)),
                pl.BlockSpec(w2.shape, lambda i: (0, 0), pipeline_mode=pl.Buffered(1))]
    args = [x, gain.reshape(1, D).astype(F32), w13, w2]
    if final:
        in_specs.append(vec)
        args.append(out_gain.reshape(1, D).astype(F32))
    return pl.pallas_call(
        kern, grid=(N // tm,), in_specs=in_specs,
        out_specs=pl.BlockSpec((tm, D), lambda i: (i, 0)),
        out_shape=jax.ShapeDtypeStruct((N, D), F32),
        compiler_params=_params("parallel"), name="ffn")(*args)


def _pad_cols(w, width):
    return jnp.pad(w, ((0, 0), (0, width - w.shape[1])))


def _even_mixer(x, gain, w_in, b_f, lam, subln, w_out, layer_idx, ropes, B, S):
    D = x.shape[1]
    blk = DIFF_HEADS * 2 * DIFF_QK
    w = _pad_cols(w_in, 6 * blk + LANES).astype(BF16)
    sc = DIFF_QK ** -0.5 * LOG2E
    segs = [Seg(0, blk, BF16, "r64", sc), Seg(blk, blk, BF16, "r64"), Seg(2 * blk, blk, BF16),
            Seg(3 * blk, blk, BF16, None, sc), Seg(4 * blk, blk, BF16), Seg(5 * blk, blk, BF16),
            Seg(6 * blk, LANES, F32)]
    aq, ak, av, fq, fk, fv, fz = _linear([x], [w], segs, gain=gain, ropes=ropes, seq=S, name="even_in")
    lam_init = 0.8 - 0.6 * math.exp(-0.3 * layer_idx)
    oa = _flash(aq, ak, av.T, B=B, S=S, P=DIFF_HEADS // 2, q_w=2 * LANES, k_w=2 * LANES, v_rows=2 * LANES,
                heads=_DIFF_HEADS, kind="diff", out_dtype=BF16, lam=lam.astype(F32),
                subln=subln.astype(F32), lam_init=lam_init, name="diff_attn")
    terms = _forget_key_terms(fz, _pad_cols(b_f.reshape(1, -1), LANES).astype(F32), B, S)
    of = _flash(fq, fk, fv.T, B=B, S=S, P=FOX_HEADS // 4, q_w=2 * LANES, k_w=2 * LANES, v_rows=2 * LANES,
                heads=_PAIR_HEADS, kind="fox", out_dtype=BF16, ext=terms, name="fox_attn")
    wo = w_out.astype(BF16)
    (y,) = _linear([oa, of], [wo[:blk], wo[blk:]], [Seg(0, D, F32)], residual=x, name="even_out")
    return y


def _odd_in_weight(w_in):
    d = NSA_DIM
    o = np.cumsum((0, NSA_HEADS * d) + (NSA_GROUPS * d,) * 6 + (NSA_HEADS * 3, MLA_Q_RANK, MLA_KV_RANK, MLA_ROPE))
    nq, kc, vc, ks, vs, kw, vw, gz, cq, ckv, kr = [w_in[:, o[j]:o[j + 1]] for j in range(11)]

    def dup(wg):
        return jnp.concatenate([wg[:, :d], wg[:, :d], wg[:, d:], wg[:, d:]], axis=1)

    zeros = lambda n: jnp.zeros((w_in.shape[0], n), w_in.dtype)
    kr_slab = jnp.concatenate([zeros(MLA_NOPE), kr, zeros(LANES - MLA_NOPE - MLA_ROPE)], axis=1)
    cols = [nq, kc, vc, dup(ks), vs, dup(kw), vw, _pad_cols(gz, LANES), cq, ckv, kr_slab]
    return jnp.concatenate(cols, axis=1).astype(BF16)


def _odd_mixer(x, gain, w_in, cmp_pos, cmp_w1, cmp_w2, q_norm, kv_norm, w_uq, w_ukv, w_out, ropes, B, S):
    D = x.shape[1]
    G, d = NSA_GROUPS, NSA_DIM
    w = _odd_in_weight(w_in)
    sc = d ** -0.5 * LOG2E
    widths = [(NSA_HEADS * d, BF16, "r64", sc), (LANES, F32, "r64", 1.0), (LANES, F32, None, 1.0),
              (2 * LANES, BF16, "r64", 1.0), (LANES, BF16, None, 1.0),
              (2 * LANES, BF16, "r64", 1.0), (LANES, BF16, None, 1.0),
              (LANES, F32, None, 1.0), (MLA_Q_RANK, F32, None, 1.0), (MLA_KV_RANK, F32, None, 1.0),
              (LANES, F32, "mla", 1.0)]
    segs, start = [], 0
    for wd, dt, rp, s_ in widths:
        segs.append(Seg(start, wd, dt, rp, s_))
        start += wd
    q, kc, vc, ks, vs, kw, vw, gz, cq, ckv, kr = _linear(
        [x], [w], segs, gain=gain, ropes=ropes, seq=S, name="odd_in")

    R = S // CMP_STRIDE

    K = CMP_STRIDE * G * d
    half = CMP_LEN // 2
    pe = jnp.broadcast_to(cmp_pos.reshape(2, 2, half, 1, d), (2, 2, half, G, d)).reshape(2, 2, 1, K).astype(F32)
    eye = jnp.eye(G, dtype=cmp_w1.dtype)
    w1x = (cmp_w1.reshape(2, 2, 1, half, 1, d, CMP_HIDDEN) * eye[None, None, :, None, :, None, None])
    w1x = w1x.reshape(2, 2, G, K, CMP_HIDDEN).astype(BF16)
    w2d = jnp.concatenate([cmp_w2, cmp_w2], axis=-1).astype(BF16)
    cmp_k, cmp_vt = _nsa_compress(kc.reshape(B, R, K), vc.reshape(B, R, K), pe, w1x, w2d)
    cs = np.arange(R) * CMP_STRIDE
    bs = np.arange(LANES) * SLC_LEN
    ov = ((cs[:, None] < bs[None, :] + SLC_LEN) & (cs[:, None] + CMP_LEN > bs[None, :])
          & (np.arange(R)[:, None] < R - 1) & (bs[None, :] < S))
    o_cmp, sel = _nsa_cmp_select(q, cmp_k, cmp_vt, jnp.asarray(ov.T.astype(np.float32), BF16), B=B, S=S)
    onehot = (np.arange(S)[:, None] // SLC_LEN == np.arange(LANES)[None, :]).astype(np.float32)
    o_slc = _flash(q, ks, vs.T, B=B, S=S, P=G, q_w=2 * LANES, k_w=LANES, v_rows=HALF_LANES,
                   heads=_QUAD_HEADS, kind="slc", out_dtype=BF16, ext=jnp.asarray(onehot, BF16),
                   sel=sel, name="nsa_slc")
    o_win = _flash(q, kw, vw.T, B=B, S=S, P=G, q_w=2 * LANES, k_w=LANES, v_rows=HALF_LANES,
                   heads=_QUAD_HEADS, kind="win", out_dtype=BF16, tq=WIN, tk=WIN, n_bufs=2, name="nsa_win")
    ex = np.zeros((3, LANES, NSA_HEADS * d), np.float32)
    for hh in range(NSA_HEADS):
        for j in range(3):
            ex[j, hh * 3 + j, hh * d:(hh + 1) * d] = 1.0
    o_nsa = _nsa_combine(o_cmp, o_slc, o_win, gz, jnp.asarray(ex, BF16))

    H = MLA_HEADS
    qk = MLA_NOPE + MLA_ROPE
    wq_slab = jnp.pad(w_uq.reshape(MLA_Q_RANK, H, qk), ((0, 0), (0, 0), (0, LANES - qk)))
    wq_slab = wq_slab.reshape(MLA_Q_RANK, H * LANES).astype(BF16)
    (qm,) = _linear([cq], [wq_slab], [Seg(0, H * LANES, BF16, "mla", qk ** -0.5 * LOG2E)],
                    gain=q_norm, ropes=ropes, seq=S, name="mla_q")
    wkv = w_ukv.reshape(MLA_KV_RANK, H, MLA_NOPE + MLA_V)
    wk = jnp.pad(wkv[:, :, :MLA_NOPE], ((0, 0), (0, 0), (0, LANES - MLA_NOPE)))
    wk = wk.reshape(MLA_KV_RANK, H * LANES).astype(BF16)
    wv = wkv[:, :, MLA_NOPE:].reshape(MLA_KV_RANK, H * MLA_V).astype(BF16)
    km, vm = _mla_kv(ckv, kv_norm, wk, wv, kr)
    o_mla = _flash(qm, km, vm.T, B=B, S=S, P=H // 4, q_w=4 * LANES, k_w=4 * LANES, v_rows=2 * LANES,
                   heads=_WIDE_HEADS, kind="mla", out_dtype=BF16, name="mla_attn")
    wo = w_out.astype(BF16)
    half = NSA_HEADS * d
    (y,) = _linear([o_nsa, o_mla], [wo[:half], wo[half:]], [Seg(0, D, F32)], residual=x, name="odd_out")
    return y


def kernel(x, mem, mem_norm, norm_mix, norm_mem, norm_ffn, ev_w_in, ev_b_f, ev_lam, ev_subln, ev_w_out, od_w_in, nsa_cmp_pos, nsa_cmp_w1, nsa_cmp_w2, mla_q_norm, mla_kv_norm, mla_w_uq, mla_w_ukv, od_w_out, xa_wq, xa_wkv, xa_wo, ffn_w13, ffn_w2, final_norm):
    B, S, D = x.shape
    M = mem.shape[1]
    depth = norm_mix.shape[0]
    ropes = {
        "r64": (_rope_tables(S, NSA_DIM // 2, 0, LANES), NSA_DIM // 2),
        "mla": (_rope_tables(S, MLA_ROPE // 2, MLA_NOPE, MLA_NOPE + MLA_ROPE), MLA_ROPE // 2),
    }
    xa_w = XA_HEADS * XA_DIM
    h = x.reshape(B * S, D)
    mem2 = mem.reshape(B * M, D)
    for li in range(depth):
        j = li // 2
        if li % 2 == 0:
            h = _even_mixer(h, norm_mix[li], ev_w_in[j], ev_b_f[j], ev_lam[j], ev_subln[j], ev_w_out[j],
                            li, ropes, B, S)
        else:
            h = _odd_mixer(h, norm_mix[li], od_w_in[j], nsa_cmp_pos[j], nsa_cmp_w1[j], nsa_cmp_w2[j],
                           mla_q_norm[j], mla_kv_norm[j], mla_w_uq[j], mla_w_ukv[j], od_w_out[j],
                           ropes, B, S)
        mk, mv = _linear([mem2], [xa_wkv[li].astype(BF16)],
                         [Seg(0, xa_w, BF16), Seg(xa_w, xa_w, BF16)], gain=mem_norm, name="mem_kv")
        h = _mem_attn(h, norm_mem[li], xa_wq[li].astype(BF16), xa_wo[li].astype(BF16),
                      mk.reshape(B, M, xa_w), mv.reshape(B, M, xa_w), S=S)
        h = _ffn(h, norm_ffn[li], ffn_w13[li].astype(BF16), ffn_w2[li].astype(BF16),
                 out_gain=final_norm if li == depth - 1 else None)
    return h.reshape(B, S, D)
```

```python
import math
from typing import NamedTuple, Optional

import numpy as np
import jax
import jax.numpy as jnp
from jax import lax
from jax.experimental import pallas as pl
from jax.experimental.pallas import tpu as pltpu

F32 = jnp.float32
BF16 = jnp.bfloat16

LANES = 128
HALF_LANES = LANES // 2
BF16_ROWS = 16
ROPE_THETA = 10000.0
EPS = 1e-6
NEG = -1e30
LOG2E = math.log2(math.e)
MASK_BIG = 2.0 ** 100

DIFF_HEADS = 4
DIFF_QK = 64
FOX_HEADS = 8
NSA_HEADS = 8
NSA_GROUPS = 2
NSA_DIM = 64
CMP_LEN = 32
CMP_STRIDE = 16
CMP_HIDDEN = 128
SLC_LEN = 64
SLC_TOPK = 16
WIN = 512
FORCE_SCORE = 1e4
MLA_HEADS = 8
MLA_NOPE = 64
MLA_ROPE = 32
MLA_V = 64
MLA_Q_RANK = 384
MLA_KV_RANK = 256
XA_HEADS = 4
XA_DIM = 128

VMEM_LIMIT = 56 * 1024 * 1024
ROW_TILE = 512
COL_CHUNK = 512
ATT_TILE = 512
KEY_TILE = 512
ATT_BUFS = 3
MASK_BLOCK = 256
SEL_TILE = 512
CUM_CHUNK = 256

_NT = (((1,), (1,)), ((), ()))


def _params(*sem):
    return pltpu.CompilerParams(dimension_semantics=sem, vmem_limit_bytes=VMEM_LIMIT)


def _sigmoid(x):
    return 1.0 / (1.0 + jnp.exp(-x))


def _split_bf16(x, terms):
    out = []
    r = x
    for _ in range(terms):
        h = r.astype(BF16)
        out.append(h)
        r = r - h.astype(F32)
    return out


def _rope_tables(S, half, lane_lo, lane_hi):
    pos = jnp.arange(S, dtype=F32)
    inv = 1.0 / (ROPE_THETA ** (jnp.arange(half, dtype=F32) / half))
    ang = pos[:, None] * inv[None, :]
    cos, sin = jnp.cos(ang), jnp.sin(ang)
    lane = np.arange(LANES)
    active = (lane >= lane_lo) & (lane < lane_hi)
    j = (lane - lane_lo) % (2 * half)
    lower = active & (j < half)
    upper = active & (j >= half)
    idx = j % half
    cos_t = jnp.where(active[None, :], cos[:, idx], 1.0)
    sin_a = jnp.where(lower[None, :], -sin[:, idx], 0.0)
    sin_b = jnp.where(upper[None, :], sin[:, idx], 0.0)
    return cos_t, sin_a, sin_b


class Seg(NamedTuple):
    start: int
    width: int
    dtype: object
    rope: Optional[str] = None
    scale: float = 1.0


def _linear(xs, ws, segs, *, gain=None, residual=None, ropes=None, seq=None, name="linear"):
    N = xs[0].shape[0]
    tm = min(ROW_TILE, N)
    assert N % tm == 0
    n_in = len(xs)
    has_gain = gain is not None
    has_res = residual is not None
    rope_keys = sorted({s.rope for s in segs if s.rope})
    halves = {k: ropes[k][1] for k in rope_keys}

    def kern(*refs):
        it = iter(refs)
        x_refs = [next(it) for _ in range(n_in)]
        w_refs = [next(it) for _ in range(n_in)]
        g_ref = next(it) if has_gain else None
        r_ref = next(it) if has_res else None
        tabs = {k: (next(it), next(it), next(it)) for k in rope_keys}
        o_refs = [next(it) for _ in segs]
        acts = []
        for j, xr in enumerate(x_refs):
            x = xr[...]
            if j == 0 and has_gain:
                xf = x.astype(F32)
                y = xf * lax.rsqrt(jnp.mean(xf * xf, axis=-1, keepdims=True) + EPS)
                acts.append((y * g_ref[...]).astype(BF16))
            else:
                acts.append(x.astype(BF16))
        for seg, o_ref in zip(segs, o_refs):
            for c0 in range(0, seg.width, COL_CHUNK):
                cw = min(COL_CHUNK, seg.width - c0)
                col = seg.start + c0
                acc = None
                for a, wr in zip(acts, w_refs):
                    d = jnp.dot(a, wr[:, col:col + cw], preferred_element_type=F32)
                    acc = d if acc is None else acc + d
                if has_res:
                    acc = acc + r_ref[:, col:col + cw]
                if seg.scale != 1.0:
                    acc = acc * seg.scale
                if seg.rope is None:
                    o_ref[:, c0:c0 + cw] = acc.astype(o_ref.dtype)
                else:
                    cos_r, sa_r, sb_r = tabs[seg.rope]
                    half = halves[seg.rope]
                    cos, sa, sb = cos_r[...], sa_r[...], sb_r[...]
                    for s0 in range(0, cw, LANES):
                        xs_ = acc[:, s0:s0 + LANES]
                        y = (xs_ * cos + pltpu.roll(xs_, LANES - half, 1) * sa
                             + pltpu.roll(xs_, half, 1) * sb)
                        o_ref[:, c0 + s0:c0 + s0 + LANES] = y.astype(o_ref.dtype)

    in_specs, args = [], []
    for x in xs:
        in_specs.append(pl.BlockSpec((tm, x.shape[1]), lambda i: (i, 0)))
        args.append(x)
    for w in ws:
        in_specs.append(pl.BlockSpec(w.shape, lambda i: (0, 0)))
        args.append(w)
    if has_gain:
        in_specs.append(pl.BlockSpec((1, gain.shape[-1]), lambda i: (0, 0)))
        args.append(gain.reshape(1, -1).astype(F32))
    if has_res:
        in_specs.append(pl.BlockSpec((tm, residual.shape[1]), lambda i: (i, 0)))
        args.append(residual)
    for k in rope_keys:
        assert seq % tm == 0
        nt = seq // tm
        for t in ropes[k][0]:
            in_specs.append(pl.BlockSpec((tm, LANES), lambda i, nt=nt: (i % nt, 0)))
            args.append(t)
    out_shape = [jax.ShapeDtypeStruct((N, s.width), s.dtype) for s in segs]
    out_specs = [pl.BlockSpec((tm, s.width), lambda i: (i, 0)) for s in segs]
    return pl.pallas_call(
        kern, grid=(N // tm,), in_specs=in_specs, out_specs=out_specs, out_shape=out_shape,
        compiler_params=_params("parallel"), name=name)(*args)


def _forget_key_terms(fz, b_f, B, S):
    ch = CUM_CHUNK
    place = np.zeros((3, LANES, LANES), np.float32)
    for h in range(FOX_HEADS):
        for j in range(3):
            place[j, h, 3 * h + j] = 1.0

    def kern(z_ref, b_ref, pl_ref, o_ref):
        r = lax.broadcasted_iota(jnp.int32, (ch, ch), 0)
        c = lax.broadcasted_iota(jnp.int32, (ch, ch), 1)
        tri = jnp.where(c <= r, 1.0, 0.0).astype(BF16)

        def body(j, carries):
            out = []
            for b, carry in enumerate(carries):
                r0 = pl.multiple_of(b * S + j * ch, ch)
                z = z_ref[pl.ds(r0, ch), :] + b_ref[...]
                logf = -(jnp.maximum(-z, 0.0) + jnp.log1p(jnp.exp(-jnp.abs(z))))
                cs = carry
                for part in _split_bf16(logf, 3):
                    cs = cs + jnp.dot(tri, part, preferred_element_type=F32)
                terms = jnp.zeros((ch, LANES), F32)
                for jj, part in enumerate(_split_bf16(cs * (-LOG2E), 3)):
                    terms = terms + jnp.dot(part, pl_ref[jj], preferred_element_type=F32)
                o_ref[pl.ds(r0, ch), :] = terms.astype(o_ref.dtype)
                out.append(cs[ch - 1:ch, :])
            return tuple(out)

        lax.fori_loop(0, S // ch, body, tuple(jnp.zeros((1, LANES), F32) for _ in range(B)))

    return pl.pallas_call(
        kern, grid=(1,),
        in_specs=[pl.BlockSpec((B * S, LANES), lambda g: (0, 0), pipeline_mode=pl.Buffered(1)),
                  pl.BlockSpec((1, LANES), lambda g: (0, 0)),
                  pl.BlockSpec(place.shape, lambda g: (0, 0, 0))],
        out_specs=pl.BlockSpec((B * S, LANES), lambda g: (0, 0)),
        out_shape=jax.ShapeDtypeStruct((B * S, LANES), BF16),
        compiler_params=_params("arbitrary"), name="forget_terms")(fz, b_f, jnp.asarray(place, BF16))


class Head(NamedTuple):
    q_off: int
    q_half: Optional[str]
    k_off: int
    v0: int
    v1: int


def _flash(q, k, vt, *, B, S, P, q_w, k_w, v_rows, heads, kind, out_dtype, tq=None, tk=None, n_bufs=None,
           ext=None, sel=None, lam=None, subln=None, lam_init=None, name="flash"):
    TQ, TK = tq or ATT_TILE, tk or KEY_TILE
    assert S % TQ == 0 and TQ % TK == 0
    diag = TQ // TK
    nq = S // TQ
    nh = len(heads)
    rows = heads[0].v1 - heads[0].v0
    out_w = nh // 2 * rows if kind == "diff" else nh * rows
    width = TK
    n_bufs = n_bufs or ATT_BUFS
    if kind == "win":
        assert WIN == TK and TQ == TK and S >= 2 * TK and n_bufs == 2
    has_ext = kind in ("fox", "slc")

    def kern(*refs):
        it = iter(refs)
        q_ref, k_ref, v_ref = next(it), next(it), next(it)
        ext_ref = next(it) if has_ext else None
        sel_ref = next(it) if kind == "slc" else None
        lam_ref, sub_ref = (next(it), next(it)) if kind == "diff" else (None, None)
        o_ref, m_ref, acc_ref, mx_ref = next(it), next(it), next(it), next(it)
        bufs = [next(it) for _ in range(n_bufs)]
        p_id = pl.program_id(1)
        i = pl.program_id(2)

        lane = lax.broadcasted_iota(jnp.int32, (TQ, LANES), 1)
        lo = lane < HALF_LANES
        q_all = q_ref[...]
        if kind == "slc":
            drop = ((sel_ref[...].astype(F32) - 1.0) * MASK_BIG).astype(BF16)
        qs = []
        for j, h in enumerate(heads):
            qh = q_all[:, h.q_off:h.q_off + LANES]
            if h.q_half == "lo":
                qh = jnp.where(lo, qh, jnp.zeros_like(qh))
            elif h.q_half == "hi":
                qh = jnp.where(lo, jnp.zeros_like(qh), qh)
            if kind == "fox":
                first = 3 * (nh * p_id + j)
                pick = jnp.where((lane >= first) & (lane < first + 3), 1.0, 0.0).astype(BF16)
                qh = jnp.concatenate([qh, pick], axis=1)
            if kind == "slc":
                qh = jnp.concatenate([qh, drop], axis=1)
            qs.append(qh)

        m_ref[...] = jnp.full(m_ref.shape, NEG, F32)
        acc_ref[...] = jnp.zeros(acc_ref.shape, F32)

        def block_kind(ahead, kb, qb):
            lo_rel = (kb - qb) * MASK_BLOCK - (MASK_BLOCK - 1)
            hi_rel = (kb - qb) * MASK_BLOCK + (MASK_BLOCK - 1)
            floor = ahead - WIN if kind == "win" else lo_rel - 1
            if lo_rel > ahead or hi_rel <= floor:
                return "dead"
            if hi_rel <= ahead and lo_rel > floor:
                return "full"
            return "part"

        def block_mask(ahead, kb, qb):
            rel = (lax.broadcasted_iota(jnp.int32, (MASK_BLOCK, MASK_BLOCK), 0)
                   - lax.broadcasted_iota(jnp.int32, (MASK_BLOCK, MASK_BLOCK), 1)
                   + (kb - qb) * MASK_BLOCK)
            msk = rel <= ahead
            if kind == "win":
                msk = msk & (rel > ahead - WIN)
            return msk

        kbs, qbs = range(width // MASK_BLOCK), range(TQ // MASK_BLOCK)
        blk = lambda b: slice(b * MASK_BLOCK, (b + 1) * MASK_BLOCK)

        def scores(k0, u, ahead=None):
            kx = k_ref[pl.ds(k0, width), :]
            ex = ext_ref[pl.ds(k0, width), :] if has_ext else None
            for hi, h in enumerate(heads):
                kk = kx[:, h.k_off:h.k_off + LANES]
                if ex is not None:
                    kk = jnp.concatenate([kk, ex], axis=1)
                if ahead is None:
                    st = lax.dot_general(kk, qs[hi], _NT, preferred_element_type=F32)
                    bufs[u][hi] = st
                    mx_ref[u * nh + hi] = jnp.max(st, axis=0, keepdims=True)
                    continue
                col_max = [None] * len(qbs)
                for kb in kbs:
                    live = [qb for qb in qbs if block_kind(ahead, kb, qb) != "dead"]
                    if not live:
                        continue
                    q_rows = slice(live[0] * MASK_BLOCK, (live[-1] + 1) * MASK_BLOCK)
                    st = lax.dot_general(kk[blk(kb)], qs[hi][q_rows], _NT, preferred_element_type=F32)
                    for qb in range(live[0], live[-1] + 1):
                        sb = st[:, blk(qb - live[0])]
                        if block_kind(ahead, kb, qb) == "part":
                            sb = jnp.where(block_mask(ahead, kb, qb), sb, NEG)
                        bufs[u][hi, blk(kb), blk(qb)] = sb
                        cm = jnp.max(sb, axis=0, keepdims=True)
                        col_max[qb] = cm if col_max[qb] is None else jnp.maximum(col_max[qb], cm)
                mx_ref[u * nh + hi] = jnp.concatenate(col_max, axis=1)

        def update(k0, u, ahead=None):
            for hi, h in enumerate(heads):
                m_prev = m_ref[hi]
                m_new = jnp.maximum(m_prev, mx_ref[u * nh + hi])
                alpha = jnp.exp2(m_prev - m_new)
                if ahead is None:
                    ones = jnp.ones((BF16_ROWS, width), BF16)
                    pt = jnp.exp2(bufs[u][hi] - m_new).astype(BF16)
                    lhs = jnp.concatenate([v_ref[h.v0:h.v1, pl.ds(k0, width)], ones], axis=0)
                    new = jnp.dot(lhs, pt, preferred_element_type=F32)
                else:
                    ones = jnp.ones((BF16_ROWS, MASK_BLOCK), BF16)
                    cols = []
                    for qb in qbs:
                        col = None
                        for kb in kbs:
                            if block_kind(ahead, kb, qb) == "dead":
                                continue
                            pt = jnp.exp2(bufs[u][hi, blk(kb), blk(qb)] - m_new[:, blk(qb)]).astype(BF16)
                            kstart = pl.multiple_of(k0 + kb * MASK_BLOCK, MASK_BLOCK)
                            lhs = jnp.concatenate(
                                [v_ref[h.v0:h.v1, pl.ds(kstart, MASK_BLOCK)], ones], axis=0)
                            part = jnp.dot(lhs, pt, preferred_element_type=F32)
                            col = part if col is None else col + part
                        cols.append(col)
                    new = jnp.concatenate(cols, axis=1)
                acc_ref[hi] = alpha * acc_ref[hi] + new
                m_ref[hi] = m_new

        tile = lambda t: pl.multiple_of(t * TK, TK)
        U = len(bufs)
        if kind == "win":
            @pl.when(i == 0)
            def _():
                scores(tile(0), 0, 0)
                update(tile(0), 0, 0)

            @pl.when(i > 0)
            def _():
                scores(tile(i - 1), 0, WIN)
                scores(tile(i), 1, 0)
                update(tile(i - 1), 0, WIN)
                update(tile(i), 1, 0)
        else:
            n_full = i * diag
            n_rounds = jnp.maximum(n_full - 1, 0) // U
            pl.when(n_full == 0)(lambda: scores(tile(0), 0, 0))
            pl.when(n_full > 0)(lambda: scores(tile(0), 0))

            def body(j, c):
                for u in range(U):
                    scores(tile(U * j + u + 1), (u + 1) % U)
                    update(tile(U * j + u), u)
                return c
            lax.fori_loop(0, n_rounds, body, 0)
            t0 = U * n_rounds

            def tail(left):
                ahead = lambda u: -(u - left) * TK if u >= left else None
                last = left + diag - 1
                for u in range(last):
                    scores(tile(t0 + u + 1), (u + 1) % U, ahead(u + 1))
                    update(tile(t0 + u), u % U, ahead(u))
                update(tile(t0 + last), last % U, ahead(last))

            for left in range(U + 1):
                pl.when(n_full - t0 == left)(lambda left=left: tail(left))

        outs = []
        for hi in range(nh):
            acc = acc_ref[hi]
            outs.append(acc[:rows] / acc[rows:rows + 1])
        if kind == "diff":
            lm = lam_ref[...]
            la = jnp.sum(lm[0:1] * lm[1:2], axis=1, keepdims=True)
            lb = jnp.sum(lm[2:3] * lm[3:4], axis=1, keepdims=True)
            lam_full = jnp.exp(la) - jnp.exp(lb) + lam_init
            for g in range(nh // 2):
                oa = jnp.transpose(outs[2 * g] - lam_full * outs[2 * g + 1])
                y = oa * lax.rsqrt(jnp.mean(oa * oa, axis=-1, keepdims=True) + EPS)
                y = (y * sub_ref[...]) * (1.0 - lam_init)
                o_ref[:, g * rows:(g + 1) * rows] = y.astype(o_ref.dtype)
        else:
            o_ref[...] = jnp.transpose(jnp.concatenate(outs, axis=0)).astype(o_ref.dtype)

    in_specs = [
        pl.BlockSpec((TQ, q_w), lambda b, p, i: (b * nq + i, p)),
        pl.BlockSpec((S, k_w), lambda b, p, i: (b, p)),
        pl.BlockSpec((v_rows, S), lambda b, p, i: (p, b)),
    ]
    args = [q, k, vt]
    if kind == "fox":
        in_specs.append(pl.BlockSpec((S, LANES), lambda b, p, i: (b, 0)))
        args.append(ext)
    if kind == "slc":
        in_specs.append(pl.BlockSpec((S, LANES), lambda b, p, i: (0, 0)))
        in_specs.append(pl.BlockSpec((TQ, LANES), lambda b, p, i: (b * nq + i, p)))
        args += [ext, sel]
    if kind == "diff":
        in_specs.append(pl.BlockSpec(lam.shape, lambda b, p, i: (0, 0)))
        in_specs.append(pl.BlockSpec((1, LANES), lambda b, p, i: (0, 0)))
        args += [lam, subln.reshape(1, LANES)]
    return pl.pallas_call(
        kern, grid=(B, P, nq), in_specs=in_specs,
        out_specs=pl.BlockSpec((TQ, out_w), lambda b, p, i: (b * nq + i, p)),
        out_shape=jax.ShapeDtypeStruct((B * S, P * out_w), out_dtype),
        scratch_shapes=[pltpu.VMEM((nh, 1, TQ), F32),
                        pltpu.VMEM((nh, rows + BF16_ROWS, TQ), F32),
                        pltpu.VMEM((n_bufs * nh, 1, TQ), F32)]
        + [pltpu.VMEM((nh, width, TQ), F32)] * n_bufs,
        compiler_params=_params("parallel", "parallel", "arbitrary"), name=name)(*args)


_DIFF_HEADS = tuple(Head(s * LANES, half, s * LANES, s * LANES, (s + 1) * LANES)
                    for s in range(2) for half in ("lo", "hi"))
_PAIR_HEADS = tuple(Head(s * LANES, half, s * LANES, (2 * s + j) * HALF_LANES, (2 * s + j + 1) * HALF_LANES)
                    for s in range(2) for j, half in enumerate(("lo", "hi")))
_QUAD_HEADS = tuple(Head(s * LANES, half, 0, 0, HALF_LANES) for s in range(2) for half in ("lo", "hi"))
_WIDE_HEADS = tuple(Head(j * LANES, None, j * LANES, j * HALF_LANES, (j + 1) * HALF_LANES) for j in range(4))


def _nsa_compress(xk, xv, pe, w1x, w2d):
    B, R, K = xk.shape
    G = NSA_GROUPS

    def kern(xk_ref, xv_ref, pe_ref, w1_ref, w2_ref, k_ref, vt_ref):
        for t, x_ref in enumerate((xk_ref, xv_ref)):
            x = x_ref[...]
            xa = (x + pe_ref[t, 0]).astype(BF16)
            xb = (x + pe_ref[t, 1]).astype(BF16)
            for g in range(G):
                a = jnp.dot(xa, w1_ref[t, 0, g], preferred_element_type=F32)
                b = jnp.dot(xb, w1_ref[t, 1, g], preferred_element_type=F32)
                h = a + pltpu.roll(b, R - 1, 0)
                hs = h * _sigmoid(h)
                o = jnp.dot(hs.astype(BF16), w2_ref[t], preferred_element_type=F32)
                if t == 0:
                    k_ref[g] = o
                else:
                    vt_ref[g] = jnp.transpose(o)

    return pl.pallas_call(
        kern, grid=(B,),
        in_specs=[pl.BlockSpec((None, R, K), lambda b: (b, 0, 0)),
                  pl.BlockSpec((None, R, K), lambda b: (b, 0, 0)),
                  pl.BlockSpec(pe.shape, lambda b: (0, 0, 0, 0)),
                  pl.BlockSpec(w1x.shape, lambda b: (0, 0, 0, 0, 0)),
                  pl.BlockSpec(w2d.shape, lambda b: (0, 0, 0))],
        out_specs=[pl.BlockSpec((None, G, R, LANES), lambda b: (b, 0, 0, 0)),
                   pl.BlockSpec((None, G, LANES, R), lambda b: (b, 0, 0, 0))],
        out_shape=[jax.ShapeDtypeStruct((B, G, R, LANES), F32),
                   jax.ShapeDtypeStruct((B, G, LANES, R), F32)],
        compiler_params=_params("parallel"), name="nsa_compress")(xk, xv, pe, w1x, w2d)


def _nsa_cmp_select(q, cmp_k, cmp_vt, ov_t, *, B, S):
    T = SEL_TILE
    nq = S // T
    R = cmp_k.shape[2]
    G = NSA_GROUPS
    d = NSA_DIM
    n_sel = min(SLC_TOPK, S // SLC_LEN)
    assert n_sel >= 3 and FORCE_SCORE > NSA_HEADS // G

    bucket = min(LANES, R)
    assert R % bucket == 0 and bucket % (SLC_LEN // CMP_STRIDE) == 0

    def kern(q_ref, kc_ref, vt_ref, ov_ref, o_ref, sel_ref):
        i = pl.program_id(2)
        lane = lax.broadcasted_iota(jnp.int32, (T, LANES), 1)
        lo = lane < HALF_LANES
        q_all = q_ref[...]
        qs = []
        for hh in range(4):
            qh = q_all[:, (hh // 2) * LANES:(hh // 2 + 1) * LANES]
            qs.append(jnp.where(lo, qh, jnp.zeros_like(qh)) if hh % 2 == 0
                      else jnp.where(lo, jnp.zeros_like(qh), qh))

        def variant(rv):
            nb = rv * CMP_STRIDE // SLC_LEN
            tq = lax.broadcasted_iota(jnp.int32, (rv, T), 1) + i * T
            cend = lax.broadcasted_iota(jnp.int32, (rv, T), 0) * CMP_STRIDE + (CMP_LEN - 1)
            vis = cend <= tq
            kc = kc_ref[:rv, :].astype(BF16)
            vt = vt_ref[:d, :rv].astype(BF16)
            psum = jnp.zeros((rv, T), F32)
            o_heads = []
            for hh in range(4):
                st = lax.dot_general(kc, qs[hh], _NT, preferred_element_type=F32)
                st = jnp.where(vis, st, -jnp.inf)
                m = jnp.max(st, axis=0, keepdims=True)
                e = jnp.exp2(st - jnp.where(m > -jnp.inf, m, 0.0))
                p = e / jnp.maximum(jnp.sum(e, axis=0, keepdims=True), 1e-30)
                o_heads.append(jnp.dot(vt, p.astype(BF16), preferred_element_type=F32))
                psum = psum + p
            o_ref[...] = jnp.transpose(jnp.concatenate(o_heads, axis=0)).astype(o_ref.dtype)
            imp = jnp.zeros((nb, T), F32)
            for part in _split_bf16(psum, 2):
                imp = imp + jnp.dot(ov_ref[:nb, :rv], part, preferred_element_type=F32)
            blk = lax.broadcasted_iota(jnp.int32, (nb, T), 0)
            cur = (lax.broadcasted_iota(jnp.int32, (nb, T), 1) + i * T) // SLC_LEN
            forced = (blk == 0) | (blk == cur) | (blk == cur - 1)
            work = jnp.where(forced, -jnp.inf, imp)
            work = jnp.where(blk > cur, NEG, work)
            blk_f = blk.astype(F32)
            for _ in range(n_sel - 3):
                mx = jnp.max(work, axis=0, keepdims=True)
                first = jnp.min(jnp.where(work == mx, blk_f, float(LANES)), axis=0, keepdims=True)
                work = jnp.where(blk_f == first, -jnp.inf, work)
            sel = jnp.where(work == -jnp.inf, 1.0, 0.0)
            if nb < LANES:
                sel = jnp.concatenate([sel, jnp.zeros((LANES - nb, T), F32)], axis=0)
            sel_ref[...] = jnp.transpose(sel).astype(sel_ref.dtype)

        need = (i + 1) * (T // CMP_STRIDE)
        which = (need - 1) // bucket
        for bk in range(R // bucket):
            pl.when(which == bk)(lambda bk=bk: variant((bk + 1) * bucket))

    return pl.pallas_call(
        kern, grid=(B, G, nq),
        in_specs=[pl.BlockSpec((T, 2 * LANES), lambda b, g, i: (b * nq + i, g)),
                  pl.BlockSpec((None, None, R, LANES), lambda b, g, i: (b, g, 0, 0)),
                  pl.BlockSpec((None, None, LANES, R), lambda b, g, i: (b, g, 0, 0)),
                  pl.BlockSpec((LANES, R), lambda b, g, i: (0, 0))],
        out_specs=[pl.BlockSpec((T, 2 * LANES), lambda b, g, i: (b * nq + i, g)),
                   pl.BlockSpec((T, LANES), lambda b, g, i: (b * nq + i, g))],
        out_shape=[jax.ShapeDtypeStruct((B * S, G * 2 * LANES), BF16),
                   jax.ShapeDtypeStruct((B * S, G * LANES), BF16)],
        compiler_params=_params("parallel", "parallel", "parallel"), name="nsa_cmp_select")(
            q, cmp_k, cmp_vt, ov_t)


def _odd_out(o_cmp, o_slc, o_win, gz, expand, o_mla, w_nsa, w_mla, x):
    N, C = o_cmp.shape
    D = x.shape[1]
    tm = min(ROW_TILE, N)

    def kern(c_ref, s_ref, w_ref, g_ref, e_ref, m_ref, wn_ref, wm_ref, x_ref, o_ref):
        gate = _sigmoid(g_ref[...])
        parts = _split_bf16(gate, 2)
        acc = jnp.zeros((tm, C), F32)
        for j, br in enumerate((c_ref, s_ref, w_ref)):
            gj = sum(jnp.dot(part, e_ref[j], preferred_element_type=F32) for part in parts)
            acc = acc + gj * br[...].astype(F32)
        o_nsa = acc.astype(BF16)
        o_mla_ = m_ref[...]
        for c0 in range(0, D, COL_CHUNK):
            cols = slice(c0, c0 + COL_CHUNK)
            y = (jnp.dot(o_nsa, wn_ref[:, cols], preferred_element_type=F32)
                 + jnp.dot(o_mla_, wm_ref[:, cols], preferred_element_type=F32))
            o_ref[:, cols] = x_ref[:, cols] + y

    row = pl.BlockSpec((tm, C), lambda i: (i, 0))
    wide = pl.BlockSpec((tm, D), lambda i: (i, 0))
    return pl.pallas_call(
        kern, grid=(N // tm,),
        in_specs=[row, row, row, pl.BlockSpec((tm, LANES), lambda i: (i, 0)),
                  pl.BlockSpec(expand.shape, lambda i: (0, 0, 0)),
                  pl.BlockSpec((tm, o_mla.shape[1]), lambda i: (i, 0)),
                  pl.BlockSpec(w_nsa.shape, lambda i: (0, 0)),
                  pl.BlockSpec(w_mla.shape, lambda i: (0, 0)), wide],
        out_specs=wide, out_shape=jax.ShapeDtypeStruct((N, D), F32),
        compiler_params=_params("parallel"), name="odd_out")(
            o_cmp, o_slc, o_win, gz, expand, o_mla, w_nsa, w_mla, x)


def _mla_kv(ckv, gain, wk, wv, kr):
    N, K = ckv.shape
    tm = min(ROW_TILE, N)
    H = MLA_HEADS

    def kern(c_ref, g_ref, wk_ref, wv_ref, kr_ref, k_ref, v_ref):
        xf = c_ref[...]
        y = xf * lax.rsqrt(jnp.mean(xf * xf, axis=-1, keepdims=True) + EPS)
        a = (y * g_ref[...]).astype(BF16)
        kr_ = kr_ref[...]
        for h in range(H):
            kh = jnp.dot(a, wk_ref[:, h * LANES:(h + 1) * LANES], preferred_element_type=F32)
            k_ref[:, h * LANES:(h + 1) * LANES] = (kh + kr_).astype(k_ref.dtype)
        v_ref[...] = jnp.dot(a, wv_ref[...], preferred_element_type=F32).astype(v_ref.dtype)

    return pl.pallas_call(
        kern, grid=(N // tm,),
        in_specs=[pl.BlockSpec((tm, K), lambda i: (i, 0)),
                  pl.BlockSpec((1, K), lambda i: (0, 0)),
                  pl.BlockSpec(wk.shape, lambda i: (0, 0)),
                  pl.BlockSpec(wv.shape, lambda i: (0, 0)),
                  pl.BlockSpec((tm, LANES), lambda i: (i, 0))],
        out_specs=[pl.BlockSpec((tm, H * LANES), lambda i: (i, 0)),
                   pl.BlockSpec((tm, H * MLA_V), lambda i: (i, 0))],
        out_shape=[jax.ShapeDtypeStruct((N, H * LANES), BF16),
                   jax.ShapeDtypeStruct((N, H * MLA_V), BF16)],
        compiler_params=_params("parallel"), name="mla_kv")(
            ckv, gain.reshape(1, K).astype(F32), wk, wv, kr)


def _mem_attn(x, gain, wq, wo, mem_k, mem_v, *, S):
    N, D = x.shape
    tm = min(ROW_TILE, S)
    M = mem_k.shape[1]
    per_b = S // tm
    scale = XA_DIM ** -0.5

    def kern(x_ref, g_ref, wq_ref, wo_ref, k_ref, v_ref, o_ref):
        xf = x_ref[...]
        y = xf * lax.rsqrt(jnp.mean(xf * xf, axis=-1, keepdims=True) + EPS)
        a = (y * g_ref[...]).astype(BF16)
        q = (jnp.dot(a, wq_ref[...], preferred_element_type=F32) * scale).astype(BF16)
        heads = []
        for h in range(XA_HEADS):
            sl = slice(h * XA_DIM, (h + 1) * XA_DIM)
            s = lax.dot_general(q[:, sl], k_ref[:, sl], _NT, preferred_element_type=F32)
            e = jnp.exp(s - jnp.max(s, axis=1, keepdims=True))
            p = e / jnp.sum(e, axis=1, keepdims=True)
            heads.append(jnp.dot(p.astype(BF16), v_ref[:, sl], preferred_element_type=F32).astype(BF16))
        o = jnp.concatenate(heads, axis=1)
        o_ref[...] = xf + jnp.dot(o, wo_ref[...], preferred_element_type=F32)

    return pl.pallas_call(
        kern, grid=(N // tm,),
        in_specs=[pl.BlockSpec((tm, D), lambda i: (i, 0)),
                  pl.BlockSpec((1, D), lambda i: (0, 0)),
                  pl.BlockSpec(wq.shape, lambda i: (0, 0)),
                  pl.BlockSpec(wo.shape, lambda i: (0, 0)),
                  pl.BlockSpec((None, M, XA_HEADS * XA_DIM), lambda i: (i // per_b, 0, 0)),
                  pl.BlockSpec((None, M, XA_HEADS * XA_DIM), lambda i: (i // per_b, 0, 0))],
        out_specs=pl.BlockSpec((tm, D), lambda i: (i, 0)),
        out_shape=jax.ShapeDtypeStruct((N, D), F32),
        compiler_params=_params("parallel"), name="mem_attn")(
            x, gain.reshape(1, D).astype(F32), wq, wo, mem_k, mem_v)


def _ffn(x, gain, w13, w2, out_gain=None):
    N, D = x.shape
    FF = w2.shape[0]
    tm = min(ROW_TILE, N)
    chunk = 2 * LANES
    assert FF % chunk == 0

    final = out_gain is not None

    def kern(x_ref, g_ref, w13_ref, w2_ref, *rest):
        og_ref, o_ref = rest if final else (None, rest[0])
        xf = x_ref[...]
        y = xf * lax.rsqrt(jnp.mean(xf * xf, axis=-1, keepdims=True) + EPS)
        a = (y * g_ref[...]).astype(BF16)
        acc = xf
        for c in range(0, FF, chunk):
            g = jnp.dot(a, w13_ref[:, c:c + chunk], preferred_element_type=F32)
            u = jnp.dot(a, w13_ref[:, FF + c:FF + c + chunk], preferred_element_type=F32)
            hdn = (g * _sigmoid(g) * u).astype(BF16)
            acc = acc + jnp.dot(hdn, w2_ref[c:c + chunk, :], preferred_element_type=F32)
        if final:
            acc = acc * lax.rsqrt(jnp.mean(acc * acc, axis=-1, keepdims=True) + EPS) * og_ref[...]
        o_ref[...] = acc

    vec = pl.BlockSpec((1, D), lambda i: (0, 0))
    in_specs = [pl.BlockSpec((tm, D), lambda i: (i, 0)), vec,
                pl.BlockSpec(w13.shape, lambda i: (0, 0), pipeline_mode=pl.Buffered(1)),
                pl.BlockSpec(w2.shape, lambda i: (0, 0), pipeline_mode=pl.Buffered(1))]
    args = [x, gain.reshape(1, D).astype(F32), w13, w2]
    if final:
        in_specs.append(vec)
        args.append(out_gain.reshape(1, D).astype(F32))
    return pl.pallas_call(
        kern, grid=(N // tm,), in_specs=in_specs,
        out_specs=pl.BlockSpec((tm, D), lambda i: (i, 0)),
        out_shape=jax.ShapeDtypeStruct((N, D), F32),
        compiler_params=_params("parallel"), name="ffn")(*args)


def _pad_cols(w, width):
    return jnp.pad(w, ((0, 0), (0, width - w.shape[1])))


def _even_mixer(x, gain, w_in, b_f, lam, subln, w_out, layer_idx, ropes, B, S):
    D = x.shape[1]
    blk = DIFF_HEADS * 2 * DIFF_QK
    w = _pad_cols(w_in, 6 * blk + LANES).astype(BF16)
    sc = DIFF_QK ** -0.5 * LOG2E
    segs = [Seg(0, blk, BF16, "r64", sc), Seg(blk, blk, BF16, "r64"), Seg(2 * blk, blk, BF16),
            Seg(3 * blk, blk, BF16, None, sc), Seg(4 * blk, blk, BF16), Seg(5 * blk, blk, BF16),
            Seg(6 * blk, LANES, F32)]
    aq, ak, av, fq, fk, fv, fz = _linear([x], [w], segs, gain=gain, ropes=ropes, seq=S, name="even_in")
    lam_init = 0.8 - 0.6 * math.exp(-0.3 * layer_idx)
    oa = _flash(aq, ak, av.T, B=B, S=S, P=DIFF_HEADS // 2, q_w=2 * LANES, k_w=2 * LANES, v_rows=2 * LANES,
                heads=_DIFF_HEADS, kind="diff", out_dtype=BF16, lam=lam.astype(F32),
                subln=subln.astype(F32), lam_init=lam_init, name="diff_attn")
    terms = _forget_key_terms(fz, _pad_cols(b_f.reshape(1, -1), LANES).astype(F32), B, S)
    of = _flash(fq, fk, fv.T, B=B, S=S, P=FOX_HEADS // 4, q_w=2 * LANES, k_w=2 * LANES, v_rows=2 * LANES,
                heads=_PAIR_HEADS, kind="fox", out_dtype=BF16, ext=terms, name="fox_attn")
    wo = w_out.astype(BF16)
    (y,) = _linear([oa, of], [wo[:blk], wo[blk:]], [Seg(0, D, F32)], residual=x, name="even_out")
    return y


def _odd_in_weight(w_in):
    d = NSA_DIM
    o = np.cumsum((0, NSA_HEADS * d) + (NSA_GROUPS * d,) * 6 + (NSA_HEADS * 3, MLA_Q_RANK, MLA_KV_RANK, MLA_ROPE))
    nq, kc, vc, ks, vs, kw, vw, gz, cq, ckv, kr = [w_in[:, o[j]:o[j + 1]] for j in range(11)]

    def dup(wg):
        return jnp.concatenate([wg[:, :d], wg[:, :d], wg[:, d:], wg[:, d:]], axis=1)

    zeros = lambda n: jnp.zeros((w_in.shape[0], n), w_in.dtype)
    kr_slab = jnp.concatenate([zeros(MLA_NOPE), kr, zeros(LANES - MLA_NOPE - MLA_ROPE)], axis=1)
    cols = [nq, kc, vc, dup(ks), vs, dup(kw), vw, _pad_cols(gz, LANES), cq, ckv, kr_slab]
    return jnp.concatenate(cols, axis=1).astype(BF16)


def _odd_mixer(x, gain, w_in, cmp_pos, cmp_w1, cmp_w2, q_norm, kv_norm, w_uq, w_ukv, w_out, ropes, B, S):
    D = x.shape[1]
    G, d = NSA_GROUPS, NSA_DIM
    w = _odd_in_weight(w_in)
    sc = d ** -0.5 * LOG2E
    widths = [(NSA_HEADS * d, BF16, "r64", sc), (LANES, F32, "r64", 1.0), (LANES, F32, None, 1.0),
              (2 * LANES, BF16, "r64", 1.0), (LANES, BF16, None, 1.0),
              (2 * LANES, BF16, "r64", 1.0), (LANES, BF16, None, 1.0),
              (LANES, F32, None, 1.0), (MLA_Q_RANK, F32, None, 1.0), (MLA_KV_RANK, F32, None, 1.0),
              (LANES, F32, "mla", 1.0)]
    segs, start = [], 0
    for wd, dt, rp, s_ in widths:
        segs.append(Seg(start, wd, dt, rp, s_))
        start += wd
    q, kc, vc, ks, vs, kw, vw, gz, cq, ckv, kr = _linear(
        [x], [w], segs, gain=gain, ropes=ropes, seq=S, name="odd_in")

    R = S // CMP_STRIDE

    K = CMP_STRIDE * G * d
    half = CMP_LEN // 2
    pe = jnp.broadcast_to(cmp_pos.reshape(2, 2, half, 1, d), (2, 2, half, G, d)).reshape(2, 2, 1, K).astype(F32)
    eye = jnp.eye(G, dtype=cmp_w1.dtype)
    w1x = (cmp_w1.reshape(2, 2, 1, half, 1, d, CMP_HIDDEN) * eye[None, None, :, None, :, None, None])
    w1x = w1x.reshape(2, 2, G, K, CMP_HIDDEN).astype(BF16)
    w2d = jnp.concatenate([cmp_w2, cmp_w2], axis=-1).astype(BF16)
    cmp_k, cmp_vt = _nsa_compress(kc.reshape(B, R, K), vc.reshape(B, R, K), pe, w1x, w2d)
    cs = np.arange(R) * CMP_STRIDE
    bs = np.arange(LANES) * SLC_LEN
    ov = ((cs[:, None] < bs[None, :] + SLC_LEN) & (cs[:, None] + CMP_LEN > bs[None, :])
          & (np.arange(R)[:, None] < R - 1) & (bs[None, :] < S))
    o_cmp, sel = _nsa_cmp_select(q, cmp_k, cmp_vt, jnp.asarray(ov.T.astype(np.float32), BF16), B=B, S=S)
    onehot = (np.arange(S)[:, None] // SLC_LEN == np.arange(LANES)[None, :]).astype(np.float32)
    o_slc = _flash(q, ks, vs.T, B=B, S=S, P=G, q_w=2 * LANES, k_w=LANES, v_rows=HALF_LANES,
                   heads=_QUAD_HEADS, kind="slc", out_dtype=BF16, ext=jnp.asarray(onehot, BF16),
                   sel=sel, name="nsa_slc")
    o_win = _flash(q, kw, vw.T, B=B, S=S, P=G, q_w=2 * LANES, k_w=LANES, v_rows=HALF_LANES,
                   heads=_QUAD_HEADS, kind="win", out_dtype=BF16, tq=WIN, tk=WIN, n_bufs=2, name="nsa_win")
    ex = np.zeros((3, LANES, NSA_HEADS * d), np.float32)
    for hh in range(NSA_HEADS):
        for j in range(3):
            ex[j, hh * 3 + j, hh * d:(hh + 1) * d] = 1.0

    H = MLA_HEADS
    qk = MLA_NOPE + MLA_ROPE
    wq_slab = jnp.pad(w_uq.reshape(MLA_Q_RANK, H, qk), ((0, 0), (0, 0), (0, LANES - qk)))
    wq_slab = wq_slab.reshape(MLA_Q_RANK, H * LANES).astype(BF16)
    (qm,) = _linear([cq], [wq_slab], [Seg(0, H * LANES, BF16, "mla", qk ** -0.5 * LOG2E)],
                    gain=q_norm, ropes=ropes, seq=S, name="mla_q")
    wkv = w_ukv.reshape(MLA_KV_RANK, H, MLA_NOPE + MLA_V)
    wk = jnp.pad(wkv[:, :, :MLA_NOPE], ((0, 0), (0, 0), (0, LANES - MLA_NOPE)))
    wk = wk.reshape(MLA_KV_RANK, H * LANES).astype(BF16)
    wv = wkv[:, :, MLA_NOPE:].reshape(MLA_KV_RANK, H * MLA_V).astype(BF16)
    km, vm = _mla_kv(ckv, kv_norm, wk, wv, kr)
    o_mla = _flash(qm, km, vm.T, B=B, S=S, P=H // 4, q_w=4 * LANES, k_w=4 * LANES, v_rows=2 * LANES,
                   heads=_WIDE_HEADS, kind="mla", out_dtype=BF16, name="mla_attn")
    wo = w_out.astype(BF16)
    half = NSA_HEADS * d
    return _odd_out(o_cmp, o_slc, o_win, gz, jnp.asarray(ex, BF16), o_mla, wo[:half], wo[half:], x)


def kernel(x, mem, mem_norm, norm_mix, norm_mem, norm_ffn, ev_w_in, ev_b_f, ev_lam, ev_subln, ev_w_out, od_w_in, nsa_cmp_pos, nsa_cmp_w1, nsa_cmp_w2, mla_q_norm, mla_kv_norm, mla_w_uq, mla_w_ukv, od_w_out, xa_wq, xa_wkv, xa_wo, ffn_w13, ffn_w2, final_norm):
    B, S, D = x.shape
    M = mem.shape[1]
    depth = norm_mix.shape[0]
    ropes = {
        "r64": (_rope_tables(S, NSA_DIM // 2, 0, LANES), NSA_DIM // 2),
        "mla": (_rope_tables(S, MLA_ROPE // 2, MLA_NOPE, MLA_NOPE + MLA_ROPE), MLA_ROPE // 2),
    }
    xa_w = XA_HEADS * XA_DIM
    h = x.reshape(B * S, D)
    mem2 = mem.reshape(B * M, D)
    for li in range(depth):
        j = li // 2
        if li % 2 == 0:
            h = _even_mixer(h, norm_mix[li], ev_w_in[j], ev_b_f[j], ev_lam[j], ev_subln[j], ev_w_out[j],
                            li, ropes, B, S)
        else:
            h = _odd_mixer(h, norm_mix[li], od_w_in[j], nsa_cmp_pos[j], nsa_cmp_w1[j], nsa_cmp_w2[j],
                           mla_q_norm[j], mla_kv_norm[j], mla_w_uq[j], mla_w_ukv[j], od_w_out[j],
                           ropes, B, S)
        mk, mv = _linear([mem2], [xa_wkv[li].astype(BF16)],
                         [Seg(0, xa_w, BF16), Seg(xa_w, xa_w, BF16)], gain=mem_norm, name="mem_kv")
        h = _mem_attn(h, norm_mem[li], xa_wq[li].astype(BF16), xa_wo[li].astype(BF16),
                      mk.reshape(B, M, xa_w), mv.reshape(B, M, xa_w), S=S)
        h = _ffn(h, norm_ffn[li], ffn_w13[li].astype(BF16), ffn_w2[li].astype(BF16),
                 out_gain=final_norm if li == depth - 1 else None)
    return h.reshape(B, S, D)
```

```python
import math
from typing import NamedTuple, Optional

import numpy as np
import jax
import jax.numpy as jnp
from jax import lax
from jax.experimental import pallas as pl
from jax.experimental.pallas import tpu as pltpu

F32 = jnp.float32
BF16 = jnp.bfloat16

LANES = 128
HALF_LANES = LANES // 2
BF16_ROWS = 16
ROPE_THETA = 10000.0
EPS = 1e-6
NEG = -1e30
LOG2E = math.log2(math.e)
MASK_BIG = 2.0 ** 100

DIFF_HEADS = 4
DIFF_QK = 64
FOX_HEADS = 8
NSA_HEADS = 8
NSA_GROUPS = 2
NSA_DIM = 64
CMP_LEN = 32
CMP_STRIDE = 16
CMP_HIDDEN = 128
SLC_LEN = 64
SLC_TOPK = 16
WIN = 512
FORCE_SCORE = 1e4
MLA_HEADS = 8
MLA_NOPE = 64
MLA_ROPE = 32
MLA_V = 64
MLA_Q_RANK = 384
MLA_KV_RANK = 256
XA_HEADS = 4
XA_DIM = 128

VMEM_LIMIT = 56 * 1024 * 1024
ROW_TILE = 512
COL_CHUNK = 512
ATT_TILE = 512
KEY_TILE = 512
ATT_BUFS = 3
MASK_BLOCK = 256
SEL_TILE = 512
CUM_CHUNK = 256

_NT = (((1,), (1,)), ((), ()))


def _params(*sem):
    return pltpu.CompilerParams(dimension_semantics=sem, vmem_limit_bytes=VMEM_LIMIT)


def _sigmoid(x):
    return 1.0 / (1.0 + jnp.exp(-x))


def _split_bf16(x, terms):
    out = []
    r = x
    for _ in range(terms):
        h = r.astype(BF16)
        out.append(h)
        r = r - h.astype(F32)
    return out


def _rope_tables(S, half, lane_lo, lane_hi):
    pos = jnp.arange(S, dtype=F32)
    inv = 1.0 / (ROPE_THETA ** (jnp.arange(half, dtype=F32) / half))
    ang = pos[:, None] * inv[None, :]
    cos, sin = jnp.cos(ang), jnp.sin(ang)
    lane = np.arange(LANES)
    active = (lane >= lane_lo) & (lane < lane_hi)
    j = (lane - lane_lo) % (2 * half)
    lower = active & (j < half)
    upper = active & (j >= half)
    idx = j % half
    cos_t = jnp.where(active[None, :], cos[:, idx], 1.0)
    sin_a = jnp.where(lower[None, :], -sin[:, idx], 0.0)
    sin_b = jnp.where(upper[None, :], sin[:, idx], 0.0)
    return cos_t, sin_a, sin_b


class Seg(NamedTuple):
    start: int
    width: int
    dtype: object
    rope: Optional[str] = None
    scale: float = 1.0


def _linear(xs, ws, segs, *, gain=None, residual=None, ropes=None, seq=None, name="linear"):
    N = xs[0].shape[0]
    tm = min(ROW_TILE, N)
    assert N % tm == 0
    n_in = len(xs)
    has_gain = gain is not None
    has_res = residual is not None
    rope_keys = sorted({s.rope for s in segs if s.rope})
    halves = {k: ropes[k][1] for k in rope_keys}

    def kern(*refs):
        it = iter(refs)
        x_refs = [next(it) for _ in range(n_in)]
        w_refs = [next(it) for _ in range(n_in)]
        g_ref = next(it) if has_gain else None
        r_ref = next(it) if has_res else None
        tabs = {k: (next(it), next(it), next(it)) for k in rope_keys}
        o_refs = [next(it) for _ in segs]
        acts = []
        for j, xr in enumerate(x_refs):
            x = xr[...]
            if j == 0 and has_gain:
                xf = x.astype(F32)
                y = xf * lax.rsqrt(jnp.mean(xf * xf, axis=-1, keepdims=True) + EPS)
                acts.append((y * g_ref[...]).astype(BF16))
            else:
                acts.append(x.astype(BF16))
        for seg, o_ref in zip(segs, o_refs):
            for c0 in range(0, seg.width, COL_CHUNK):
                cw = min(COL_CHUNK, seg.width - c0)
                col = seg.start + c0
                acc = None
                for a, wr in zip(acts, w_refs):
                    d = jnp.dot(a, wr[:, col:col + cw], preferred_element_type=F32)
                    acc = d if acc is None else acc + d
                if has_res:
                    acc = acc + r_ref[:, col:col + cw]
                if seg.scale != 1.0:
                    acc = acc * seg.scale
                if seg.rope is None:
                    o_ref[:, c0:c0 + cw] = acc.astype(o_ref.dtype)
                else:
                    cos_r, sa_r, sb_r = tabs[seg.rope]
                    half = halves[seg.rope]
                    cos, sa, sb = cos_r[...], sa_r[...], sb_r[...]
                    for s0 in range(0, cw, LANES):
                        xs_ = acc[:, s0:s0 + LANES]
                        y = (xs_ * cos + pltpu.roll(xs_, LANES - half, 1) * sa
                             + pltpu.roll(xs_, half, 1) * sb)
                        o_ref[:, c0 + s0:c0 + s0 + LANES] = y.astype(o_ref.dtype)

    in_specs, args = [], []
    for x in xs:
        in_specs.append(pl.BlockSpec((tm, x.shape[1]), lambda i: (i, 0)))
        args.append(x)
    for w in ws:
        in_specs.append(pl.BlockSpec(w.shape, lambda i: (0, 0)))
        args.append(w)
    if has_gain:
        in_specs.append(pl.BlockSpec((1, gain.shape[-1]), lambda i: (0, 0)))
        args.append(gain.reshape(1, -1).astype(F32))
    if has_res:
        in_specs.append(pl.BlockSpec((tm, residual.shape[1]), lambda i: (i, 0)))
        args.append(residual)
    for k in rope_keys:
        assert seq % tm == 0
        nt = seq // tm
        for t in ropes[k][0]:
            in_specs.append(pl.BlockSpec((tm, LANES), lambda i, nt=nt: (i % nt, 0)))
            args.append(t)
    out_shape = [jax.ShapeDtypeStruct((N, s.width), s.dtype) for s in segs]
    out_specs = [pl.BlockSpec((tm, s.width), lambda i: (i, 0)) for s in segs]
    return pl.pallas_call(
        kern, grid=(N // tm,), in_specs=in_specs, out_specs=out_specs, out_shape=out_shape,
        compiler_params=_params("parallel"), name=name)(*args)


def _forget_key_terms(fz, b_f, B, S):
    ch = CUM_CHUNK
    place = np.zeros((3, LANES, LANES), np.float32)
    for h in range(FOX_HEADS):
        for j in range(3):
            place[j, h, 3 * h + j] = 1.0

    def kern(z_ref, b_ref, pl_ref, o_ref):
        r = lax.broadcasted_iota(jnp.int32, (ch, ch), 0)
        c = lax.broadcasted_iota(jnp.int32, (ch, ch), 1)
        tri = jnp.where(c <= r, 1.0, 0.0).astype(BF16)

        def body(j, carries):
            out = []
            for b, carry in enumerate(carries):
                r0 = pl.multiple_of(b * S + j * ch, ch)
                z = z_ref[pl.ds(r0, ch), :] + b_ref[...]
                logf = -(jnp.maximum(-z, 0.0) + jnp.log1p(jnp.exp(-jnp.abs(z))))
                cs = carry
                for part in _split_bf16(logf, 3):
                    cs = cs + jnp.dot(tri, part, preferred_element_type=F32)
                terms = jnp.zeros((ch, LANES), F32)
                for jj, part in enumerate(_split_bf16(cs * (-LOG2E), 3)):
                    terms = terms + jnp.dot(part, pl_ref[jj], preferred_element_type=F32)
                o_ref[pl.ds(r0, ch), :] = terms.astype(o_ref.dtype)
                out.append(cs[ch - 1:ch, :])
            return tuple(out)

        lax.fori_loop(0, S // ch, body, tuple(jnp.zeros((1, LANES), F32) for _ in range(B)))

    return pl.pallas_call(
        kern, grid=(1,),
        in_specs=[pl.BlockSpec((B * S, LANES), lambda g: (0, 0), pipeline_mode=pl.Buffered(1)),
                  pl.BlockSpec((1, LANES), lambda g: (0, 0)),
                  pl.BlockSpec(place.shape, lambda g: (0, 0, 0))],
        out_specs=pl.BlockSpec((B * S, LANES), lambda g: (0, 0)),
        out_shape=jax.ShapeDtypeStruct((B * S, LANES), BF16),
        compiler_params=_params("arbitrary"), name="forget_terms")(fz, b_f, jnp.asarray(place, BF16))


class Head(NamedTuple):
    q_off: int
    q_half: Optional[str]
    k_off: int
    v0: int
    v1: int


def _flash(q, k, vt, *, B, S, P, q_w, k_w, v_rows, heads, kind, out_dtype, tq=None, tk=None, n_bufs=None,
           ext=None, sel=None, lam=None, subln=None, lam_init=None, name="flash"):
    TQ, TK = tq or ATT_TILE, tk or KEY_TILE
    assert S % TQ == 0 and TQ % TK == 0
    diag = TQ // TK
    nq = S // TQ
    nh = len(heads)
    rows = heads[0].v1 - heads[0].v0
    out_w = nh // 2 * rows if kind == "diff" else nh * rows
    width = TK
    n_bufs = n_bufs or ATT_BUFS
    if kind == "win":
        assert WIN == TK and TQ == TK and S >= 2 * TK and n_bufs == 2
    has_ext = kind in ("fox", "slc")

    def kern(*refs):
        it = iter(refs)
        q_ref, k_ref, v_ref = next(it), next(it), next(it)
        ext_ref = next(it) if has_ext else None
        sel_ref = next(it) if kind == "slc" else None
        lam_ref, sub_ref = (next(it), next(it)) if kind == "diff" else (None, None)
        o_ref, m_ref, acc_ref, mx_ref = next(it), next(it), next(it), next(it)
        bufs = [next(it) for _ in range(n_bufs)]
        p_id = pl.program_id(1)
        i = pl.program_id(2)

        lane = lax.broadcasted_iota(jnp.int32, (TQ, LANES), 1)
        lo = lane < HALF_LANES
        q_all = q_ref[...]
        if kind == "slc":
            drop = ((sel_ref[...].astype(F32) - 1.0) * MASK_BIG).astype(BF16)
        qs = []
        for j, h in enumerate(heads):
            qh = q_all[:, h.q_off:h.q_off + LANES]
            if h.q_half == "lo":
                qh = jnp.where(lo, qh, jnp.zeros_like(qh))
            elif h.q_half == "hi":
                qh = jnp.where(lo, jnp.zeros_like(qh), qh)
            if kind == "fox":
                first = 3 * (nh * p_id + j)
                pick = jnp.where((lane >= first) & (lane < first + 3), 1.0, 0.0).astype(BF16)
                qh = jnp.concatenate([qh, pick], axis=1)
            if kind == "slc":
                qh = jnp.concatenate([qh, drop], axis=1)
            qs.append(qh)

        m_ref[...] = jnp.full(m_ref.shape, NEG, F32)
        acc_ref[...] = jnp.zeros(acc_ref.shape, F32)

        def block_kind(ahead, kb, qb):
            lo_rel = (kb - qb) * MASK_BLOCK - (MASK_BLOCK - 1)
            hi_rel = (kb - qb) * MASK_BLOCK + (MASK_BLOCK - 1)
            floor = ahead - WIN if kind == "win" else lo_rel - 1
            if lo_rel > ahead or hi_rel <= floor:
                return "dead"
            if hi_rel <= ahead and lo_rel > floor:
                return "full"
            return "part"

        def block_mask(ahead, kb, qb):
            rel = (lax.broadcasted_iota(jnp.int32, (MASK_BLOCK, MASK_BLOCK), 0)
                   - lax.broadcasted_iota(jnp.int32, (MASK_BLOCK, MASK_BLOCK), 1)
                   + (kb - qb) * MASK_BLOCK)
            msk = rel <= ahead
            if kind == "win":
                msk = msk & (rel > ahead - WIN)
            return msk

        kbs, qbs = range(width // MASK_BLOCK), range(TQ // MASK_BLOCK)
        blk = lambda b: slice(b * MASK_BLOCK, (b + 1) * MASK_BLOCK)

        def scores(k0, u, ahead=None):
            kx = k_ref[pl.ds(k0, width), :]
            ex = ext_ref[pl.ds(k0, width), :] if has_ext else None
            for hi, h in enumerate(heads):
                kk = kx[:, h.k_off:h.k_off + LANES]
                if ex is not None:
                    kk = jnp.concatenate([kk, ex], axis=1)
                if ahead is None:
                    st = lax.dot_general(kk, qs[hi], _NT, preferred_element_type=F32)
                    bufs[u][hi] = st
                    mx_ref[u * nh + hi] = jnp.max(st, axis=0, keepdims=True)
                    continue
                col_max = [None] * len(qbs)
                for kb in kbs:
                    live = [qb for qb in qbs if block_kind(ahead, kb, qb) != "dead"]
                    if not live:
                        continue
                    q_rows = slice(live[0] * MASK_BLOCK, (live[-1] + 1) * MASK_BLOCK)
                    st = lax.dot_general(kk[blk(kb)], qs[hi][q_rows], _NT, preferred_element_type=F32)
                    for qb in range(live[0], live[-1] + 1):
                        sb = st[:, blk(qb - live[0])]
                        if block_kind(ahead, kb, qb) == "part":
                            sb = jnp.where(block_mask(ahead, kb, qb), sb, NEG)
                        bufs[u][hi, blk(kb), blk(qb)] = sb
                        cm = jnp.max(sb, axis=0, keepdims=True)
                        col_max[qb] = cm if col_max[qb] is None else jnp.maximum(col_max[qb], cm)
                mx_ref[u * nh + hi] = jnp.concatenate(col_max, axis=1)

        def update(k0, u, ahead=None):
            for hi, h in enumerate(heads):
                m_prev = m_ref[hi]
                m_new = jnp.maximum(m_prev, mx_ref[u * nh + hi])
                alpha = jnp.exp2(m_prev - m_new)
                if ahead is None:
                    ones = jnp.ones((BF16_ROWS, width), BF16)
                    pt = jnp.exp2(bufs[u][hi] - m_new).astype(BF16)
                    lhs = jnp.concatenate([v_ref[h.v0:h.v1, pl.ds(k0, width)], ones], axis=0)
                    new = jnp.dot(lhs, pt, preferred_element_type=F32)
                else:
                    ones = jnp.ones((BF16_ROWS, MASK_BLOCK), BF16)
                    cols = []
                    for qb in qbs:
                        col = None
                        for kb in kbs:
                            if block_kind(ahead, kb, qb) == "dead":
                                continue
                            pt = jnp.exp2(bufs[u][hi, blk(kb), blk(qb)] - m_new[:, blk(qb)]).astype(BF16)
                            kstart = pl.multiple_of(k0 + kb * MASK_BLOCK, MASK_BLOCK)
                            lhs = jnp.concatenate(
                                [v_ref[h.v0:h.v1, pl.ds(kstart, MASK_BLOCK)], ones], axis=0)
                            part = jnp.dot(lhs, pt, preferred_element_type=F32)
                            col = part if col is None else col + part
                        cols.append(col)
                    new = jnp.concatenate(cols, axis=1)
                acc_ref[hi] = alpha * acc_ref[hi] + new
                m_ref[hi] = m_new

        tile = lambda t: pl.multiple_of(t * TK, TK)
        U = len(bufs)
        if kind == "win":
            @pl.when(i == 0)
            def _():
                scores(tile(0), 0, 0)
                update(tile(0), 0, 0)

            @pl.when(i > 0)
            def _():
                scores(tile(i - 1), 0, WIN)
                scores(tile(i), 1, 0)
                update(tile(i - 1), 0, WIN)
                update(tile(i), 1, 0)
        else:
            n_full = i * diag
            n_rounds = jnp.maximum(n_full - 1, 0) // U
            pl.when(n_full == 0)(lambda: scores(tile(0), 0, 0))
            pl.when(n_full > 0)(lambda: scores(tile(0), 0))

            def body(j, c):
                for u in range(U):
                    scores(tile(U * j + u + 1), (u + 1) % U)
                    update(tile(U * j + u), u)
                return c
            lax.fori_loop(0, n_rounds, body, 0)
            t0 = U * n_rounds

            def tail(left):
                ahead = lambda u: -(u - left) * TK if u >= left else None
                last = left + diag - 1
                for u in range(last):
                    scores(tile(t0 + u + 1), (u + 1) % U, ahead(u + 1))
                    update(tile(t0 + u), u % U, ahead(u))
                update(tile(t0 + last), last % U, ahead(last))

            for left in range(U + 1):
                pl.when(n_full - t0 == left)(lambda left=left: tail(left))

        outs = []
        for hi in range(nh):
            acc = acc_ref[hi]
            outs.append(acc[:rows] / acc[rows:rows + 1])
        if kind == "diff":
            lm = lam_ref[...]
            la = jnp.sum(lm[0:1] * lm[1:2], axis=1, keepdims=True)
            lb = jnp.sum(lm[2:3] * lm[3:4], axis=1, keepdims=True)
            lam_full = jnp.exp(la) - jnp.exp(lb) + lam_init
            for g in range(nh // 2):
                oa = jnp.transpose(outs[2 * g] - lam_full * outs[2 * g + 1])
                y = oa * lax.rsqrt(jnp.mean(oa * oa, axis=-1, keepdims=True) + EPS)
                y = (y * sub_ref[...]) * (1.0 - lam_init)
                o_ref[:, g * rows:(g + 1) * rows] = y.astype(o_ref.dtype)
        else:
            o_ref[...] = jnp.transpose(jnp.concatenate(outs, axis=0)).astype(o_ref.dtype)

    in_specs = [
        pl.BlockSpec((TQ, q_w), lambda b, p, i: (b * nq + i, p)),
        pl.BlockSpec((S, k_w), lambda b, p, i: (b, p)),
        pl.BlockSpec((v_rows, S), lambda b, p, i: (p, b)),
    ]
    args = [q, k, vt]
    if kind == "fox":
        in_specs.append(pl.BlockSpec((S, LANES), lambda b, p, i: (b, 0)))
        args.append(ext)
    if kind == "slc":
        in_specs.append(pl.BlockSpec((S, LANES), lambda b, p, i: (0, 0)))
        in_specs.append(pl.BlockSpec((TQ, LANES), lambda b, p, i: (b * nq + i, p)))
        args += [ext, sel]
    if kind == "diff":
        in_specs.append(pl.BlockSpec(lam.shape, lambda b, p, i: (0, 0)))
        in_specs.append(pl.BlockSpec((1, LANES), lambda b, p, i: (0, 0)))
        args += [lam, subln.reshape(1, LANES)]
    return pl.pallas_call(
        kern, grid=(B, P, nq), in_specs=in_specs,
        out_specs=pl.BlockSpec((TQ, out_w), lambda b, p, i: (b * nq + i, p)),
        out_shape=jax.ShapeDtypeStruct((B * S, P * out_w), out_dtype),
        scratch_shapes=[pltpu.VMEM((nh, 1, TQ), F32),
                        pltpu.VMEM((nh, rows + BF16_ROWS, TQ), F32),
                        pltpu.VMEM((n_bufs * nh, 1, TQ), F32)]
        + [pltpu.VMEM((nh, width, TQ), F32)] * n_bufs,
        compiler_params=_params("parallel", "parallel", "arbitrary"), name=name)(*args)


_DIFF_HEADS = tuple(Head(s * LANES, half, s * LANES, s * LANES, (s + 1) * LANES)
                    for s in range(2) for half in ("lo", "hi"))
_PAIR_HEADS = tuple(Head(s * LANES, half, s * LANES, (2 * s + j) * HALF_LANES, (2 * s + j + 1) * HALF_LANES)
                    for s in range(2) for j, half in enumerate(("lo", "hi")))
_QUAD_HEADS = tuple(Head(s * LANES, half, 0, 0, HALF_LANES) for s in range(2) for half in ("lo", "hi"))
_WIDE_HEADS = tuple(Head(j * LANES, None, j * LANES, j * HALF_LANES, (j + 1) * HALF_LANES) for j in range(4))


def _nsa_compress(xk, xv, pe, w1x, w2d):
    B, R, K = xk.shape
    G = NSA_GROUPS

    def kern(xk_ref, xv_ref, pe_ref, w1_ref, w2_ref, k_ref, vt_ref):
        for t, x_ref in enumerate((xk_ref, xv_ref)):
            x = x_ref[...]
            xa = (x + pe_ref[t, 0]).astype(BF16)
            xb = (x + pe_ref[t, 1]).astype(BF16)
            for g in range(G):
                a = jnp.dot(xa, w1_ref[t, 0, g], preferred_element_type=F32)
                b = jnp.dot(xb, w1_ref[t, 1, g], preferred_element_type=F32)
                h = a + pltpu.roll(b, R - 1, 0)
                hs = h * _sigmoid(h)
                o = jnp.dot(hs.astype(BF16), w2_ref[t], preferred_element_type=F32)
                if t == 0:
                    k_ref[g] = o
                else:
                    vt_ref[g] = jnp.transpose(o)

    return pl.pallas_call(
        kern, grid=(B,),
        in_specs=[pl.BlockSpec((None, R, K), lambda b: (b, 0, 0)),
                  pl.BlockSpec((None, R, K), lambda b: (b, 0, 0)),
                  pl.BlockSpec(pe.shape, lambda b: (0, 0, 0, 0)),
                  pl.BlockSpec(w1x.shape, lambda b: (0, 0, 0, 0, 0)),
                  pl.BlockSpec(w2d.shape, lambda b: (0, 0, 0))],
        out_specs=[pl.BlockSpec((None, G, R, LANES), lambda b: (b, 0, 0, 0)),
                   pl.BlockSpec((None, G, LANES, R), lambda b: (b, 0, 0, 0))],
        out_shape=[jax.ShapeDtypeStruct((B, G, R, LANES), F32),
                   jax.ShapeDtypeStruct((B, G, LANES, R), F32)],
        compiler_params=_params("parallel"), name="nsa_compress")(xk, xv, pe, w1x, w2d)


def _nsa_cmp_select(q, cmp_k, cmp_vt, ov_t, *, B, S):
    T = SEL_TILE
    nq = S // T
    R = cmp_k.shape[2]
    G = NSA_GROUPS
    d = NSA_DIM
    n_sel = min(SLC_TOPK, S // SLC_LEN)
    assert n_sel >= 3 and FORCE_SCORE > NSA_HEADS // G

    bucket = min(LANES, R)
    assert R % bucket == 0 and bucket % (SLC_LEN // CMP_STRIDE) == 0

    def kern(q_ref, kc_ref, vt_ref, ov_ref, o_ref, sel_ref):
        i = pl.program_id(2)
        lane = lax.broadcasted_iota(jnp.int32, (T, LANES), 1)
        lo = lane < HALF_LANES
        q_all = q_ref[...]
        qs = []
        for hh in range(4):
            qh = q_all[:, (hh // 2) * LANES:(hh // 2 + 1) * LANES]
            qs.append(jnp.where(lo, qh, jnp.zeros_like(qh)) if hh % 2 == 0
                      else jnp.where(lo, jnp.zeros_like(qh), qh))

        def variant(rv):
            nb = rv * CMP_STRIDE // SLC_LEN
            tq = lax.broadcasted_iota(jnp.int32, (rv, T), 1) + i * T
            cend = lax.broadcasted_iota(jnp.int32, (rv, T), 0) * CMP_STRIDE + (CMP_LEN - 1)
            vis = cend <= tq
            kc = kc_ref[:rv, :].astype(BF16)
            vt = vt_ref[:d, :rv].astype(BF16)
            psum = jnp.zeros((rv, T), F32)
            o_heads = []
            for hh in range(4):
                st = lax.dot_general(kc, qs[hh], _NT, preferred_element_type=F32)
                st = jnp.where(vis, st, -jnp.inf)
                m = jnp.max(st, axis=0, keepdims=True)
                e = jnp.exp2(st - jnp.where(m > -jnp.inf, m, 0.0))
                p = e / jnp.maximum(jnp.sum(e, axis=0, keepdims=True), 1e-30)
                o_heads.append(jnp.dot(vt, p.astype(BF16), preferred_element_type=F32))
                psum = psum + p
            o_ref[...] = jnp.transpose(jnp.concatenate(o_heads, axis=0)).astype(o_ref.dtype)
            imp = jnp.zeros((nb, T), F32)
            for part in _split_bf16(psum, 2):
                imp = imp + jnp.dot(ov_ref[:nb, :rv], part, preferred_element_type=F32)
            blk = lax.broadcasted_iota(jnp.int32, (nb, T), 0)
            cur = (lax.broadcasted_iota(jnp.int32, (nb, T), 1) + i * T) // SLC_LEN
            forced = (blk == 0) | (blk == cur) | (blk == cur - 1)
            work = jnp.where(forced, -jnp.inf, imp)
            work = jnp.where(blk > cur, NEG, work)
            blk_f = blk.astype(F32)
            for _ in range(n_sel - 3):
                mx = jnp.max(work, axis=0, keepdims=True)
                first = jnp.min(jnp.where(work == mx, blk_f, float(LANES)), axis=0, keepdims=True)
                work = jnp.where(blk_f == first, -jnp.inf, work)
            sel = jnp.where(work == -jnp.inf, 1.0, 0.0)
            if nb < LANES:
                sel = jnp.concatenate([sel, jnp.zeros((LANES - nb, T), F32)], axis=0)
            sel_ref[...] = jnp.transpose(sel).astype(sel_ref.dtype)

        need = (i + 1) * (T // CMP_STRIDE)
        which = (need - 1) // bucket
        for bk in range(R // bucket):
            pl.when(which == bk)(lambda bk=bk: variant((bk + 1) * bucket))

    return pl.pallas_call(
        kern, grid=(B, G, nq),
        in_specs=[pl.BlockSpec((T, 2 * LANES), lambda b, g, i: (b * nq + i, g)),
                  pl.BlockSpec((None, None, R, LANES), lambda b, g, i: (b, g, 0, 0)),
                  pl.BlockSpec((None, None, LANES, R), lambda b, g, i: (b, g, 0, 0)),
                  pl.BlockSpec((LANES, R), lambda b, g, i: (0, 0))],
        out_specs=[pl.BlockSpec((T, 2 * LANES), lambda b, g, i: (b * nq + i, g)),
                   pl.BlockSpec((T, LANES), lambda b, g, i: (b * nq + i, g))],
        out_shape=[jax.ShapeDtypeStruct((B * S, G * 2 * LANES), BF16),
                   jax.ShapeDtypeStruct((B * S, G * LANES), BF16)],
        compiler_params=_params("parallel", "parallel", "parallel"), name="nsa_cmp_select")(
            q, cmp_k, cmp_vt, ov_t)


def _odd_out(o_cmp, o_slc, o_win, gz, expand, o_mla, w_nsa, w_mla, x):
    N, C = o_cmp.shape
    D = x.shape[1]
    tm = min(ROW_TILE, N)

    def kern(c_ref, s_ref, w_ref, g_ref, e_ref, m_ref, wn_ref, wm_ref, x_ref, o_ref):
        gate = _sigmoid(g_ref[...])
        parts = _split_bf16(gate, 2)
        acc = jnp.zeros((tm, C), F32)
        for j, br in enumerate((c_ref, s_ref, w_ref)):
            gj = sum(jnp.dot(part, e_ref[j], preferred_element_type=F32) for part in parts)
            acc = acc + gj * br[...].astype(F32)
        o_nsa = acc.astype(BF16)
        o_mla_ = m_ref[...]
        for c0 in range(0, D, COL_CHUNK):
            cols = slice(c0, c0 + COL_CHUNK)
            y = (jnp.dot(o_nsa, wn_ref[:, cols], preferred_element_type=F32)
                 + jnp.dot(o_mla_, wm_ref[:, cols], preferred_element_type=F32))
            o_ref[:, cols] = x_ref[:, cols] + y

    row = pl.BlockSpec((tm, C), lambda i: (i, 0))
    wide = pl.BlockSpec((tm, D), lambda i: (i, 0))
    return pl.pallas_call(
        kern, grid=(N // tm,),
        in_specs=[row, row, row, pl.BlockSpec((tm, LANES), lambda i: (i, 0)),
                  pl.BlockSpec(expand.shape, lambda i: (0, 0, 0)),
                  pl.BlockSpec((tm, o_mla.shape[1]), lambda i: (i, 0)),
                  pl.BlockSpec(w_nsa.shape, lambda i: (0, 0)),
                  pl.BlockSpec(w_mla.shape, lambda i: (0, 0)), wide],
        out_specs=wide, out_shape=jax.ShapeDtypeStruct((N, D), F32),
        compiler_params=_params("parallel"), name="odd_out")(
            o_cmp, o_slc, o_win, gz, expand, o_mla, w_nsa, w_mla, x)


def _mla_kv(ckv, gain, wk, wv, kr):
    N, K = ckv.shape
    tm = min(ROW_TILE, N)
    H = MLA_HEADS

    def kern(c_ref, g_ref, wk_ref, wv_ref, kr_ref, k_ref, v_ref):
        xf = c_ref[...]
        y = xf * lax.rsqrt(jnp.mean(xf * xf, axis=-1, keepdims=True) + EPS)
        a = (y * g_ref[...]).astype(BF16)
        kr2 = jnp.concatenate([kr_ref[...]] * 2, axis=1)
        for h in range(0, H, 2):
            cols = slice(h * LANES, (h + 2) * LANES)
            kh = jnp.dot(a, wk_ref[:, cols], preferred_element_type=F32)
            k_ref[:, cols] = (kh + kr2).astype(k_ref.dtype)
        v_ref[...] = jnp.dot(a, wv_ref[...], preferred_element_type=F32).astype(v_ref.dtype)

    return pl.pallas_call(
        kern, grid=(N // tm,),
        in_specs=[pl.BlockSpec((tm, K), lambda i: (i, 0)),
                  pl.BlockSpec((1, K), lambda i: (0, 0)),
                  pl.BlockSpec(wk.shape, lambda i: (0, 0)),
                  pl.BlockSpec(wv.shape, lambda i: (0, 0)),
                  pl.BlockSpec((tm, LANES), lambda i: (i, 0))],
        out_specs=[pl.BlockSpec((tm, H * LANES), lambda i: (i, 0)),
                   pl.BlockSpec((tm, H * MLA_V), lambda i: (i, 0))],
        out_shape=[jax.ShapeDtypeStruct((N, H * LANES), BF16),
                   jax.ShapeDtypeStruct((N, H * MLA_V), BF16)],
        compiler_params=_params("parallel"), name="mla_kv")(
            ckv, gain.reshape(1, K).astype(F32), wk, wv, kr)


def _mem_attn(x, gain, wq, wo, mem_k, mem_v, *, S):
    N, D = x.shape
    tm = min(ROW_TILE, S)
    M = mem_k.shape[1]
    per_b = S // tm
    scale = XA_DIM ** -0.5

    def kern(x_ref, g_ref, wq_ref, wo_ref, k_ref, v_ref, o_ref):
        xf = x_ref[...]
        y = xf * lax.rsqrt(jnp.mean(xf * xf, axis=-1, keepdims=True) + EPS)
        a = (y * g_ref[...]).astype(BF16)
        q = (jnp.dot(a, wq_ref[...], preferred_element_type=F32) * scale).astype(BF16)
        heads = []
        for h in range(XA_HEADS):
            sl = slice(h * XA_DIM, (h + 1) * XA_DIM)
            s = lax.dot_general(q[:, sl], k_ref[:, sl], _NT, preferred_element_type=F32)
            e = jnp.exp(s - jnp.max(s, axis=1, keepdims=True))
            p = e / jnp.sum(e, axis=1, keepdims=True)
            heads.append(jnp.dot(p.astype(BF16), v_ref[:, sl], preferred_element_type=F32).astype(BF16))
        o = jnp.concatenate(heads, axis=1)
        o_ref[...] = xf + jnp.dot(o, wo_ref[...], preferred_element_type=F32)

    return pl.pallas_call(
        kern, grid=(N // tm,),
        in_specs=[pl.BlockSpec((tm, D), lambda i: (i, 0)),
                  pl.BlockSpec((1, D), lambda i: (0, 0)),
                  pl.BlockSpec(wq.shape, lambda i: (0, 0)),
                  pl.BlockSpec(wo.shape, lambda i: (0, 0)),
                  pl.BlockSpec((None, M, XA_HEADS * XA_DIM), lambda i: (i // per_b, 0, 0)),
                  pl.BlockSpec((None, M, XA_HEADS * XA_DIM), lambda i: (i // per_b, 0, 0))],
        out_specs=pl.BlockSpec((tm, D), lambda i: (i, 0)),
        out_shape=jax.ShapeDtypeStruct((N, D), F32),
        compiler_params=_params("parallel"), name="mem_attn")(
            x, gain.reshape(1, D).astype(F32), wq, wo, mem_k, mem_v)


def _ffn(x, gain, w13, w2, out_gain=None):
    N, D = x.shape
    FF = w2.shape[0]
    tm = min(ROW_TILE, N)
    chunk = 2 * LANES
    assert FF % chunk == 0

    final = out_gain is not None

    def kern(x_ref, g_ref, w13_ref, w2_ref, *rest):
        og_ref, o_ref = rest if final else (None, rest[0])
        xf = x_ref[...]
        y = xf * lax.rsqrt(jnp.mean(xf * xf, axis=-1, keepdims=True) + EPS)
        a = (y * g_ref[...]).astype(BF16)
        acc = xf
        for c in range(0, FF, chunk):
            g = jnp.dot(a, w13_ref[:, c:c + chunk], preferred_element_type=F32)
            u = jnp.dot(a, w13_ref[:, FF + c:FF + c + chunk], preferred_element_type=F32)
            hdn = (g * _sigmoid(g) * u).astype(BF16)
            acc = acc + jnp.dot(hdn, w2_ref[c:c + chunk, :], preferred_element_type=F32)
        if final:
            acc = acc * lax.rsqrt(jnp.mean(acc * acc, axis=-1, keepdims=True) + EPS) * og_ref[...]
        o_ref[...] = acc

    vec = pl.BlockSpec((1, D), lambda i: (0, 0))
    in_specs = [pl.BlockSpec((tm, D), lambda i: (i, 0)), vec,
                pl.BlockSpec(w13.shape, lambda i: (0, 0), pipeline_mode=pl.Buffered(1)),
                pl.BlockSpec(w2.shape, lambda i: (0, 0), pipeline_mode=pl.Buffered(1))]
    args = [x, gain.reshape(1, D).astype(F32), w13, w2]
    if final:
        in_specs.append(vec)
        args.append(out_gain.reshape(1, D).astype(F32))
    return pl.pallas_call(
        kern, grid=(N // tm,), in_specs=in_specs,
        out_specs=pl.BlockSpec((tm, D), lambda i: (i, 0)),
        out_shape=jax.ShapeDtypeStruct((N, D), F32),
        compiler_params=_params("parallel"), name="ffn")(*args)


def _pad_cols(w, width):
    return jnp.pad(w, ((0, 0), (0, width - w.shape[1])))


def _even_mixer(x, gain, w_in, b_f, lam, subln, w_out, layer_idx, ropes, B, S):
    D = x.shape[1]
    blk = DIFF_HEADS * 2 * DIFF_QK
    w = _pad_cols(w_in, 6 * blk + LANES).astype(BF16)
    sc = DIFF_QK ** -0.5 * LOG2E
    segs = [Seg(0, blk, BF16, "r64", sc), Seg(blk, blk, BF16, "r64"), Seg(2 * blk, blk, BF16),
            Seg(3 * blk, blk, BF16, None, sc), Seg(4 * blk, blk, BF16), Seg(5 * blk, blk, BF16),
            Seg(6 * blk, LANES, F32)]
    aq, ak, av, fq, fk, fv, fz = _linear([x], [w], segs, gain=gain, ropes=ropes, seq=S, name="even_in")
    lam_init = 0.8 - 0.6 * math.exp(-0.3 * layer_idx)
    oa = _flash(aq, ak, av.T, B=B, S=S, P=DIFF_HEADS // 2, q_w=2 * LANES, k_w=2 * LANES, v_rows=2 * LANES,
                heads=_DIFF_HEADS, kind="diff", out_dtype=BF16, lam=lam.astype(F32),
                subln=subln.astype(F32), lam_init=lam_init, name="diff_attn")
    terms = _forget_key_terms(fz, _pad_cols(b_f.reshape(1, -1), LANES).astype(F32), B, S)
    of = _flash(fq, fk, fv.T, B=B, S=S, P=FOX_HEADS // 4, q_w=2 * LANES, k_w=2 * LANES, v_rows=2 * LANES,
                heads=_PAIR_HEADS, kind="fox", out_dtype=BF16, ext=terms, name="fox_attn")
    wo = w_out.astype(BF16)
    (y,) = _linear([oa, of], [wo[:blk], wo[blk:]], [Seg(0, D, F32)], residual=x, name="even_out")
    return y


def _odd_in_weight(w_in):
    d = NSA_DIM
    o = np.cumsum((0, NSA_HEADS * d) + (NSA_GROUPS * d,) * 6 + (NSA_HEADS * 3, MLA_Q_RANK, MLA_KV_RANK, MLA_ROPE))
    nq, kc, vc, ks, vs, kw, vw, gz, cq, ckv, kr = [w_in[:, o[j]:o[j + 1]] for j in range(11)]

    def dup(wg):
        return jnp.concatenate([wg[:, :d], wg[:, :d], wg[:, d:], wg[:, d:]], axis=1)

    zeros = lambda n: jnp.zeros((w_in.shape[0], n), w_in.dtype)
    kr_slab = jnp.concatenate([zeros(MLA_NOPE), kr, zeros(LANES - MLA_NOPE - MLA_ROPE)], axis=1)
    cols = [nq, kc, vc, dup(ks), vs, dup(kw), vw, _pad_cols(gz, LANES), cq, ckv, kr_slab]
    return jnp.concatenate(cols, axis=1).astype(BF16)


def _odd_mixer(x, gain, w_in, cmp_pos, cmp_w1, cmp_w2, q_norm, kv_norm, w_uq, w_ukv, w_out, ropes, B, S):
    D = x.shape[1]
    G, d = NSA_GROUPS, NSA_DIM
    w = _odd_in_weight(w_in)
    sc = d ** -0.5 * LOG2E
    widths = [(NSA_HEADS * d, BF16, "r64", sc), (LANES, F32, "r64", 1.0), (LANES, F32, None, 1.0),
              (2 * LANES, BF16, "r64", 1.0), (LANES, BF16, None, 1.0),
              (2 * LANES, BF16, "r64", 1.0), (LANES, BF16, None, 1.0),
              (LANES, F32, None, 1.0), (MLA_Q_RANK, F32, None, 1.0), (MLA_KV_RANK, F32, None, 1.0),
              (LANES, F32, "mla", 1.0)]
    segs, start = [], 0
    for wd, dt, rp, s_ in widths:
        segs.append(Seg(start, wd, dt, rp, s_))
        start += wd
    q, kc, vc, ks, vs, kw, vw, gz, cq, ckv, kr = _linear(
        [x], [w], segs, gain=gain, ropes=ropes, seq=S, name="odd_in")

    R = S // CMP_STRIDE

    K = CMP_STRIDE * G * d
    half = CMP_LEN // 2
    pe = jnp.broadcast_to(cmp_pos.reshape(2, 2, half, 1, d), (2, 2, half, G, d)).reshape(2, 2, 1, K).astype(F32)
    eye = jnp.eye(G, dtype=cmp_w1.dtype)
    w1x = (cmp_w1.reshape(2, 2, 1, half, 1, d, CMP_HIDDEN) * eye[None, None, :, None, :, None, None])
    w1x = w1x.reshape(2, 2, G, K, CMP_HIDDEN).astype(BF16)
    w2d = jnp.concatenate([cmp_w2, cmp_w2], axis=-1).astype(BF16)
    cmp_k, cmp_vt = _nsa_compress(kc.reshape(B, R, K), vc.reshape(B, R, K), pe, w1x, w2d)
    cs = np.arange(R) * CMP_STRIDE
    bs = np.arange(LANES) * SLC_LEN
    ov = ((cs[:, None] < bs[None, :] + SLC_LEN) & (cs[:, None] + CMP_LEN > bs[None, :])
          & (np.arange(R)[:, None] < R - 1) & (bs[None, :] < S))
    o_cmp, sel = _nsa_cmp_select(q, cmp_k, cmp_vt, jnp.asarray(ov.T.astype(np.float32), BF16), B=B, S=S)
    onehot = (np.arange(S)[:, None] // SLC_LEN == np.arange(LANES)[None, :]).astype(np.float32)
    o_slc = _flash(q, ks, vs.T, B=B, S=S, P=G, q_w=2 * LANES, k_w=LANES, v_rows=HALF_LANES,
                   heads=_QUAD_HEADS, kind="slc", out_dtype=BF16, ext=jnp.asarray(onehot, BF16),
                   sel=sel, name="nsa_slc")
    o_win = _flash(q, kw, vw.T, B=B, S=S, P=G, q_w=2 * LANES, k_w=LANES, v_rows=HALF_LANES,
                   heads=_QUAD_HEADS, kind="win", out_dtype=BF16, tq=WIN, tk=WIN, n_bufs=2, name="nsa_win")
    ex = np.zeros((3, LANES, NSA_HEADS * d), np.float32)
    for hh in range(NSA_HEADS):
        for j in range(3):
            ex[j, hh * 3 + j, hh * d:(hh + 1) * d] = 1.0

    H = MLA_HEADS
    qk = MLA_NOPE + MLA_ROPE
    wq_slab = jnp.pad(w_uq.reshape(MLA_Q_RANK, H, qk), ((0, 0), (0, 0), (0, LANES - qk)))
    wq_slab = wq_slab.reshape(MLA_Q_RANK, H * LANES).astype(BF16)
    (qm,) = _linear([cq], [wq_slab], [Seg(0, H * LANES, BF16, "mla", qk ** -0.5 * LOG2E)],
                    gain=q_norm, ropes=ropes, seq=S, name="mla_q")
    wkv = w_ukv.reshape(MLA_KV_RANK, H, MLA_NOPE + MLA_V)
    wk = jnp.pad(wkv[:, :, :MLA_NOPE], ((0, 0), (0, 0), (0, LANES - MLA_NOPE)))
    wk = wk.reshape(MLA_KV_RANK, H * LANES).astype(BF16)
    wv = wkv[:, :, MLA_NOPE:].reshape(MLA_KV_RANK, H * MLA_V).astype(BF16)
    km, vm = _mla_kv(ckv, kv_norm, wk, wv, kr)
    o_mla = _flash(qm, km, vm.T, B=B, S=S, P=H // 4, q_w=4 * LANES, k_w=4 * LANES, v_rows=2 * LANES,
                   heads=_WIDE_HEADS, kind="mla", out_dtype=BF16, name="mla_attn")
    wo = w_out.astype(BF16)
    half = NSA_HEADS * d
    return _odd_out(o_cmp, o_slc, o_win, gz, jnp.asarray(ex, BF16), o_mla, wo[:half], wo[half:], x)


def kernel(x, mem, mem_norm, norm_mix, norm_mem, norm_ffn, ev_w_in, ev_b_f, ev_lam, ev_subln, ev_w_out, od_w_in, nsa_cmp_pos, nsa_cmp_w1, nsa_cmp_w2, mla_q_norm, mla_kv_norm, mla_w_uq, mla_w_ukv, od_w_out, xa_wq, xa_wkv, xa_wo, ffn_w13, ffn_w2, final_norm):
    B, S, D = x.shape
    M = mem.shape[1]
    depth = norm_mix.shape[0]
    ropes = {
        "r64": (_rope_tables(S, NSA_DIM // 2, 0, LANES), NSA_DIM // 2),
        "mla": (_rope_tables(S, MLA_ROPE // 2, MLA_NOPE, MLA_NOPE + MLA_ROPE), MLA_ROPE // 2),
    }
    xa_w = XA_HEADS * XA_DIM
    h = x.reshape(B * S, D)
    mem2 = mem.reshape(B * M, D)
    for li in range(depth):
        j = li // 2
        if li % 2 == 0:
            h = _even_mixer(h, norm_mix[li], ev_w_in[j], ev_b_f[j], ev_lam[j], ev_subln[j], ev_w_out[j],
                            li, ropes, B, S)
        else:
            h = _odd_mixer(h, norm_mix[li], od_w_in[j], nsa_cmp_pos[j], nsa_cmp_w1[j], nsa_cmp_w2[j],
                           mla_q_norm[j], mla_kv_norm[j], mla_w_uq[j], mla_w_ukv[j], od_w_out[j],
                           ropes, B, S)
        mk, mv = _linear([mem2], [xa_wkv[li].astype(BF16)],
                         [Seg(0, xa_w, BF16), Seg(xa_w, xa_w, BF16)], gain=mem_norm, name="mem_kv")
        h = _mem_attn(h, norm_mem[li], xa_wq[li].astype(BF16), xa_wo[li].astype(BF16),
                      mk.reshape(B, M, xa_w), mv.reshape(B, M, xa_w), S=S)
        h = _ffn(h, norm_ffn[li], ffn_w13[li].astype(BF16), ffn_w2[li].astype(BF16),
                 out_gain=final_norm if li == depth - 1 else None)
    return h.reshape(B, S, D)
```

```python
import math
from typing import NamedTuple, Optional

import numpy as np
import jax
import jax.numpy as jnp
from jax import lax
from jax.experimental import pallas as pl
from jax.experimental.pallas import tpu as pltpu

F32 = jnp.float32
BF16 = jnp.bfloat16

LANES = 128
HALF_LANES = LANES // 2
BF16_ROWS = 16
ROPE_THETA = 10000.0
EPS = 1e-6
NEG = -1e30
LOG2E = math.log2(math.e)
MASK_BIG = 2.0 ** 100

DIFF_HEADS = 4
DIFF_QK = 64
FOX_HEADS = 8
NSA_HEADS = 8
NSA_GROUPS = 2
NSA_DIM = 64
CMP_LEN = 32
CMP_STRIDE = 16
CMP_HIDDEN = 128
SLC_LEN = 64
SLC_TOPK = 16
WIN = 512
FORCE_SCORE = 1e4
MLA_HEADS = 8
MLA_NOPE = 64
MLA_ROPE = 32
MLA_V = 64
MLA_Q_RANK = 384
MLA_KV_RANK = 256
XA_HEADS = 4
XA_DIM = 128

VMEM_LIMIT = 56 * 1024 * 1024
ROW_TILE = 512
COL_CHUNK = 512
ATT_TILE = 512
KEY_TILE = 512
ATT_BUFS = 3
MASK_BLOCK = 256
SEL_TILE = 1024
CUM_CHUNK = 256

_NT = (((1,), (1,)), ((), ()))


def _params(*sem):
    return pltpu.CompilerParams(dimension_semantics=sem, vmem_limit_bytes=VMEM_LIMIT)


def _sigmoid(x):
    return 1.0 / (1.0 + jnp.exp(-x))


def _split_bf16(x, terms):
    out = []
    r = x
    for _ in range(terms):
        h = r.astype(BF16)
        out.append(h)
        r = r - h.astype(F32)
    return out


def _rope_tables(S, half, lane_lo, lane_hi):
    pos = jnp.arange(S, dtype=F32)
    inv = 1.0 / (ROPE_THETA ** (jnp.arange(half, dtype=F32) / half))
    ang = pos[:, None] * inv[None, :]
    cos, sin = jnp.cos(ang), jnp.sin(ang)
    lane = np.arange(LANES)
    active = (lane >= lane_lo) & (lane < lane_hi)
    j = (lane - lane_lo) % (2 * half)
    lower = active & (j < half)
    upper = active & (j >= half)
    idx = j % half
    cos_t = jnp.where(active[None, :], cos[:, idx], 1.0)
    sin_a = jnp.where(lower[None, :], -sin[:, idx], 0.0)
    sin_b = jnp.where(upper[None, :], sin[:, idx], 0.0)
    return cos_t, sin_a, sin_b


class Seg(NamedTuple):
    start: int
    width: int
    dtype: object
    rope: Optional[str] = None
    scale: float = 1.0


def _linear(xs, ws, segs, *, gain=None, residual=None, ropes=None, seq=None, name="linear"):
    N = xs[0].shape[0]
    tm = min(ROW_TILE, N)
    assert N % tm == 0
    n_in = len(xs)
    has_gain = gain is not None
    has_res = residual is not None
    rope_keys = sorted({s.rope for s in segs if s.rope})
    halves = {k: ropes[k][1] for k in rope_keys}

    def kern(*refs):
        it = iter(refs)
        x_refs = [next(it) for _ in range(n_in)]
        w_refs = [next(it) for _ in range(n_in)]
        g_ref = next(it) if has_gain else None
        r_ref = next(it) if has_res else None
        tabs = {k: (next(it), next(it), next(it)) for k in rope_keys}
        o_refs = [next(it) for _ in segs]
        acts = []
        for j, xr in enumerate(x_refs):
            x = xr[...]
            if j == 0 and has_gain:
                xf = x.astype(F32)
                y = xf * lax.rsqrt(jnp.mean(xf * xf, axis=-1, keepdims=True) + EPS)
                acts.append((y * g_ref[...]).astype(BF16))
            else:
                acts.append(x.astype(BF16))
        for seg, o_ref in zip(segs, o_refs):
            for c0 in range(0, seg.width, COL_CHUNK):
                cw = min(COL_CHUNK, seg.width - c0)
                col = seg.start + c0
                acc = None
                for a, wr in zip(acts, w_refs):
                    d = jnp.dot(a, wr[:, col:col + cw], preferred_element_type=F32)
                    acc = d if acc is None else acc + d
                if has_res:
                    acc = acc + r_ref[:, col:col + cw]
                if seg.scale != 1.0:
                    acc = acc * seg.scale
                if seg.rope is None:
                    o_ref[:, c0:c0 + cw] = acc.astype(o_ref.dtype)
                else:
                    cos_r, sa_r, sb_r = tabs[seg.rope]
                    half = halves[seg.rope]
                    cos, sa, sb = cos_r[...], sa_r[...], sb_r[...]
                    for s0 in range(0, cw, LANES):
                        xs_ = acc[:, s0:s0 + LANES]
                        y = (xs_ * cos + pltpu.roll(xs_, LANES - half, 1) * sa
                             + pltpu.roll(xs_, half, 1) * sb)
                        o_ref[:, c0 + s0:c0 + s0 + LANES] = y.astype(o_ref.dtype)

    in_specs, args = [], []
    for x in xs:
        in_specs.append(pl.BlockSpec((tm, x.shape[1]), lambda i: (i, 0)))
        args.append(x)
    for w in ws:
        in_specs.append(pl.BlockSpec(w.shape, lambda i: (0, 0)))
        args.append(w)
    if has_gain:
        in_specs.append(pl.BlockSpec((1, gain.shape[-1]), lambda i: (0, 0)))
        args.append(gain.reshape(1, -1).astype(F32))
    if has_res:
        in_specs.append(pl.BlockSpec((tm, residual.shape[1]), lambda i: (i, 0)))
        args.append(residual)
    for k in rope_keys:
        assert seq % tm == 0
        nt = seq // tm
        for t in ropes[k][0]:
            in_specs.append(pl.BlockSpec((tm, LANES), lambda i, nt=nt: (i % nt, 0)))
            args.append(t)
    out_shape = [jax.ShapeDtypeStruct((N, s.width), s.dtype) for s in segs]
    out_specs = [pl.BlockSpec((tm, s.width), lambda i: (i, 0)) for s in segs]
    return pl.pallas_call(
        kern, grid=(N // tm,), in_specs=in_specs, out_specs=out_specs, out_shape=out_shape,
        compiler_params=_params("parallel"), name=name)(*args)


def _forget_key_terms(fz, b_f, B, S):
    ch = CUM_CHUNK
    place = np.zeros((3, LANES, LANES), np.float32)
    for h in range(FOX_HEADS):
        for j in range(3):
            place[j, h, 3 * h + j] = 1.0

    def kern(z_ref, b_ref, pl_ref, o_ref):
        r = lax.broadcasted_iota(jnp.int32, (ch, ch), 0)
        c = lax.broadcasted_iota(jnp.int32, (ch, ch), 1)
        tri = jnp.where(c <= r, 1.0, 0.0).astype(BF16)

        def body(j, carries):
            out = []
            for b, carry in enumerate(carries):
                r0 = pl.multiple_of(b * S + j * ch, ch)
                z = z_ref[pl.ds(r0, ch), :] + b_ref[...]
                logf = -(jnp.maximum(-z, 0.0) + jnp.log1p(jnp.exp(-jnp.abs(z))))
                cs = carry
                for part in _split_bf16(logf, 3):
                    cs = cs + jnp.dot(tri, part, preferred_element_type=F32)
                terms = jnp.zeros((ch, LANES), F32)
                for jj, part in enumerate(_split_bf16(cs * (-LOG2E), 3)):
                    terms = terms + jnp.dot(part, pl_ref[jj], preferred_element_type=F32)
                o_ref[pl.ds(r0, ch), :] = terms.astype(o_ref.dtype)
                out.append(cs[ch - 1:ch, :])
            return tuple(out)

        lax.fori_loop(0, S // ch, body, tuple(jnp.zeros((1, LANES), F32) for _ in range(B)))

    return pl.pallas_call(
        kern, grid=(1,),
        in_specs=[pl.BlockSpec((B * S, LANES), lambda g: (0, 0), pipeline_mode=pl.Buffered(1)),
                  pl.BlockSpec((1, LANES), lambda g: (0, 0)),
                  pl.BlockSpec(place.shape, lambda g: (0, 0, 0))],
        out_specs=pl.BlockSpec((B * S, LANES), lambda g: (0, 0)),
        out_shape=jax.ShapeDtypeStruct((B * S, LANES), BF16),
        compiler_params=_params("arbitrary"), name="forget_terms")(fz, b_f, jnp.asarray(place, BF16))


class Head(NamedTuple):
    q_off: int
    q_half: Optional[str]
    k_off: int
    v0: int
    v1: int


def _flash(q, k, vt, *, B, S, P, q_w, k_w, v_rows, heads, kind, out_dtype, tq=None, tk=None, n_bufs=None,
           ext=None, sel=None, lam=None, subln=None, lam_init=None, name="flash"):
    TQ, TK = tq or ATT_TILE, tk or KEY_TILE
    assert S % TQ == 0 and TQ % TK == 0
    diag = TQ // TK
    nq = S // TQ
    nh = len(heads)
    rows = heads[0].v1 - heads[0].v0
    out_w = nh // 2 * rows if kind == "diff" else nh * rows
    width = TK
    n_bufs = n_bufs or ATT_BUFS
    if kind == "win":
        assert WIN == TK and TQ == TK and S >= 2 * TK and n_bufs == 2
    has_ext = kind in ("fox", "slc")

    def kern(*refs):
        it = iter(refs)
        q_ref, k_ref, v_ref = next(it), next(it), next(it)
        ext_ref = next(it) if has_ext else None
        sel_ref = next(it) if kind == "slc" else None
        lam_ref, sub_ref = (next(it), next(it)) if kind == "diff" else (None, None)
        o_ref, m_ref, acc_ref, mx_ref = next(it), next(it), next(it), next(it)
        bufs = [next(it) for _ in range(n_bufs)]
        p_id = pl.program_id(1)
        i = pl.program_id(2)

        lane = lax.broadcasted_iota(jnp.int32, (TQ, LANES), 1)
        lo = lane < HALF_LANES
        q_all = q_ref[...]
        if kind == "slc":
            drop = ((sel_ref[...].astype(F32) - 1.0) * MASK_BIG).astype(BF16)
        qs = []
        for j, h in enumerate(heads):
            qh = q_all[:, h.q_off:h.q_off + LANES]
            if h.q_half == "lo":
                qh = jnp.where(lo, qh, jnp.zeros_like(qh))
            elif h.q_half == "hi":
                qh = jnp.where(lo, jnp.zeros_like(qh), qh)
            if kind == "fox":
                first = 3 * (nh * p_id + j)
                pick = jnp.where((lane >= first) & (lane < first + 3), 1.0, 0.0).astype(BF16)
                qh = jnp.concatenate([qh, pick], axis=1)
            if kind == "slc":
                qh = jnp.concatenate([qh, drop], axis=1)
            qs.append(qh)

        m_ref[...] = jnp.full(m_ref.shape, NEG, F32)
        acc_ref[...] = jnp.zeros(acc_ref.shape, F32)

        def block_kind(ahead, kb, qb):
            lo_rel = (kb - qb) * MASK_BLOCK - (MASK_BLOCK - 1)
            hi_rel = (kb - qb) * MASK_BLOCK + (MASK_BLOCK - 1)
            floor = ahead - WIN if kind == "win" else lo_rel - 1
            if lo_rel > ahead or hi_rel <= floor:
                return "dead"
            if hi_rel <= ahead and lo_rel > floor:
                return "full"
            return "part"

        def block_mask(ahead, kb, qb):
            rel = (lax.broadcasted_iota(jnp.int32, (MASK_BLOCK, MASK_BLOCK), 0)
                   - lax.broadcasted_iota(jnp.int32, (MASK_BLOCK, MASK_BLOCK), 1)
                   + (kb - qb) * MASK_BLOCK)
            msk = rel <= ahead
            if kind == "win":
                msk = msk & (rel > ahead - WIN)
            return msk

        kbs, qbs = range(width // MASK_BLOCK), range(TQ // MASK_BLOCK)
        blk = lambda b: slice(b * MASK_BLOCK, (b + 1) * MASK_BLOCK)

        def scores(k0, u, ahead=None):
            kx = k_ref[pl.ds(k0, width), :]
            ex = ext_ref[pl.ds(k0, width), :] if has_ext else None
            for hi, h in enumerate(heads):
                kk = kx[:, h.k_off:h.k_off + LANES]
                if ex is not None:
                    kk = jnp.concatenate([kk, ex], axis=1)
                if ahead is None:
                    st = lax.dot_general(kk, qs[hi], _NT, preferred_element_type=F32)
                    bufs[u][hi] = st
                    mx_ref[u * nh + hi] = jnp.max(st, axis=0, keepdims=True)
                    continue
                col_max = [None] * len(qbs)
                for kb in kbs:
                    live = [qb for qb in qbs if block_kind(ahead, kb, qb) != "dead"]
                    if not live:
                        continue
                    q_rows = slice(live[0] * MASK_BLOCK, (live[-1] + 1) * MASK_BLOCK)
                    st = lax.dot_general(kk[blk(kb)], qs[hi][q_rows], _NT, preferred_element_type=F32)
                    for qb in range(live[0], live[-1] + 1):
                        sb = st[:, blk(qb - live[0])]
                        if block_kind(ahead, kb, qb) == "part":
                            sb = jnp.where(block_mask(ahead, kb, qb), sb, NEG)
                        bufs[u][hi, blk(kb), blk(qb)] = sb
                        cm = jnp.max(sb, axis=0, keepdims=True)
                        col_max[qb] = cm if col_max[qb] is None else jnp.maximum(col_max[qb], cm)
                mx_ref[u * nh + hi] = jnp.concatenate(col_max, axis=1)

        def update(k0, u, ahead=None):
            for hi, h in enumerate(heads):
                m_prev = m_ref[hi]
                m_new = jnp.maximum(m_prev, mx_ref[u * nh + hi])
                alpha = jnp.exp2(m_prev - m_new)
                if ahead is None:
                    ones = jnp.ones((BF16_ROWS, width), BF16)
                    pt = jnp.exp2(bufs[u][hi] - m_new).astype(BF16)
                    lhs = jnp.concatenate([v_ref[h.v0:h.v1, pl.ds(k0, width)], ones], axis=0)
                    new = jnp.dot(lhs, pt, preferred_element_type=F32)
                else:
                    ones = jnp.ones((BF16_ROWS, MASK_BLOCK), BF16)
                    cols = []
                    for qb in qbs:
                        col = None
                        for kb in kbs:
                            if block_kind(ahead, kb, qb) == "dead":
                                continue
                            pt = jnp.exp2(bufs[u][hi, blk(kb), blk(qb)] - m_new[:, blk(qb)]).astype(BF16)
                            kstart = pl.multiple_of(k0 + kb * MASK_BLOCK, MASK_BLOCK)
                            lhs = jnp.concatenate(
                                [v_ref[h.v0:h.v1, pl.ds(kstart, MASK_BLOCK)], ones], axis=0)
                            part = jnp.dot(lhs, pt, preferred_element_type=F32)
                            col = part if col is None else col + part
                        cols.append(col)
                    new = jnp.concatenate(cols, axis=1)
                acc_ref[hi] = alpha * acc_ref[hi] + new
                m_ref[hi] = m_new

        tile = lambda t: pl.multiple_of(t * TK, TK)
        U = len(bufs)
        if kind == "win":
            @pl.when(i == 0)
            def _():
                scores(tile(0), 0, 0)
                update(tile(0), 0, 0)

            @pl.when(i > 0)
            def _():
                scores(tile(i - 1), 0, WIN)
                scores(tile(i), 1, 0)
                update(tile(i - 1), 0, WIN)
                update(tile(i), 1, 0)
        else:
            n_full = i * diag
            n_rounds = jnp.maximum(n_full - 1, 0) // U
            pl.when(n_full == 0)(lambda: scores(tile(0), 0, 0))
            pl.when(n_full > 0)(lambda: scores(tile(0), 0))

            def body(j, c):
                for u in range(U):
                    scores(tile(U * j + u + 1), (u + 1) % U)
                    update(tile(U * j + u), u)
                return c
            lax.fori_loop(0, n_rounds, body, 0)
            t0 = U * n_rounds

            def tail(left):
                ahead = lambda u: -(u - left) * TK if u >= left else None
                last = left + diag - 1
                for u in range(last):
                    scores(tile(t0 + u + 1), (u + 1) % U, ahead(u + 1))
                    update(tile(t0 + u), u % U, ahead(u))
                update(tile(t0 + last), last % U, ahead(last))

            for left in range(U + 1):
                pl.when(n_full - t0 == left)(lambda left=left: tail(left))

        outs = []
        for hi in range(nh):
            acc = acc_ref[hi]
            outs.append(acc[:rows] / acc[rows:rows + 1])
        if kind == "diff":
            lm = lam_ref[...]
            la = jnp.sum(lm[0:1] * lm[1:2], axis=1, keepdims=True)
            lb = jnp.sum(lm[2:3] * lm[3:4], axis=1, keepdims=True)
            lam_full = jnp.exp(la) - jnp.exp(lb) + lam_init
            for g in range(nh // 2):
                oa = jnp.transpose(outs[2 * g] - lam_full * outs[2 * g + 1])
                y = oa * lax.rsqrt(jnp.mean(oa * oa, axis=-1, keepdims=True) + EPS)
                y = (y * sub_ref[...]) * (1.0 - lam_init)
                o_ref[:, g * rows:(g + 1) * rows] = y.astype(o_ref.dtype)
        else:
            o_ref[...] = jnp.transpose(jnp.concatenate(outs, axis=0)).astype(o_ref.dtype)

    in_specs = [
        pl.BlockSpec((TQ, q_w), lambda b, p, i: (b * nq + i, p)),
        pl.BlockSpec((S, k_w), lambda b, p, i: (b, p)),
        pl.BlockSpec((v_rows, S), lambda b, p, i: (p, b)),
    ]
    args = [q, k, vt]
    if kind == "fox":
        in_specs.append(pl.BlockSpec((S, LANES), lambda b, p, i: (b, 0)))
        args.append(ext)
    if kind == "slc":
        in_specs.append(pl.BlockSpec((S, LANES), lambda b, p, i: (0, 0)))
        in_specs.append(pl.BlockSpec((TQ, LANES), lambda b, p, i: (b * nq + i, p)))
        args += [ext, sel]
    if kind == "diff":
        in_specs.append(pl.BlockSpec(lam.shape, lambda b, p, i: (0, 0)))
        in_specs.append(pl.BlockSpec((1, LANES), lambda b, p, i: (0, 0)))
        args += [lam, subln.reshape(1, LANES)]
    return pl.pallas_call(
        kern, grid=(B, P, nq), in_specs=in_specs,
        out_specs=pl.BlockSpec((TQ, out_w), lambda b, p, i: (b * nq + i, p)),
        out_shape=jax.ShapeDtypeStruct((B * S, P * out_w), out_dtype),
        scratch_shapes=[pltpu.VMEM((nh, 1, TQ), F32),
                        pltpu.VMEM((nh, rows + BF16_ROWS, TQ), F32),
                        pltpu.VMEM((n_bufs * nh, 1, TQ), F32)]
        + [pltpu.VMEM((nh, width, TQ), F32)] * n_bufs,
        compiler_params=_params("parallel", "parallel", "arbitrary"), name=name)(*args)


_DIFF_HEADS = tuple(Head(s * LANES, half, s * LANES, s * LANES, (s + 1) * LANES)
                    for s in range(2) for half in ("lo", "hi"))
_PAIR_HEADS = tuple(Head(s * LANES, half, s * LANES, (2 * s + j) * HALF_LANES, (2 * s + j + 1) * HALF_LANES)
                    for s in range(2) for j, half in enumerate(("lo", "hi")))
_QUAD_HEADS = tuple(Head(s * LANES, half, 0, 0, HALF_LANES) for s in range(2) for half in ("lo", "hi"))
_WIDE_HEADS = tuple(Head(j * LANES, None, j * LANES, j * HALF_LANES, (j + 1) * HALF_LANES) for j in range(4))


def _nsa_compress(xk, xv, pe, w1x, w2d):
    B, R, K = xk.shape
    G = NSA_GROUPS

    def kern(xk_ref, xv_ref, pe_ref, w1_ref, w2_ref, k_ref, vt_ref):
        for t, x_ref in enumerate((xk_ref, xv_ref)):
            x = x_ref[...]
            xa = (x + pe_ref[t, 0]).astype(BF16)
            xb = (x + pe_ref[t, 1]).astype(BF16)
            for g in range(G):
                a = jnp.dot(xa, w1_ref[t, 0, g], preferred_element_type=F32)
                b = jnp.dot(xb, w1_ref[t, 1, g], preferred_element_type=F32)
                h = a + pltpu.roll(b, R - 1, 0)
                hs = h * _sigmoid(h)
                o = jnp.dot(hs.astype(BF16), w2_ref[t], preferred_element_type=F32)
                if t == 0:
                    k_ref[g] = o
                else:
                    vt_ref[g] = jnp.transpose(o)

    return pl.pallas_call(
        kern, grid=(B,),
        in_specs=[pl.BlockSpec((None, R, K), lambda b: (b, 0, 0)),
                  pl.BlockSpec((None, R, K), lambda b: (b, 0, 0)),
                  pl.BlockSpec(pe.shape, lambda b: (0, 0, 0, 0)),
                  pl.BlockSpec(w1x.shape, lambda b: (0, 0, 0, 0, 0)),
                  pl.BlockSpec(w2d.shape, lambda b: (0, 0, 0))],
        out_specs=[pl.BlockSpec((None, G, R, LANES), lambda b: (b, 0, 0, 0)),
                   pl.BlockSpec((None, G, LANES, R), lambda b: (b, 0, 0, 0))],
        out_shape=[jax.ShapeDtypeStruct((B, G, R, LANES), F32),
                   jax.ShapeDtypeStruct((B, G, LANES, R), F32)],
        compiler_params=_params("parallel"), name="nsa_compress")(xk, xv, pe, w1x, w2d)


def _nsa_cmp_select(q, cmp_k, cmp_vt, ov_t, *, B, S):
    T = SEL_TILE
    nq = S // T
    R = cmp_k.shape[2]
    G = NSA_GROUPS
    d = NSA_DIM
    n_sel = min(SLC_TOPK, S // SLC_LEN)
    assert n_sel >= 3 and FORCE_SCORE > NSA_HEADS // G

    bucket = min(LANES, R)
    assert R % bucket == 0 and bucket % (SLC_LEN // CMP_STRIDE) == 0

    def kern(q_ref, kc_ref, vt_ref, ov_ref, o_ref, sel_ref):
        i = pl.program_id(2)
        lane = lax.broadcasted_iota(jnp.int32, (T, LANES), 1)
        lo = lane < HALF_LANES
        q_all = q_ref[...]
        qs = []
        for hh in range(4):
            qh = q_all[:, (hh // 2) * LANES:(hh // 2 + 1) * LANES]
            qs.append(jnp.where(lo, qh, jnp.zeros_like(qh)) if hh % 2 == 0
                      else jnp.where(lo, jnp.zeros_like(qh), qh))

        def variant(rv):
            nb = rv * CMP_STRIDE // SLC_LEN
            tq = lax.broadcasted_iota(jnp.int32, (rv, T), 1) + i * T
            cend = lax.broadcasted_iota(jnp.int32, (rv, T), 0) * CMP_STRIDE + (CMP_LEN - 1)
            vis = cend <= tq
            kc = kc_ref[:rv, :].astype(BF16)
            vt = vt_ref[:d, :rv].astype(BF16)
            psum = jnp.zeros((rv, T), F32)
            o_heads = []
            for hh in range(4):
                st = lax.dot_general(kc, qs[hh], _NT, preferred_element_type=F32)
                st = jnp.where(vis, st, -jnp.inf)
                m = jnp.max(st, axis=0, keepdims=True)
                e = jnp.exp2(st - jnp.where(m > -jnp.inf, m, 0.0))
                p = e / jnp.maximum(jnp.sum(e, axis=0, keepdims=True), 1e-30)
                o_heads.append(jnp.dot(vt, p.astype(BF16), preferred_element_type=F32))
                psum = psum + p
            o_ref[...] = jnp.transpose(jnp.concatenate(o_heads, axis=0)).astype(o_ref.dtype)
            imp = jnp.zeros((nb, T), F32)
            for part in _split_bf16(psum, 2):
                imp = imp + jnp.dot(ov_ref[:nb, :rv], part, preferred_element_type=F32)
            blk = lax.broadcasted_iota(jnp.int32, (nb, T), 0)
            cur = (lax.broadcasted_iota(jnp.int32, (nb, T), 1) + i * T) // SLC_LEN
            forced = (blk == 0) | (blk == cur) | (blk == cur - 1)
            work = jnp.where(forced, -jnp.inf, imp)
            work = jnp.where(blk > cur, NEG, work)
            blk_f = blk.astype(F32)
            for _ in range(n_sel - 3):
                mx = jnp.max(work, axis=0, keepdims=True)
                first = jnp.min(jnp.where(work == mx, blk_f, float(LANES)), axis=0, keepdims=True)
                work = jnp.where(blk_f == first, -jnp.inf, work)
            sel = jnp.where(work == -jnp.inf, 1.0, 0.0)
            if nb < LANES:
                sel = jnp.concatenate([sel, jnp.zeros((LANES - nb, T), F32)], axis=0)
            sel_ref[...] = jnp.transpose(sel).astype(sel_ref.dtype)

        need = (i + 1) * (T // CMP_STRIDE)
        which = (need - 1) // bucket
        for bk in range(R // bucket):
            pl.when(which == bk)(lambda bk=bk: variant((bk + 1) * bucket))

    return pl.pallas_call(
        kern, grid=(B, G, nq),
        in_specs=[pl.BlockSpec((T, 2 * LANES), lambda b, g, i: (b * nq + i, g)),
                  pl.BlockSpec((None, None, R, LANES), lambda b, g, i: (b, g, 0, 0)),
                  pl.BlockSpec((None, None, LANES, R), lambda b, g, i: (b, g, 0, 0)),
                  pl.BlockSpec((LANES, R), lambda b, g, i: (0, 0))],
        out_specs=[pl.BlockSpec((T, 2 * LANES), lambda b, g, i: (b * nq + i, g)),
                   pl.BlockSpec((T, LANES), lambda b, g, i: (b * nq + i, g))],
        out_shape=[jax.ShapeDtypeStruct((B * S, G * 2 * LANES), BF16),
                   jax.ShapeDtypeStruct((B * S, G * LANES), BF16)],
        compiler_params=_params("parallel", "parallel", "parallel"), name="nsa_cmp_select")(
            q, cmp_k, cmp_vt, ov_t)


def _odd_out(o_cmp, o_slc, o_win, gz, expand, o_mla, w_nsa, w_mla, x):
    N, C = o_cmp.shape
    D = x.shape[1]
    tm = min(ROW_TILE, N)

    def kern(c_ref, s_ref, w_ref, g_ref, e_ref, m_ref, wn_ref, wm_ref, x_ref, o_ref):
        gate = _sigmoid(g_ref[...])
        parts = _split_bf16(gate, 2)
        acc = jnp.zeros((tm, C), F32)
        for j, br in enumerate((c_ref, s_ref, w_ref)):
            gj = sum(jnp.dot(part, e_ref[j], preferred_element_type=F32) for part in parts)
            acc = acc + gj * br[...].astype(F32)
        o_nsa = acc.astype(BF16)
        o_mla_ = m_ref[...]
        for c0 in range(0, D, COL_CHUNK):
            cols = slice(c0, c0 + COL_CHUNK)
            y = (jnp.dot(o_nsa, wn_ref[:, cols], preferred_element_type=F32)
                 + jnp.dot(o_mla_, wm_ref[:, cols], preferred_element_type=F32))
            o_ref[:, cols] = x_ref[:, cols] + y

    row = pl.BlockSpec((tm, C), lambda i: (i, 0))
    wide = pl.BlockSpec((tm, D), lambda i: (i, 0))
    return pl.pallas_call(
        kern, grid=(N // tm,),
        in_specs=[row, row, row, pl.BlockSpec((tm, LANES), lambda i: (i, 0)),
                  pl.BlockSpec(expand.shape, lambda i: (0, 0, 0)),
                  pl.BlockSpec((tm, o_mla.shape[1]), lambda i: (i, 0)),
                  pl.BlockSpec(w_nsa.shape, lambda i: (0, 0)),
                  pl.BlockSpec(w_mla.shape, lambda i: (0, 0)), wide],
        out_specs=wide, out_shape=jax.ShapeDtypeStruct((N, D), F32),
        compiler_params=_params("parallel"), name="odd_out")(
            o_cmp, o_slc, o_win, gz, expand, o_mla, w_nsa, w_mla, x)


def _mla_kv(ckv, gain, wk, wv, kr):
    N, K = ckv.shape
    tm = min(ROW_TILE, N)
    H = MLA_HEADS

    def kern(c_ref, g_ref, wk_ref, wv_ref, kr_ref, k_ref, v_ref):
        xf = c_ref[...]
        y = xf * lax.rsqrt(jnp.mean(xf * xf, axis=-1, keepdims=True) + EPS)
        a = (y * g_ref[...]).astype(BF16)
        kr2 = jnp.concatenate([kr_ref[...]] * 2, axis=1)
        for h in range(0, H, 2):
            cols = slice(h * LANES, (h + 2) * LANES)
            kh = jnp.dot(a, wk_ref[:, cols], preferred_element_type=F32)
            k_ref[:, cols] = (kh + kr2).astype(k_ref.dtype)
        v_ref[...] = jnp.dot(a, wv_ref[...], preferred_element_type=F32).astype(v_ref.dtype)

    return pl.pallas_call(
        kern, grid=(N // tm,),
        in_specs=[pl.BlockSpec((tm, K), lambda i: (i, 0)),
                  pl.BlockSpec((1, K), lambda i: (0, 0)),
                  pl.BlockSpec(wk.shape, lambda i: (0, 0)),
                  pl.BlockSpec(wv.shape, lambda i: (0, 0)),
                  pl.BlockSpec((tm, LANES), lambda i: (i, 0))],
        out_specs=[pl.BlockSpec((tm, H * LANES), lambda i: (i, 0)),
                   pl.BlockSpec((tm, H * MLA_V), lambda i: (i, 0))],
        out_shape=[jax.ShapeDtypeStruct((N, H * LANES), BF16),
                   jax.ShapeDtypeStruct((N, H * MLA_V), BF16)],
        compiler_params=_params("parallel"), name="mla_kv")(
            ckv, gain.reshape(1, K).astype(F32), wk, wv, kr)


def _mem_attn(x, gain, wq, wo, mem_k, mem_v, *, S):
    N, D = x.shape
    tm = min(ROW_TILE, S)
    M = mem_k.shape[1]
    per_b = S // tm
    scale = XA_DIM ** -0.5

    def kern(x_ref, g_ref, wq_ref, wo_ref, k_ref, v_ref, o_ref):
        xf = x_ref[...]
        y = xf * lax.rsqrt(jnp.mean(xf * xf, axis=-1, keepdims=True) + EPS)
        a = (y * g_ref[...]).astype(BF16)
        q = (jnp.dot(a, wq_ref[...], preferred_element_type=F32) * scale).astype(BF16)
        heads = []
        for h in range(XA_HEADS):
            sl = slice(h * XA_DIM, (h + 1) * XA_DIM)
            s = lax.dot_general(q[:, sl], k_ref[:, sl], _NT, preferred_element_type=F32)
            e = jnp.exp(s - jnp.max(s, axis=1, keepdims=True))
            p = e / jnp.sum(e, axis=1, keepdims=True)
            heads.append(jnp.dot(p.astype(BF16), v_ref[:, sl], preferred_element_type=F32).astype(BF16))
        o = jnp.concatenate(heads, axis=1)
        o_ref[...] = xf + jnp.dot(o, wo_ref[...], preferred_element_type=F32)

    return pl.pallas_call(
        kern, grid=(N // tm,),
        in_specs=[pl.BlockSpec((tm, D), lambda i: (i, 0)),
                  pl.BlockSpec((1, D), lambda i: (0, 0)),
                  pl.BlockSpec(wq.shape, lambda i: (0, 0)),
                  pl.BlockSpec(wo.shape, lambda i: (0, 0)),
                  pl.BlockSpec((None, M, XA_HEADS * XA_DIM), lambda i: (i // per_b, 0, 0)),
                  pl.BlockSpec((None, M, XA_HEADS * XA_DIM), lambda i: (i // per_b, 0, 0))],
        out_specs=pl.BlockSpec((tm, D), lambda i: (i, 0)),
        out_shape=jax.ShapeDtypeStruct((N, D), F32),
        compiler_params=_params("parallel"), name="mem_attn")(
            x, gain.reshape(1, D).astype(F32), wq, wo, mem_k, mem_v)


def _ffn(x, gain, w13, w2, out_gain=None):
    N, D = x.shape
    FF = w2.shape[0]
    tm = min(ROW_TILE, N)
    chunk = 2 * LANES
    assert FF % chunk == 0

    final = out_gain is not None

    def kern(x_ref, g_ref, w13_ref, w2_ref, *rest):
        og_ref, o_ref = rest if final else (None, rest[0])
        xf = x_ref[...]
        y = xf * lax.rsqrt(jnp.mean(xf * xf, axis=-1, keepdims=True) + EPS)
        a = (y * g_ref[...]).astype(BF16)
        acc = xf
        for c in range(0, FF, chunk):
            g = jnp.dot(a, w13_ref[:, c:c + chunk], preferred_element_type=F32)
            u = jnp.dot(a, w13_ref[:, FF + c:FF + c + chunk], preferred_element_type=F32)
            hdn = (g * _sigmoid(g) * u).astype(BF16)
            acc = acc + jnp.dot(hdn, w2_ref[c:c + chunk, :], preferred_element_type=F32)
        if final:
            acc = acc * lax.rsqrt(jnp.mean(acc * acc, axis=-1, keepdims=True) + EPS) * og_ref[...]
        o_ref[...] = acc

    vec = pl.BlockSpec((1, D), lambda i: (0, 0))
    in_specs = [pl.BlockSpec((tm, D), lambda i: (i, 0)), vec,
                pl.BlockSpec(w13.shape, lambda i: (0, 0), pipeline_mode=pl.Buffered(1)),
                pl.BlockSpec(w2.shape, lambda i: (0, 0), pipeline_mode=pl.Buffered(1))]
    args = [x, gain.reshape(1, D).astype(F32), w13, w2]
    if final:
        in_specs.append(vec)
        args.append(out_gain.reshape(1, D).astype(F32))
    return pl.pallas_call(
        kern, grid=(N // tm,), in_specs=in_specs,
        out_specs=pl.BlockSpec((tm, D), lambda i: (i, 0)),
        out_shape=jax.ShapeDtypeStruct((N, D), F32),
        compiler_params=_params("parallel"), name="ffn")(*args)


def _pad_cols(w, width):
    return jnp.pad(w, ((0, 0), (0, width - w.shape[1])))


def _even_mixer(x, gain, w_in, b_f, lam, subln, w_out, layer_idx, ropes, B, S):
    D = x.shape[1]
    blk = DIFF_HEADS * 2 * DIFF_QK
    w = _pad_cols(w_in, 6 * blk + LANES).astype(BF16)
    sc = DIFF_QK ** -0.5 * LOG2E
    segs = [Seg(0, blk, BF16, "r64", sc), Seg(blk, blk, BF16, "r64"), Seg(2 * blk, blk, BF16),
            Seg(3 * blk, blk, BF16, None, sc), Seg(4 * blk, blk, BF16), Seg(5 * blk, blk, BF16),
            Seg(6 * blk, LANES, F32)]
    aq, ak, av, fq, fk, fv, fz = _linear([x], [w], segs, gain=gain, ropes=ropes, seq=S, name="even_in")
    lam_init = 0.8 - 0.6 * math.exp(-0.3 * layer_idx)
    oa = _flash(aq, ak, av.T, B=B, S=S, P=DIFF_HEADS // 2, q_w=2 * LANES, k_w=2 * LANES, v_rows=2 * LANES,
                heads=_DIFF_HEADS, kind="diff", out_dtype=BF16, lam=lam.astype(F32),
                subln=subln.astype(F32), lam_init=lam_init, name="diff_attn")
    terms = _forget_key_terms(fz, _pad_cols(b_f.reshape(1, -1), LANES).astype(F32), B, S)
    of = _flash(fq, fk, fv.T, B=B, S=S, P=FOX_HEADS // 4, q_w=2 * LANES, k_w=2 * LANES, v_rows=2 * LANES,
                heads=_PAIR_HEADS, kind="fox", out_dtype=BF16, ext=terms, name="fox_attn")
    wo = w_out.astype(BF16)
    (y,) = _linear([oa, of], [wo[:blk], wo[blk:]], [Seg(0, D, F32)], residual=x, name="even_out")
    return y


def _odd_in_weight(w_in):
    d = NSA_DIM
    o = np.cumsum((0, NSA_HEADS * d) + (NSA_GROUPS * d,) * 6 + (NSA_HEADS * 3, MLA_Q_RANK, MLA_KV_RANK, MLA_ROPE))
    nq, kc, vc, ks, vs, kw, vw, gz, cq, ckv, kr = [w_in[:, o[j]:o[j + 1]] for j in range(11)]

    def dup(wg):
        return jnp.concatenate([wg[:, :d], wg[:, :d], wg[:, d:], wg[:, d:]], axis=1)

    zeros = lambda n: jnp.zeros((w_in.shape[0], n), w_in.dtype)
    kr_slab = jnp.concatenate([zeros(MLA_NOPE), kr, zeros(LANES - MLA_NOPE - MLA_ROPE)], axis=1)
    cols = [nq, kc, vc, dup(ks), vs, dup(kw), vw, _pad_cols(gz, LANES), cq, ckv, kr_slab]
    return jnp.concatenate(cols, axis=1).astype(BF16)


def _odd_mixer(x, gain, w_in, cmp_pos, cmp_w1, cmp_w2, q_norm, kv_norm, w_uq, w_ukv, w_out, ropes, B, S):
    D = x.shape[1]
    G, d = NSA_GROUPS, NSA_DIM
    w = _odd_in_weight(w_in)
    sc = d ** -0.5 * LOG2E
    widths = [(NSA_HEADS * d, BF16, "r64", sc), (LANES, F32, "r64", 1.0), (LANES, F32, None, 1.0),
              (2 * LANES, BF16, "r64", 1.0), (LANES, BF16, None, 1.0),
              (2 * LANES, BF16, "r64", 1.0), (LANES, BF16, None, 1.0),
              (LANES, F32, None, 1.0), (MLA_Q_RANK, F32, None, 1.0), (MLA_KV_RANK, F32, None, 1.0),
              (LANES, F32, "mla", 1.0)]
    segs, start = [], 0
    for wd, dt, rp, s_ in widths:
        segs.append(Seg(start, wd, dt, rp, s_))
        start += wd
    q, kc, vc, ks, vs, kw, vw, gz, cq, ckv, kr = _linear(
        [x], [w], segs, gain=gain, ropes=ropes, seq=S, name="odd_in")

    R = S // CMP_STRIDE

    K = CMP_STRIDE * G * d
    half = CMP_LEN // 2
    pe = jnp.broadcast_to(cmp_pos.reshape(2, 2, half, 1, d), (2, 2, half, G, d)).reshape(2, 2, 1, K).astype(F32)
    eye = jnp.eye(G, dtype=cmp_w1.dtype)
    w1x = (cmp_w1.reshape(2, 2, 1, half, 1, d, CMP_HIDDEN) * eye[None, None, :, None, :, None, None])
    w1x = w1x.reshape(2, 2, G, K, CMP_HIDDEN).astype(BF16)
    w2d = jnp.concatenate([cmp_w2, cmp_w2], axis=-1).astype(BF16)
    cmp_k, cmp_vt = _nsa_compress(kc.reshape(B, R, K), vc.reshape(B, R, K), pe, w1x, w2d)
    cs = np.arange(R) * CMP_STRIDE
    bs = np.arange(LANES) * SLC_LEN
    ov = ((cs[:, None] < bs[None, :] + SLC_LEN) & (cs[:, None] + CMP_LEN > bs[None, :])
          & (np.arange(R)[:, None] < R - 1) & (bs[None, :] < S))
    o_cmp, sel = _nsa_cmp_select(q, cmp_k, cmp_vt, jnp.asarray(ov.T.astype(np.float32), BF16), B=B, S=S)
    onehot = (np.arange(S)[:, None] // SLC_LEN == np.arange(LANES)[None, :]).astype(np.float32)
    o_slc = _flash(q, ks, vs.T, B=B, S=S, P=G, q_w=2 * LANES, k_w=LANES, v_rows=HALF_LANES,
                   heads=_QUAD_HEADS, kind="slc", out_dtype=BF16, ext=jnp.asarray(onehot, BF16),
                   sel=sel, name="nsa_slc")
    o_win = _flash(q, kw, vw.T, B=B, S=S, P=G, q_w=2 * LANES, k_w=LANES, v_rows=HALF_LANES,
                   heads=_QUAD_HEADS, kind="win", out_dtype=BF16, tq=WIN, tk=WIN, n_bufs=2, name="nsa_win")
    ex = np.zeros((3, LANES, NSA_HEADS * d), np.float32)
    for hh in range(NSA_HEADS):
        for j in range(3):
            ex[j, hh * 3 + j, hh * d:(hh + 1) * d] = 1.0

    H = MLA_HEADS
    qk = MLA_NOPE + MLA_ROPE
    wq_slab = jnp.pad(w_uq.reshape(MLA_Q_RANK, H, qk), ((0, 0), (0, 0), (0, LANES - qk)))
    wq_slab = wq_slab.reshape(MLA_Q_RANK, H * LANES).astype(BF16)
    (qm,) = _linear([cq], [wq_slab], [Seg(0, H * LANES, BF16, "mla", qk ** -0.5 * LOG2E)],
                    gain=q_norm, ropes=ropes, seq=S, name="mla_q")
    wkv = w_ukv.reshape(MLA_KV_RANK, H, MLA_NOPE + MLA_V)
    wk = jnp.pad(wkv[:, :, :MLA_NOPE], ((0, 0), (0, 0), (0, LANES - MLA_NOPE)))
    wk = wk.reshape(MLA_KV_RANK, H * LANES).astype(BF16)
    wv = wkv[:, :, MLA_NOPE:].reshape(MLA_KV_RANK, H * MLA_V).astype(BF16)
    km, vm = _mla_kv(ckv, kv_norm, wk, wv, kr)
    o_mla = _flash(qm, km, vm.T, B=B, S=S, P=H // 4, q_w=4 * LANES, k_w=4 * LANES, v_rows=2 * LANES,
                   heads=_WIDE_HEADS, kind="mla", out_dtype=BF16, name="mla_attn")
    wo = w_out.astype(BF16)
    half = NSA_HEADS * d
    return _odd_out(o_cmp, o_slc, o_win, gz, jnp.asarray(ex, BF16), o_mla, wo[:half], wo[half:], x)


def kernel(x, mem, mem_norm, norm_mix, norm_mem, norm_ffn, ev_w_in, ev_b_f, ev_lam, ev_subln, ev_w_out, od_w_in, nsa_cmp_pos, nsa_cmp_w1, nsa_cmp_w2, mla_q_norm, mla_kv_norm, mla_w_uq, mla_w_ukv, od_w_out, xa_wq, xa_wkv, xa_wo, ffn_w13, ffn_w2, final_norm):
    B, S, D = x.shape
    M = mem.shape[1]
    depth = norm_mix.shape[0]
    ropes = {
        "r64": (_rope_tables(S, NSA_DIM // 2, 0, LANES), NSA_DIM // 2),
        "mla": (_rope_tables(S, MLA_ROPE // 2, MLA_NOPE, MLA_NOPE + MLA_ROPE), MLA_ROPE // 2),
    }
    xa_w = XA_HEADS * XA_DIM
    h = x.reshape(B * S, D)
    mem2 = mem.reshape(B * M, D)
    for li in range(depth):
        j = li // 2
        if li % 2 == 0:
            h = _even_mixer(h, norm_mix[li], ev_w_in[j], ev_b_f[j], ev_lam[j], ev_subln[j], ev_w_out[j],
                            li, ropes, B, S)
        else:
            h = _odd_mixer(h, norm_mix[li], od_w_in[j], nsa_cmp_pos[j], nsa_cmp_w1[j], nsa_cmp_w2[j],
                           mla_q_norm[j], mla_kv_norm[j], mla_w_uq[j], mla_w_ukv[j], od_w_out[j],
                           ropes, B, S)
        mk, mv = _linear([mem2], [xa_wkv[li].astype(BF16)],
                         [Seg(0, xa_w, BF16), Seg(xa_w, xa_w, BF16)], gain=mem_norm, name="mem_kv")
        h = _mem_attn(h, norm_mem[li], xa_wq[li].astype(BF16), xa_wo[li].astype(BF16),
                      mk.reshape(B, M, xa_w), mv.reshape(B, M, xa_w), S=S)
        h = _ffn(h, norm_ffn[li], ffn_w13[li].astype(BF16), ffn_w2[li].astype(BF16),
                 out_gain=final_norm if li == depth - 1 else None)
    return h.reshape(B, S, D)
```

```python
import math
from typing import NamedTuple, Optional

import numpy as np
import jax
import jax.numpy as jnp
from jax import lax
from jax.experimental import pallas as pl
from jax.experimental.pallas import tpu as pltpu

F32 = jnp.float32
BF16 = jnp.bfloat16

LANES = 128
HALF_LANES = LANES // 2
BF16_ROWS = 16
ROPE_THETA = 10000.0
EPS = 1e-6
NEG = -1e30
LOG2E = math.log2(math.e)
MASK_BIG = 2.0 ** 100

DIFF_HEADS = 4
DIFF_QK = 64
FOX_HEADS = 8
NSA_HEADS = 8
NSA_GROUPS = 2
NSA_DIM = 64
CMP_LEN = 32
CMP_STRIDE = 16
CMP_HIDDEN = 128
SLC_LEN = 64
SLC_TOPK = 16
WIN = 512
FORCE_SCORE = 1e4
MLA_HEADS = 8
MLA_NOPE = 64
MLA_ROPE = 32
MLA_V = 64
MLA_Q_RANK = 384
MLA_KV_RANK = 256
XA_HEADS = 4
XA_DIM = 128

VMEM_LIMIT = 56 * 1024 * 1024
ROW_TILE = 512
COL_CHUNK = 512
ATT_TILE = 512
KEY_TILE = 512
ATT_BUFS = 4
MASK_BLOCK = 256
SEL_TILE = 1024
CUM_CHUNK = 256

_NT = (((1,), (1,)), ((), ()))


def _params(*sem):
    return pltpu.CompilerParams(dimension_semantics=sem, vmem_limit_bytes=VMEM_LIMIT)


def _sigmoid(x):
    return 1.0 / (1.0 + jnp.exp(-x))


def _split_bf16(x, terms):
    out = []
    r = x
    for _ in range(terms):
        h = r.astype(BF16)
        out.append(h)
        r = r - h.astype(F32)
    return out


def _rope_tables(S, half, lane_lo, lane_hi):
    pos = jnp.arange(S, dtype=F32)
    inv = 1.0 / (ROPE_THETA ** (jnp.arange(half, dtype=F32) / half))
    ang = pos[:, None] * inv[None, :]
    cos, sin = jnp.cos(ang), jnp.sin(ang)
    lane = np.arange(LANES)
    active = (lane >= lane_lo) & (lane < lane_hi)
    j = (lane - lane_lo) % (2 * half)
    lower = active & (j < half)
    upper = active & (j >= half)
    idx = j % half
    cos_t = jnp.where(active[None, :], cos[:, idx], 1.0)
    sin_a = jnp.where(lower[None, :], -sin[:, idx], 0.0)
    sin_b = jnp.where(upper[None, :], sin[:, idx], 0.0)
    return cos_t, sin_a, sin_b


class Seg(NamedTuple):
    start: int
    width: int
    dtype: object
    rope: Optional[str] = None
    scale: float = 1.0


def _linear(xs, ws, segs, *, gain=None, residual=None, ropes=None, seq=None, name="linear"):
    N = xs[0].shape[0]
    tm = min(ROW_TILE, N)
    assert N % tm == 0
    n_in = len(xs)
    has_gain = gain is not None
    has_res = residual is not None
    rope_keys = sorted({s.rope for s in segs if s.rope})
    halves = {k: ropes[k][1] for k in rope_keys}

    def kern(*refs):
        it = iter(refs)
        x_refs = [next(it) for _ in range(n_in)]
        w_refs = [next(it) for _ in range(n_in)]
        g_ref = next(it) if has_gain else None
        r_ref = next(it) if has_res else None
        tabs = {k: (next(it), next(it), next(it)) for k in rope_keys}
        o_refs = [next(it) for _ in segs]
        acts = []
        for j, xr in enumerate(x_refs):
            x = xr[...]
            if j == 0 and has_gain:
                xf = x.astype(F32)
                y = xf * lax.rsqrt(jnp.mean(xf * xf, axis=-1, keepdims=True) + EPS)
                acts.append((y * g_ref[...]).astype(BF16))
            else:
                acts.append(x.astype(BF16))
        for seg, o_ref in zip(segs, o_refs):
            for c0 in range(0, seg.width, COL_CHUNK):
                cw = min(COL_CHUNK, seg.width - c0)
                col = seg.start + c0
                acc = None
                for a, wr in zip(acts, w_refs):
                    d = jnp.dot(a, wr[:, col:col + cw], preferred_element_type=F32)
                    acc = d if acc is None else acc + d
                if has_res:
                    acc = acc + r_ref[:, col:col + cw]
                if seg.scale != 1.0:
                    acc = acc * seg.scale
                if seg.rope is None:
                    o_ref[:, c0:c0 + cw] = acc.astype(o_ref.dtype)
                else:
                    cos_r, sa_r, sb_r = tabs[seg.rope]
                    half = halves[seg.rope]
                    cos, sa, sb = cos_r[...], sa_r[...], sb_r[...]
                    for s0 in range(0, cw, LANES):
                        xs_ = acc[:, s0:s0 + LANES]
                        y = (xs_ * cos + pltpu.roll(xs_, LANES - half, 1) * sa
                             + pltpu.roll(xs_, half, 1) * sb)
                        o_ref[:, c0 + s0:c0 + s0 + LANES] = y.astype(o_ref.dtype)

    in_specs, args = [], []
    for x in xs:
        in_specs.append(pl.BlockSpec((tm, x.shape[1]), lambda i: (i, 0)))
        args.append(x)
    for w in ws:
        in_specs.append(pl.BlockSpec(w.shape, lambda i: (0, 0)))
        args.append(w)
    if has_gain:
        in_specs.append(pl.BlockSpec((1, gain.shape[-1]), lambda i: (0, 0)))
        args.append(gain.reshape(1, -1).astype(F32))
    if has_res:
        in_specs.append(pl.BlockSpec((tm, residual.shape[1]), lambda i: (i, 0)))
        args.append(residual)
    for k in rope_keys:
        assert seq % tm == 0
        nt = seq // tm
        for t in ropes[k][0]:
            in_specs.append(pl.BlockSpec((tm, LANES), lambda i, nt=nt: (i % nt, 0)))
            args.append(t)
    out_shape = [jax.ShapeDtypeStruct((N, s.width), s.dtype) for s in segs]
    out_specs = [pl.BlockSpec((tm, s.width), lambda i: (i, 0)) for s in segs]
    return pl.pallas_call(
        kern, grid=(N // tm,), in_specs=in_specs, out_specs=out_specs, out_shape=out_shape,
        compiler_params=_params("parallel"), name=name)(*args)


def _forget_key_terms(fz, b_f, B, S):
    ch = CUM_CHUNK
    place = np.zeros((3, LANES, LANES), np.float32)
    for h in range(FOX_HEADS):
        for j in range(3):
            place[j, h, 3 * h + j] = 1.0

    def kern(z_ref, b_ref, pl_ref, o_ref):
        r = lax.broadcasted_iota(jnp.int32, (ch, ch), 0)
        c = lax.broadcasted_iota(jnp.int32, (ch, ch), 1)
        tri = jnp.where(c <= r, 1.0, 0.0).astype(BF16)

        def body(j, carries):
            out = []
            for b, carry in enumerate(carries):
                r0 = pl.multiple_of(b * S + j * ch, ch)
                z = z_ref[pl.ds(r0, ch), :] + b_ref[...]
                logf = -(jnp.maximum(-z, 0.0) + jnp.log1p(jnp.exp(-jnp.abs(z))))
                cs = carry
                for part in _split_bf16(logf, 3):
                    cs = cs + jnp.dot(tri, part, preferred_element_type=F32)
                terms = jnp.zeros((ch, LANES), F32)
                for jj, part in enumerate(_split_bf16(cs * (-LOG2E), 3)):
                    terms = terms + jnp.dot(part, pl_ref[jj], preferred_element_type=F32)
                o_ref[pl.ds(r0, ch), :] = terms.astype(o_ref.dtype)
                out.append(cs[ch - 1:ch, :])
            return tuple(out)

        lax.fori_loop(0, S // ch, body, tuple(jnp.zeros((1, LANES), F32) for _ in range(B)))

    return pl.pallas_call(
        kern, grid=(1,),
        in_specs=[pl.BlockSpec((B * S, LANES), lambda g: (0, 0), pipeline_mode=pl.Buffered(1)),
                  pl.BlockSpec((1, LANES), lambda g: (0, 0)),
                  pl.BlockSpec(place.shape, lambda g: (0, 0, 0))],
        out_specs=pl.BlockSpec((B * S, LANES), lambda g: (0, 0)),
        out_shape=jax.ShapeDtypeStruct((B * S, LANES), BF16),
        compiler_params=_params("arbitrary"), name="forget_terms")(fz, b_f, jnp.asarray(place, BF16))


class Head(NamedTuple):
    q_off: int
    q_half: Optional[str]
    k_off: int
    v0: int
    v1: int


def _flash(q, k, vt, *, B, S, P, q_w, k_w, v_rows, heads, kind, out_dtype, tq=None, tk=None, n_bufs=None,
           ext=None, sel=None, lam=None, subln=None, lam_init=None, name="flash"):
    TQ, TK = tq or ATT_TILE, tk or KEY_TILE
    assert S % TQ == 0 and TQ % TK == 0
    diag = TQ // TK
    nq = S // TQ
    nh = len(heads)
    rows = heads[0].v1 - heads[0].v0
    out_w = nh // 2 * rows if kind == "diff" else nh * rows
    width = TK
    n_bufs = n_bufs or ATT_BUFS
    if kind == "win":
        assert WIN == TK and TQ == TK and S >= 2 * TK and n_bufs == 2
    has_ext = kind in ("fox", "slc")

    def kern(*refs):
        it = iter(refs)
        q_ref, k_ref, v_ref = next(it), next(it), next(it)
        ext_ref = next(it) if has_ext else None
        sel_ref = next(it) if kind == "slc" else None
        lam_ref, sub_ref = (next(it), next(it)) if kind == "diff" else (None, None)
        o_ref, m_ref, acc_ref, mx_ref = next(it), next(it), next(it), next(it)
        bufs = [next(it) for _ in range(n_bufs)]
        p_id = pl.program_id(1)
        i = pl.program_id(2)

        lane = lax.broadcasted_iota(jnp.int32, (TQ, LANES), 1)
        lo = lane < HALF_LANES
        q_all = q_ref[...]
        if kind == "slc":
            drop = ((sel_ref[...].astype(F32) - 1.0) * MASK_BIG).astype(BF16)
        qs = []
        for j, h in enumerate(heads):
            qh = q_all[:, h.q_off:h.q_off + LANES]
            if h.q_half == "lo":
                qh = jnp.where(lo, qh, jnp.zeros_like(qh))
            elif h.q_half == "hi":
                qh = jnp.where(lo, jnp.zeros_like(qh), qh)
            if kind == "fox":
                first = 3 * (nh * p_id + j)
                pick = jnp.where((lane >= first) & (lane < first + 3), 1.0, 0.0).astype(BF16)
                qh = jnp.concatenate([qh, pick], axis=1)
            if kind == "slc":
                qh = jnp.concatenate([qh, drop], axis=1)
            qs.append(qh)

        m_ref[...] = jnp.full(m_ref.shape, NEG, F32)
        acc_ref[...] = jnp.zeros(acc_ref.shape, F32)

        def block_kind(ahead, kb, qb):
            lo_rel = (kb - qb) * MASK_BLOCK - (MASK_BLOCK - 1)
            hi_rel = (kb - qb) * MASK_BLOCK + (MASK_BLOCK - 1)
            floor = ahead - WIN if kind == "win" else lo_rel - 1
            if lo_rel > ahead or hi_rel <= floor:
                return "dead"
            if hi_rel <= ahead and lo_rel > floor:
                return "full"
            return "part"

        def block_mask(ahead, kb, qb):
            rel = (lax.broadcasted_iota(jnp.int32, (MASK_BLOCK, MASK_BLOCK), 0)
                   - lax.broadcasted_iota(jnp.int32, (MASK_BLOCK, MASK_BLOCK), 1)
                   + (kb - qb) * MASK_BLOCK)
            msk = rel <= ahead
            if kind == "win":
                msk = msk & (rel > ahead - WIN)
            return msk

        kbs, qbs = range(width // MASK_BLOCK), range(TQ // MASK_BLOCK)
        blk = lambda b: slice(b * MASK_BLOCK, (b + 1) * MASK_BLOCK)

        def scores(k0, u, ahead=None):
            kx = k_ref[pl.ds(k0, width), :]
            ex = ext_ref[pl.ds(k0, width), :] if has_ext else None
            for hi, h in enumerate(heads):
                kk = kx[:, h.k_off:h.k_off + LANES]
                if ex is not None:
                    kk = jnp.concatenate([kk, ex], axis=1)
                if ahead is None:
                    st = lax.dot_general(kk, qs[hi], _NT, preferred_element_type=F32)
                    bufs[u][hi] = st
                    mx_ref[u * nh + hi] = jnp.max(st, axis=0, keepdims=True)
                    continue
                col_max = [None] * len(qbs)
                for kb in kbs:
                    live = [qb for qb in qbs if block_kind(ahead, kb, qb) != "dead"]
                    if not live:
                        continue
                    q_rows = slice(live[0] * MASK_BLOCK, (live[-1] + 1) * MASK_BLOCK)
                    st = lax.dot_general(kk[blk(kb)], qs[hi][q_rows], _NT, preferred_element_type=F32)
                    for qb in range(live[0], live[-1] + 1):
                        sb = st[:, blk(qb - live[0])]
                        if block_kind(ahead, kb, qb) == "part":
                            sb = jnp.where(block_mask(ahead, kb, qb), sb, NEG)
                        bufs[u][hi, blk(kb), blk(qb)] = sb
                        cm = jnp.max(sb, axis=0, keepdims=True)
                        col_max[qb] = cm if col_max[qb] is None else jnp.maximum(col_max[qb], cm)
                mx_ref[u * nh + hi] = jnp.concatenate(col_max, axis=1)

        def update(k0, u, ahead=None):
            for hi, h in enumerate(heads):
                m_prev = m_ref[hi]
                m_new = jnp.maximum(m_prev, mx_ref[u * nh + hi])
                alpha = jnp.exp2(m_prev - m_new)
                if ahead is None:
                    ones = jnp.ones((BF16_ROWS, width), BF16)
                    pt = jnp.exp2(bufs[u][hi] - m_new).astype(BF16)
                    lhs = jnp.concatenate([v_ref[h.v0:h.v1, pl.ds(k0, width)], ones], axis=0)
                    new = jnp.dot(lhs, pt, preferred_element_type=F32)
                else:
                    ones = jnp.ones((BF16_ROWS, MASK_BLOCK), BF16)
                    cols = []
                    for qb in qbs:
                        col = None
                        for kb in kbs:
                            if block_kind(ahead, kb, qb) == "dead":
                                continue
                            pt = jnp.exp2(bufs[u][hi, blk(kb), blk(qb)] - m_new[:, blk(qb)]).astype(BF16)
                            kstart = pl.multiple_of(k0 + kb * MASK_BLOCK, MASK_BLOCK)
                            lhs = jnp.concatenate(
                                [v_ref[h.v0:h.v1, pl.ds(kstart, MASK_BLOCK)], ones], axis=0)
                            part = jnp.dot(lhs, pt, preferred_element_type=F32)
                            col = part if col is None else col + part
                        cols.append(col)
                    new = jnp.concatenate(cols, axis=1)
                acc_ref[hi] = alpha * acc_ref[hi] + new
                m_ref[hi] = m_new

        tile = lambda t: pl.multiple_of(t * TK, TK)
        U = len(bufs)
        if kind == "win":
            @pl.when(i == 0)
            def _():
                scores(tile(0), 0, 0)
                update(tile(0), 0, 0)

            @pl.when(i > 0)
            def _():
                scores(tile(i - 1), 0, WIN)
                scores(tile(i), 1, 0)
                update(tile(i - 1), 0, WIN)
                update(tile(i), 1, 0)
        else:
            n_full = i * diag
            n_rounds = jnp.maximum(n_full - 1, 0) // U
            pl.when(n_full == 0)(lambda: scores(tile(0), 0, 0))
            pl.when(n_full > 0)(lambda: scores(tile(0), 0))

            def body(j, c):
                for u in range(U):
                    scores(tile(U * j + u + 1), (u + 1) % U)
                    update(tile(U * j + u), u)
                return c
            lax.fori_loop(0, n_rounds, body, 0)
            t0 = U * n_rounds

            def tail(left):
                ahead = lambda u: -(u - left) * TK if u >= left else None
                last = left + diag - 1
                for u in range(last):
                    scores(tile(t0 + u + 1), (u + 1) % U, ahead(u + 1))
                    update(tile(t0 + u), u % U, ahead(u))
                update(tile(t0 + last), last % U, ahead(last))

            for left in range(U + 1):
                pl.when(n_full - t0 == left)(lambda left=left: tail(left))

        outs = []
        for hi in range(nh):
            acc = acc_ref[hi]
            outs.append(acc[:rows] / acc[rows:rows + 1])
        if kind == "diff":
            lm = lam_ref[...]
            la = jnp.sum(lm[0:1] * lm[1:2], axis=1, keepdims=True)
            lb = jnp.sum(lm[2:3] * lm[3:4], axis=1, keepdims=True)
            lam_full = jnp.exp(la) - jnp.exp(lb) + lam_init
            for g in range(nh // 2):
                oa = jnp.transpose(outs[2 * g] - lam_full * outs[2 * g + 1])
                y = oa * lax.rsqrt(jnp.mean(oa * oa, axis=-1, keepdims=True) + EPS)
                y = (y * sub_ref[...]) * (1.0 - lam_init)
                o_ref[:, g * rows:(g + 1) * rows] = y.astype(o_ref.dtype)
        else:
            o_ref[...] = jnp.transpose(jnp.concatenate(outs, axis=0)).astype(o_ref.dtype)

    in_specs = [
        pl.BlockSpec((TQ, q_w), lambda b, p, i: (b * nq + i, p)),
        pl.BlockSpec((S, k_w), lambda b, p, i: (b, p)),
        pl.BlockSpec((v_rows, S), lambda b, p, i: (p, b)),
    ]
    args = [q, k, vt]
    if kind == "fox":
        in_specs.append(pl.BlockSpec((S, LANES), lambda b, p, i: (b, 0)))
        args.append(ext)
    if kind == "slc":
        in_specs.append(pl.BlockSpec((S, LANES), lambda b, p, i: (0, 0)))
        in_specs.append(pl.BlockSpec((TQ, LANES), lambda b, p, i: (b * nq + i, p)))
        args += [ext, sel]
    if kind == "diff":
        in_specs.append(pl.BlockSpec(lam.shape, lambda b, p, i: (0, 0)))
        in_specs.append(pl.BlockSpec((1, LANES), lambda b, p, i: (0, 0)))
        args += [lam, subln.reshape(1, LANES)]
    return pl.pallas_call(
        kern, grid=(B, P, nq), in_specs=in_specs,
        out_specs=pl.BlockSpec((TQ, out_w), lambda b, p, i: (b * nq + i, p)),
        out_shape=jax.ShapeDtypeStruct((B * S, P * out_w), out_dtype),
        scratch_shapes=[pltpu.VMEM((nh, 1, TQ), F32),
                        pltpu.VMEM((nh, rows + BF16_ROWS, TQ), F32),
                        pltpu.VMEM((n_bufs * nh, 1, TQ), F32)]
        + [pltpu.VMEM((nh, width, TQ), F32)] * n_bufs,
        compiler_params=_params("parallel", "parallel", "arbitrary"), name=name)(*args)


_DIFF_HEADS = tuple(Head(s * LANES, half, s * LANES, s * LANES, (s + 1) * LANES)
                    for s in range(2) for half in ("lo", "hi"))
_PAIR_HEADS = tuple(Head(s * LANES, half, s * LANES, (2 * s + j) * HALF_LANES, (2 * s + j + 1) * HALF_LANES)
                    for s in range(2) for j, half in enumerate(("lo", "hi")))
_QUAD_HEADS = tuple(Head(s * LANES, half, 0, 0, HALF_LANES) for s in range(2) for half in ("lo", "hi"))
_WIDE_HEADS = tuple(Head(j * LANES, None, j * LANES, j * HALF_LANES, (j + 1) * HALF_LANES) for j in range(4))


def _nsa_compress(xk, xv, pe, w1x, w2d):
    B, R, K = xk.shape
    G = NSA_GROUPS

    def kern(xk_ref, xv_ref, pe_ref, w1_ref, w2_ref, k_ref, vt_ref):
        for t, x_ref in enumerate((xk_ref, xv_ref)):
            x = x_ref[...]
            xa = (x + pe_ref[t, 0]).astype(BF16)
            xb = (x + pe_ref[t, 1]).astype(BF16)
            for g in range(G):
                a = jnp.dot(xa, w1_ref[t, 0, g], preferred_element_type=F32)
                b = jnp.dot(xb, w1_ref[t, 1, g], preferred_element_type=F32)
                h = a + pltpu.roll(b, R - 1, 0)
                hs = h * _sigmoid(h)
                o = jnp.dot(hs.astype(BF16), w2_ref[t], preferred_element_type=F32)
                if t == 0:
                    k_ref[g] = o
                else:
                    vt_ref[g] = jnp.transpose(o)

    return pl.pallas_call(
        kern, grid=(B,),
        in_specs=[pl.BlockSpec((None, R, K), lambda b: (b, 0, 0)),
                  pl.BlockSpec((None, R, K), lambda b: (b, 0, 0)),
                  pl.BlockSpec(pe.shape, lambda b: (0, 0, 0, 0)),
                  pl.BlockSpec(w1x.shape, lambda b: (0, 0, 0, 0, 0)),
                  pl.BlockSpec(w2d.shape, lambda b: (0, 0, 0))],
        out_specs=[pl.BlockSpec((None, G, R, LANES), lambda b: (b, 0, 0, 0)),
                   pl.BlockSpec((None, G, LANES, R), lambda b: (b, 0, 0, 0))],
        out_shape=[jax.ShapeDtypeStruct((B, G, R, LANES), F32),
                   jax.ShapeDtypeStruct((B, G, LANES, R), F32)],
        compiler_params=_params("parallel"), name="nsa_compress")(xk, xv, pe, w1x, w2d)


def _nsa_cmp_select(q, cmp_k, cmp_vt, ov_t, *, B, S):
    T = SEL_TILE
    nq = S // T
    R = cmp_k.shape[2]
    G = NSA_GROUPS
    d = NSA_DIM
    n_sel = min(SLC_TOPK, S // SLC_LEN)
    assert n_sel >= 3 and FORCE_SCORE > NSA_HEADS // G

    bucket = min(LANES, R)
    assert R % bucket == 0 and bucket % (SLC_LEN // CMP_STRIDE) == 0

    def kern(q_ref, kc_ref, vt_ref, ov_ref, o_ref, sel_ref):
        i = pl.program_id(2)
        lane = lax.broadcasted_iota(jnp.int32, (T, LANES), 1)
        lo = lane < HALF_LANES
        q_all = q_ref[...]
        qs = []
        for hh in range(4):
            qh = q_all[:, (hh // 2) * LANES:(hh // 2 + 1) * LANES]
            qs.append(jnp.where(lo, qh, jnp.zeros_like(qh)) if hh % 2 == 0
                      else jnp.where(lo, jnp.zeros_like(qh), qh))

        def variant(rv):
            nb = rv * CMP_STRIDE // SLC_LEN
            tq = lax.broadcasted_iota(jnp.int32, (rv, T), 1) + i * T
            cend = lax.broadcasted_iota(jnp.int32, (rv, T), 0) * CMP_STRIDE + (CMP_LEN - 1)
            vis = cend <= tq
            kc = kc_ref[:rv, :].astype(BF16)
            vt = vt_ref[:d, :rv].astype(BF16)
            psum = jnp.zeros((rv, T), F32)
            o_heads = []
            for hh in range(4):
                st = lax.dot_general(kc, qs[hh], _NT, preferred_element_type=F32)
                st = jnp.where(vis, st, -jnp.inf)
                m = jnp.max(st, axis=0, keepdims=True)
                e = jnp.exp2(st - jnp.where(m > -jnp.inf, m, 0.0))
                p = e / jnp.maximum(jnp.sum(e, axis=0, keepdims=True), 1e-30)
                o_heads.append(jnp.dot(vt, p.astype(BF16), preferred_element_type=F32))
                psum = psum + p
            o_ref[...] = jnp.transpose(jnp.concatenate(o_heads, axis=0)).astype(o_ref.dtype)
            imp = jnp.zeros((nb, T), F32)
            for part in _split_bf16(psum, 2):
                imp = imp + jnp.dot(ov_ref[:nb, :rv], part, preferred_element_type=F32)
            blk = lax.broadcasted_iota(jnp.int32, (nb, T), 0)
            cur = (lax.broadcasted_iota(jnp.int32, (nb, T), 1) + i * T) // SLC_LEN
            forced = (blk == 0) | (blk == cur) | (blk == cur - 1)
            work = jnp.where(forced, -jnp.inf, imp)
            work = jnp.where(blk > cur, NEG, work)
            blk_f = blk.astype(F32)
            for _ in range(n_sel - 3):
                mx = jnp.max(work, axis=0, keepdims=True)
                first = jnp.min(jnp.where(work == mx, blk_f, float(LANES)), axis=0, keepdims=True)
                work = jnp.where(blk_f == first, -jnp.inf, work)
            sel = jnp.where(work == -jnp.inf, 1.0, 0.0)
            if nb < LANES:
                sel = jnp.concatenate([sel, jnp.zeros((LANES - nb, T), F32)], axis=0)
            sel_ref[...] = jnp.transpose(sel).astype(sel_ref.dtype)

        need = (i + 1) * (T // CMP_STRIDE)
        which = (need - 1) // bucket
        for bk in range(R // bucket):
            pl.when(which == bk)(lambda bk=bk: variant((bk + 1) * bucket))

    return pl.pallas_call(
        kern, grid=(B, G, nq),
        in_specs=[pl.BlockSpec((T, 2 * LANES), lambda b, g, i: (b * nq + i, g)),
                  pl.BlockSpec((None, None, R, LANES), lambda b, g, i: (b, g, 0, 0)),
                  pl.BlockSpec((None, None, LANES, R), lambda b, g, i: (b, g, 0, 0)),
                  pl.BlockSpec((LANES, R), lambda b, g, i: (0, 0))],
        out_specs=[pl.BlockSpec((T, 2 * LANES), lambda b, g, i: (b * nq + i, g)),
                   pl.BlockSpec((T, LANES), lambda b, g, i: (b * nq + i, g))],
        out_shape=[jax.ShapeDtypeStruct((B * S, G * 2 * LANES), BF16),
                   jax.ShapeDtypeStruct((B * S, G * LANES), BF16)],
        compiler_params=_params("parallel", "parallel", "parallel"), name="nsa_cmp_select")(
            q, cmp_k, cmp_vt, ov_t)


def _odd_out(o_cmp, o_slc, o_win, gz, expand, o_mla, w_nsa, w_mla, x):
    N, C = o_cmp.shape
    D = x.shape[1]
    tm = min(ROW_TILE, N)

    def kern(c_ref, s_ref, w_ref, g_ref, e_ref, m_ref, wn_ref, wm_ref, x_ref, o_ref):
        gate = _sigmoid(g_ref[...])
        parts = _split_bf16(gate, 2)
        acc = jnp.zeros((tm, C), F32)
        for j, br in enumerate((c_ref, s_ref, w_ref)):
            gj = sum(jnp.dot(part, e_ref[j], preferred_element_type=F32) for part in parts)
            acc = acc + gj * br[...].astype(F32)
        o_nsa = acc.astype(BF16)
        o_mla_ = m_ref[...]
        for c0 in range(0, D, COL_CHUNK):
            cols = slice(c0, c0 + COL_CHUNK)
            y = (jnp.dot(o_nsa, wn_ref[:, cols], preferred_element_type=F32)
                 + jnp.dot(o_mla_, wm_ref[:, cols], preferred_element_type=F32))
            o_ref[:, cols] = x_ref[:, cols] + y

    row = pl.BlockSpec((tm, C), lambda i: (i, 0))
    wide = pl.BlockSpec((tm, D), lambda i: (i, 0))
    return pl.pallas_call(
        kern, grid=(N // tm,),
        in_specs=[row, row, row, pl.BlockSpec((tm, LANES), lambda i: (i, 0)),
                  pl.BlockSpec(expand.shape, lambda i: (0, 0, 0)),
                  pl.BlockSpec((tm, o_mla.shape[1]), lambda i: (i, 0)),
                  pl.BlockSpec(w_nsa.shape, lambda i: (0, 0)),
                  pl.BlockSpec(w_mla.shape, lambda i: (0, 0)), wide],
        out_specs=wide, out_shape=jax.ShapeDtypeStruct((N, D), F32),
        compiler_params=_params("parallel"), name="odd_out")(
            o_cmp, o_slc, o_win, gz, expand, o_mla, w_nsa, w_mla, x)


def _mla_kv(ckv, gain, wk, wv, kr):
    N, K = ckv.shape
    tm = min(ROW_TILE, N)
    H = MLA_HEADS

    def kern(c_ref, g_ref, wk_ref, wv_ref, kr_ref, k_ref, v_ref):
        xf = c_ref[...]
        y = xf * lax.rsqrt(jnp.mean(xf * xf, axis=-1, keepdims=True) + EPS)
        a = (y * g_ref[...]).astype(BF16)
        kr2 = jnp.concatenate([kr_ref[...]] * 2, axis=1)
        for h in range(0, H, 2):
            cols = slice(h * LANES, (h + 2) * LANES)
            kh = jnp.dot(a, wk_ref[:, cols], preferred_element_type=F32)
            k_ref[:, cols] = (kh + kr2).astype(k_ref.dtype)
        v_ref[...] = jnp.dot(a, wv_ref[...], preferred_element_type=F32).astype(v_ref.dtype)

    return pl.pallas_call(
        kern, grid=(N // tm,),
        in_specs=[pl.BlockSpec((tm, K), lambda i: (i, 0)),
                  pl.BlockSpec((1, K), lambda i: (0, 0)),
                  pl.BlockSpec(wk.shape, lambda i: (0, 0)),
                  pl.BlockSpec(wv.shape, lambda i: (0, 0)),
                  pl.BlockSpec((tm, LANES), lambda i: (i, 0))],
        out_specs=[pl.BlockSpec((tm, H * LANES), lambda i: (i, 0)),
                   pl.BlockSpec((tm, H * MLA_V), lambda i: (i, 0))],
        out_shape=[jax.ShapeDtypeStruct((N, H * LANES), BF16),
                   jax.ShapeDtypeStruct((N, H * MLA_V), BF16)],
        compiler_params=_params("parallel"), name="mla_kv")(
            ckv, gain.reshape(1, K).astype(F32), wk, wv, kr)


def _mem_attn(x, gain, wq, wo, mem_k, mem_v, *, S):
    N, D = x.shape
    tm = min(ROW_TILE, S)
    M = mem_k.shape[1]
    per_b = S // tm
    scale = XA_DIM ** -0.5

    def kern(x_ref, g_ref, wq_ref, wo_ref, k_ref, v_ref, o_ref):
        xf = x_ref[...]
        y = xf * lax.rsqrt(jnp.mean(xf * xf, axis=-1, keepdims=True) + EPS)
        a = (y * g_ref[...]).astype(BF16)
        q = (jnp.dot(a, wq_ref[...], preferred_element_type=F32) * scale).astype(BF16)
        heads = []
        for h in range(XA_HEADS):
            sl = slice(h * XA_DIM, (h + 1) * XA_DIM)
            s = lax.dot_general(q[:, sl], k_ref[:, sl], _NT, preferred_element_type=F32)
            e = jnp.exp(s - jnp.max(s, axis=1, keepdims=True))
            p = e / jnp.sum(e, axis=1, keepdims=True)
            heads.append(jnp.dot(p.astype(BF16), v_ref[:, sl], preferred_element_type=F32).astype(BF16))
        o = jnp.concatenate(heads, axis=1)
        o_ref[...] = xf + jnp.dot(o, wo_ref[...], preferred_element_type=F32)

    return pl.pallas_call(
        kern, grid=(N // tm,),
        in_specs=[pl.BlockSpec((tm, D), lambda i: (i, 0)),
                  pl.BlockSpec((1, D), lambda i: (0, 0)),
                  pl.BlockSpec(wq.shape, lambda i: (0, 0)),
                  pl.BlockSpec(wo.shape, lambda i: (0, 0)),
                  pl.BlockSpec((None, M, XA_HEADS * XA_DIM), lambda i: (i // per_b, 0, 0)),
                  pl.BlockSpec((None, M, XA_HEADS * XA_DIM), lambda i: (i // per_b, 0, 0))],
        out_specs=pl.BlockSpec((tm, D), lambda i: (i, 0)),
        out_shape=jax.ShapeDtypeStruct((N, D), F32),
        compiler_params=_params("parallel"), name="mem_attn")(
            x, gain.reshape(1, D).astype(F32), wq, wo, mem_k, mem_v)


def _ffn(x, gain, w13, w2, out_gain=None):
    N, D = x.shape
    FF = w2.shape[0]
    tm = min(ROW_TILE, N)
    chunk = 2 * LANES
    assert FF % chunk == 0

    final = out_gain is not None

    def kern(x_ref, g_ref, w13_ref, w2_ref, *rest):
        og_ref, o_ref = rest if final else (None, rest[0])
        xf = x_ref[...]
        y = xf * lax.rsqrt(jnp.mean(xf * xf, axis=-1, keepdims=True) + EPS)
        a = (y * g_ref[...]).astype(BF16)
        acc = xf
        for c in range(0, FF, chunk):
            g = jnp.dot(a, w13_ref[:, c:c + chunk], preferred_element_type=F32)
            u = jnp.dot(a, w13_ref[:, FF + c:FF + c + chunk], preferred_element_type=F32)
            hdn = (g * _sigmoid(g) * u).astype(BF16)
            acc = acc + jnp.dot(hdn, w2_ref[c:c + chunk, :], preferred_element_type=F32)
        if final:
            acc = acc * lax.rsqrt(jnp.mean(acc * acc, axis=-1, keepdims=True) + EPS) * og_ref[...]
        o_ref[...] = acc

    vec = pl.BlockSpec((1, D), lambda i: (0, 0))
    in_specs = [pl.BlockSpec((tm, D), lambda i: (i, 0)), vec,
                pl.BlockSpec(w13.shape, lambda i: (0, 0), pipeline_mode=pl.Buffered(1)),
                pl.BlockSpec(w2.shape, lambda i: (0, 0), pipeline_mode=pl.Buffered(1))]
    args = [x, gain.reshape(1, D).astype(F32), w13, w2]
    if final:
        in_specs.append(vec)
        args.append(out_gain.reshape(1, D).astype(F32))
    return pl.pallas_call(
        kern, grid=(N // tm,), in_specs=in_specs,
        out_specs=pl.BlockSpec((tm, D), lambda i: (i, 0)),
        out_shape=jax.ShapeDtypeStruct((N, D), F32),
        compiler_params=_params("parallel"), name="ffn")(*args)


def _pad_cols(w, width):
    return jnp.pad(w, ((0, 0), (0, width - w.shape[1])))


def _even_mixer(x, gain, w_in, b_f, lam, subln, w_out, layer_idx, ropes, B, S):
    D = x.shape[1]
    blk = DIFF_HEADS * 2 * DIFF_QK
    w = _pad_cols(w_in, 6 * blk + LANES).astype(BF16)
    sc = DIFF_QK ** -0.5 * LOG2E
    segs = [Seg(0, blk, BF16, "r64", sc), Seg(blk, blk, BF16, "r64"), Seg(2 * blk, blk, BF16),
            Seg(3 * blk, blk, BF16, None, sc), Seg(4 * blk, blk, BF16), Seg(5 * blk, blk, BF16),
            Seg(6 * blk, LANES, F32)]
    aq, ak, av, fq, fk, fv, fz = _linear([x], [w], segs, gain=gain, ropes=ropes, seq=S, name="even_in")
    lam_init = 0.8 - 0.6 * math.exp(-0.3 * layer_idx)
    oa = _flash(aq, ak, av.T, B=B, S=S, P=DIFF_HEADS // 2, q_w=2 * LANES, k_w=2 * LANES, v_rows=2 * LANES,
                heads=_DIFF_HEADS, kind="diff", out_dtype=BF16, lam=lam.astype(F32),
                subln=subln.astype(F32), lam_init=lam_init, name="diff_attn")
    terms = _forget_key_terms(fz, _pad_cols(b_f.reshape(1, -1), LANES).astype(F32), B, S)
    of = _flash(fq, fk, fv.T, B=B, S=S, P=FOX_HEADS // 4, q_w=2 * LANES, k_w=2 * LANES, v_rows=2 * LANES,
                heads=_PAIR_HEADS, kind="fox", out_dtype=BF16, ext=terms, name="fox_attn")
    wo = w_out.astype(BF16)
    (y,) = _linear([oa, of], [wo[:blk], wo[blk:]], [Seg(0, D, F32)], residual=x, name="even_out")
    return y


def _odd_in_weight(w_in):
    d = NSA_DIM
    o = np.cumsum((0, NSA_HEADS * d) + (NSA_GROUPS * d,) * 6 + (NSA_HEADS * 3, MLA_Q_RANK, MLA_KV_RANK, MLA_ROPE))
    nq, kc, vc, ks, vs, kw, vw, gz, cq, ckv, kr = [w_in[:, o[j]:o[j + 1]] for j in range(11)]

    def dup(wg):
        return jnp.concatenate([wg[:, :d], wg[:, :d], wg[:, d:], wg[:, d:]], axis=1)

    zeros = lambda n: jnp.zeros((w_in.shape[0], n), w_in.dtype)
    kr_slab = jnp.concatenate([zeros(MLA_NOPE), kr, zeros(LANES - MLA_NOPE - MLA_ROPE)], axis=1)
    cols = [nq, kc, vc, dup(ks), vs, dup(kw), vw, _pad_cols(gz, LANES), cq, ckv, kr_slab]
    return jnp.concatenate(cols, axis=1).astype(BF16)


def _odd_mixer(x, gain, w_in, cmp_pos, cmp_w1, cmp_w2, q_norm, kv_norm, w_uq, w_ukv, w_out, ropes, B, S):
    D = x.shape[1]
    G, d = NSA_GROUPS, NSA_DIM
    w = _odd_in_weight(w_in)
    sc = d ** -0.5 * LOG2E
    widths = [(NSA_HEADS * d, BF16, "r64", sc), (LANES, F32, "r64", 1.0), (LANES, F32, None, 1.0),
              (2 * LANES, BF16, "r64", 1.0), (LANES, BF16, None, 1.0),
              (2 * LANES, BF16, "r64", 1.0), (LANES, BF16, None, 1.0),
              (LANES, F32, None, 1.0), (MLA_Q_RANK, F32, None, 1.0), (MLA_KV_RANK, F32, None, 1.0),
              (LANES, F32, "mla", 1.0)]
    segs, start = [], 0
    for wd, dt, rp, s_ in widths:
        segs.append(Seg(start, wd, dt, rp, s_))
        start += wd
    q, kc, vc, ks, vs, kw, vw, gz, cq, ckv, kr = _linear(
        [x], [w], segs, gain=gain, ropes=ropes, seq=S, name="odd_in")

    R = S // CMP_STRIDE

    K = CMP_STRIDE * G * d
    half = CMP_LEN // 2
    pe = jnp.broadcast_to(cmp_pos.reshape(2, 2, half, 1, d), (2, 2, half, G, d)).reshape(2, 2, 1, K).astype(F32)
    eye = jnp.eye(G, dtype=cmp_w1.dtype)
    w1x = (cmp_w1.reshape(2, 2, 1, half, 1, d, CMP_HIDDEN) * eye[None, None, :, None, :, None, None])
    w1x = w1x.reshape(2, 2, G, K, CMP_HIDDEN).astype(BF16)
    w2d = jnp.concatenate([cmp_w2, cmp_w2], axis=-1).astype(BF16)
    cmp_k, cmp_vt = _nsa_compress(kc.reshape(B, R, K), vc.reshape(B, R, K), pe, w1x, w2d)
    cs = np.arange(R) * CMP_STRIDE
    bs = np.arange(LANES) * SLC_LEN
    ov = ((cs[:, None] < bs[None, :] + SLC_LEN) & (cs[:, None] + CMP_LEN > bs[None, :])
          & (np.arange(R)[:, None] < R - 1) & (bs[None, :] < S))
    o_cmp, sel = _nsa_cmp_select(q, cmp_k, cmp_vt, jnp.asarray(ov.T.astype(np.float32), BF16), B=B, S=S)
    onehot = (np.arange(S)[:, None] // SLC_LEN == np.arange(LANES)[None, :]).astype(np.float32)
    o_slc = _flash(q, ks, vs.T, B=B, S=S, P=G, q_w=2 * LANES, k_w=LANES, v_rows=HALF_LANES,
                   heads=_QUAD_HEADS, kind="slc", out_dtype=BF16, ext=jnp.asarray(onehot, BF16),
                   sel=sel, name="nsa_slc")
    o_win = _flash(q, kw, vw.T, B=B, S=S, P=G, q_w=2 * LANES, k_w=LANES, v_rows=HALF_LANES,
                   heads=_QUAD_HEADS, kind="win", out_dtype=BF16, tq=WIN, tk=WIN, n_bufs=2, name="nsa_win")
    ex = np.zeros((3, LANES, NSA_HEADS * d), np.float32)
    for hh in range(NSA_HEADS):
        for j in range(3):
            ex[j, hh * 3 + j, hh * d:(hh + 1) * d] = 1.0

    H = MLA_HEADS
    qk = MLA_NOPE + MLA_ROPE
    wq_slab = jnp.pad(w_uq.reshape(MLA_Q_RANK, H, qk), ((0, 0), (0, 0), (0, LANES - qk)))
    wq_slab = wq_slab.reshape(MLA_Q_RANK, H * LANES).astype(BF16)
    (qm,) = _linear([cq], [wq_slab], [Seg(0, H * LANES, BF16, "mla", qk ** -0.5 * LOG2E)],
                    gain=q_norm, ropes=ropes, seq=S, name="mla_q")
    wkv = w_ukv.reshape(MLA_KV_RANK, H, MLA_NOPE + MLA_V)
    wk = jnp.pad(wkv[:, :, :MLA_NOPE], ((0, 0), (0, 0), (0, LANES - MLA_NOPE)))
    wk = wk.reshape(MLA_KV_RANK, H * LANES).astype(BF16)
    wv = wkv[:, :, MLA_NOPE:].reshape(MLA_KV_RANK, H * MLA_V).astype(BF16)
    km, vm = _mla_kv(ckv, kv_norm, wk, wv, kr)
    o_mla = _flash(qm, km, vm.T, B=B, S=S, P=H // 4, q_w=4 * LANES, k_w=4 * LANES, v_rows=2 * LANES,
                   heads=_WIDE_HEADS, kind="mla", out_dtype=BF16, name="mla_attn")
    wo = w_out.astype(BF16)
    half = NSA_HEADS * d
    return _odd_out(o_cmp, o_slc, o_win, gz, jnp.asarray(ex, BF16), o_mla, wo[:half], wo[half:], x)


def kernel(x, mem, mem_norm, norm_mix, norm_mem, norm_ffn, ev_w_in, ev_b_f, ev_lam, ev_subln, ev_w_out, od_w_in, nsa_cmp_pos, nsa_cmp_w1, nsa_cmp_w2, mla_q_norm, mla_kv_norm, mla_w_uq, mla_w_ukv, od_w_out, xa_wq, xa_wkv, xa_wo, ffn_w13, ffn_w2, final_norm):
    B, S, D = x.shape
    M = mem.shape[1]
    depth = norm_mix.shape[0]
    ropes = {
        "r64": (_rope_tables(S, NSA_DIM // 2, 0, LANES), NSA_DIM // 2),
        "mla": (_rope_tables(S, MLA_ROPE // 2, MLA_NOPE, MLA_NOPE + MLA_ROPE), MLA_ROPE // 2),
    }
    xa_w = XA_HEADS * XA_DIM
    h = x.reshape(B * S, D)
    mem2 = mem.reshape(B * M, D)
    for li in range(depth):
        j = li // 2
        if li % 2 == 0:
            h = _even_mixer(h, norm_mix[li], ev_w_in[j], ev_b_f[j], ev_lam[j], ev_subln[j], ev_w_out[j],
                            li, ropes, B, S)
        else:
            h = _odd_mixer(h, norm_mix[li], od_w_in[j], nsa_cmp_pos[j], nsa_cmp_w1[j], nsa_cmp_w2[j],
                           mla_q_norm[j], mla_kv_norm[j], mla_w_uq[j], mla_w_ukv[j], od_w_out[j],
                           ropes, B, S)
        mk, mv = _linear([mem2], [xa_wkv[li].astype(BF16)],
                         [Seg(0, xa_w, BF16), Seg(xa_w, xa_w, BF16)], gain=mem_norm, name="mem_kv")
        h = _mem_attn(h, norm_mem[li], xa_wq[li].astype(BF16), xa_wo[li].astype(BF16),
                      mk.reshape(B, M, xa_w), mv.reshape(B, M, xa_w), S=S)
        h = _ffn(h, norm_ffn[li], ffn_w13[li].astype(BF16), ffn_w2[li].astype(BF16),
                 out_gain=final_norm if li == depth - 1 else None)
    return h.reshape(B, S, D)
```

```python
import math
from typing import NamedTuple, Optional

import numpy as np
import jax
import jax.numpy as jnp
from jax import lax
from jax.experimental import pallas as pl
from jax.experimental.pallas import tpu as pltpu

F32 = jnp.float32
BF16 = jnp.bfloat16

LANES = 128
HALF_LANES = LANES // 2
BF16_ROWS = 16
ROPE_THETA = 10000.0
EPS = 1e-6
NEG = -1e30
LOG2E = math.log2(math.e)
MASK_BIG = 2.0 ** 100

DIFF_HEADS = 4
DIFF_QK = 64
FOX_HEADS = 8
FOX_ONES_LANE = 3 * FOX_HEADS
NSA_HEADS = 8
NSA_GROUPS = 2
NSA_DIM = 64
CMP_LEN = 32
CMP_STRIDE = 16
CMP_HIDDEN = 128
SLC_LEN = 64
SLC_TOPK = 16
WIN = 512
FORCE_SCORE = 1e4
MLA_HEADS = 8
MLA_NOPE = 64
MLA_ROPE = 32
MLA_V = 64
MLA_Q_RANK = 384
MLA_KV_RANK = 256
XA_HEADS = 4
XA_DIM = 128

VMEM_LIMIT = 56 * 1024 * 1024
ROW_TILE = 512
COL_CHUNK = 512
ATT_TILE = 512
KEY_TILE = 512
ATT_BUFS = 3
MASK_BLOCK = 256
SEL_TILE = 1024
CUM_CHUNK = 256

_NT = (((1,), (1,)), ((), ()))


def _params(*sem):
    return pltpu.CompilerParams(dimension_semantics=sem, vmem_limit_bytes=VMEM_LIMIT)


def _sigmoid(x):
    return 1.0 / (1.0 + jnp.exp(-x))


def _split_bf16(x, terms):
    out = []
    r = x
    for _ in range(terms):
        h = r.astype(BF16)
        out.append(h)
        r = r - h.astype(F32)
    return out


def _rope_tables(S, half, lane_lo, lane_hi):
    pos = jnp.arange(S, dtype=F32)
    inv = 1.0 / (ROPE_THETA ** (jnp.arange(half, dtype=F32) / half))
    ang = pos[:, None] * inv[None, :]
    cos, sin = jnp.cos(ang), jnp.sin(ang)
    lane = np.arange(LANES)
    active = (lane >= lane_lo) & (lane < lane_hi)
    j = (lane - lane_lo) % (2 * half)
    lower = active & (j < half)
    upper = active & (j >= half)
    idx = j % half
    cos_t = jnp.where(active[None, :], cos[:, idx], 1.0)
    sin_a = jnp.where(lower[None, :], -sin[:, idx], 0.0)
    sin_b = jnp.where(upper[None, :], sin[:, idx], 0.0)
    return cos_t, sin_a, sin_b


class Seg(NamedTuple):
    start: int
    width: int
    dtype: object
    rope: Optional[str] = None
    scale: float = 1.0


def _linear(xs, ws, segs, *, gain=None, residual=None, ropes=None, seq=None, name="linear"):
    N = xs[0].shape[0]
    tm = min(ROW_TILE, N)
    assert N % tm == 0
    n_in = len(xs)
    has_gain = gain is not None
    has_res = residual is not None
    rope_keys = sorted({s.rope for s in segs if s.rope})
    halves = {k: ropes[k][1] for k in rope_keys}

    def kern(*refs):
        it = iter(refs)
        x_refs = [next(it) for _ in range(n_in)]
        w_refs = [next(it) for _ in range(n_in)]
        g_ref = next(it) if has_gain else None
        r_ref = next(it) if has_res else None
        tabs = {k: (next(it), next(it), next(it)) for k in rope_keys}
        o_refs = [next(it) for _ in segs]
        acts = []
        for j, xr in enumerate(x_refs):
            x = xr[...]
            if j == 0 and has_gain:
                xf = x.astype(F32)
                y = xf * lax.rsqrt(jnp.mean(xf * xf, axis=-1, keepdims=True) + EPS)
                acts.append((y * g_ref[...]).astype(BF16))
            else:
                acts.append(x.astype(BF16))
        for seg, o_ref in zip(segs, o_refs):
            for c0 in range(0, seg.width, COL_CHUNK):
                cw = min(COL_CHUNK, seg.width - c0)
                col = seg.start + c0
                acc = None
                for a, wr in zip(acts, w_refs):
                    d = jnp.dot(a, wr[:, col:col + cw], preferred_element_type=F32)
                    acc = d if acc is None else acc + d
                if has_res:
                    acc = acc + r_ref[:, col:col + cw]
                if seg.scale != 1.0:
                    acc = acc * seg.scale
                if seg.rope is None:
                    o_ref[:, c0:c0 + cw] = acc.astype(o_ref.dtype)
                else:
                    cos_r, sa_r, sb_r = tabs[seg.rope]
                    half = halves[seg.rope]
                    cos, sa, sb = cos_r[...], sa_r[...], sb_r[...]
                    for s0 in range(0, cw, LANES):
                        xs_ = acc[:, s0:s0 + LANES]
                        y = (xs_ * cos + pltpu.roll(xs_, LANES - half, 1) * sa
                             + pltpu.roll(xs_, half, 1) * sb)
                        o_ref[:, c0 + s0:c0 + s0 + LANES] = y.astype(o_ref.dtype)

    in_specs, args = [], []
    for x in xs:
        in_specs.append(pl.BlockSpec((tm, x.shape[1]), lambda i: (i, 0)))
        args.append(x)
    for w in ws:
        in_specs.append(pl.BlockSpec(w.shape, lambda i: (0, 0)))
        args.append(w)
    if has_gain:
        in_specs.append(pl.BlockSpec((1, gain.shape[-1]), lambda i: (0, 0)))
        args.append(gain.reshape(1, -1).astype(F32))
    if has_res:
        in_specs.append(pl.BlockSpec((tm, residual.shape[1]), lambda i: (i, 0)))
        args.append(residual)
    for k in rope_keys:
        assert seq % tm == 0
        nt = seq // tm
        for t in ropes[k][0]:
            in_specs.append(pl.BlockSpec((tm, LANES), lambda i, nt=nt: (i % nt, 0)))
            args.append(t)
    out_shape = [jax.ShapeDtypeStruct((N, s.width), s.dtype) for s in segs]
    out_specs = [pl.BlockSpec((tm, s.width), lambda i: (i, 0)) for s in segs]
    return pl.pallas_call(
        kern, grid=(N // tm,), in_specs=in_specs, out_specs=out_specs, out_shape=out_shape,
        compiler_params=_params("parallel"), name=name)(*args)


def _forget_key_terms(fz, b_f, B, S):
    ch = CUM_CHUNK
    place = np.zeros((3, LANES, LANES), np.float32)
    for h in range(FOX_HEADS):
        for j in range(3):
            place[j, h, 3 * h + j] = 1.0

    def kern(z_ref, b_ref, pl_ref, o_ref):
        r = lax.broadcasted_iota(jnp.int32, (ch, ch), 0)
        c = lax.broadcasted_iota(jnp.int32, (ch, ch), 1)
        tri = jnp.where(c <= r, 1.0, 0.0).astype(BF16)
        lane = lax.broadcasted_iota(jnp.int32, (1, LANES), 1)
        ones_lanes = jnp.where((lane >= FOX_ONES_LANE) & (lane < FOX_ONES_LANE + 3), 1.0, 0.0)

        def body(j, carries):
            out = []
            for b, carry in enumerate(carries):
                r0 = pl.multiple_of(b * S + j * ch, ch)
                z = z_ref[pl.ds(r0, ch), :] + b_ref[...]
                logf = -(jnp.maximum(-z, 0.0) + jnp.log1p(jnp.exp(-jnp.abs(z))))
                cs = carry
                for part in _split_bf16(logf, 3):
                    cs = cs + jnp.dot(tri, part, preferred_element_type=F32)
                terms = jnp.zeros((ch, LANES), F32)
                for jj, part in enumerate(_split_bf16(cs * (-LOG2E), 3)):
                    terms = terms + jnp.dot(part, pl_ref[jj], preferred_element_type=F32)
                o_ref[pl.ds(r0, ch), :] = (terms + ones_lanes).astype(o_ref.dtype)
                out.append(cs[ch - 1:ch, :])
            return tuple(out)

        lax.fori_loop(0, S // ch, body, tuple(jnp.zeros((1, LANES), F32) for _ in range(B)))

    return pl.pallas_call(
        kern, grid=(1,),
        in_specs=[pl.BlockSpec((B * S, LANES), lambda g: (0, 0), pipeline_mode=pl.Buffered(1)),
                  pl.BlockSpec((1, LANES), lambda g: (0, 0)),
                  pl.BlockSpec(place.shape, lambda g: (0, 0, 0))],
        out_specs=pl.BlockSpec((B * S, LANES), lambda g: (0, 0)),
        out_shape=jax.ShapeDtypeStruct((B * S, LANES), BF16),
        compiler_params=_params("arbitrary"), name="forget_terms")(fz, b_f, jnp.asarray(place, BF16))


class Head(NamedTuple):
    q_off: int
    q_half: Optional[str]
    k_off: int
    v0: int
    v1: int


def _flash(q, k, vt, *, B, S, P, q_w, k_w, v_rows, heads, kind, out_dtype, tq=None, tk=None, n_bufs=None,
           ext=None, sel=None, lam=None, subln=None, lam_init=None, name="flash"):
    TQ, TK = tq or ATT_TILE, tk or KEY_TILE
    assert S % TQ == 0 and TQ % TK == 0
    diag = TQ // TK
    nq = S // TQ
    nh = len(heads)
    rows = heads[0].v1 - heads[0].v0
    out_w = nh // 2 * rows if kind == "diff" else nh * rows
    width = TK
    n_bufs = n_bufs or ATT_BUFS
    if kind == "win":
        assert WIN == TK and TQ == TK and S >= 2 * TK and n_bufs == 2
    has_ext = kind in ("fox", "slc")

    def kern(*refs):
        it = iter(refs)
        q_ref, k_ref, v_ref = next(it), next(it), next(it)
        ext_ref = next(it) if has_ext else None
        qplace_ref = next(it) if kind == "fox" else None
        sel_ref = next(it) if kind == "slc" else None
        lam_ref, sub_ref = (next(it), next(it)) if kind == "diff" else (None, None)
        o_ref, m_ref, acc_ref, mx_ref = next(it), next(it), next(it), next(it)
        bufs = [next(it) for _ in range(n_bufs)]
        p_id = pl.program_id(1)
        i = pl.program_id(2)

        lane = lax.broadcasted_iota(jnp.int32, (TQ, LANES), 1)
        lo = lane < HALF_LANES
        q_all = q_ref[...]
        if kind == "slc":
            drop = ((sel_ref[...].astype(F32) - 1.0) * MASK_BIG).astype(BF16)
        if kind == "fox":
            q_rows = ext_ref[pl.ds(pl.multiple_of(i * TQ, TQ), TQ), :]
            q_terms = jnp.dot(q_rows, qplace_ref[...], preferred_element_type=F32)
        qs = []
        for j, h in enumerate(heads):
            qh = q_all[:, h.q_off:h.q_off + LANES]
            if h.q_half == "lo":
                qh = jnp.where(lo, qh, jnp.zeros_like(qh))
            elif h.q_half == "hi":
                qh = jnp.where(lo, jnp.zeros_like(qh), qh)
            if kind == "fox":
                first = 3 * (nh * p_id + j)
                pick = jnp.where((lane >= first) & (lane < first + 3), 1.0, 0.0)
                qh = jnp.concatenate([qh, (pick + q_terms[:, j * LANES:(j + 1) * LANES]).astype(BF16)], axis=1)
            if kind == "slc":
                qh = jnp.concatenate([qh, drop], axis=1)
            qs.append(qh)

        m_ref[...] = jnp.full(m_ref.shape, NEG, F32)
        acc_ref[...] = jnp.zeros(acc_ref.shape, F32)

        def block_kind(ahead, kb, qb):
            lo_rel = (kb - qb) * MASK_BLOCK - (MASK_BLOCK - 1)
            hi_rel = (kb - qb) * MASK_BLOCK + (MASK_BLOCK - 1)
            floor = ahead - WIN if kind == "win" else lo_rel - 1
            if lo_rel > ahead or hi_rel <= floor:
                return "dead"
            if hi_rel <= ahead and lo_rel > floor:
                return "full"
            return "part"

        def block_mask(ahead, kb, qb):
            rel = (lax.broadcasted_iota(jnp.int32, (MASK_BLOCK, MASK_BLOCK), 0)
                   - lax.broadcasted_iota(jnp.int32, (MASK_BLOCK, MASK_BLOCK), 1)
                   + (kb - qb) * MASK_BLOCK)
            msk = rel <= ahead
            if kind == "win":
                msk = msk & (rel > ahead - WIN)
            return msk

        kbs, qbs = range(width // MASK_BLOCK), range(TQ // MASK_BLOCK)
        blk = lambda b: slice(b * MASK_BLOCK, (b + 1) * MASK_BLOCK)

        def scores(k0, u, ahead=None):
            kx = k_ref[pl.ds(k0, width), :]
            ex = ext_ref[pl.ds(k0, width), :] if has_ext else None
            for hi, h in enumerate(heads):
                kk = kx[:, h.k_off:h.k_off + LANES]
                if ex is not None:
                    kk = jnp.concatenate([kk, ex], axis=1)
                if ahead is None:
                    st = lax.dot_general(kk, qs[hi], _NT, preferred_element_type=F32)
                    bufs[u][hi] = st
                    mx_ref[u * nh + hi] = jnp.max(st, axis=0, keepdims=True)
                    continue
                col_max = [None] * len(qbs)
                for kb in kbs:
                    live = [qb for qb in qbs if block_kind(ahead, kb, qb) != "dead"]
                    if not live:
                        continue
                    q_rows = slice(live[0] * MASK_BLOCK, (live[-1] + 1) * MASK_BLOCK)
                    st = lax.dot_general(kk[blk(kb)], qs[hi][q_rows], _NT, preferred_element_type=F32)
                    for qb in range(live[0], live[-1] + 1):
                        sb = st[:, blk(qb - live[0])]
                        if block_kind(ahead, kb, qb) == "part":
                            sb = jnp.where(block_mask(ahead, kb, qb), sb, NEG)
                        bufs[u][hi, blk(kb), blk(qb)] = sb
                        cm = jnp.max(sb, axis=0, keepdims=True)
                        col_max[qb] = cm if col_max[qb] is None else jnp.maximum(col_max[qb], cm)
                mx_ref[u * nh + hi] = jnp.concatenate(col_max, axis=1)

        def update(k0, u, ahead=None):
            for hi, h in enumerate(heads):
                m_prev = m_ref[hi]
                m_new = jnp.maximum(m_prev, mx_ref[u * nh + hi])
                alpha = jnp.exp2(m_prev - m_new)
                if ahead is None:
                    ones = jnp.ones((BF16_ROWS, width), BF16)
                    pt = jnp.exp2(bufs[u][hi] - m_new).astype(BF16)
                    lhs = jnp.concatenate([v_ref[h.v0:h.v1, pl.ds(k0, width)], ones], axis=0)
                    new = jnp.dot(lhs, pt, preferred_element_type=F32)
                else:
                    ones = jnp.ones((BF16_ROWS, MASK_BLOCK), BF16)
                    cols = []
                    for qb in qbs:
                        col = None
                        for kb in kbs:
                            if block_kind(ahead, kb, qb) == "dead":
                                continue
                            pt = jnp.exp2(bufs[u][hi, blk(kb), blk(qb)] - m_new[:, blk(qb)]).astype(BF16)
                            kstart = pl.multiple_of(k0 + kb * MASK_BLOCK, MASK_BLOCK)
                            lhs = jnp.concatenate(
                                [v_ref[h.v0:h.v1, pl.ds(kstart, MASK_BLOCK)], ones], axis=0)
                            part = jnp.dot(lhs, pt, preferred_element_type=F32)
                            col = part if col is None else col + part
                        cols.append(col)
                    new = jnp.concatenate(cols, axis=1)
                acc_ref[hi] = alpha * acc_ref[hi] + new
                m_ref[hi] = m_new

        tile = lambda t: pl.multiple_of(t * TK, TK)
        U = len(bufs)
        if kind == "win":
            @pl.when(i == 0)
            def _():
                scores(tile(0), 0, 0)
                update(tile(0), 0, 0)

            @pl.when(i > 0)
            def _():
                scores(tile(i - 1), 0, WIN)
                scores(tile(i), 1, 0)
                update(tile(i - 1), 0, WIN)
                update(tile(i), 1, 0)
        else:
            n_full = i * diag
            n_rounds = jnp.maximum(n_full - 1, 0) // U
            pl.when(n_full == 0)(lambda: scores(tile(0), 0, 0))
            pl.when(n_full > 0)(lambda: scores(tile(0), 0))

            def body(j, c):
                for u in range(U):
                    scores(tile(U * j + u + 1), (u + 1) % U)
                    update(tile(U * j + u), u)
                return c
            lax.fori_loop(0, n_rounds, body, 0)
            t0 = U * n_rounds

            def tail(left):
                ahead = lambda u: -(u - left) * TK if u >= left else None
                last = left + diag - 1
                for u in range(last):
                    scores(tile(t0 + u + 1), (u + 1) % U, ahead(u + 1))
                    update(tile(t0 + u), u % U, ahead(u))
                update(tile(t0 + last), last % U, ahead(last))

            for left in range(U + 1):
                pl.when(n_full - t0 == left)(lambda left=left: tail(left))

        outs = []
        for hi in range(nh):
            acc = acc_ref[hi]
            outs.append(acc[:rows] / acc[rows:rows + 1])
        if kind == "diff":
            lm = lam_ref[...]
            la = jnp.sum(lm[0:1] * lm[1:2], axis=1, keepdims=True)
            lb = jnp.sum(lm[2:3] * lm[3:4], axis=1, keepdims=True)
            lam_full = jnp.exp(la) - jnp.exp(lb) + lam_init
            for g in range(nh // 2):
                oa = jnp.transpose(outs[2 * g] - lam_full * outs[2 * g + 1])
                y = oa * lax.rsqrt(jnp.mean(oa * oa, axis=-1, keepdims=True) + EPS)
                y = (y * sub_ref[...]) * (1.0 - lam_init)
                o_ref[:, g * rows:(g + 1) * rows] = y.astype(o_ref.dtype)
        else:
            o_ref[...] = jnp.transpose(jnp.concatenate(outs, axis=0)).astype(o_ref.dtype)

    in_specs = [
        pl.BlockSpec((TQ, q_w), lambda b, p, i: (b * nq + i, p)),
        pl.BlockSpec((S, k_w), lambda b, p, i: (b, p)),
        pl.BlockSpec((v_rows, S), lambda b, p, i: (p, b)),
    ]
    args = [q, k, vt]
    if kind == "fox":
        qplace = np.zeros((P, LANES, nh * LANES), np.float32)
        for p in range(P):
            for j in range(nh):
                for t in range(3):
                    qplace[p, 3 * (nh * p + j) + t, j * LANES + FOX_ONES_LANE + t] = -1.0
        in_specs.append(pl.BlockSpec((S, LANES), lambda b, p, i: (b, 0)))
        in_specs.append(pl.BlockSpec((None, LANES, nh * LANES), lambda b, p, i: (p, 0, 0)))
        args += [ext, jnp.asarray(qplace, BF16)]
    if kind == "slc":
        in_specs.append(pl.BlockSpec((S, LANES), lambda b, p, i: (0, 0)))
        in_specs.append(pl.BlockSpec((TQ, LANES), lambda b, p, i: (b * nq + i, p)))
        args += [ext, sel]
    if kind == "diff":
        in_specs.append(pl.BlockSpec(lam.shape, lambda b, p, i: (0, 0)))
        in_specs.append(pl.BlockSpec((1, LANES), lambda b, p, i: (0, 0)))
        args += [lam, subln.reshape(1, LANES)]
    return pl.pallas_call(
        kern, grid=(B, P, nq), in_specs=in_specs,
        out_specs=pl.BlockSpec((TQ, out_w), lambda b, p, i: (b * nq + i, p)),
        out_shape=jax.ShapeDtypeStruct((B * S, P * out_w), out_dtype),
        scratch_shapes=[pltpu.VMEM((nh, 1, TQ), F32),
                        pltpu.VMEM((nh, rows + BF16_ROWS, TQ), F32),
                        pltpu.VMEM((n_bufs * nh, 1, TQ), F32)]
        + [pltpu.VMEM((nh, width, TQ), F32)] * n_bufs,
        compiler_params=_params("parallel", "parallel", "arbitrary"), name=name)(*args)


_DIFF_HEADS = tuple(Head(s * LANES, half, s * LANES, s * LANES, (s + 1) * LANES)
                    for s in range(2) for half in ("lo", "hi"))
_PAIR_HEADS = tuple(Head(s * LANES, half, s * LANES, (2 * s + j) * HALF_LANES, (2 * s + j + 1) * HALF_LANES)
                    for s in range(2) for j, half in enumerate(("lo", "hi")))
_QUAD_HEADS = tuple(Head(s * LANES, half, 0, 0, HALF_LANES) for s in range(2) for half in ("lo", "hi"))
_WIDE_HEADS = tuple(Head(j * LANES, None, j * LANES, j * HALF_LANES, (j + 1) * HALF_LANES) for j in range(4))


def _nsa_compress(xk, xv, pe, w1x, w2d):
    B, R, K = xk.shape
    G = NSA_GROUPS

    def kern(xk_ref, xv_ref, pe_ref, w1_ref, w2_ref, k_ref, vt_ref):
        for t, x_ref in enumerate((xk_ref, xv_ref)):
            x = x_ref[...]
            xa = (x + pe_ref[t, 0]).astype(BF16)
            xb = (x + pe_ref[t, 1]).astype(BF16)
            for g in range(G):
                a = jnp.dot(xa, w1_ref[t, 0, g], preferred_element_type=F32)
                b = jnp.dot(xb, w1_ref[t, 1, g], preferred_element_type=F32)
                h = a + pltpu.roll(b, R - 1, 0)
                hs = h * _sigmoid(h)
                o = jnp.dot(hs.astype(BF16), w2_ref[t], preferred_element_type=F32)
                if t == 0:
                    k_ref[g] = o
                else:
                    vt_ref[g] = jnp.transpose(o)

    return pl.pallas_call(
        kern, grid=(B,),
        in_specs=[pl.BlockSpec((None, R, K), lambda b: (b, 0, 0)),
                  pl.BlockSpec((None, R, K), lambda b: (b, 0, 0)),
                  pl.BlockSpec(pe.shape, lambda b: (0, 0, 0, 0)),
                  pl.BlockSpec(w1x.shape, lambda b: (0, 0, 0, 0, 0)),
                  pl.BlockSpec(w2d.shape, lambda b: (0, 0, 0))],
        out_specs=[pl.BlockSpec((None, G, R, LANES), lambda b: (b, 0, 0, 0)),
                   pl.BlockSpec((None, G, LANES, R), lambda b: (b, 0, 0, 0))],
        out_shape=[jax.ShapeDtypeStruct((B, G, R, LANES), F32),
                   jax.ShapeDtypeStruct((B, G, LANES, R), F32)],
        compiler_params=_params("parallel"), name="nsa_compress")(xk, xv, pe, w1x, w2d)


def _nsa_cmp_select(q, cmp_k, cmp_vt, ov_t, *, B, S):
    T = SEL_TILE
    nq = S // T
    R = cmp_k.shape[2]
    G = NSA_GROUPS
    d = NSA_DIM
    n_sel = min(SLC_TOPK, S // SLC_LEN)
    assert n_sel >= 3 and FORCE_SCORE > NSA_HEADS // G

    bucket = min(LANES, R)
    assert R % bucket == 0 and bucket % (SLC_LEN // CMP_STRIDE) == 0

    def kern(q_ref, kc_ref, vt_ref, ov_ref, o_ref, sel_ref):
        i = pl.program_id(2)
        lane = lax.broadcasted_iota(jnp.int32, (T, LANES), 1)
        lo = lane < HALF_LANES
        q_all = q_ref[...]
        qs = []
        for hh in range(4):
            qh = q_all[:, (hh // 2) * LANES:(hh // 2 + 1) * LANES]
            qs.append(jnp.where(lo, qh, jnp.zeros_like(qh)) if hh % 2 == 0
                      else jnp.where(lo, jnp.zeros_like(qh), qh))

        def variant(rv):
            nb = rv * CMP_STRIDE // SLC_LEN
            tq = lax.broadcasted_iota(jnp.int32, (rv, T), 1) + i * T
            cend = lax.broadcasted_iota(jnp.int32, (rv, T), 0) * CMP_STRIDE + (CMP_LEN - 1)
            vis = cend <= tq
            kc = kc_ref[:rv, :].astype(BF16)
            vt = vt_ref[:d, :rv].astype(BF16)
            psum = jnp.zeros((rv, T), F32)
            o_heads = []
            for hh in range(4):
                st = lax.dot_general(kc, qs[hh], _NT, preferred_element_type=F32)
                st = jnp.where(vis, st, -jnp.inf)
                m = jnp.max(st, axis=0, keepdims=True)
                e = jnp.exp2(st - jnp.where(m > -jnp.inf, m, 0.0))
                p = e / jnp.maximum(jnp.sum(e, axis=0, keepdims=True), 1e-30)
                o_heads.append(jnp.dot(vt, p.astype(BF16), preferred_element_type=F32))
                psum = psum + p
            o_ref[...] = jnp.transpose(jnp.concatenate(o_heads, axis=0)).astype(o_ref.dtype)
            imp = jnp.zeros((nb, T), F32)
            for part in _split_bf16(psum, 2):
                imp = imp + jnp.dot(ov_ref[:nb, :rv], part, preferred_element_type=F32)
            blk = lax.broadcasted_iota(jnp.int32, (nb, T), 0)
            cur = (lax.broadcasted_iota(jnp.int32, (nb, T), 1) + i * T) // SLC_LEN
            forced = (blk == 0) | (blk == cur) | (blk == cur - 1)
            work = jnp.where(forced, -jnp.inf, imp)
            work = jnp.where(blk > cur, NEG, work)
            blk_f = blk.astype(F32)
            for _ in range(n_sel - 3):
                mx = jnp.max(work, axis=0, keepdims=True)
                first = jnp.min(jnp.where(work == mx, blk_f, float(LANES)), axis=0, keepdims=True)
                work = jnp.where(blk_f == first, -jnp.inf, work)
            sel = jnp.where(work == -jnp.inf, 1.0, 0.0)
            if nb < LANES:
                sel = jnp.concatenate([sel, jnp.zeros((LANES - nb, T), F32)], axis=0)
            sel_ref[...] = jnp.transpose(sel).astype(sel_ref.dtype)

        need = (i + 1) * (T // CMP_STRIDE)
        which = (need - 1) // bucket
        for bk in range(R // bucket):
            pl.when(which == bk)(lambda bk=bk: variant((bk + 1) * bucket))

    return pl.pallas_call(
        kern, grid=(B, G, nq),
        in_specs=[pl.BlockSpec((T, 2 * LANES), lambda b, g, i: (b * nq + i, g)),
                  pl.BlockSpec((None, None, R, LANES), lambda b, g, i: (b, g, 0, 0)),
                  pl.BlockSpec((None, None, LANES, R), lambda b, g, i: (b, g, 0, 0)),
                  pl.BlockSpec((LANES, R), lambda b, g, i: (0, 0))],
        out_specs=[pl.BlockSpec((T, 2 * LANES), lambda b, g, i: (b * nq + i, g)),
                   pl.BlockSpec((T, LANES), lambda b, g, i: (b * nq + i, g))],
        out_shape=[jax.ShapeDtypeStruct((B * S, G * 2 * LANES), BF16),
                   jax.ShapeDtypeStruct((B * S, G * LANES), BF16)],
        compiler_params=_params("parallel", "parallel", "parallel"), name="nsa_cmp_select")(
            q, cmp_k, cmp_vt, ov_t)


def _odd_out(o_cmp, o_slc, o_win, gz, expand, o_mla, w_nsa, w_mla, x):
    N, C = o_cmp.shape
    D = x.shape[1]
    tm = min(ROW_TILE, N)

    def kern(c_ref, s_ref, w_ref, g_ref, e_ref, m_ref, wn_ref, wm_ref, x_ref, o_ref):
        gate = _sigmoid(g_ref[...])
        parts = _split_bf16(gate, 2)
        acc = jnp.zeros((tm, C), F32)
        for j, br in enumerate((c_ref, s_ref, w_ref)):
            gj = sum(jnp.dot(part, e_ref[j], preferred_element_type=F32) for part in parts)
            acc = acc + gj * br[...].astype(F32)
        o_nsa = acc.astype(BF16)
        o_mla_ = m_ref[...]
        for c0 in range(0, D, COL_CHUNK):
            cols = slice(c0, c0 + COL_CHUNK)
            y = (jnp.dot(o_nsa, wn_ref[:, cols], preferred_element_type=F32)
                 + jnp.dot(o_mla_, wm_ref[:, cols], preferred_element_type=F32))
            o_ref[:, cols] = x_ref[:, cols] + y

    row = pl.BlockSpec((tm, C), lambda i: (i, 0))
    wide = pl.BlockSpec((tm, D), lambda i: (i, 0))
    return pl.pallas_call(
        kern, grid=(N // tm,),
        in_specs=[row, row, row, pl.BlockSpec((tm, LANES), lambda i: (i, 0)),
                  pl.BlockSpec(expand.shape, lambda i: (0, 0, 0)),
                  pl.BlockSpec((tm, o_mla.shape[1]), lambda i: (i, 0)),
                  pl.BlockSpec(w_nsa.shape, lambda i: (0, 0)),
                  pl.BlockSpec(w_mla.shape, lambda i: (0, 0)), wide],
        out_specs=wide, out_shape=jax.ShapeDtypeStruct((N, D), F32),
        compiler_params=_params("parallel"), name="odd_out")(
            o_cmp, o_slc, o_win, gz, expand, o_mla, w_nsa, w_mla, x)


def _mla_kv(ckv, gain, wk, wv, kr):
    N, K = ckv.shape
    tm = min(ROW_TILE, N)
    H = MLA_HEADS

    def kern(c_ref, g_ref, wk_ref, wv_ref, kr_ref, k_ref, v_ref):
        xf = c_ref[...]
        y = xf * lax.rsqrt(jnp.mean(xf * xf, axis=-1, keepdims=True) + EPS)
        a = (y * g_ref[...]).astype(BF16)
        kr2 = jnp.concatenate([kr_ref[...]] * 2, axis=1)
        for h in range(0, H, 2):
            cols = slice(h * LANES, (h + 2) * LANES)
            kh = jnp.dot(a, wk_ref[:, cols], preferred_element_type=F32)
            k_ref[:, cols] = (kh + kr2).astype(k_ref.dtype)
        v_ref[...] = jnp.dot(a, wv_ref[...], preferred_element_type=F32).astype(v_ref.dtype)

    return pl.pallas_call(
        kern, grid=(N // tm,),
        in_specs=[pl.BlockSpec((tm, K), lambda i: (i, 0)),
                  pl.BlockSpec((1, K), lambda i: (0, 0)),
                  pl.BlockSpec(wk.shape, lambda i: (0, 0)),
                  pl.BlockSpec(wv.shape, lambda i: (0, 0)),
                  pl.BlockSpec((tm, LANES), lambda i: (i, 0))],
        out_specs=[pl.BlockSpec((tm, H * LANES), lambda i: (i, 0)),
                   pl.BlockSpec((tm, H * MLA_V), lambda i: (i, 0))],
        out_shape=[jax.ShapeDtypeStruct((N, H * LANES), BF16),
                   jax.ShapeDtypeStruct((N, H * MLA_V), BF16)],
        compiler_params=_params("parallel"), name="mla_kv")(
            ckv, gain.reshape(1, K).astype(F32), wk, wv, kr)


def _mem_attn(x, gain, wq, wo, mem_k, mem_v, *, S):
    N, D = x.shape
    tm = min(ROW_TILE, S)
    M = mem_k.shape[1]
    per_b = S // tm
    scale = XA_DIM ** -0.5

    def kern(x_ref, g_ref, wq_ref, wo_ref, k_ref, v_ref, o_ref):
        xf = x_ref[...]
        y = xf * lax.rsqrt(jnp.mean(xf * xf, axis=-1, keepdims=True) + EPS)
        a = (y * g_ref[...]).astype(BF16)
        q = (jnp.dot(a, wq_ref[...], preferred_element_type=F32) * scale).astype(BF16)
        heads = []
        for h in range(XA_HEADS):
            sl = slice(h * XA_DIM, (h + 1) * XA_DIM)
            s = lax.dot_general(q[:, sl], k_ref[:, sl], _NT, preferred_element_type=F32)
            e = jnp.exp(s - jnp.max(s, axis=1, keepdims=True))
            p = e / jnp.sum(e, axis=1, keepdims=True)
            heads.append(jnp.dot(p.astype(BF16), v_ref[:, sl], preferred_element_type=F32).astype(BF16))
        o = jnp.concatenate(heads, axis=1)
        o_ref[...] = xf + jnp.dot(o, wo_ref[...], preferred_element_type=F32)

    return pl.pallas_call(
        kern, grid=(N // tm,),
        in_specs=[pl.BlockSpec((tm, D), lambda i: (i, 0)),
                  pl.BlockSpec((1, D), lambda i: (0, 0)),
                  pl.BlockSpec(wq.shape, lambda i: (0, 0)),
                  pl.BlockSpec(wo.shape, lambda i: (0, 0)),
                  pl.BlockSpec((None, M, XA_HEADS * XA_DIM), lambda i: (i // per_b, 0, 0)),
                  pl.BlockSpec((None, M, XA_HEADS * XA_DIM), lambda i: (i // per_b, 0, 0))],
        out_specs=pl.BlockSpec((tm, D), lambda i: (i, 0)),
        out_shape=jax.ShapeDtypeStruct((N, D), F32),
        compiler_params=_params("parallel"), name="mem_attn")(
            x, gain.reshape(1, D).astype(F32), wq, wo, mem_k, mem_v)


def _ffn(x, gain, w13, w2, out_gain=None):
    N, D = x.shape
    FF = w2.shape[0]
    tm = min(ROW_TILE, N)
    chunk = 2 * LANES
    assert FF % chunk == 0

    final = out_gain is not None

    def kern(x_ref, g_ref, w13_ref, w2_ref, *rest):
        og_ref, o_ref = rest if final else (None, rest[0])
        xf = x_ref[...]
        y = xf * lax.rsqrt(jnp.mean(xf * xf, axis=-1, keepdims=True) + EPS)
        a = (y * g_ref[...]).astype(BF16)
        acc = xf
        for c in range(0, FF, chunk):
            g = jnp.dot(a, w13_ref[:, c:c + chunk], preferred_element_type=F32)
            u = jnp.dot(a, w13_ref[:, FF + c:FF + c + chunk], preferred_element_type=F32)
            hdn = (g * _sigmoid(g) * u).astype(BF16)
            acc = acc + jnp.dot(hdn, w2_ref[c:c + chunk, :], preferred_element_type=F32)
        if final:
            acc = acc * lax.rsqrt(jnp.mean(acc * acc, axis=-1, keepdims=True) + EPS) * og_ref[...]
        o_ref[...] = acc

    vec = pl.BlockSpec((1, D), lambda i: (0, 0))
    in_specs = [pl.BlockSpec((tm, D), lambda i: (i, 0)), vec,
                pl.BlockSpec(w13.shape, lambda i: (0, 0), pipeline_mode=pl.Buffered(1)),
                pl.BlockSpec(w2.shape, lambda i: (0, 0), pipeline_mode=pl.Buffered(1))]
    args = [x, gain.reshape(1, D).astype(F32), w13, w2]
    if final:
        in_specs.append(vec)
        args.append(out_gain.reshape(1, D).astype(F32))
    return pl.pallas_call(
        kern, grid=(N // tm,), in_specs=in_specs,
        out_specs=pl.BlockSpec((tm, D), lambda i: (i, 0)),
        out_shape=jax.ShapeDtypeStruct((N, D), F32),
        compiler_params=_params("parallel"), name="ffn")(*args)


def _pad_cols(w, width):
    return jnp.pad(w, ((0, 0), (0, width - w.shape[1])))


def _even_mixer(x, gain, w_in, b_f, lam, subln, w_out, layer_idx, ropes, B, S):
    D = x.shape[1]
    blk = DIFF_HEADS * 2 * DIFF_QK
    w = _pad_cols(w_in, 6 * blk + LANES).astype(BF16)
    sc = DIFF_QK ** -0.5 * LOG2E
    segs = [Seg(0, blk, BF16, "r64", sc), Seg(blk, blk, BF16, "r64"), Seg(2 * blk, blk, BF16),
            Seg(3 * blk, blk, BF16, None, sc), Seg(4 * blk, blk, BF16), Seg(5 * blk, blk, BF16),
            Seg(6 * blk, LANES, F32)]
    aq, ak, av, fq, fk, fv, fz = _linear([x], [w], segs, gain=gain, ropes=ropes, seq=S, name="even_in")
    lam_init = 0.8 - 0.6 * math.exp(-0.3 * layer_idx)
    oa = _flash(aq, ak, av.T, B=B, S=S, P=DIFF_HEADS // 2, q_w=2 * LANES, k_w=2 * LANES, v_rows=2 * LANES,
                heads=_DIFF_HEADS, kind="diff", out_dtype=BF16, lam=lam.astype(F32),
                subln=subln.astype(F32), lam_init=lam_init, name="diff_attn")
    terms = _forget_key_terms(fz, _pad_cols(b_f.reshape(1, -1), LANES).astype(F32), B, S)
    of = _flash(fq, fk, fv.T, B=B, S=S, P=FOX_HEADS // 4, q_w=2 * LANES, k_w=2 * LANES, v_rows=2 * LANES,
                heads=_PAIR_HEADS, kind="fox", out_dtype=BF16, ext=terms, name="fox_attn")
    wo = w_out.astype(BF16)
    (y,) = _linear([oa, of], [wo[:blk], wo[blk:]], [Seg(0, D, F32)], residual=x, name="even_out")
    return y


def _odd_in_weight(w_in):
    d = NSA_DIM
    o = np.cumsum((0, NSA_HEADS * d) + (NSA_GROUPS * d,) * 6 + (NSA_HEADS * 3, MLA_Q_RANK, MLA_KV_RANK, MLA_ROPE))
    nq, kc, vc, ks, vs, kw, vw, gz, cq, ckv, kr = [w_in[:, o[j]:o[j + 1]] for j in range(11)]

    def dup(wg):
        return jnp.concatenate([wg[:, :d], wg[:, :d], wg[:, d:], wg[:, d:]], axis=1)

    zeros = lambda n: jnp.zeros((w_in.shape[0], n), w_in.dtype)
    kr_slab = jnp.concatenate([zeros(MLA_NOPE), kr, zeros(LANES - MLA_NOPE - MLA_ROPE)], axis=1)
    cols = [nq, kc, vc, dup(ks), vs, dup(kw), vw, _pad_cols(gz, LANES), cq, ckv, kr_slab]
    return jnp.concatenate(cols, axis=1).astype(BF16)


def _odd_mixer(x, gain, w_in, cmp_pos, cmp_w1, cmp_w2, q_norm, kv_norm, w_uq, w_ukv, w_out, ropes, B, S):
    D = x.shape[1]
    G, d = NSA_GROUPS, NSA_DIM
    w = _odd_in_weight(w_in)
    sc = d ** -0.5 * LOG2E
    widths = [(NSA_HEADS * d, BF16, "r64", sc), (LANES, F32, "r64", 1.0), (LANES, F32, None, 1.0),
              (2 * LANES, BF16, "r64", 1.0), (LANES, BF16, None, 1.0),
              (2 * LANES, BF16, "r64", 1.0), (LANES, BF16, None, 1.0),
              (LANES, F32, None, 1.0), (MLA_Q_RANK, F32, None, 1.0), (MLA_KV_RANK, F32, None, 1.0),
              (LANES, F32, "mla", 1.0)]
    segs, start = [], 0
    for wd, dt, rp, s_ in widths:
        segs.append(Seg(start, wd, dt, rp, s_))
        start += wd
    q, kc, vc, ks, vs, kw, vw, gz, cq, ckv, kr = _linear(
        [x], [w], segs, gain=gain, ropes=ropes, seq=S, name="odd_in")

    R = S // CMP_STRIDE

    K = CMP_STRIDE * G * d
    half = CMP_LEN // 2
    pe = jnp.broadcast_to(cmp_pos.reshape(2, 2, half, 1, d), (2, 2, half, G, d)).reshape(2, 2, 1, K).astype(F32)
    eye = jnp.eye(G, dtype=cmp_w1.dtype)
    w1x = (cmp_w1.reshape(2, 2, 1, half, 1, d, CMP_HIDDEN) * eye[None, None, :, None, :, None, None])
    w1x = w1x.reshape(2, 2, G, K, CMP_HIDDEN).astype(BF16)
    w2d = jnp.concatenate([cmp_w2, cmp_w2], axis=-1).astype(BF16)
    cmp_k, cmp_vt = _nsa_compress(kc.reshape(B, R, K), vc.reshape(B, R, K), pe, w1x, w2d)
    cs = np.arange(R) * CMP_STRIDE
    bs = np.arange(LANES) * SLC_LEN
    ov = ((cs[:, None] < bs[None, :] + SLC_LEN) & (cs[:, None] + CMP_LEN > bs[None, :])
          & (np.arange(R)[:, None] < R - 1) & (bs[None, :] < S))
    o_cmp, sel = _nsa_cmp_select(q, cmp_k, cmp_vt, jnp.asarray(ov.T.astype(np.float32), BF16), B=B, S=S)
    onehot = (np.arange(S)[:, None] // SLC_LEN == np.arange(LANES)[None, :]).astype(np.float32)
    o_slc = _flash(q, ks, vs.T, B=B, S=S, P=G, q_w=2 * LANES, k_w=LANES, v_rows=HALF_LANES,
                   heads=_QUAD_HEADS, kind="slc", out_dtype=BF16, ext=jnp.asarray(onehot, BF16),
                   sel=sel, name="nsa_slc")
    o_win = _flash(q, kw, vw.T, B=B, S=S, P=G, q_w=2 * LANES, k_w=LANES, v_rows=HALF_LANES,
                   heads=_QUAD_HEADS, kind="win", out_dtype=BF16, tq=WIN, tk=WIN, n_bufs=2, name="nsa_win")
    ex = np.zeros((3, LANES, NSA_HEADS * d), np.float32)
    for hh in range(NSA_HEADS):
        for j in range(3):
            ex[j, hh * 3 + j, hh * d:(hh + 1) * d] = 1.0

    H = MLA_HEADS
    qk = MLA_NOPE + MLA_ROPE
    wq_slab = jnp.pad(w_uq.reshape(MLA_Q_RANK, H, qk), ((0, 0), (0, 0), (0, LANES - qk)))
    wq_slab = wq_slab.reshape(MLA_Q_RANK, H * LANES).astype(BF16)
    (qm,) = _linear([cq], [wq_slab], [Seg(0, H * LANES, BF16, "mla", qk ** -0.5 * LOG2E)],
                    gain=q_norm, ropes=ropes, seq=S, name="mla_q")
    wkv = w_ukv.reshape(MLA_KV_RANK, H, MLA_NOPE + MLA_V)
    wk = jnp.pad(wkv[:, :, :MLA_NOPE], ((0, 0), (0, 0), (0, LANES - MLA_NOPE)))
    wk = wk.reshape(MLA_KV_RANK, H * LANES).astype(BF16)
    wv = wkv[:, :, MLA_NOPE:].reshape(MLA_KV_RANK, H * MLA_V).astype(BF16)
    km, vm = _mla_kv(ckv, kv_norm, wk, wv, kr)
    o_mla = _flash(qm, km, vm.T, B=B, S=S, P=H // 4, q_w=4 * LANES, k_w=4 * LANES, v_rows=2 * LANES,
                   heads=_WIDE_HEADS, kind="mla", out_dtype=BF16, name="mla_attn")
    wo = w_out.astype(BF16)
    half = NSA_HEADS * d
    return _odd_out(o_cmp, o_slc, o_win, gz, jnp.asarray(ex, BF16), o_mla, wo[:half], wo[half:], x)


def kernel(x, mem, mem_norm, norm_mix, norm_mem, norm_ffn, ev_w_in, ev_b_f, ev_lam, ev_subln, ev_w_out, od_w_in, nsa_cmp_pos, nsa_cmp_w1, nsa_cmp_w2, mla_q_norm, mla_kv_norm, mla_w_uq, mla_w_ukv, od_w_out, xa_wq, xa_wkv, xa_wo, ffn_w13, ffn_w2, final_norm):
    B, S, D = x.shape
    M = mem.shape[1]
    depth = norm_mix.shape[0]
    ropes = {
        "r64": (_rope_tables(S, NSA_DIM // 2, 0, LANES), NSA_DIM // 2),
        "mla": (_rope_tables(S, MLA_ROPE // 2, MLA_NOPE, MLA_NOPE + MLA_ROPE), MLA_ROPE // 2),
    }
    xa_w = XA_HEADS * XA_DIM
    h = x.reshape(B * S, D)
    mem2 = mem.reshape(B * M, D)
    for li in range(depth):
        j = li // 2
        if li % 2 == 0:
            h = _even_mixer(h, norm_mix[li], ev_w_in[j], ev_b_f[j], ev_lam[j], ev_subln[j], ev_w_out[j],
                            li, ropes, B, S)
        else:
            h = _odd_mixer(h, norm_mix[li], od_w_in[j], nsa_cmp_pos[j], nsa_cmp_w1[j], nsa_cmp_w2[j],
                           mla_q_norm[j], mla_kv_norm[j], mla_w_uq[j], mla_w_ukv[j], od_w_out[j],
                           ropes, B, S)
        mk, mv = _linear([mem2], [xa_wkv[li].astype(BF16)],
                         [Seg(0, xa_w, BF16), Seg(xa_w, xa_w, BF16)], gain=mem_norm, name="mem_kv")
        h = _mem_attn(h, norm_mem[li], xa_wq[li].astype(BF16), xa_wo[li].astype(BF16),
                      mk.reshape(B, M, xa_w), mv.reshape(B, M, xa_w), S=S)
        h = _ffn(h, norm_ffn[li], ffn_w13[li].astype(BF16), ffn_w2[li].astype(BF16),
                 out_gain=final_norm if li == depth - 1 else None)
    return h.reshape(B, S, D)
```

```python
import math
from typing import NamedTuple, Optional

import numpy as np
import jax
import jax.numpy as jnp
from jax import lax
from jax.experimental import pallas as pl
from jax.experimental.pallas import tpu as pltpu

F32 = jnp.float32
BF16 = jnp.bfloat16

LANES = 128
HALF_LANES = LANES // 2
BF16_ROWS = 16
ROPE_THETA = 10000.0
EPS = 1e-6
NEG = -1e30
LOG2E = math.log2(math.e)
MASK_BIG = 2.0 ** 100

DIFF_HEADS = 4
DIFF_QK = 64
FOX_HEADS = 8
FOX_ONES_LANE = 3 * FOX_HEADS
NSA_HEADS = 8
NSA_GROUPS = 2
NSA_DIM = 64
CMP_LEN = 32
CMP_STRIDE = 16
CMP_HIDDEN = 128
SLC_LEN = 64
SLC_TOPK = 16
WIN = 512
FORCE_SCORE = 1e4
MLA_HEADS = 8
MLA_NOPE = 64
MLA_ROPE = 32
MLA_V = 64
MLA_Q_RANK = 384
MLA_KV_RANK = 256
XA_HEADS = 4
XA_DIM = 128

VMEM_LIMIT = 56 * 1024 * 1024
ROW_TILE = 512
COL_CHUNK = 512
ATT_TILE = 512
KEY_TILE = 512
ATT_BUFS = 3
MASK_BLOCK = 256
SEL_TILE = 1024
CUM_CHUNK = 256

_NT = (((1,), (1,)), ((), ()))


def _params(*sem):
    return pltpu.CompilerParams(dimension_semantics=sem, vmem_limit_bytes=VMEM_LIMIT)


def _sigmoid(x):
    return 1.0 / (1.0 + jnp.exp(-x))


def _split_bf16(x, terms):
    out = []
    r = x
    for _ in range(terms):
        h = r.astype(BF16)
        out.append(h)
        r = r - h.astype(F32)
    return out


def _rope_tables(S, half, lane_lo, lane_hi):
    pos = jnp.arange(S, dtype=F32)
    inv = 1.0 / (ROPE_THETA ** (jnp.arange(half, dtype=F32) / half))
    ang = pos[:, None] * inv[None, :]
    cos, sin = jnp.cos(ang), jnp.sin(ang)
    lane = np.arange(LANES)
    active = (lane >= lane_lo) & (lane < lane_hi)
    j = (lane - lane_lo) % (2 * half)
    lower = active & (j < half)
    upper = active & (j >= half)
    idx = j % half
    cos_t = jnp.where(active[None, :], cos[:, idx], 1.0)
    sin_a = jnp.where(lower[None, :], -sin[:, idx], 0.0)
    sin_b = jnp.where(upper[None, :], sin[:, idx], 0.0)
    return cos_t, sin_a, sin_b


class Seg(NamedTuple):
    start: int
    width: int
    dtype: object
    rope: Optional[str] = None
    scale: float = 1.0


def _linear(xs, ws, segs, *, gain=None, residual=None, ropes=None, seq=None, name="linear"):
    N = xs[0].shape[0]
    tm = min(ROW_TILE, N)
    assert N % tm == 0
    n_in = len(xs)
    has_gain = gain is not None
    has_res = residual is not None
    rope_keys = sorted({s.rope for s in segs if s.rope})
    halves = {k: ropes[k][1] for k in rope_keys}

    def kern(*refs):
        it = iter(refs)
        x_refs = [next(it) for _ in range(n_in)]
        w_refs = [next(it) for _ in range(n_in)]
        g_ref = next(it) if has_gain else None
        r_ref = next(it) if has_res else None
        tabs = {k: (next(it), next(it), next(it)) for k in rope_keys}
        o_refs = [next(it) for _ in segs]
        acts = []
        for j, xr in enumerate(x_refs):
            x = xr[...]
            if j == 0 and has_gain:
                xf = x.astype(F32)
                y = xf * lax.rsqrt(jnp.mean(xf * xf, axis=-1, keepdims=True) + EPS)
                acts.append((y * g_ref[...]).astype(BF16))
            else:
                acts.append(x.astype(BF16))
        for seg, o_ref in zip(segs, o_refs):
            for c0 in range(0, seg.width, COL_CHUNK):
                cw = min(COL_CHUNK, seg.width - c0)
                col = seg.start + c0
                acc = None
                for a, wr in zip(acts, w_refs):
                    d = jnp.dot(a, wr[:, col:col + cw], preferred_element_type=F32)
                    acc = d if acc is None else acc + d
                if has_res:
                    acc = acc + r_ref[:, col:col + cw]
                if seg.scale != 1.0:
                    acc = acc * seg.scale
                if seg.rope is None:
                    o_ref[:, c0:c0 + cw] = acc.astype(o_ref.dtype)
                else:
                    cos_r, sa_r, sb_r = tabs[seg.rope]
                    half = halves[seg.rope]
                    cos, sa, sb = cos_r[...], sa_r[...], sb_r[...]
                    for s0 in range(0, cw, LANES):
                        xs_ = acc[:, s0:s0 + LANES]
                        y = (xs_ * cos + pltpu.roll(xs_, LANES - half, 1) * sa
                             + pltpu.roll(xs_, half, 1) * sb)
                        o_ref[:, c0 + s0:c0 + s0 + LANES] = y.astype(o_ref.dtype)

    in_specs, args = [], []
    for x in xs:
        in_specs.append(pl.BlockSpec((tm, x.shape[1]), lambda i: (i, 0)))
        args.append(x)
    for w in ws:
        in_specs.append(pl.BlockSpec(w.shape, lambda i: (0, 0)))
        args.append(w)
    if has_gain:
        in_specs.append(pl.BlockSpec((1, gain.shape[-1]), lambda i: (0, 0)))
        args.append(gain.reshape(1, -1).astype(F32))
    if has_res:
        in_specs.append(pl.BlockSpec((tm, residual.shape[1]), lambda i: (i, 0)))
        args.append(residual)
    for k in rope_keys:
        assert seq % tm == 0
        nt = seq // tm
        for t in ropes[k][0]:
            in_specs.append(pl.BlockSpec((tm, LANES), lambda i, nt=nt: (i % nt, 0)))
            args.append(t)
    out_shape = [jax.ShapeDtypeStruct((N, s.width), s.dtype) for s in segs]
    out_specs = [pl.BlockSpec((tm, s.width), lambda i: (i, 0)) for s in segs]
    return pl.pallas_call(
        kern, grid=(N // tm,), in_specs=in_specs, out_specs=out_specs, out_shape=out_shape,
        compiler_params=_params("parallel"), name=name)(*args)


def _forget_key_terms(fz, b_f, B, S):
    ch = CUM_CHUNK
    place = np.zeros((3, LANES, LANES), np.float32)
    for h in range(FOX_HEADS):
        for j in range(3):
            place[j, h, 3 * h + j] = 1.0

    def kern(z_ref, b_ref, pl_ref, o_ref):
        r = lax.broadcasted_iota(jnp.int32, (ch, ch), 0)
        c = lax.broadcasted_iota(jnp.int32, (ch, ch), 1)
        tri = jnp.where(c <= r, 1.0, 0.0).astype(BF16)
        lane = lax.broadcasted_iota(jnp.int32, (1, LANES), 1)
        ones_lanes = jnp.where((lane >= FOX_ONES_LANE) & (lane < FOX_ONES_LANE + 3), 1.0, 0.0)

        def body(j, carries):
            out = []
            for b, carry in enumerate(carries):
                r0 = pl.multiple_of(b * S + j * ch, ch)
                z = z_ref[pl.ds(r0, ch), :] + b_ref[...]
                logf = -(jnp.maximum(-z, 0.0) + jnp.log1p(jnp.exp(-jnp.abs(z))))
                cs = carry
                for part in _split_bf16(logf, 3):
                    cs = cs + jnp.dot(tri, part, preferred_element_type=F32)
                terms = jnp.zeros((ch, LANES), F32)
                for jj, part in enumerate(_split_bf16(cs * (-LOG2E), 3)):
                    terms = terms + jnp.dot(part, pl_ref[jj], preferred_element_type=F32)
                o_ref[pl.ds(r0, ch), :] = (terms + ones_lanes).astype(o_ref.dtype)
                out.append(cs[ch - 1:ch, :])
            return tuple(out)

        lax.fori_loop(0, S // ch, body, tuple(jnp.zeros((1, LANES), F32) for _ in range(B)))

    return pl.pallas_call(
        kern, grid=(1,),
        in_specs=[pl.BlockSpec((B * S, LANES), lambda g: (0, 0), pipeline_mode=pl.Buffered(1)),
                  pl.BlockSpec((1, LANES), lambda g: (0, 0)),
                  pl.BlockSpec(place.shape, lambda g: (0, 0, 0))],
        out_specs=pl.BlockSpec((B * S, LANES), lambda g: (0, 0)),
        out_shape=jax.ShapeDtypeStruct((B * S, LANES), BF16),
        compiler_params=_params("arbitrary"), name="forget_terms")(fz, b_f, jnp.asarray(place, BF16))


class Head(NamedTuple):
    q_off: int
    q_half: Optional[str]
    k_off: int
    v0: int
    v1: int


def _flash(q, k, vt, *, B, S, P, q_w, k_w, v_rows, heads, kind, out_dtype, tq=None, tk=None, n_bufs=None,
           ext=None, sel=None, lam=None, subln=None, lam_init=None, name="flash"):
    TQ, TK = tq or ATT_TILE, tk or KEY_TILE
    assert S % TQ == 0 and TQ % TK == 0
    diag = TQ // TK
    nq = S // TQ
    nh = len(heads)
    rows = heads[0].v1 - heads[0].v0
    out_w = nh // 2 * rows if kind == "diff" else nh * rows
    width = TK
    n_bufs = n_bufs or ATT_BUFS
    if kind == "win":
        assert WIN == TK and TQ == TK and S >= 2 * TK and n_bufs == 2
    has_ext = kind in ("fox", "slc")

    def kern(*refs):
        it = iter(refs)
        q_ref, k_ref, v_ref = next(it), next(it), next(it)
        ext_ref = next(it) if has_ext else None
        qplace_ref = next(it) if kind == "fox" else None
        sel_ref = next(it) if kind == "slc" else None
        lam_ref, sub_ref = (next(it), next(it)) if kind == "diff" else (None, None)
        o_ref, m_ref, acc_ref, mx_ref = next(it), next(it), next(it), next(it)
        bufs = [next(it) for _ in range(n_bufs)]
        p_id = pl.program_id(1)
        i = pl.program_id(2)

        lane = lax.broadcasted_iota(jnp.int32, (TQ, LANES), 1)
        lo = lane < HALF_LANES
        q_all = q_ref[...]
        if kind == "slc":
            drop = ((sel_ref[...].astype(F32) - 1.0) * MASK_BIG).astype(BF16)
        if kind == "fox":
            q_rows = ext_ref[pl.ds(pl.multiple_of(i * TQ, TQ), TQ), :]
            q_terms = jnp.dot(q_rows, qplace_ref[...], preferred_element_type=F32)
        qs = []
        for j, h in enumerate(heads):
            qh = q_all[:, h.q_off:h.q_off + LANES]
            if h.q_half == "lo":
                qh = jnp.where(lo, qh, jnp.zeros_like(qh))
            elif h.q_half == "hi":
                qh = jnp.where(lo, jnp.zeros_like(qh), qh)
            if kind == "fox":
                first = 3 * (nh * p_id + j)
                pick = jnp.where((lane >= first) & (lane < first + 3), 1.0, 0.0)
                qh = jnp.concatenate([qh, (pick + q_terms[:, j * LANES:(j + 1) * LANES]).astype(BF16)], axis=1)
            if kind == "slc":
                qh = jnp.concatenate([qh, drop], axis=1)
            qs.append(qh)

        m_ref[...] = jnp.full(m_ref.shape, NEG, F32)
        acc_ref[...] = jnp.zeros(acc_ref.shape, F32)

        def block_kind(ahead, kb, qb):
            lo_rel = (kb - qb) * MASK_BLOCK - (MASK_BLOCK - 1)
            hi_rel = (kb - qb) * MASK_BLOCK + (MASK_BLOCK - 1)
            floor = ahead - WIN if kind == "win" else lo_rel - 1
            if lo_rel > ahead or hi_rel <= floor:
                return "dead"
            if hi_rel <= ahead and lo_rel > floor:
                return "full"
            return "part"

        def block_mask(ahead, kb, qb):
            rel = (lax.broadcasted_iota(jnp.int32, (MASK_BLOCK, MASK_BLOCK), 0)
                   - lax.broadcasted_iota(jnp.int32, (MASK_BLOCK, MASK_BLOCK), 1)
                   + (kb - qb) * MASK_BLOCK)
            msk = rel <= ahead
            if kind == "win":
                msk = msk & (rel > ahead - WIN)
            return msk

        kbs, qbs = range(width // MASK_BLOCK), range(TQ // MASK_BLOCK)
        blk = lambda b: slice(b * MASK_BLOCK, (b + 1) * MASK_BLOCK)

        def scores(k0, u, ahead=None):
            kx = k_ref[pl.ds(k0, width), :]
            ex = ext_ref[pl.ds(k0, width), :] if has_ext else None
            for hi, h in enumerate(heads):
                kk = kx[:, h.k_off:h.k_off + LANES]
                if ex is not None:
                    kk = jnp.concatenate([kk, ex], axis=1)
                if ahead is None:
                    st = lax.dot_general(kk, qs[hi], _NT, preferred_element_type=F32)
                    bufs[u][hi] = st
                    mx_ref[u * nh + hi] = jnp.max(st, axis=0, keepdims=True)
                    continue
                col_max = [None] * len(qbs)
                for kb in kbs:
                    live = [qb for qb in qbs if block_kind(ahead, kb, qb) != "dead"]
                    if not live:
                        continue
                    q_rows = slice(live[0] * MASK_BLOCK, (live[-1] + 1) * MASK_BLOCK)
                    st = lax.dot_general(kk[blk(kb)], qs[hi][q_rows], _NT, preferred_element_type=F32)
                    for qb in range(live[0], live[-1] + 1):
                        sb = st[:, blk(qb - live[0])]
                        if block_kind(ahead, kb, qb) == "part":
                            sb = jnp.where(block_mask(ahead, kb, qb), sb, NEG)
                        bufs[u][hi, blk(kb), blk(qb)] = sb
                        cm = jnp.max(sb, axis=0, keepdims=True)
                        col_max[qb] = cm if col_max[qb] is None else jnp.maximum(col_max[qb], cm)
                mx_ref[u * nh + hi] = jnp.concatenate(col_max, axis=1)

        def update(k0, u, ahead=None):
            for hi, h in enumerate(heads):
                m_prev = m_ref[hi]
                m_new = jnp.maximum(m_prev, mx_ref[u * nh + hi])
                alpha = jnp.exp2(m_prev - m_new)
                if ahead is None:
                    ones = jnp.ones((BF16_ROWS, width), BF16)
                    pt = jnp.exp2(bufs[u][hi] - m_new).astype(BF16)
                    lhs = jnp.concatenate([v_ref[h.v0:h.v1, pl.ds(k0, width)], ones], axis=0)
                    new = jnp.dot(lhs, pt, preferred_element_type=F32)
                else:
                    ones = jnp.ones((BF16_ROWS, MASK_BLOCK), BF16)
                    cols = []
                    for qb in qbs:
                        col = None
                        for kb in kbs:
                            if block_kind(ahead, kb, qb) == "dead":
                                continue
                            pt = jnp.exp2(bufs[u][hi, blk(kb), blk(qb)] - m_new[:, blk(qb)]).astype(BF16)
                            kstart = pl.multiple_of(k0 + kb * MASK_BLOCK, MASK_BLOCK)
                            lhs = jnp.concatenate(
                                [v_ref[h.v0:h.v1, pl.ds(kstart, MASK_BLOCK)], ones], axis=0)
                            part = jnp.dot(lhs, pt, preferred_element_type=F32)
                            col = part if col is None else col + part
                        cols.append(col)
                    new = jnp.concatenate(cols, axis=1)
                acc_ref[hi] = alpha * acc_ref[hi] + new
                m_ref[hi] = m_new

        tile = lambda t: pl.multiple_of(t * TK, TK)
        U = len(bufs)
        if kind == "win":
            @pl.when(i == 0)
            def _():
                scores(tile(0), 0, 0)
                update(tile(0), 0, 0)

            @pl.when(i > 0)
            def _():
                scores(tile(i - 1), 0, WIN)
                scores(tile(i), 1, 0)
                update(tile(i - 1), 0, WIN)
                update(tile(i), 1, 0)
        else:
            n_full = i * diag
            n_rounds = jnp.maximum(n_full - 1, 0) // U
            pl.when(n_full == 0)(lambda: scores(tile(0), 0, 0))
            pl.when(n_full > 0)(lambda: scores(tile(0), 0))

            def body(j, c):
                for u in range(U):
                    scores(tile(U * j + u + 1), (u + 1) % U)
                    update(tile(U * j + u), u)
                return c
            lax.fori_loop(0, n_rounds, body, 0)
            t0 = U * n_rounds

            def tail(left):
                ahead = lambda u: -(u - left) * TK if u >= left else None
                last = left + diag - 1
                for u in range(last):
                    scores(tile(t0 + u + 1), (u + 1) % U, ahead(u + 1))
                    update(tile(t0 + u), u % U, ahead(u))
                update(tile(t0 + last), last % U, ahead(last))

            for left in range(U + 1):
                pl.when(n_full - t0 == left)(lambda left=left: tail(left))

        outs = []
        for hi in range(nh):
            acc = acc_ref[hi]
            outs.append(acc[:rows] / acc[rows:rows + 1])
        if kind == "diff":
            lm = lam_ref[...]
            la = jnp.sum(lm[0:1] * lm[1:2], axis=1, keepdims=True)
            lb = jnp.sum(lm[2:3] * lm[3:4], axis=1, keepdims=True)
            lam_full = jnp.exp(la) - jnp.exp(lb) + lam_init
            for g in range(nh // 2):
                oa = jnp.transpose(outs[2 * g] - lam_full * outs[2 * g + 1])
                y = oa * lax.rsqrt(jnp.mean(oa * oa, axis=-1, keepdims=True) + EPS)
                y = (y * sub_ref[...]) * (1.0 - lam_init)
                o_ref[:, g * rows:(g + 1) * rows] = y.astype(o_ref.dtype)
        else:
            o_ref[...] = jnp.transpose(jnp.concatenate(outs, axis=0)).astype(o_ref.dtype)

    in_specs = [
        pl.BlockSpec((TQ, q_w), lambda b, p, i: (b * nq + i, p)),
        pl.BlockSpec((S, k_w), lambda b, p, i: (b, p)),
        pl.BlockSpec((v_rows, S), lambda b, p, i: (p, b)),
    ]
    args = [q, k, vt]
    if kind == "fox":
        qplace = np.zeros((P, LANES, nh * LANES), np.float32)
        for p in range(P):
            for j in range(nh):
                for t in range(3):
                    qplace[p, 3 * (nh * p + j) + t, j * LANES + FOX_ONES_LANE + t] = -1.0
        in_specs.append(pl.BlockSpec((S, LANES), lambda b, p, i: (b, 0)))
        in_specs.append(pl.BlockSpec((None, LANES, nh * LANES), lambda b, p, i: (p, 0, 0)))
        args += [ext, jnp.asarray(qplace, BF16)]
    if kind == "slc":
        in_specs.append(pl.BlockSpec((S, LANES), lambda b, p, i: (0, 0)))
        in_specs.append(pl.BlockSpec((TQ, LANES), lambda b, p, i: (b * nq + i, p)))
        args += [ext, sel]
    if kind == "diff":
        in_specs.append(pl.BlockSpec(lam.shape, lambda b, p, i: (0, 0)))
        in_specs.append(pl.BlockSpec((1, LANES), lambda b, p, i: (0, 0)))
        args += [lam, subln.reshape(1, LANES)]
    return pl.pallas_call(
        kern, grid=(B, P, nq), in_specs=in_specs,
        out_specs=pl.BlockSpec((TQ, out_w), lambda b, p, i: (b * nq + i, p)),
        out_shape=jax.ShapeDtypeStruct((B * S, P * out_w), out_dtype),
        scratch_shapes=[pltpu.VMEM((nh, 1, TQ), F32),
                        pltpu.VMEM((nh, rows + BF16_ROWS, TQ), F32),
                        pltpu.VMEM((n_bufs * nh, 1, TQ), F32)]
        + [pltpu.VMEM((nh, width, TQ), F32)] * n_bufs,
        compiler_params=_params("parallel", "parallel", "arbitrary"), name=name)(*args)


_DIFF_HEADS = tuple(Head(s * LANES, half, s * LANES, s * LANES, (s + 1) * LANES)
                    for s in range(2) for half in ("lo", "hi"))
_PAIR_HEADS = tuple(Head(s * LANES, half, s * LANES, (2 * s + j) * HALF_LANES, (2 * s + j + 1) * HALF_LANES)
                    for s in range(2) for j, half in enumerate(("lo", "hi")))
_QUAD_HEADS = tuple(Head(s * LANES, half, 0, 0, HALF_LANES) for s in range(2) for half in ("lo", "hi"))
_WIDE_HEADS = tuple(Head(j * LANES, None, j * LANES, j * HALF_LANES, (j + 1) * HALF_LANES) for j in range(4))


def _nsa_compress(xk, xv, pe, w1x, w2d):
    B, R, K = xk.shape
    G = NSA_GROUPS

    def kern(xk_ref, xv_ref, pe_ref, w1_ref, w2_ref, k_ref, vt_ref):
        for t, x_ref in enumerate((xk_ref, xv_ref)):
            x = x_ref[...]
            xa = (x + pe_ref[t, 0]).astype(BF16)
            xb = (x + pe_ref[t, 1]).astype(BF16)
            for g in range(G):
                a = jnp.dot(xa, w1_ref[t, 0, g], preferred_element_type=F32)
                b = jnp.dot(xb, w1_ref[t, 1, g], preferred_element_type=F32)
                h = a + pltpu.roll(b, R - 1, 0)
                hs = h * _sigmoid(h)
                o = jnp.dot(hs.astype(BF16), w2_ref[t], preferred_element_type=F32)
                if t == 0:
                    k_ref[g] = o
                else:
                    vt_ref[g] = jnp.transpose(o)

    return pl.pallas_call(
        kern, grid=(B,),
        in_specs=[pl.BlockSpec((None, R, K), lambda b: (b, 0, 0)),
                  pl.BlockSpec((None, R, K), lambda b: (b, 0, 0)),
                  pl.BlockSpec(pe.shape, lambda b: (0, 0, 0, 0)),
                  pl.BlockSpec(w1x.shape, lambda b: (0, 0, 0, 0, 0)),
                  pl.BlockSpec(w2d.shape, lambda b: (0, 0, 0))],
        out_specs=[pl.BlockSpec((None, G, R, LANES), lambda b: (b, 0, 0, 0)),
                   pl.BlockSpec((None, G, LANES, R), lambda b: (b, 0, 0, 0))],
        out_shape=[jax.ShapeDtypeStruct((B, G, R, LANES), F32),
                   jax.ShapeDtypeStruct((B, G, LANES, R), F32)],
        compiler_params=_params("parallel"), name="nsa_compress")(xk, xv, pe, w1x, w2d)


def _nsa_cmp_select(q, cmp_k, cmp_vt, ov_t, *, B, S):
    T = SEL_TILE
    nq = S // T
    R = cmp_k.shape[2]
    G = NSA_GROUPS
    d = NSA_DIM
    n_sel = min(SLC_TOPK, S // SLC_LEN)
    assert n_sel >= 3 and FORCE_SCORE > NSA_HEADS // G

    bucket = min(LANES, R)
    assert R % bucket == 0 and bucket % (SLC_LEN // CMP_STRIDE) == 0

    def kern(q_ref, kc_ref, vt_ref, ov_ref, o_ref, sel_ref):
        i = pl.program_id(2)
        lane = lax.broadcasted_iota(jnp.int32, (T, LANES), 1)
        lo = lane < HALF_LANES
        q_all = q_ref[...]
        qs = []
        for hh in range(4):
            qh = q_all[:, (hh // 2) * LANES:(hh // 2 + 1) * LANES]
            qs.append(jnp.where(lo, qh, jnp.zeros_like(qh)) if hh % 2 == 0
                      else jnp.where(lo, jnp.zeros_like(qh), qh))

        def variant(rv):
            nb = rv * CMP_STRIDE // SLC_LEN
            tq = lax.broadcasted_iota(jnp.int32, (rv, T), 1) + i * T
            cend = lax.broadcasted_iota(jnp.int32, (rv, T), 0) * CMP_STRIDE + (CMP_LEN - 1)
            vis = cend <= tq
            kc = kc_ref[:rv, :].astype(BF16)
            vt = vt_ref[:d, :rv].astype(BF16)
            psum = jnp.zeros((rv, T), F32)
            o_heads = []
            for hh in range(4):
                st = lax.dot_general(kc, qs[hh], _NT, preferred_element_type=F32)
                st = jnp.where(vis, st, -jnp.inf)
                m = jnp.max(st, axis=0, keepdims=True)
                e = jnp.exp2(st - jnp.where(m > -jnp.inf, m, 0.0))
                p = e / jnp.maximum(jnp.sum(e, axis=0, keepdims=True), 1e-30)
                o_heads.append(jnp.dot(vt, p.astype(BF16), preferred_element_type=F32))
                psum = psum + p
            o_ref[...] = jnp.transpose(jnp.concatenate(o_heads, axis=0)).astype(o_ref.dtype)
            imp = jnp.zeros((nb, T), F32)
            for part in _split_bf16(psum, 2):
                imp = imp + jnp.dot(ov_ref[:nb, :rv], part, preferred_element_type=F32)
            blk = lax.broadcasted_iota(jnp.int32, (nb, T), 0)
            cur = (lax.broadcasted_iota(jnp.int32, (nb, T), 1) + i * T) // SLC_LEN
            forced = (blk == 0) | (blk == cur) | (blk == cur - 1)
            work = jnp.where(forced, -jnp.inf, imp)
            work = jnp.where(blk > cur, NEG, work)
            blk_f = blk.astype(F32)
            for _ in range(n_sel - 3):
                mx = jnp.max(work, axis=0, keepdims=True)
                first = jnp.min(jnp.where(work == mx, blk_f, float(LANES)), axis=0, keepdims=True)
                work = jnp.where(blk_f == first, -jnp.inf, work)
            sel = jnp.where(work == -jnp.inf, 1.0, 0.0)
            if nb < LANES:
                sel = jnp.concatenate([sel, jnp.zeros((LANES - nb, T), F32)], axis=0)
            sel_ref[...] = jnp.transpose(sel).astype(sel_ref.dtype)

        need = (i + 1) * (T // CMP_STRIDE)
        which = (need - 1) // bucket
        for bk in range(R // bucket):
            pl.when(which == bk)(lambda bk=bk: variant((bk + 1) * bucket))

    return pl.pallas_call(
        kern, grid=(B, G, nq),
        in_specs=[pl.BlockSpec((T, 2 * LANES), lambda b, g, i: (b * nq + i, g)),
                  pl.BlockSpec((None, None, R, LANES), lambda b, g, i: (b, g, 0, 0)),
                  pl.BlockSpec((None, None, LANES, R), lambda b, g, i: (b, g, 0, 0)),
                  pl.BlockSpec((LANES, R), lambda b, g, i: (0, 0))],
        out_specs=[pl.BlockSpec((T, 2 * LANES), lambda b, g, i: (b * nq + i, g)),
                   pl.BlockSpec((T, LANES), lambda b, g, i: (b * nq + i, g))],
        out_shape=[jax.ShapeDtypeStruct((B * S, G * 2 * LANES), BF16),
                   jax.ShapeDtypeStruct((B * S, G * LANES), BF16)],
        compiler_params=_params("parallel", "parallel", "parallel"), name="nsa_cmp_select")(
            q, cmp_k, cmp_vt, ov_t)


def _odd_out(o_cmp, o_slc, o_win, gz, expand, o_mla, w_nsa, w_mla, x):
    N, C = o_cmp.shape
    D = x.shape[1]
    tm = min(ROW_TILE, N)

    def kern(c_ref, s_ref, w_ref, g_ref, e_ref, m_ref, wn_ref, wm_ref, x_ref, o_ref):
        gate = _sigmoid(g_ref[...])
        parts = _split_bf16(gate, 2)
        acc = jnp.zeros((tm, C), F32)
        for j, br in enumerate((c_ref, s_ref, w_ref)):
            gj = sum(jnp.dot(part, e_ref[j], preferred_element_type=F32) for part in parts)
            acc = acc + gj * br[...].astype(F32)
        o_nsa = acc.astype(BF16)
        o_mla_ = m_ref[...]
        for c0 in range(0, D, COL_CHUNK):
            cols = slice(c0, c0 + COL_CHUNK)
            y = (jnp.dot(o_nsa, wn_ref[:, cols], preferred_element_type=F32)
                 + jnp.dot(o_mla_, wm_ref[:, cols], preferred_element_type=F32))
            o_ref[:, cols] = x_ref[:, cols] + y

    row = pl.BlockSpec((tm, C), lambda i: (i, 0))
    wide = pl.BlockSpec((tm, D), lambda i: (i, 0))
    return pl.pallas_call(
        kern, grid=(N // tm,),
        in_specs=[row, row, row, pl.BlockSpec((tm, LANES), lambda i: (i, 0)),
                  pl.BlockSpec(expand.shape, lambda i: (0, 0, 0)),
                  pl.BlockSpec((tm, o_mla.shape[1]), lambda i: (i, 0)),
                  pl.BlockSpec(w_nsa.shape, lambda i: (0, 0)),
                  pl.BlockSpec(w_mla.shape, lambda i: (0, 0)), wide],
        out_specs=wide, out_shape=jax.ShapeDtypeStruct((N, D), F32),
        compiler_params=_params("parallel"), name="odd_out")(
            o_cmp, o_slc, o_win, gz, expand, o_mla, w_nsa, w_mla, x)


def _mla_kv(ckv, gain, wk, wv, kr):
    N, K = ckv.shape
    tm = min(ROW_TILE, N)
    H = MLA_HEADS

    def kern(c_ref, g_ref, wk_ref, wv_ref, kr_ref, k_ref, v_ref):
        xf = c_ref[...]
        y = xf * lax.rsqrt(jnp.mean(xf * xf, axis=-1, keepdims=True) + EPS)
        a = (y * g_ref[...]).astype(BF16)
        kr2 = jnp.concatenate([kr_ref[...]] * 2, axis=1)
        for h in range(0, H, 2):
            cols = slice(h * LANES, (h + 2) * LANES)
            kh = jnp.dot(a, wk_ref[:, cols], preferred_element_type=F32)
            k_ref[:, cols] = (kh + kr2).astype(k_ref.dtype)
        v_ref[...] = jnp.dot(a, wv_ref[...], preferred_element_type=F32).astype(v_ref.dtype)

    return pl.pallas_call(
        kern, grid=(N // tm,),
        in_specs=[pl.BlockSpec((tm, K), lambda i: (i, 0)),
                  pl.BlockSpec((1, K), lambda i: (0, 0)),
                  pl.BlockSpec(wk.shape, lambda i: (0, 0)),
                  pl.BlockSpec(wv.shape, lambda i: (0, 0)),
                  pl.BlockSpec((tm, LANES), lambda i: (i, 0))],
        out_specs=[pl.BlockSpec((tm, H * LANES), lambda i: (i, 0)),
                   pl.BlockSpec((tm, H * MLA_V), lambda i: (i, 0))],
        out_shape=[jax.ShapeDtypeStruct((N, H * LANES), BF16),
                   jax.ShapeDtypeStruct((N, H * MLA_V), BF16)],
        compiler_params=_params("parallel"), name="mla_kv")(
            ckv, gain.reshape(1, K).astype(F32), wk, wv, kr)


def _mem_attn(x, gain, wq, wo, mem_k, mem_v, *, S, pre=()):
    N, D = x.shape
    tm = min(ROW_TILE, S)
    M = mem_k.shape[1]
    per_b = S // tm
    scale = XA_DIM ** -0.5
    n_pre = len(pre)

    def kern(x_ref, g_ref, wq_ref, wo_ref, k_ref, v_ref, *rest):
        o_ref = rest[-1]
        xf = x_ref[...]
        for a_ref, w_ref in zip(rest[:n_pre], rest[n_pre:2 * n_pre]):
            xf = xf + jnp.dot(a_ref[...], w_ref[...], preferred_element_type=F32)
        y = xf * lax.rsqrt(jnp.mean(xf * xf, axis=-1, keepdims=True) + EPS)
        a = (y * g_ref[...]).astype(BF16)
        q = (jnp.dot(a, wq_ref[...], preferred_element_type=F32) * scale).astype(BF16)
        heads = []
        for h in range(XA_HEADS):
            sl = slice(h * XA_DIM, (h + 1) * XA_DIM)
            s = lax.dot_general(q[:, sl], k_ref[:, sl], _NT, preferred_element_type=F32)
            e = jnp.exp(s - jnp.max(s, axis=1, keepdims=True))
            p = e / jnp.sum(e, axis=1, keepdims=True)
            heads.append(jnp.dot(p.astype(BF16), v_ref[:, sl], preferred_element_type=F32).astype(BF16))
        o = jnp.concatenate(heads, axis=1)
        o_ref[...] = xf + jnp.dot(o, wo_ref[...], preferred_element_type=F32)

    return pl.pallas_call(
        kern, grid=(N // tm,),
        in_specs=[pl.BlockSpec((tm, D), lambda i: (i, 0)),
                  pl.BlockSpec((1, D), lambda i: (0, 0)),
                  pl.BlockSpec(wq.shape, lambda i: (0, 0)),
                  pl.BlockSpec(wo.shape, lambda i: (0, 0)),
                  pl.BlockSpec((None, M, XA_HEADS * XA_DIM), lambda i: (i // per_b, 0, 0)),
                  pl.BlockSpec((None, M, XA_HEADS * XA_DIM), lambda i: (i // per_b, 0, 0))]
        + [pl.BlockSpec((tm, a.shape[1]), lambda i: (i, 0)) for a, _ in pre]
        + [pl.BlockSpec(w.shape, lambda i: (0, 0)) for _, w in pre],
        out_specs=pl.BlockSpec((tm, D), lambda i: (i, 0)),
        out_shape=jax.ShapeDtypeStruct((N, D), F32),
        compiler_params=_params("parallel"), name="mem_attn")(
            x, gain.reshape(1, D).astype(F32), wq, wo, mem_k, mem_v,
            *[a for a, _ in pre], *[w for _, w in pre])


def _ffn(x, gain, w13, w2, out_gain=None):
    N, D = x.shape
    FF = w2.shape[0]
    tm = min(ROW_TILE, N)
    chunk = 2 * LANES
    assert FF % chunk == 0

    final = out_gain is not None

    def kern(x_ref, g_ref, w13_ref, w2_ref, *rest):
        og_ref, o_ref = rest if final else (None, rest[0])
        xf = x_ref[...]
        y = xf * lax.rsqrt(jnp.mean(xf * xf, axis=-1, keepdims=True) + EPS)
        a = (y * g_ref[...]).astype(BF16)
        acc = xf
        for c in range(0, FF, chunk):
            g = jnp.dot(a, w13_ref[:, c:c + chunk], preferred_element_type=F32)
            u = jnp.dot(a, w13_ref[:, FF + c:FF + c + chunk], preferred_element_type=F32)
            hdn = (g * _sigmoid(g) * u).astype(BF16)
            acc = acc + jnp.dot(hdn, w2_ref[c:c + chunk, :], preferred_element_type=F32)
        if final:
            acc = acc * lax.rsqrt(jnp.mean(acc * acc, axis=-1, keepdims=True) + EPS) * og_ref[...]
        o_ref[...] = acc

    vec = pl.BlockSpec((1, D), lambda i: (0, 0))
    in_specs = [pl.BlockSpec((tm, D), lambda i: (i, 0)), vec,
                pl.BlockSpec(w13.shape, lambda i: (0, 0), pipeline_mode=pl.Buffered(1)),
                pl.BlockSpec(w2.shape, lambda i: (0, 0), pipeline_mode=pl.Buffered(1))]
    args = [x, gain.reshape(1, D).astype(F32), w13, w2]
    if final:
        in_specs.append(vec)
        args.append(out_gain.reshape(1, D).astype(F32))
    return pl.pallas_call(
        kern, grid=(N // tm,), in_specs=in_specs,
        out_specs=pl.BlockSpec((tm, D), lambda i: (i, 0)),
        out_shape=jax.ShapeDtypeStruct((N, D), F32),
        compiler_params=_params("parallel"), name="ffn")(*args)


def _pad_cols(w, width):
    return jnp.pad(w, ((0, 0), (0, width - w.shape[1])))


def _even_mixer(x, gain, w_in, b_f, lam, subln, w_out, layer_idx, ropes, B, S):
    D = x.shape[1]
    blk = DIFF_HEADS * 2 * DIFF_QK
    w = _pad_cols(w_in, 6 * blk + LANES).astype(BF16)
    sc = DIFF_QK ** -0.5 * LOG2E
    segs = [Seg(0, blk, BF16, "r64", sc), Seg(blk, blk, BF16, "r64"), Seg(2 * blk, blk, BF16),
            Seg(3 * blk, blk, BF16, None, sc), Seg(4 * blk, blk, BF16), Seg(5 * blk, blk, BF16),
            Seg(6 * blk, LANES, F32)]
    aq, ak, av, fq, fk, fv, fz = _linear([x], [w], segs, gain=gain, ropes=ropes, seq=S, name="even_in")
    lam_init = 0.8 - 0.6 * math.exp(-0.3 * layer_idx)
    oa = _flash(aq, ak, av.T, B=B, S=S, P=DIFF_HEADS // 2, q_w=2 * LANES, k_w=2 * LANES, v_rows=2 * LANES,
                heads=_DIFF_HEADS, kind="diff", out_dtype=BF16, lam=lam.astype(F32),
                subln=subln.astype(F32), lam_init=lam_init, name="diff_attn")
    terms = _forget_key_terms(fz, _pad_cols(b_f.reshape(1, -1), LANES).astype(F32), B, S)
    of = _flash(fq, fk, fv.T, B=B, S=S, P=FOX_HEADS // 4, q_w=2 * LANES, k_w=2 * LANES, v_rows=2 * LANES,
                heads=_PAIR_HEADS, kind="fox", out_dtype=BF16, ext=terms, name="fox_attn")
    wo = w_out.astype(BF16)
    return x, ((oa, wo[:blk]), (of, wo[blk:]))


def _odd_in_weight(w_in):
    d = NSA_DIM
    o = np.cumsum((0, NSA_HEADS * d) + (NSA_GROUPS * d,) * 6 + (NSA_HEADS * 3, MLA_Q_RANK, MLA_KV_RANK, MLA_ROPE))
    nq, kc, vc, ks, vs, kw, vw, gz, cq, ckv, kr = [w_in[:, o[j]:o[j + 1]] for j in range(11)]

    def dup(wg):
        return jnp.concatenate([wg[:, :d], wg[:, :d], wg[:, d:], wg[:, d:]], axis=1)

    zeros = lambda n: jnp.zeros((w_in.shape[0], n), w_in.dtype)
    kr_slab = jnp.concatenate([zeros(MLA_NOPE), kr, zeros(LANES - MLA_NOPE - MLA_ROPE)], axis=1)
    cols = [nq, kc, vc, dup(ks), vs, dup(kw), vw, _pad_cols(gz, LANES), cq, ckv, kr_slab]
    return jnp.concatenate(cols, axis=1).astype(BF16)


def _odd_mixer(x, gain, w_in, cmp_pos, cmp_w1, cmp_w2, q_norm, kv_norm, w_uq, w_ukv, w_out, ropes, B, S):
    D = x.shape[1]
    G, d = NSA_GROUPS, NSA_DIM
    w = _odd_in_weight(w_in)
    sc = d ** -0.5 * LOG2E
    widths = [(NSA_HEADS * d, BF16, "r64", sc), (LANES, F32, "r64", 1.0), (LANES, F32, None, 1.0),
              (2 * LANES, BF16, "r64", 1.0), (LANES, BF16, None, 1.0),
              (2 * LANES, BF16, "r64", 1.0), (LANES, BF16, None, 1.0),
              (LANES, F32, None, 1.0), (MLA_Q_RANK, F32, None, 1.0), (MLA_KV_RANK, F32, None, 1.0),
              (LANES, F32, "mla", 1.0)]
    segs, start = [], 0
    for wd, dt, rp, s_ in widths:
        segs.append(Seg(start, wd, dt, rp, s_))
        start += wd
    q, kc, vc, ks, vs, kw, vw, gz, cq, ckv, kr = _linear(
        [x], [w], segs, gain=gain, ropes=ropes, seq=S, name="odd_in")

    R = S // CMP_STRIDE

    K = CMP_STRIDE * G * d
    half = CMP_LEN // 2
    pe = jnp.broadcast_to(cmp_pos.reshape(2, 2, half, 1, d), (2, 2, half, G, d)).reshape(2, 2, 1, K).astype(F32)
    eye = jnp.eye(G, dtype=cmp_w1.dtype)
    w1x = (cmp_w1.reshape(2, 2, 1, half, 1, d, CMP_HIDDEN) * eye[None, None, :, None, :, None, None])
    w1x = w1x.reshape(2, 2, G, K, CMP_HIDDEN).astype(BF16)
    w2d = jnp.concatenate([cmp_w2, cmp_w2], axis=-1).astype(BF16)
    cmp_k, cmp_vt = _nsa_compress(kc.reshape(B, R, K), vc.reshape(B, R, K), pe, w1x, w2d)
    cs = np.arange(R) * CMP_STRIDE
    bs = np.arange(LANES) * SLC_LEN
    ov = ((cs[:, None] < bs[None, :] + SLC_LEN) & (cs[:, None] + CMP_LEN > bs[None, :])
          & (np.arange(R)[:, None] < R - 1) & (bs[None, :] < S))
    o_cmp, sel = _nsa_cmp_select(q, cmp_k, cmp_vt, jnp.asarray(ov.T.astype(np.float32), BF16), B=B, S=S)
    onehot = (np.arange(S)[:, None] // SLC_LEN == np.arange(LANES)[None, :]).astype(np.float32)
    o_slc = _flash(q, ks, vs.T, B=B, S=S, P=G, q_w=2 * LANES, k_w=LANES, v_rows=HALF_LANES,
                   heads=_QUAD_HEADS, kind="slc", out_dtype=BF16, ext=jnp.asarray(onehot, BF16),
                   sel=sel, name="nsa_slc")
    o_win = _flash(q, kw, vw.T, B=B, S=S, P=G, q_w=2 * LANES, k_w=LANES, v_rows=HALF_LANES,
                   heads=_QUAD_HEADS, kind="win", out_dtype=BF16, tq=WIN, tk=WIN, n_bufs=2, name="nsa_win")
    ex = np.zeros((3, LANES, NSA_HEADS * d), np.float32)
    for hh in range(NSA_HEADS):
        for j in range(3):
            ex[j, hh * 3 + j, hh * d:(hh + 1) * d] = 1.0

    H = MLA_HEADS
    qk = MLA_NOPE + MLA_ROPE
    wq_slab = jnp.pad(w_uq.reshape(MLA_Q_RANK, H, qk), ((0, 0), (0, 0), (0, LANES - qk)))
    wq_slab = wq_slab.reshape(MLA_Q_RANK, H * LANES).astype(BF16)
    (qm,) = _linear([cq], [wq_slab], [Seg(0, H * LANES, BF16, "mla", qk ** -0.5 * LOG2E)],
                    gain=q_norm, ropes=ropes, seq=S, name="mla_q")
    wkv = w_ukv.reshape(MLA_KV_RANK, H, MLA_NOPE + MLA_V)
    wk = jnp.pad(wkv[:, :, :MLA_NOPE], ((0, 0), (0, 0), (0, LANES - MLA_NOPE)))
    wk = wk.reshape(MLA_KV_RANK, H * LANES).astype(BF16)
    wv = wkv[:, :, MLA_NOPE:].reshape(MLA_KV_RANK, H * MLA_V).astype(BF16)
    km, vm = _mla_kv(ckv, kv_norm, wk, wv, kr)
    o_mla = _flash(qm, km, vm.T, B=B, S=S, P=H // 4, q_w=4 * LANES, k_w=4 * LANES, v_rows=2 * LANES,
                   heads=_WIDE_HEADS, kind="mla", out_dtype=BF16, name="mla_attn")
    wo = w_out.astype(BF16)
    half = NSA_HEADS * d
    return _odd_out(o_cmp, o_slc, o_win, gz, jnp.asarray(ex, BF16), o_mla, wo[:half], wo[half:], x)


def kernel(x, mem, mem_norm, norm_mix, norm_mem, norm_ffn, ev_w_in, ev_b_f, ev_lam, ev_subln, ev_w_out, od_w_in, nsa_cmp_pos, nsa_cmp_w1, nsa_cmp_w2, mla_q_norm, mla_kv_norm, mla_w_uq, mla_w_ukv, od_w_out, xa_wq, xa_wkv, xa_wo, ffn_w13, ffn_w2, final_norm):
    B, S, D = x.shape
    M = mem.shape[1]
    depth = norm_mix.shape[0]
    ropes = {
        "r64": (_rope_tables(S, NSA_DIM // 2, 0, LANES), NSA_DIM // 2),
        "mla": (_rope_tables(S, MLA_ROPE // 2, MLA_NOPE, MLA_NOPE + MLA_ROPE), MLA_ROPE // 2),
    }
    xa_w = XA_HEADS * XA_DIM
    h = x.reshape(B * S, D)
    mem2 = mem.reshape(B * M, D)
    for li in range(depth):
        j = li // 2
        pre = ()
        if li % 2 == 0:
            h, pre = _even_mixer(h, norm_mix[li], ev_w_in[j], ev_b_f[j], ev_lam[j], ev_subln[j], ev_w_out[j],
                                 li, ropes, B, S)
        else:
            h = _odd_mixer(h, norm_mix[li], od_w_in[j], nsa_cmp_pos[j], nsa_cmp_w1[j], nsa_cmp_w2[j],
                           mla_q_norm[j], mla_kv_norm[j], mla_w_uq[j], mla_w_ukv[j], od_w_out[j],
                           ropes, B, S)
        mk, mv = _linear([mem2], [xa_wkv[li].astype(BF16)],
                         [Seg(0, xa_w, BF16), Seg(xa_w, xa_w, BF16)], gain=mem_norm, name="mem_kv")
        h = _mem_attn(h, norm_mem[li], xa_wq[li].astype(BF16), xa_wo[li].astype(BF16),
                      mk.reshape(B, M, xa_w), mv.reshape(B, M, xa_w), S=S, pre=pre)
        h = _ffn(h, norm_ffn[li], ffn_w13[li].astype(BF16), ffn_w2[li].astype(BF16),
                 out_gain=final_norm if li == depth - 1 else None)
    return h.reshape(B, S, D)
```
